```python
import jax, jax.numpy as jnp
from jax import lax
import numpy as np

D_MODEL = 2048
BATCH = 2
SEQ = 4096
DEPTH = 1
DEC_BATCH = 8
DEC_SEQ = 16
PAST_LEN = 4096

CHUNK = 64
QBLOCK = 128
HEAD_DIM = 128
SB_HEADS = 6
BAND_HEADS = 6
MEM_HEADS = 4
SB_WIDTH = SB_HEADS * HEAD_DIM
BAND_WIDTH = BAND_HEADS * HEAD_DIM
MEM_WIDTH = MEM_HEADS * HEAD_DIM
N_MEM = 256
BAND_LEFT_CHUNKS = 8
BAND_ROWS = BAND_LEFT_CHUNKS * CHUNK
MAX_REL = 256
IN_WIDTH = 4 * SB_WIDTH + 4 * BAND_WIDTH + 2 * MEM_WIDTH + 3 * D_MODEL
RMS_EPS = 1e-6
NEG_INF = -1e30

kernel_name = "sandwich_gated_stickbreak_chunkband_stream_step"


def rms_norm(x, g):
    xf = x.astype(jnp.float32)
    xf = xf * lax.rsqrt(jnp.mean(xf * xf, axis=-1, keepdims=True) + RMS_EPS)
    return (xf * g.astype(jnp.float32)).astype(x.dtype)


def in_projection(x, g_pre, w_in):
    b, t, _ = x.shape
    widths = [SB_WIDTH] * 4 + [BAND_WIDTH] * 4 + [MEM_WIDTH] * 2 + [D_MODEL] * 3
    points = np.cumsum(widths)[:-1].tolist()
    parts = jnp.split(rms_norm(x, g_pre) @ w_in, points, axis=-1)
    sb_q, sb_k, sb_v, sb_g, bd_q, bd_k, bd_v, bd_g, mm_q, mm_g, mg_sb, mg_bd, mg_mm = parts
    hs = lambda a, n: a.reshape(b, t, n, HEAD_DIM)
    return (hs(sb_q, SB_HEADS), hs(sb_k, SB_HEADS), hs(sb_v, SB_HEADS), sb_g,
            hs(bd_q, BAND_HEADS), hs(bd_k, BAND_HEADS), hs(bd_v, BAND_HEADS), bd_g,
            hs(mm_q, MEM_HEADS), mm_g, mg_sb, mg_bd, mg_mm)


def memory_kv(mem, g_mem, w_mem_kv):
    b, m, _ = mem.shape
    mk, mv = jnp.split(rms_norm(mem, g_mem) @ w_mem_kv, 2, axis=-1)
    return (mk.reshape(b, m, MEM_HEADS, HEAD_DIM), mv.reshape(b, m, MEM_HEADS, HEAD_DIM))


def stick_breaking(q, k, v, q_pos, k_pos):
    z = jnp.einsum('bqhd,bkhd->bhqk', q, k).astype(jnp.float32) * (HEAD_DIM ** -0.5)
    mask = k_pos[None, :] < q_pos[:, None]
    log_1m = jnp.where(mask, jax.nn.log_sigmoid(-z), 0.0)
    suffix = lax.cumsum(log_1m, axis=3, reverse=True) - log_1m
    w = jnp.where(mask, jnp.exp(jax.nn.log_sigmoid(z) + suffix), 0.0)
    return jnp.einsum('bhqk,bkhd->bqhd', w.astype(v.dtype), v)


def stick_breaking_prompt(q, k, v):
    t = q.shape[1]
    pos = jnp.arange(t)
    outs = []
    for i in range(t // QBLOCK):
        lo, hi = i * QBLOCK, (i + 1) * QBLOCK
        outs.append(stick_breaking(q[:, lo:hi], k[:, :hi], v[:, :hi], pos[lo:hi], pos[:hi]))
    return jnp.concatenate(outs, axis=1)


def band_attention(q, k, v, q_pos, k_pos, rel_bias):
    s = jnp.einsum('bnqhd,bnkhd->bnhqk', q, k).astype(jnp.float32) * (HEAD_DIM ** -0.5)
    rel = jnp.clip(q_pos[:, :, None] - k_pos[:, None, :], -MAX_REL, MAX_REL) + MAX_REL
    bias = jnp.moveaxis(rel_bias.astype(jnp.float32)[:, rel], 0, 1)
    s = jnp.where((k_pos >= 0)[None, :, None, None, :], s + bias[None], NEG_INF)
    p = jax.nn.softmax(s, axis=-1)
    return jnp.einsum('bnhqk,bnkhd->bnqhd', p.astype(v.dtype), v)


def band_prompt(q, k, v, rel_bias):
    b, t, h, d = q.shape
    nc = t // CHUNK
    lc = BAND_LEFT_CHUNKS

    def gather_band(a):
        ac = jnp.pad(a.reshape(b, nc, CHUNK, h, d), ((0, 0), (lc, 0), (0, 0), (0, 0), (0, 0)))
        return jnp.concatenate([ac[:, j:j + nc] for j in range(lc + 1)], axis=2)

    c0 = jnp.arange(nc)[:, None] * CHUNK
    q_pos = c0 + jnp.arange(CHUNK)[None, :]
    k_pos = c0 - lc * CHUNK + jnp.arange((lc + 1) * CHUNK)[None, :]
    out = band_attention(q.reshape(b, nc, CHUNK, h, d), gather_band(k), gather_band(v),
                         q_pos, k_pos, rel_bias)
    return out.reshape(b, t, h, d)


def memory_attention(q, mk, mv):
    s = jnp.einsum('bthd,bmhd->bhtm', q, mk).astype(jnp.float32) * (HEAD_DIM ** -0.5)
    p = jax.nn.softmax(s, axis=-1)
    return jnp.einsum('bhtm,bmhd->bthd', p.astype(mv.dtype), mv)


def merge_output(x, o_sb, g_sb, o_bd, g_bd, o_mm, g_mm, mg_sb, mg_bd, mg_mm,
                 w_up_sb, w_up_band, w_up_mem, w_out, g_post):
    b, t, _ = x.shape

    def branch(o, g, w):
        return (o.reshape(b, t, -1) * jax.nn.silu(g)) @ w

    merged = (jax.nn.sigmoid(mg_sb) * branch(o_sb, g_sb, w_up_sb)
              + jax.nn.sigmoid(mg_bd) * branch(o_bd, g_bd, w_up_band)
              + jax.nn.sigmoid(mg_mm) * branch(o_mm, g_mm, w_up_mem))
    return x + rms_norm(merged @ w_out, g_post)


def setup_inputs(seed: int = 0) -> dict:
    key = jax.random.key(seed)
    ks = jax.random.split(key, 24)
    nrm = lambda k, shape, scale: jax.random.normal(k, shape, jnp.float32) * scale
    band_rows = min(BAND_ROWS, PAST_LEN)
    return {
        'x_prompt': nrm(ks[0], (BATCH, SEQ, D_MODEL), 1.0),
        'x_sample': nrm(ks[1], (DEC_BATCH, DEC_SEQ, D_MODEL), 1.0),
        'cache_sb_k': nrm(ks[2], (DEPTH, DEC_BATCH, PAST_LEN, SB_HEADS, HEAD_DIM), 1.0),
        'cache_sb_v': nrm(ks[3], (DEPTH, DEC_BATCH, PAST_LEN, SB_HEADS, HEAD_DIM), 1.0),
        'cache_band_k': nrm(ks[4], (DEPTH, DEC_BATCH, band_rows, BAND_HEADS, HEAD_DIM), 1.0),
        'cache_band_v': nrm(ks[5], (DEPTH, DEC_BATCH, band_rows, BAND_HEADS, HEAD_DIM), 1.0),
        'cache_mem_k': nrm(ks[6], (DEPTH, DEC_BATCH, N_MEM, MEM_HEADS, HEAD_DIM), 1.0),
        'cache_mem_v': nrm(ks[7], (DEPTH, DEC_BATCH, N_MEM, MEM_HEADS, HEAD_DIM), 1.0),
        'mem_prompt': nrm(ks[8], (BATCH, N_MEM, D_MODEL), 1.0),
        'g_pre': 1.0 + nrm(ks[9], (DEPTH, D_MODEL), 0.02),
        'w_in': nrm(ks[10], (DEPTH, D_MODEL, IN_WIDTH), D_MODEL ** -0.5),
        'rel_bias': nrm(ks[11], (DEPTH, BAND_HEADS, 2 * MAX_REL + 1), 0.1),
        'g_mem': 1.0 + nrm(ks[12], (DEPTH, D_MODEL), 0.02),
        'w_mem_kv': nrm(ks[13], (DEPTH, D_MODEL, 2 * MEM_WIDTH), D_MODEL ** -0.5),
        'w_up_sb': nrm(ks[14], (DEPTH, SB_WIDTH, D_MODEL), SB_WIDTH ** -0.5),
        'w_up_band': nrm(ks[15], (DEPTH, BAND_WIDTH, D_MODEL), BAND_WIDTH ** -0.5),
        'w_up_mem': nrm(ks[16], (DEPTH, MEM_WIDTH, D_MODEL), MEM_WIDTH ** -0.5),
        'w_out': nrm(ks[17], (DEPTH, D_MODEL, D_MODEL), D_MODEL ** -0.5),
        'g_post': 1.0 + nrm(ks[18], (DEPTH, D_MODEL), 0.02),
    }


def reference(x_prompt, x_sample, cache_sb_k, cache_sb_v, cache_band_k, cache_band_v,
              cache_mem_k, cache_mem_v, mem_prompt, g_pre, w_in, rel_bias, g_mem, w_mem_kv,
              w_up_sb, w_up_band, w_up_mem, w_out, g_post):
    past = cache_sb_k.shape[2]
    r_band = cache_band_k.shape[2]
    n_new = x_sample.shape[1]
    q_pos_s = past + jnp.arange(n_new)
    k_pos_sb = jnp.arange(past + n_new)
    k_pos_bd = past - r_band + jnp.arange(r_band + n_new)

    xp, xs = x_prompt, x_sample
    sbk_p, sbv_p, bdk_p, bdv_p, mk_p, mv_p = [], [], [], [], [], []
    sbk_s, sbv_s, bdk_s, bdv_s = [], [], [], []
    for l in range(DEPTH):
        (sq, sk, sv, sg, bq, bk, bv, bg, mq, mg, gsb, gbd, gmm) = in_projection(xp, g_pre[l], w_in[l])
        mk, mv = memory_kv(mem_prompt, g_mem[l], w_mem_kv[l])
        o_sb = stick_breaking_prompt(sq, sk, sv)
        o_bd = band_prompt(bq, bk, bv, rel_bias[l])
        o_mm = memory_attention(mq, mk, mv)
        xp = merge_output(xp, o_sb, sg, o_bd, bg, o_mm, mg, gsb, gbd, gmm,
                          w_up_sb[l], w_up_band[l], w_up_mem[l], w_out[l], g_post[l])
        sbk_p.append(sk); sbv_p.append(sv)
        bdk_p.append(bk[:, -BAND_ROWS:]); bdv_p.append(bv[:, -BAND_ROWS:])
        mk_p.append(mk); mv_p.append(mv)

        (sq2, sk2, sv2, sg2, bq2, bk2, bv2, bg2, mq2, mg2, gsb2, gbd2, gmm2) = in_projection(xs, g_pre[l], w_in[l])
        k_all = jnp.concatenate([cache_sb_k[l], sk2], axis=1)
        v_all = jnp.concatenate([cache_sb_v[l], sv2], axis=1)
        o_sb2 = stick_breaking(sq2, k_all, v_all, q_pos_s, k_pos_sb)
        kb = jnp.concatenate([cache_band_k[l], bk2], axis=1)[:, None]
        vb = jnp.concatenate([cache_band_v[l], bv2], axis=1)[:, None]
        o_bd2 = band_attention(bq2[:, None], kb, vb, q_pos_s[None], k_pos_bd[None], rel_bias[l])[:, 0]
        o_mm2 = memory_attention(mq2, cache_mem_k[l], cache_mem_v[l])
        xs = merge_output(xs, o_sb2, sg2, o_bd2, bg2, o_mm2, mg2, gsb2, gbd2, gmm2,
                          w_up_sb[l], w_up_band[l], w_up_mem[l], w_out[l], g_post[l])
        sbk_s.append(sk2); sbv_s.append(sv2)
        bdk_s.append(bk2); bdv_s.append(bv2)

    return (xp, xs,
            jnp.stack(sbk_p), jnp.stack(sbv_p), jnp.stack(bdk_p), jnp.stack(bdv_p),
            jnp.stack(mk_p), jnp.stack(mv_p),
            jnp.stack(sbk_s), jnp.stack(sbv_s), jnp.stack(bdk_s), jnp.stack(bdv_s))
```

```python
import functools
import math

import jax
import jax.numpy as jnp
from jax import lax
from jax.experimental import pallas as pl
from jax.experimental.pallas import tpu as pltpu

F32 = jnp.float32
BF16 = jnp.bfloat16

HEAD_DIM = 128
SB_HEADS = 6
BAND_HEADS = 6
MEM_HEADS = 4
SB_WIDTH = SB_HEADS * HEAD_DIM
BAND_WIDTH = BAND_HEADS * HEAD_DIM
MEM_WIDTH = MEM_HEADS * HEAD_DIM
CHUNK = 64
BAND_LEFT_CHUNKS = 8
BAND_ROWS = BAND_LEFT_CHUNKS * CHUNK
MAX_REL = 256
RMS_EPS = 1e-6
NEG_INF = -1e30
SCALE = HEAD_DIM ** -0.5

COL_SB_Q = 0
COL_SB_K = COL_SB_Q + SB_WIDTH
COL_SB_V = COL_SB_K + SB_WIDTH
COL_SB_G = COL_SB_V + SB_WIDTH
COL_BD_Q = COL_SB_G + SB_WIDTH
COL_BD_K = COL_BD_Q + BAND_WIDTH
COL_BD_V = COL_BD_K + BAND_WIDTH
COL_BD_G = COL_BD_V + BAND_WIDTH
COL_MM_Q = COL_BD_G + BAND_WIDTH
COL_MM_G = COL_MM_Q + MEM_WIDTH
COL_MG = COL_MM_G + MEM_WIDTH

VMEM_LIMIT_BYTES = 56 * 1024 * 1024
IN_TN = 1024
SB_TQ = 256
SB_TK = 256
BAND_GROUP = 4
BAND_TQ = BAND_GROUP * CHUNK
BAND_WIN = BAND_TQ + BAND_ROWS
NEW_PAD = 128


def _params(n_axes, vmem=VMEM_LIMIT_BYTES):
    return pltpu.CompilerParams(dimension_semantics=("arbitrary",) * n_axes,
                                vmem_limit_bytes=vmem)


def _nt_dot(a, b):
    return lax.dot_general(a, b, (((1,), (1,)), ((), ())), preferred_element_type=F32)


def _copy_cols(dst_ref, acc, col0, width, tile_j, tn, rows=None):
    lo = max(col0, tile_j * tn)
    hi = min(col0 + width, (tile_j + 1) * tn)
    if lo >= hi:
        return False
    src = acc[:, lo - tile_j * tn:hi - tile_j * tn]
    if rows is not None:
        src = src[rows[0]:rows[0] + rows[1], :]
    dst_ref[:, lo - col0:hi - col0] = src
    return True


def _inproj_kernel(x_ref, g_ref, w_ref, y_ref, sbk_ref, sbv_ref, bdk_ref, bdv_ref, h_ref,
                   *, tn, n_tiles, band_period, band_rows):
    i = pl.program_id(0)
    j = pl.program_id(1)

    @pl.when(j == 0)
    def _():
        x = x_ref[...]
        ms = jnp.mean(x * x, axis=-1, keepdims=True)
        h_ref[...] = ((x * lax.rsqrt(ms + RMS_EPS)) * g_ref[...]).astype(BF16)

    acc = jnp.dot(h_ref[...], w_ref[...], preferred_element_type=F32)
    y_ref[...] = acc.astype(BF16)

    is_band_block = (i % band_period) == (band_period - 1)
    for tj in range(n_tiles):
        touches_sb = any(max(c, tj * tn) < min(c + SB_WIDTH, (tj + 1) * tn) for c in (COL_SB_K, COL_SB_V))
        touches_bd = any(max(c, tj * tn) < min(c + BAND_WIDTH, (tj + 1) * tn) for c in (COL_BD_K, COL_BD_V))
        if touches_sb:
            @pl.when(j == tj)
            def _(tj=tj):
                _copy_cols(sbk_ref, acc, COL_SB_K, SB_WIDTH, tj, tn)
                _copy_cols(sbv_ref, acc, COL_SB_V, SB_WIDTH, tj, tn)
        if touches_bd:
            @pl.when(jnp.logical_and(j == tj, is_band_block))
            def _(tj=tj):
                _copy_cols(bdk_ref, acc, COL_BD_K, BAND_WIDTH, tj, tn, rows=band_rows)
                _copy_cols(bdv_ref, acc, COL_BD_V, BAND_WIDTH, tj, tn, rows=band_rows)


def _in_projection(x2d, g_pre, w_bf16, *, tm, n_band_batches, band_period, band_rows):
    m, d = x2d.shape
    n = w_bf16.shape[1]
    tn = IN_TN
    n_tiles = n // tn
    nb = band_rows[1]
    kern = functools.partial(_inproj_kernel, tn=tn, n_tiles=n_tiles, band_period=band_period,
                             band_rows=band_rows)
    return pl.pallas_call(
        kern,
        grid=(m // tm, n_tiles),
        in_specs=[
            pl.BlockSpec((tm, d), lambda i, j: (i, 0)),
            pl.BlockSpec((1, d), lambda i, j: (0, 0)),
            pl.BlockSpec((d, tn), lambda i, j: (0, j)),
        ],
        out_specs=[
            pl.BlockSpec((tm, tn), lambda i, j: (i, j)),
            pl.BlockSpec((tm, SB_WIDTH), lambda i, j: (i, 0)),
            pl.BlockSpec((tm, SB_WIDTH), lambda i, j: (i, 0)),
            pl.BlockSpec((nb, BAND_WIDTH), lambda i, j: (i // band_period, 0)),
            pl.BlockSpec((nb, BAND_WIDTH), lambda i, j: (i // band_period, 0)),
        ],
        out_shape=[
            jax.ShapeDtypeStruct((m, n), BF16),
            jax.ShapeDtypeStruct((m, SB_WIDTH), F32),
            jax.ShapeDtypeStruct((m, SB_WIDTH), F32),
            jax.ShapeDtypeStruct((n_band_batches * nb, BAND_WIDTH), F32),
            jax.ShapeDtypeStruct((n_band_batches * nb, BAND_WIDTH), F32),
        ],
        scratch_shapes=[pltpu.VMEM((tm, d), BF16)],
        compiler_params=_params(2),
        name="in_projection",
    )(x2d, g_pre.reshape(1, d), w_bf16)


def _memkv_kernel(x_ref, g_ref, w_ref, mk_ref, mv_ref):
    x = x_ref[...]
    ms = jnp.mean(x * x, axis=-1, keepdims=True)
    h = ((x * lax.rsqrt(ms + RMS_EPS)) * g_ref[...]).astype(BF16)
    acc = jnp.dot(h, w_ref[...], preferred_element_type=F32)
    mk_ref[...] = acc[:, :MEM_WIDTH]
    mv_ref[...] = acc[:, MEM_WIDTH:]


def _memory_kv(mem2d, g_mem, w_bf16, *, tm):
    m, d = mem2d.shape
    return pl.pallas_call(
        _memkv_kernel,
        grid=(m // tm,),
        in_specs=[
            pl.BlockSpec((tm, d), lambda i: (i, 0)),
            pl.BlockSpec((1, d), lambda i: (0, 0)),
            pl.BlockSpec((d, 2 * MEM_WIDTH), lambda i: (0, 0)),
        ],
        out_specs=[pl.BlockSpec((tm, MEM_WIDTH), lambda i: (i, 0)),
                   pl.BlockSpec((tm, MEM_WIDTH), lambda i: (i, 0))],
        out_shape=[jax.ShapeDtypeStruct((m, MEM_WIDTH), F32),
                   jax.ShapeDtypeStruct((m, MEM_WIDTH), F32)],
        compiler_params=_params(1),
        name="memory_kv",
    )(mem2d, g_mem.reshape(1, d), w_bf16)


def _sb_block(q, k_blk, v_blk, negu, carry, mask):
    z = _nt_dot(q, k_blk) * SCALE
    p = jnp.maximum(z, 0.0) + jnp.log(1.0 + jnp.exp(-jnp.abs(z)))
    if mask is not None:
        p = jnp.where(mask, p, 0.0)
    p_hi = p.astype(BF16)
    p_lo = (p - p_hi.astype(F32)).astype(BF16)
    suffix = (jnp.dot(p_hi, negu, preferred_element_type=F32)
              + jnp.dot(p_lo, negu, preferred_element_type=F32))
    w = jnp.exp(z + suffix + carry)
    if mask is not None:
        w = jnp.where(mask, w, 0.0)
    pv = jnp.dot(w.astype(BF16), v_blk, preferred_element_type=F32)
    return pv, carry - jnp.sum(p, axis=-1, keepdims=True)


def _strict_causal_mask(tq, tk):
    row = lax.broadcasted_iota(jnp.int32, (tq, tk), 0)
    col = lax.broadcasted_iota(jnp.int32, (tq, tk), 1)
    return col < row


def _sb_prompt_kernel(q_ref, k_ref, v_ref, negu_ref, o_ref):
    i = pl.program_id(2)
    q = q_ref[...]
    negu = negu_ref[...]
    start = pl.multiple_of(i * SB_TK, SB_TK)
    acc, carry = _sb_block(q, k_ref[pl.ds(start, SB_TK), :], v_ref[pl.ds(start, SB_TK), :], negu,
                           jnp.zeros((SB_TQ, 1), F32), _strict_causal_mask(SB_TQ, SB_TK))

    def body(jj, c):
        acc, carry = c
        st = pl.multiple_of((i - 1 - jj) * SB_TK, SB_TK)
        pv, carry = _sb_block(q, k_ref[pl.ds(st, SB_TK), :], v_ref[pl.ds(st, SB_TK), :], negu,
                              carry, None)
        return acc + pv, carry

    acc, _ = lax.fori_loop(0, i, body, (acc, carry))
    o_ref[...] = acc.astype(BF16)


def _sb_prompt(y3, negu):
    b, t, _ = y3.shape
    qb, kb, vb = COL_SB_Q // HEAD_DIM, COL_SB_K // HEAD_DIM, COL_SB_V // HEAD_DIM
    return pl.pallas_call(
        _sb_prompt_kernel,
        grid=(b, SB_HEADS, t // SB_TQ),
        in_specs=[
            pl.BlockSpec((None, SB_TQ, HEAD_DIM), lambda b, h, i: (b, i, qb + h)),
            pl.BlockSpec((None, t, HEAD_DIM), lambda b, h, i: (b, 0, kb + h)),
            pl.BlockSpec((None, t, HEAD_DIM), lambda b, h, i: (b, 0, vb + h)),
            pl.BlockSpec((SB_TK, SB_TK), lambda b, h, i: (0, 0)),
        ],
        out_specs=pl.BlockSpec((None, SB_TQ, HEAD_DIM), lambda b, h, i: (b, i, h)),
        out_shape=jax.ShapeDtypeStruct((b, t, SB_WIDTH), BF16),
        compiler_params=_params(3),
        name="sb_prompt",
    )(y3, y3, y3, negu)


def _sb_decode_kernel(q_ref, kn_ref, vn_ref, kc_ref, vc_ref, negu_ref, o_ref, kpad_ref, vpad_ref,
                      *, n_new, past):
    q = q_ref[...]
    negu = negu_ref[...]
    kpad_ref[...] = jnp.zeros_like(kpad_ref)
    vpad_ref[...] = jnp.zeros_like(vpad_ref)
    kpad_ref[0:n_new, :] = kn_ref[...]
    vpad_ref[0:n_new, :] = vn_ref[...]
    row = lax.broadcasted_iota(jnp.int32, (NEW_PAD, NEW_PAD), 0)
    col = lax.broadcasted_iota(jnp.int32, (NEW_PAD, NEW_PAD), 1)
    negu_new = jnp.where(row >= col, -1.0, 0.0).astype(BF16)
    acc, carry = _sb_block(q, kpad_ref[...], vpad_ref[...], negu_new,
                           jnp.zeros((n_new, 1), F32), _strict_causal_mask(n_new, NEW_PAD))
    n_blocks = past // SB_TK

    def body(jj, c):
        acc, carry = c
        st = pl.multiple_of((n_blocks - 1 - jj) * SB_TK, SB_TK)
        pv, carry = _sb_block(q, kc_ref[pl.ds(st, SB_TK), :].astype(BF16),
                              vc_ref[pl.ds(st, SB_TK), :].astype(BF16), negu, carry, None)
        return acc + pv, carry

    acc, _ = lax.fori_loop(0, n_blocks, body, (acc, carry))
    o_ref[...] = acc.astype(BF16)


def _sb_decode(ys3, cache_k, cache_v, negu):
    bd, n_new, _ = ys3.shape
    past = cache_k.shape[1]
    assert past % SB_TK == 0 and n_new <= NEW_PAD
    qb, kb, vb = COL_SB_Q // HEAD_DIM, COL_SB_K // HEAD_DIM, COL_SB_V // HEAD_DIM
    kern = functools.partial(_sb_decode_kernel, n_new=n_new, past=past)
    return pl.pallas_call(
        kern,
        grid=(bd, SB_HEADS),
        in_specs=[
            pl.BlockSpec((None, n_new, HEAD_DIM), lambda b, h: (b, 0, qb + h)),
            pl.BlockSpec((None, n_new, HEAD_DIM), lambda b, h: (b, 0, kb + h)),
            pl.BlockSpec((None, n_new, HEAD_DIM), lambda b, h: (b, 0, vb + h)),
            pl.BlockSpec((None, past, HEAD_DIM), lambda b, h: (b, 0, h)),
            pl.BlockSpec((None, past, HEAD_DIM), lambda b, h: (b, 0, h)),
            pl.BlockSpec((SB_TK, SB_TK), lambda b, h: (0, 0)),
        ],
        out_specs=pl.BlockSpec((None, n_new, HEAD_DIM), lambda b, h: (b, 0, h)),
        out_shape=jax.ShapeDtypeStruct((bd, n_new, SB_WIDTH), BF16),
        scratch_shapes=[pltpu.VMEM((NEW_PAD, HEAD_DIM), BF16), pltpu.VMEM((NEW_PAD, HEAD_DIM), BF16)],
        compiler_params=_params(2),
        name="sb_decode",
    )(ys3, ys3, ys3, cache_k, cache_v, negu)


def _softmax_pv(parts):
    mx = functools.reduce(jnp.maximum, [jnp.max(s, axis=-1, keepdims=True) for s, _ in parts])
    num = None
    den = None
    for s, v in parts:
        p = jnp.exp(s - mx)
        d = jnp.sum(p, axis=-1, keepdims=True)
        o = jnp.dot(p.astype(BF16), v, preferred_element_type=F32)
        num = o if num is None else num + o
        den = d if den is None else den + d
    return num / den


def _band_prompt_kernel(q_ref, k_ref, v_ref, bias_ref, o_ref, kpad_ref, vpad_ref, *, t):
    g = pl.program_id(2)

    @pl.when(g == 0)
    def _():
        kpad_ref[0:BAND_ROWS, :] = jnp.zeros((BAND_ROWS, HEAD_DIM), BF16)
        vpad_ref[0:BAND_ROWS, :] = jnp.zeros((BAND_ROWS, HEAD_DIM), BF16)
        kpad_ref[BAND_ROWS:BAND_ROWS + t, :] = k_ref[...]
        vpad_ref[BAND_ROWS:BAND_ROWS + t, :] = v_ref[...]

    start = pl.multiple_of(g * BAND_TQ, BAND_TQ)
    kwin = kpad_ref[pl.ds(start, BAND_WIN), :]
    vwin = vpad_ref[pl.ds(start, BAND_WIN), :]
    s = _nt_dot(q_ref[...], kwin) * SCALE + bias_ref[...]
    col = lax.broadcasted_iota(jnp.int32, (BAND_TQ, BAND_WIN), 1)
    s = jnp.where(col + g * BAND_TQ >= BAND_ROWS, s, NEG_INF)
    o_ref[...] = _softmax_pv([(s, vwin)]).astype(BF16)


def _band_prompt(y3, bias_tbl):
    b, t, _ = y3.shape
    qb, kb, vb = COL_BD_Q // HEAD_DIM, COL_BD_K // HEAD_DIM, COL_BD_V // HEAD_DIM
    kern = functools.partial(_band_prompt_kernel, t=t)
    return pl.pallas_call(
        kern,
        grid=(b, BAND_HEADS, t // BAND_TQ),
        in_specs=[
            pl.BlockSpec((None, BAND_TQ, HEAD_DIM), lambda b, h, g: (b, g, qb + h)),
            pl.BlockSpec((None, t, HEAD_DIM), lambda b, h, g: (b, 0, kb + h)),
            pl.BlockSpec((None, t, HEAD_DIM), lambda b, h, g: (b, 0, vb + h)),
            pl.BlockSpec((None, BAND_TQ, BAND_WIN), lambda b, h, g: (h, 0, 0)),
        ],
        out_specs=pl.BlockSpec((None, BAND_TQ, HEAD_DIM), lambda b, h, g: (b, g, h)),
        out_shape=jax.ShapeDtypeStruct((b, t, BAND_WIDTH), BF16),
        scratch_shapes=[pltpu.VMEM((BAND_ROWS + t, HEAD_DIM), BF16),
                        pltpu.VMEM((BAND_ROWS + t, HEAD_DIM), BF16)],
        compiler_params=_params(3),
        name="band_prompt",
    )(y3, y3, y3, bias_tbl)


def _band_decode_kernel(q_ref, kn_ref, vn_ref, kc_ref, vc_ref, bias_ref, o_ref, kpad_ref, vpad_ref,
                        *, n_new, r_band):
    q = q_ref[...]
    kpad_ref[...] = jnp.zeros_like(kpad_ref)
    vpad_ref[...] = jnp.zeros_like(vpad_ref)
    kpad_ref[0:n_new, :] = kn_ref[...]
    vpad_ref[0:n_new, :] = vn_ref[...]
    s_cache = _nt_dot(q, kc_ref[...].astype(BF16)) * SCALE + bias_ref[:, 0:r_band]
    s_new = _nt_dot(q, kpad_ref[...]) * SCALE + bias_ref[:, r_band:r_band + NEW_PAD]
    o_ref[...] = _softmax_pv([(s_cache, vc_ref[...].astype(BF16)), (s_new, vpad_ref[...])]).astype(BF16)


def _band_decode(ys3, cache_k, cache_v, bias_tbl):
    bd, n_new, _ = ys3.shape
    r_band = cache_k.shape[1]
    qb, kb, vb = COL_BD_Q // HEAD_DIM, COL_BD_K // HEAD_DIM, COL_BD_V // HEAD_DIM
    kern = functools.partial(_band_decode_kernel, n_new=n_new, r_band=r_band)
    return pl.pallas_call(
        kern,
        grid=(bd, BAND_HEADS),
        in_specs=[
            pl.BlockSpec((None, n_new, HEAD_DIM), lambda b, h: (b, 0, qb + h)),
            pl.BlockSpec((None, n_new, HEAD_DIM), lambda b, h: (b, 0, kb + h)),
            pl.BlockSpec((None, n_new, HEAD_DIM), lambda b, h: (b, 0, vb + h)),
            pl.BlockSpec((None, r_band, HEAD_DIM), lambda b, h: (b, 0, h)),
            pl.BlockSpec((None, r_band, HEAD_DIM), lambda b, h: (b, 0, h)),
            pl.BlockSpec((None, n_new, r_band + NEW_PAD), lambda b, h: (h, 0, 0)),
        ],
        out_specs=pl.BlockSpec((None, n_new, HEAD_DIM), lambda b, h: (b, 0, h)),
        out_shape=jax.ShapeDtypeStruct((bd, n_new, BAND_WIDTH), BF16),
        scratch_shapes=[pltpu.VMEM((NEW_PAD, HEAD_DIM), BF16), pltpu.VMEM((NEW_PAD, HEAD_DIM), BF16)],
        compiler_params=_params(2),
        name="band_decode",
    )(ys3, ys3, ys3, cache_k, cache_v, bias_tbl)


def _band_bias_tables(rel_bias, n_new, r_band):
    rb = rel_bias.astype(F32)
    r = jnp.arange(BAND_TQ)[:, None]
    j = jnp.arange(BAND_WIN)[None, :]
    idx = jnp.clip(r - j + BAND_ROWS, -MAX_REL, MAX_REL) + MAX_REL
    dc = j // CHUNK - r // CHUNK
    visible = jnp.logical_and(dc >= 0, dc <= BAND_LEFT_CHUNKS)
    tbl_p = jnp.where(visible[None], rb[:, idx], NEG_INF)
    i = jnp.arange(n_new)[:, None]
    jd = jnp.arange(r_band + NEW_PAD)[None, :]
    idx_d = jnp.clip(i - jd + r_band, -MAX_REL, MAX_REL) + MAX_REL
    tbl_d = jnp.where((jd < r_band + n_new)[None], rb[:, idx_d], NEG_INF)
    return tbl_p, tbl_d


def _mem_attn_kernel(q_ref, mk_ref, mv_ref, o_ref):
    for h in range(MEM_HEADS):
        sl = slice(h * HEAD_DIM, (h + 1) * HEAD_DIM)
        s = _nt_dot(q_ref[:, sl], mk_ref[:, sl].astype(BF16)) * SCALE
        o_ref[:, sl] = _softmax_pv([(s, mv_ref[:, sl].astype(BF16))]).astype(BF16)


def _mem_attention(y3, mk, mv, *, tq):
    b, t, _ = y3.shape
    n_mem = mk.shape[1]
    qb = COL_MM_Q // MEM_WIDTH
    return pl.pallas_call(
        _mem_attn_kernel,
        grid=(b, t // tq),
        in_specs=[
            pl.BlockSpec((None, tq, MEM_WIDTH), lambda b, i: (b, i, qb)),
            pl.BlockSpec((None, n_mem, MEM_WIDTH), lambda b, i: (b, 0, 0)),
            pl.BlockSpec((None, n_mem, MEM_WIDTH), lambda b, i: (b, 0, 0)),
        ],
        out_specs=pl.BlockSpec((None, tq, MEM_WIDTH), lambda b, i: (b, i, 0)),
        out_shape=jax.ShapeDtypeStruct((b, t, MEM_WIDTH), BF16),
        compiler_params=_params(2),
        name="mem_attention",
    )(y3, mk, mv)


def _silu(g):
    return g / (1.0 + jnp.exp(-g))


def _sigmoid(g):
    return 1.0 / (1.0 + jnp.exp(-g))


def _out_kernel(x_ref, osb_ref, obd_ref, omm_ref, gsb_ref, gbd_ref, gmm_ref,
                mg0_ref, mg1_ref, mg2_ref, mg3_ref, mg4_ref, mg5_ref,
                wsb_ref, wbd_ref, wmm_ref, wout_ref, gpost_ref, y_ref, merged_ref, *, half):
    u_sb = (osb_ref[...].astype(F32) * _silu(gsb_ref[...].astype(F32))).astype(BF16)
    u_bd = (obd_ref[...].astype(F32) * _silu(gbd_ref[...].astype(F32))).astype(BF16)
    u_mm = (omm_ref[...].astype(F32) * _silu(gmm_ref[...].astype(F32))).astype(BF16)
    mg = ((mg0_ref, mg2_ref, mg4_ref), (mg1_ref, mg3_ref, mg5_ref))
    for n in range(2):
        cols = slice(n * half, (n + 1) * half)
        m_sb, m_bd, m_mm = mg[n]
        merged = (_sigmoid(m_sb[...].astype(F32)) * jnp.dot(u_sb, wsb_ref[:, cols], preferred_element_type=F32)
                  + _sigmoid(m_bd[...].astype(F32)) * jnp.dot(u_bd, wbd_ref[:, cols], preferred_element_type=F32)
                  + _sigmoid(m_mm[...].astype(F32)) * jnp.dot(u_mm, wmm_ref[:, cols], preferred_element_type=F32))
        merged_ref[:, cols] = merged.astype(BF16)
    y = jnp.dot(merged_ref[...], wout_ref[...], preferred_element_type=F32)
    ms = jnp.mean(y * y, axis=-1, keepdims=True)
    y_ref[...] = x_ref[...] + (y * lax.rsqrt(ms + RMS_EPS)) * gpost_ref[...]


def _output_stage(x2d, y2d, o_sb, o_bd, o_mm, w_sb, w_bd, w_mm, w_out, g_post, *, tm):
    m, d = x2d.shape
    half = d // 2
    assert COL_MG % half == 0
    mgb = COL_MG // half
    const = dict(pipeline_mode=pl.Buffered(1))
    kern = functools.partial(_out_kernel, half=half)
    return pl.pallas_call(
        kern,
        grid=(m // tm,),
        in_specs=[
            pl.BlockSpec((tm, d), lambda i: (i, 0)),
            pl.BlockSpec((tm, SB_WIDTH), lambda i: (i, 0)),
            pl.BlockSpec((tm, BAND_WIDTH), lambda i: (i, 0)),
            pl.BlockSpec((tm, MEM_WIDTH), lambda i: (i, 0)),
            pl.BlockSpec((tm, SB_WIDTH), lambda i: (i, COL_SB_G // SB_WIDTH)),
            pl.BlockSpec((tm, BAND_WIDTH), lambda i: (i, COL_BD_G // BAND_WIDTH)),
            pl.BlockSpec((tm, MEM_WIDTH), lambda i: (i, COL_MM_G // MEM_WIDTH)),
        ] + [pl.BlockSpec((tm, half), functools.partial(lambda i, c: (i, c), c=mgb + c)) for c in range(6)] + [
            pl.BlockSpec((SB_WIDTH, d), lambda i: (0, 0), **const),
            pl.BlockSpec((BAND_WIDTH, d), lambda i: (0, 0), **const),
            pl.BlockSpec((MEM_WIDTH, d), lambda i: (0, 0), **const),
            pl.BlockSpec((d, d), lambda i: (0, 0), **const),
            pl.BlockSpec((1, d), lambda i: (0, 0)),
        ],
        out_specs=pl.BlockSpec((tm, d), lambda i: (i, 0)),
        out_shape=jax.ShapeDtypeStruct((m, d), F32),
        scratch_shapes=[pltpu.VMEM((tm, d), BF16)],
        compiler_params=_params(1),
        name="output_stage",
    )(x2d, o_sb, o_bd, o_mm, y2d, y2d, y2d, *([y2d] * 6), w_sb, w_bd, w_mm, w_out, g_post.reshape(1, d))


def kernel(x_prompt, x_sample, cache_sb_k, cache_sb_v, cache_band_k, cache_band_v, cache_mem_k, cache_mem_v, mem_prompt, g_pre, w_in, rel_bias, g_mem, w_mem_kv, w_up_sb, w_up_band, w_up_mem, w_out, g_post):
    depth = w_in.shape[0]
    b, t, d = x_prompt.shape
    bd, n_new, _ = x_sample.shape
    n_mem = mem_prompt.shape[1]
    past = cache_sb_k.shape[2]
    r_band = cache_band_k.shape[2]
    band_keep = min(BAND_ROWS, t)
    assert COL_MG + 3 * d == w_in.shape[2]
    assert t % SB_TQ == 0 and t % BAND_TQ == 0 and r_band == BAND_ROWS and n_new <= CHUNK

    jj = jnp.arange(SB_TK)
    negu = jnp.where(jj[:, None] >= jj[None, :], -1.0, 0.0).astype(BF16)

    tm_p = 512
    xp = x_prompt.reshape(b * t, d)
    xs = x_sample.reshape(bd * n_new, d)
    outs = [[] for _ in range(10)]
    for l in range(depth):
        w_in_b = w_in[l].astype(BF16)
        w_kv_b = w_mem_kv[l].astype(BF16)
        w_sb_b = w_up_sb[l].astype(BF16)
        w_bd_b = w_up_band[l].astype(BF16)
        w_mm_b = w_up_mem[l].astype(BF16)
        w_out_b = w_out[l].astype(BF16)
        bias_p, bias_d = _band_bias_tables(rel_bias[l], n_new, r_band)

        rows_per_batch = t // tm_p
        y, sbk, sbv, bdk, bdv = _in_projection(
            xp, g_pre[l], w_in_b, tm=tm_p, n_band_batches=b, band_period=rows_per_batch,
            band_rows=(tm_p - band_keep, band_keep))
        mk, mv = _memory_kv(mem_prompt.reshape(b * n_mem, d), g_mem[l], w_kv_b, tm=n_mem)
        mk3 = mk.reshape(b, n_mem, MEM_WIDTH)
        mv3 = mv.reshape(b, n_mem, MEM_WIDTH)
        y3 = y.reshape(b, t, -1)
        o_sb = _sb_prompt(y3, negu)
        o_bd = _band_prompt(y3, bias_p)
        o_mm = _mem_attention(y3, mk3, mv3, tq=512)
        xp = _output_stage(xp, y, o_sb.reshape(b * t, -1), o_bd.reshape(b * t, -1),
                           o_mm.reshape(b * t, -1), w_sb_b, w_bd_b, w_mm_b, w_out_b, g_post[l], tm=256)
        outs[0].append(sbk.reshape(b, t, SB_HEADS, HEAD_DIM))
        outs[1].append(sbv.reshape(b, t, SB_HEADS, HEAD_DIM))
        outs[2].append(bdk.reshape(b, band_keep, BAND_HEADS, HEAD_DIM))
        outs[3].append(bdv.reshape(b, band_keep, BAND_HEADS, HEAD_DIM))
        outs[4].append(mk.reshape(b, n_mem, MEM_HEADS, HEAD_DIM))
        outs[5].append(mv.reshape(b, n_mem, MEM_HEADS, HEAD_DIM))

        ms = bd * n_new
        ys, sbk2, sbv2, bdk2, bdv2 = _in_projection(
            xs, g_pre[l], w_in_b, tm=ms, n_band_batches=1, band_period=1, band_rows=(0, ms))
        ys3 = ys.reshape(bd, n_new, -1)
        o_sb2 = _sb_decode(ys3, cache_sb_k[l].reshape(bd, past, SB_WIDTH),
                           cache_sb_v[l].reshape(bd, past, SB_WIDTH), negu)
        o_bd2 = _band_decode(ys3, cache_band_k[l].reshape(bd, r_band, BAND_WIDTH),
                             cache_band_v[l].reshape(bd, r_band, BAND_WIDTH), bias_d)
        o_mm2 = _mem_attention(ys3, cache_mem_k[l].reshape(bd, n_mem, MEM_WIDTH),
                               cache_mem_v[l].reshape(bd, n_mem, MEM_WIDTH), tq=n_new)
        xs = _output_stage(xs, ys, o_sb2.reshape(ms, -1), o_bd2.reshape(ms, -1), o_mm2.reshape(ms, -1),
                           w_sb_b, w_bd_b, w_mm_b, w_out_b, g_post[l], tm=ms)
        outs[6].append(sbk2.reshape(bd, n_new, SB_HEADS, HEAD_DIM))
        outs[7].append(sbv2.reshape(bd, n_new, SB_HEADS, HEAD_DIM))
        outs[8].append(bdk2.reshape(bd, n_new, BAND_HEADS, HEAD_DIM))
        outs[9].append(bdv2.reshape(bd, n_new, BAND_HEADS, HEAD_DIM))

    return (xp.reshape(b, t, d), xs.reshape(bd, n_new, d)) + tuple(jnp.stack(o) for o in outs)
```

```python
import functools
import math

import jax
import jax.numpy as jnp
from jax import lax
from jax.experimental import pallas as pl
from jax.experimental.pallas import tpu as pltpu

F32 = jnp.float32
BF16 = jnp.bfloat16

HEAD_DIM = 128
SB_HEADS = 6
BAND_HEADS = 6
MEM_HEADS = 4
SB_WIDTH = SB_HEADS * HEAD_DIM
BAND_WIDTH = BAND_HEADS * HEAD_DIM
MEM_WIDTH = MEM_HEADS * HEAD_DIM
CHUNK = 64
CHUNK_SHIFT = 6
BAND_LEFT_CHUNKS = 8
BAND_ROWS = BAND_LEFT_CHUNKS * CHUNK
MAX_REL = 256
RMS_EPS = 1e-6
NEG_INF = -1e30
LOG2E = math.log2(math.e)
Q_SCALE = HEAD_DIM ** -0.5 * LOG2E

COL_SB_Q = 0
COL_SB_K = COL_SB_Q + SB_WIDTH
COL_SB_V = COL_SB_K + SB_WIDTH
COL_SB_G = COL_SB_V + SB_WIDTH
COL_BD_Q = COL_SB_G + SB_WIDTH
COL_BD_K = COL_BD_Q + BAND_WIDTH
COL_BD_V = COL_BD_K + BAND_WIDTH
COL_BD_G = COL_BD_V + BAND_WIDTH
COL_MM_Q = COL_BD_G + BAND_WIDTH
COL_MM_G = COL_MM_Q + MEM_WIDTH
COL_MG = COL_MM_G + MEM_WIDTH

VMEM_LIMIT_BYTES = 56 * 1024 * 1024
IN_TN = 1024
SB_TK = 256
SB_TQ = 4 * SB_TK
SB_DEC_CHUNK = 4 * SB_TK
BAND_TQ = 4 * CHUNK
BAND_WIN = BAND_TQ + BAND_ROWS
BAND_STEP_GROUPS = 4
BIAS_LANES = 1024
NEW_PAD = 128


def _params(n_axes, vmem=VMEM_LIMIT_BYTES):
    return pltpu.CompilerParams(dimension_semantics=("arbitrary",) * n_axes,
                                vmem_limit_bytes=vmem)


def _nt_dot(a, b):
    return lax.dot_general(a, b, (((1,), (1,)), ((), ())), preferred_element_type=F32)


def _dot(a, b):
    return jnp.dot(a, b, preferred_element_type=F32)


def _inproj_kernel(x_ref, g_ref, w_ref, cs_ref, y_ref, sbk_ref, sbv_ref, bdk_ref, bdv_ref, h_ref,
                   *, tn, n_tiles, band_period, seqs, rows):
    i = pl.program_id(0)
    j = pl.program_id(1)

    @pl.when(j == 0)
    def _():
        x = x_ref[...]
        ms = jnp.mean(x * x, axis=-1, keepdims=True)
        h_ref[...] = ((x * lax.rsqrt(ms + RMS_EPS)) * g_ref[...]).astype(BF16)

    acc = _dot(h_ref[...], w_ref[...])
    y_ref[...] = (acc * cs_ref[...]).astype(BF16)

    def store_heads(dst_ref, col0, tj):
        for h in range(SB_HEADS):
            c = col0 + h * HEAD_DIM
            if c // tn == tj:
                lc = c - tj * tn
                for s in range(seqs):
                    dst_ref[s, h] = acc[s * rows:(s + 1) * rows, lc:lc + HEAD_DIM]

    def tile_has(col0, tj):
        return any((col0 + h * HEAD_DIM) // tn == tj for h in range(SB_HEADS))

    is_band_block = (i % band_period) == (band_period - 1)
    for tj in range(n_tiles):
        if tile_has(COL_SB_K, tj) or tile_has(COL_SB_V, tj):
            @pl.when(j == tj)
            def _(tj=tj):
                store_heads(sbk_ref, COL_SB_K, tj)
                store_heads(sbv_ref, COL_SB_V, tj)
        if tile_has(COL_BD_K, tj) or tile_has(COL_BD_V, tj):
            @pl.when(jnp.logical_and(j == tj, is_band_block))
            def _(tj=tj):
                store_heads(bdk_ref, COL_BD_K, tj)
                store_heads(bdv_ref, COL_BD_V, tj)


def _in_projection(x2d, g_pre, w_bf16, col_scale, *, tm, seqs, n_seq, band_period):
    m, d = x2d.shape
    n = w_bf16.shape[1]
    tn = IN_TN
    n_tiles = n // tn
    rows = tm // seqs
    n_row_blocks = m // tm
    seq_rows = (m // n_seq)
    kern = functools.partial(_inproj_kernel, tn=tn, n_tiles=n_tiles, band_period=band_period,
                             seqs=seqs, rows=rows)
    sb_spec = pl.BlockSpec((seqs, SB_HEADS, rows, HEAD_DIM),
                           lambda i, j: (i // band_period, 0, i % band_period, 0))
    bd_spec = pl.BlockSpec((seqs, BAND_HEADS, rows, HEAD_DIM), lambda i, j: (i // band_period, 0, 0, 0))
    return pl.pallas_call(
        kern,
        grid=(n_row_blocks, n_tiles),
        in_specs=[
            pl.BlockSpec((tm, d), lambda i, j: (i, 0)),
            pl.BlockSpec((1, d), lambda i, j: (0, 0)),
            pl.BlockSpec((d, tn), lambda i, j: (0, j)),
            pl.BlockSpec((1, tn), lambda i, j: (0, j)),
        ],
        out_specs=[pl.BlockSpec((tm, tn), lambda i, j: (i, j)), sb_spec, sb_spec, bd_spec, bd_spec],
        out_shape=[
            jax.ShapeDtypeStruct((m, n), BF16),
            jax.ShapeDtypeStruct((n_seq, SB_HEADS, seq_rows, HEAD_DIM), F32),
            jax.ShapeDtypeStruct((n_seq, SB_HEADS, seq_rows, HEAD_DIM), F32),
            jax.ShapeDtypeStruct((n_seq, BAND_HEADS, rows, HEAD_DIM), F32),
            jax.ShapeDtypeStruct((n_seq, BAND_HEADS, rows, HEAD_DIM), F32),
        ],
        scratch_shapes=[pltpu.VMEM((tm, d), BF16)],
        compiler_params=_params(2),
        name="in_projection",
    )(x2d, g_pre.reshape(1, d), w_bf16, col_scale)


def _memkv_kernel(x_ref, g_ref, w_ref, mk_ref, mv_ref):
    x = x_ref[...]
    ms = jnp.mean(x * x, axis=-1, keepdims=True)
    h = ((x * lax.rsqrt(ms + RMS_EPS)) * g_ref[...]).astype(BF16)
    acc = _dot(h, w_ref[...])
    mk_ref[...] = acc[:, :MEM_WIDTH]
    mv_ref[...] = acc[:, MEM_WIDTH:]


def _memory_kv(mem2d, g_mem, w_bf16, *, tm):
    m, d = mem2d.shape
    return pl.pallas_call(
        _memkv_kernel,
        grid=(m // tm,),
        in_specs=[
            pl.BlockSpec((tm, d), lambda i: (i, 0)),
            pl.BlockSpec((1, d), lambda i: (0, 0)),
            pl.BlockSpec((d, 2 * MEM_WIDTH), lambda i: (0, 0)),
        ],
        out_specs=[pl.BlockSpec((tm, MEM_WIDTH), lambda i: (i, 0)),
                   pl.BlockSpec((tm, MEM_WIDTH), lambda i: (i, 0))],
        out_shape=[jax.ShapeDtypeStruct((m, MEM_WIDTH), F32),
                   jax.ShapeDtypeStruct((m, MEM_WIDTH), F32)],
        compiler_params=_params(1),
        name="memory_kv",
    )(mem2d, g_mem.reshape(1, d), w_bf16)


def _neg_suffix_matrix(n):
    row = lax.broadcasted_iota(jnp.int32, (2 * n, n), 0)
    col = lax.broadcasted_iota(jnp.int32, (2 * n, n), 1)
    row = jnp.where(row >= n, row - n, row)
    return jnp.where(row >= col, -1.0, 0.0).astype(BF16)


def _sb_weights(z2, carry2, negu2, mask):
    p = jnp.maximum(z2, 0.0) + jnp.log(1.0 + jnp.exp2(-jnp.abs(z2))) * LOG2E
    if mask is not None:
        p = jnp.where(mask, p, 0.0)
    p_hi = p.astype(BF16)
    p_lo = (p - p_hi.astype(F32)).astype(BF16)
    suffix = _dot(jnp.concatenate([p_hi, p_lo], axis=1), negu2)
    w = jnp.exp2(z2 + suffix + carry2)
    if mask is not None:
        w = jnp.where(mask, w, 0.0)
    return w, carry2 - jnp.sum(p, axis=-1, keepdims=True)


def _sb_prompt_kernel(q_ref, k_ref, v_ref, negu2_ref, o_ref, acc_ref, carry_ref):
    i = pl.program_id(2)
    n_sub = SB_TQ // SB_TK
    negu2 = negu2_ref[...]
    acc_ref[...] = jnp.zeros_like(acc_ref)
    carry_ref[...] = jnp.zeros_like(carry_ref)

    for jj in reversed(range(n_sub)):
        r0 = jj * SB_TK
        nr = SB_TQ - r0
        start = pl.multiple_of((i * n_sub + jj) * SB_TK, SB_TK)
        row = lax.broadcasted_iota(jnp.int32, (nr, SB_TK), 0)
        col = lax.broadcasted_iota(jnp.int32, (nr, SB_TK), 1)
        z2 = _nt_dot(q_ref[r0:, :], k_ref[pl.ds(start, SB_TK), :])
        w, carry = _sb_weights(z2, carry_ref[r0:, :], negu2, col < row)
        acc_ref[r0:, :] += _dot(w.astype(BF16), v_ref[pl.ds(start, SB_TK), :])
        carry_ref[r0:, :] = carry

    n_full = i * n_sub

    def body(jj, _):
        st = pl.multiple_of((n_full - 1 - jj) * SB_TK, SB_TK)
        z2 = _nt_dot(q_ref[...], k_ref[pl.ds(st, SB_TK), :])
        w, carry = _sb_weights(z2, carry_ref[...], negu2, None)
        acc_ref[...] += _dot(w.astype(BF16), v_ref[pl.ds(st, SB_TK), :])
        carry_ref[...] = carry
        return 0

    lax.fori_loop(0, n_full, body, 0)
    o_ref[...] = acc_ref[...].astype(BF16)


def _sb_prompt(y3, negu2):
    b, t, _ = y3.shape
    qb, kb, vb = COL_SB_Q // HEAD_DIM, COL_SB_K // HEAD_DIM, COL_SB_V // HEAD_DIM
    return pl.pallas_call(
        _sb_prompt_kernel,
        grid=(b, SB_HEADS, t // SB_TQ),
        in_specs=[
            pl.BlockSpec((None, SB_TQ, HEAD_DIM), lambda b, h, i: (b, i, qb + h)),
            pl.BlockSpec((None, t, HEAD_DIM), lambda b, h, i: (b, 0, kb + h)),
            pl.BlockSpec((None, t, HEAD_DIM), lambda b, h, i: (b, 0, vb + h)),
            pl.BlockSpec((2 * SB_TK, SB_TK), lambda b, h, i: (0, 0)),
        ],
        out_specs=pl.BlockSpec((None, SB_TQ, HEAD_DIM), lambda b, h, i: (b, i, h)),
        out_shape=jax.ShapeDtypeStruct((b, t, SB_WIDTH), BF16),
        scratch_shapes=[pltpu.VMEM((SB_TQ, HEAD_DIM), F32), pltpu.VMEM((SB_TQ, 1), F32)],
        compiler_params=_params(3),
        name="sb_prompt",
    )(y3, y3, y3, negu2)


def _sb_decode_kernel(q_ref, kn_ref, vn_ref, kc_ref, vc_ref, negu2_ref, o_ref,
                      acc_ref, carry_ref, kpad_ref, vpad_ref, *, n_new, n_chunks):
    c = pl.program_id(1)
    heads = SB_HEADS

    def head_cols(h):
        return slice(h * HEAD_DIM, (h + 1) * HEAD_DIM)

    def head_rows(h):
        return slice(h * n_new, (h + 1) * n_new)

    def block(k_of, v_of, negu2, mask):
        z2 = jnp.concatenate([_nt_dot(q_ref[:, head_cols(h)], k_of(h)) for h in range(heads)], axis=0)
        w, carry = _sb_weights(z2, carry_ref[...], negu2, mask)
        wb = w.astype(BF16)
        for h in range(heads):
            acc_ref[head_rows(h), :] += _dot(wb[head_rows(h), :], v_of(h))
        carry_ref[...] = carry

    @pl.when(c == 0)
    def _():
        acc_ref[...] = jnp.zeros_like(acc_ref)
        carry_ref[...] = jnp.zeros_like(carry_ref)
        kpad_ref[...] = jnp.zeros_like(kpad_ref)
        vpad_ref[...] = jnp.zeros_like(vpad_ref)
        for h in range(heads):
            kpad_ref[h, 0:n_new, :] = kn_ref[:, head_cols(h)]
            vpad_ref[h, 0:n_new, :] = vn_ref[:, head_cols(h)]
        row = lax.broadcasted_iota(jnp.int32, (n_new, NEW_PAD), 0)
        col = lax.broadcasted_iota(jnp.int32, (n_new, NEW_PAD), 1)
        mask = jnp.concatenate([(col < row).astype(jnp.int32)] * heads, axis=0) == 1
        block(lambda h: kpad_ref[h], lambda h: vpad_ref[h], _neg_suffix_matrix(NEW_PAD), mask)

    negu2 = negu2_ref[...]
    for kb in reversed(range(SB_DEC_CHUNK // SB_TK)):
        rows = slice(kb * SB_TK, (kb + 1) * SB_TK)
        block(lambda h: kc_ref[h, rows, :].astype(BF16), lambda h: vc_ref[h, rows, :].astype(BF16),
              negu2, None)

    @pl.when(c == n_chunks - 1)
    def _():
        for h in range(heads):
            o_ref[:, head_cols(h)] = acc_ref[head_rows(h), :].astype(BF16)


def _sb_decode(ys3, cache_k, cache_v, negu2):
    bd, n_new, _ = ys3.shape
    past = cache_k.shape[2]
    assert past % SB_DEC_CHUNK == 0 and n_new <= NEW_PAD and n_new % 16 == 0
    n_chunks = past // SB_DEC_CHUNK
    kern = functools.partial(_sb_decode_kernel, n_new=n_new, n_chunks=n_chunks)
    cache_spec = pl.BlockSpec((None, SB_HEADS, SB_DEC_CHUNK, HEAD_DIM),
                              lambda b, c: (b, 0, n_chunks - 1 - c, 0))
    return pl.pallas_call(
        kern,
        grid=(bd, n_chunks),
        in_specs=[
            pl.BlockSpec((None, n_new, SB_WIDTH), lambda b, c: (b, 0, COL_SB_Q // SB_WIDTH)),
            pl.BlockSpec((None, n_new, SB_WIDTH), lambda b, c: (b, 0, COL_SB_K // SB_WIDTH)),
            pl.BlockSpec((None, n_new, SB_WIDTH), lambda b, c: (b, 0, COL_SB_V // SB_WIDTH)),
            cache_spec,
            cache_spec,
            pl.BlockSpec((2 * SB_TK, SB_TK), lambda b, c: (0, 0)),
        ],
        out_specs=pl.BlockSpec((None, n_new, SB_WIDTH), lambda b, c: (b, 0, 0)),
        out_shape=jax.ShapeDtypeStruct((bd, n_new, SB_WIDTH), BF16),
        scratch_shapes=[pltpu.VMEM((SB_HEADS * n_new, HEAD_DIM), F32),
                        pltpu.VMEM((SB_HEADS * n_new, 1), F32),
                        pltpu.VMEM((SB_HEADS, NEW_PAD, HEAD_DIM), BF16),
                        pltpu.VMEM((SB_HEADS, NEW_PAD, HEAD_DIM), BF16)],
        compiler_params=_params(2),
        name="sb_decode",
    )(ys3, ys3, ys3, cache_k, cache_v, negu2)


def _softmax2_pv(parts):
    mx = functools.reduce(jnp.maximum, [jnp.max(s, axis=-1, keepdims=True) for s, _ in parts])
    num = None
    den = None
    for s, v in parts:
        p = jnp.exp2(s - mx)
        d = jnp.sum(p, axis=-1, keepdims=True)
        o = _dot(p.astype(BF16), v)
        num = o if num is None else num + o
        den = d if den is None else den + d
    return num / den


def _band_bias_kernel(g_ref, tp_ref, td_ref, *, n_new, r_band):
    x = jnp.broadcast_to(g_ref[...], (BAND_TQ, BIAS_LANES))
    row = lax.broadcasted_iota(jnp.int32, (BAND_TQ, BIAS_LANES), 0)
    x = pltpu.roll(x, BAND_TQ, 1)
    for bit in range(BAND_TQ.bit_length() - 1):
        x = jnp.where(((row >> bit) & 1) == 1, pltpu.roll(x, 1 << bit, 1), x)
    tbl = x[:, :BAND_WIN] * LOG2E
    r = lax.broadcasted_iota(jnp.int32, (BAND_TQ, BAND_WIN), 0)
    j = lax.broadcasted_iota(jnp.int32, (BAND_TQ, BAND_WIN), 1)
    dc = (j >> CHUNK_SHIFT) - (r >> CHUNK_SHIFT)
    tp_ref[...] = jnp.where(jnp.logical_and(dc >= 0, dc <= BAND_LEFT_CHUNKS), tbl, NEG_INF)
    jd = lax.broadcasted_iota(jnp.int32, (n_new, r_band + NEW_PAD), 1)
    td_ref[...] = jnp.where(jd < r_band + n_new, tbl[:n_new, :r_band + NEW_PAD], NEG_INF)


def _band_bias_tables(rel_bias, n_new, r_band):
    h = rel_bias.shape[0]
    assert BAND_ROWS == 2 * MAX_REL and r_band == BAND_ROWS and BIAS_LANES == 2 * BAND_ROWS
    rb = rel_bias.astype(F32)
    g = jnp.concatenate([rb[:, :0:-1], jnp.broadcast_to(rb[:, -1:], (h, BIAS_LANES - 2 * MAX_REL))], axis=1)
    kern = functools.partial(_band_bias_kernel, n_new=n_new, r_band=r_band)
    return pl.pallas_call(
        kern,
        grid=(h,),
        in_specs=[pl.BlockSpec((None, 1, BIAS_LANES), lambda i: (i, 0, 0))],
        out_specs=[pl.BlockSpec((None, BAND_TQ, BAND_WIN), lambda i: (i, 0, 0)),
                   pl.BlockSpec((None, n_new, r_band + NEW_PAD), lambda i: (i, 0, 0))],
        out_shape=[jax.ShapeDtypeStruct((h, BAND_TQ, BAND_WIN), F32),
                   jax.ShapeDtypeStruct((h, n_new, r_band + NEW_PAD), F32)],
        compiler_params=_params(1),
        name="band_bias",
    )(g.reshape(h, 1, BIAS_LANES))


def _band_prompt_kernel(q_ref, k_ref, v_ref, bias_ref, o_ref, kpad_ref, vpad_ref, *, t):
    s_idx = pl.program_id(2)

    @pl.when(s_idx == 0)
    def _():
        kpad_ref[0:BAND_ROWS, :] = jnp.zeros((BAND_ROWS, HEAD_DIM), BF16)
        vpad_ref[0:BAND_ROWS, :] = jnp.zeros((BAND_ROWS, HEAD_DIM), BF16)
        kpad_ref[BAND_ROWS:BAND_ROWS + t, :] = k_ref[...]
        vpad_ref[BAND_ROWS:BAND_ROWS + t, :] = v_ref[...]

    col = lax.broadcasted_iota(jnp.int32, (BAND_TQ, BAND_WIN), 1)
    for gg in range(BAND_STEP_GROUPS):
        g = s_idx * BAND_STEP_GROUPS + gg
        start = pl.multiple_of(g * BAND_TQ, BAND_TQ)
        rows = slice(gg * BAND_TQ, (gg + 1) * BAND_TQ)
        s = _nt_dot(q_ref[rows, :], kpad_ref[pl.ds(start, BAND_WIN), :]) + bias_ref[...]
        s = jnp.where(col + g * BAND_TQ >= BAND_ROWS, s, NEG_INF)
        o_ref[rows, :] = _softmax2_pv([(s, vpad_ref[pl.ds(start, BAND_WIN), :])]).astype(BF16)


def _band_prompt(y3, bias_tbl):
    b, t, _ = y3.shape
    tq = BAND_TQ * BAND_STEP_GROUPS
    qb, kb, vb = COL_BD_Q // HEAD_DIM, COL_BD_K // HEAD_DIM, COL_BD_V // HEAD_DIM
    kern = functools.partial(_band_prompt_kernel, t=t)
    return pl.pallas_call(
        kern,
        grid=(b, BAND_HEADS, t // tq),
        in_specs=[
            pl.BlockSpec((None, tq, HEAD_DIM), lambda b, h, g: (b, g, qb + h)),
            pl.BlockSpec((None, t, HEAD_DIM), lambda b, h, g: (b, 0, kb + h)),
            pl.BlockSpec((None, t, HEAD_DIM), lambda b, h, g: (b, 0, vb + h)),
            pl.BlockSpec((None, BAND_TQ, BAND_WIN), lambda b, h, g: (h, 0, 0)),
        ],
        out_specs=pl.BlockSpec((None, tq, HEAD_DIM), lambda b, h, g: (b, g, h)),
        out_shape=jax.ShapeDtypeStruct((b, t, BAND_WIDTH), BF16),
        scratch_shapes=[pltpu.VMEM((BAND_ROWS + t, HEAD_DIM), BF16),
                        pltpu.VMEM((BAND_ROWS + t, HEAD_DIM), BF16)],
        compiler_params=_params(3),
        name="band_prompt",
    )(y3, y3, y3, bias_tbl)


def _band_decode_kernel(q_ref, kn_ref, vn_ref, kc_ref, vc_ref, bias_ref, o_ref, kpad_ref, vpad_ref,
                        *, n_new, r_band):
    kpad_ref[...] = jnp.zeros_like(kpad_ref)
    vpad_ref[...] = jnp.zeros_like(vpad_ref)
    for h in range(BAND_HEADS):
        cols = slice(h * HEAD_DIM, (h + 1) * HEAD_DIM)
        kpad_ref[h, 0:n_new, :] = kn_ref[:, cols]
        vpad_ref[h, 0:n_new, :] = vn_ref[:, cols]
    for h in range(BAND_HEADS):
        cols = slice(h * HEAD_DIM, (h + 1) * HEAD_DIM)
        q = q_ref[:, cols]
        s_cache = _nt_dot(q, kc_ref[h].astype(BF16)) + bias_ref[h, :, 0:r_band]
        s_new = _nt_dot(q, kpad_ref[h]) + bias_ref[h, :, r_band:r_band + NEW_PAD]
        o_ref[:, cols] = _softmax2_pv([(s_cache, vc_ref[h].astype(BF16)),
                                       (s_new, vpad_ref[h])]).astype(BF16)


def _band_decode(ys3, cache_k, cache_v, bias_tbl):
    bd, n_new, _ = ys3.shape
    r_band = cache_k.shape[2]
    kern = functools.partial(_band_decode_kernel, n_new=n_new, r_band=r_band)
    cache_spec = pl.BlockSpec((None, BAND_HEADS, r_band, HEAD_DIM), lambda b: (b, 0, 0, 0))
    return pl.pallas_call(
        kern,
        grid=(bd,),
        in_specs=[
            pl.BlockSpec((None, n_new, BAND_WIDTH), lambda b: (b, 0, COL_BD_Q // BAND_WIDTH)),
            pl.BlockSpec((None, n_new, BAND_WIDTH), lambda b: (b, 0, COL_BD_K // BAND_WIDTH)),
            pl.BlockSpec((None, n_new, BAND_WIDTH), lambda b: (b, 0, COL_BD_V // BAND_WIDTH)),
            cache_spec,
            cache_spec,
            pl.BlockSpec((BAND_HEADS, n_new, r_band + NEW_PAD), lambda b: (0, 0, 0)),
        ],
        out_specs=pl.BlockSpec((None, n_new, BAND_WIDTH), lambda b: (b, 0, 0)),
        out_shape=jax.ShapeDtypeStruct((bd, n_new, BAND_WIDTH), BF16),
        scratch_shapes=[pltpu.VMEM((BAND_HEADS, NEW_PAD, HEAD_DIM), BF16),
                        pltpu.VMEM((BAND_HEADS, NEW_PAD, HEAD_DIM), BF16)],
        compiler_params=_params(1),
        name="band_decode",
    )(ys3, ys3, ys3, cache_k, cache_v, bias_tbl)


def _mem_attn_kernel(q_ref, mk_ref, mv_ref, o_ref):
    for h in range(MEM_HEADS):
        sl = slice(h * HEAD_DIM, (h + 1) * HEAD_DIM)
        s = _nt_dot(q_ref[:, sl], mk_ref[:, sl].astype(BF16))
        o_ref[:, sl] = _softmax2_pv([(s, mv_ref[:, sl].astype(BF16))]).astype(BF16)


def _mem_attention(y3, mk, mv, *, tq):
    b, t, _ = y3.shape
    n_mem = mk.shape[1]
    qb = COL_MM_Q // MEM_WIDTH
    return pl.pallas_call(
        _mem_attn_kernel,
        grid=(b, t // tq),
        in_specs=[
            pl.BlockSpec((None, tq, MEM_WIDTH), lambda b, i: (b, i, qb)),
            pl.BlockSpec((None, n_mem, MEM_WIDTH), lambda b, i: (b, 0, 0)),
            pl.BlockSpec((None, n_mem, MEM_WIDTH), lambda b, i: (b, 0, 0)),
        ],
        out_specs=pl.BlockSpec((None, tq, MEM_WIDTH), lambda b, i: (b, i, 0)),
        out_shape=jax.ShapeDtypeStruct((b, t, MEM_WIDTH), BF16),
        compiler_params=_params(2),
        name="mem_attention",
    )(y3, mk, mv)


def _silu(g):
    return g / (1.0 + jnp.exp(-g))


def _sigmoid(g):
    return 1.0 / (1.0 + jnp.exp(-g))


def _out_kernel(x_ref, osb_ref, obd_ref, omm_ref, gsb_ref, gbd_ref, gmm_ref,
                mg0_ref, mg1_ref, mg2_ref, mg3_ref, mg4_ref, mg5_ref,
                wsb_ref, wbd_ref, wmm_ref, wout_ref, gpost_ref, y_ref, merged_ref, *, half):
    u_sb = (osb_ref[...].astype(F32) * _silu(gsb_ref[...].astype(F32))).astype(BF16)
    u_bd = (obd_ref[...].astype(F32) * _silu(gbd_ref[...].astype(F32))).astype(BF16)
    u_mm = (omm_ref[...].astype(F32) * _silu(gmm_ref[...].astype(F32))).astype(BF16)
    mg = ((mg0_ref, mg2_ref, mg4_ref), (mg1_ref, mg3_ref, mg5_ref))
    for n in range(2):
        cols = slice(n * half, (n + 1) * half)
        m_sb, m_bd, m_mm = mg[n]
        merged = (_sigmoid(m_sb[...].astype(F32)) * _dot(u_sb, wsb_ref[:, cols])
                  + _sigmoid(m_bd[...].astype(F32)) * _dot(u_bd, wbd_ref[:, cols])
                  + _sigmoid(m_mm[...].astype(F32)) * _dot(u_mm, wmm_ref[:, cols]))
        merged_ref[:, cols] = merged.astype(BF16)
    y = _dot(merged_ref[...], wout_ref[...])
    ms = jnp.mean(y * y, axis=-1, keepdims=True)
    y_ref[...] = x_ref[...] + (y * lax.rsqrt(ms + RMS_EPS)) * gpost_ref[...]


def _output_stage(x2d, y2d, o_sb, o_bd, o_mm, w_sb, w_bd, w_mm, w_out, g_post, *, tm):
    m, d = x2d.shape
    half = d // 2
    assert COL_MG % half == 0
    mgb = COL_MG // half
    const = dict(pipeline_mode=pl.Buffered(1))
    kern = functools.partial(_out_kernel, half=half)
    return pl.pallas_call(
        kern,
        grid=(m // tm,),
        in_specs=[
            pl.BlockSpec((tm, d), lambda i: (i, 0)),
            pl.BlockSpec((tm, SB_WIDTH), lambda i: (i, 0)),
            pl.BlockSpec((tm, BAND_WIDTH), lambda i: (i, 0)),
            pl.BlockSpec((tm, MEM_WIDTH), lambda i: (i, 0)),
            pl.BlockSpec((tm, SB_WIDTH), lambda i: (i, COL_SB_G // SB_WIDTH)),
            pl.BlockSpec((tm, BAND_WIDTH), lambda i: (i, COL_BD_G // BAND_WIDTH)),
            pl.BlockSpec((tm, MEM_WIDTH), lambda i: (i, COL_MM_G // MEM_WIDTH)),
        ] + [pl.BlockSpec((tm, half), functools.partial(lambda i, c: (i, c), c=mgb + c)) for c in range(6)] + [
            pl.BlockSpec((SB_WIDTH, d), lambda i: (0, 0), **const),
            pl.BlockSpec((BAND_WIDTH, d), lambda i: (0, 0), **const),
            pl.BlockSpec((MEM_WIDTH, d), lambda i: (0, 0), **const),
            pl.BlockSpec((d, d), lambda i: (0, 0), **const),
            pl.BlockSpec((1, d), lambda i: (0, 0)),
        ],
        out_specs=pl.BlockSpec((tm, d), lambda i: (i, 0)),
        out_shape=jax.ShapeDtypeStruct((m, d), F32),
        scratch_shapes=[pltpu.VMEM((tm, d), BF16)],
        compiler_params=_params(1),
        name="output_stage",
    )(x2d, o_sb, o_bd, o_mm, y2d, y2d, y2d, *([y2d] * 6), w_sb, w_bd, w_mm, w_out, g_post.reshape(1, d))


def _head_major(a):
    return jnp.transpose(a, (0, 2, 1, 3))


def kernel(x_prompt, x_sample, cache_sb_k, cache_sb_v, cache_band_k, cache_band_v, cache_mem_k, cache_mem_v, mem_prompt, g_pre, w_in, rel_bias, g_mem, w_mem_kv, w_up_sb, w_up_band, w_up_mem, w_out, g_post):
    depth = w_in.shape[0]
    b, t, d = x_prompt.shape
    bd, n_new, _ = x_sample.shape
    n_mem = mem_prompt.shape[1]
    r_band = cache_band_k.shape[2]
    in_width = w_in.shape[2]
    tm_p = min(BAND_ROWS, t)
    assert COL_MG + 3 * d == in_width
    assert t % SB_TQ == 0 and t % (BAND_TQ * BAND_STEP_GROUPS) == 0 and t % tm_p == 0
    assert r_band == BAND_ROWS and n_new <= CHUNK

    negu2 = jnp.where(jnp.arange(2 * SB_TK)[:, None] % SB_TK >= jnp.arange(SB_TK)[None, :], -1.0, 0.0).astype(BF16)
    cols = jnp.arange(in_width)
    is_q = ((cols < COL_SB_K) | ((cols >= COL_BD_Q) & (cols < COL_BD_K))
            | ((cols >= COL_MM_Q) & (cols < COL_MM_G)))
    col_scale = jnp.where(is_q, Q_SCALE, 1.0).astype(F32).reshape(1, in_width)

    xp = x_prompt.reshape(b * t, d)
    xs = x_sample.reshape(bd * n_new, d)
    outs = [[] for _ in range(10)]
    for l in range(depth):
        w_in_b = w_in[l].astype(BF16)
        w_kv_b = w_mem_kv[l].astype(BF16)
        w_sb_b = w_up_sb[l].astype(BF16)
        w_bd_b = w_up_band[l].astype(BF16)
        w_mm_b = w_up_mem[l].astype(BF16)
        w_out_b = w_out[l].astype(BF16)
        bias_p, bias_d = _band_bias_tables(rel_bias[l], n_new, r_band)

        y, sbk, sbv, bdk, bdv = _in_projection(xp, g_pre[l], w_in_b, col_scale, tm=tm_p, seqs=1,
                                               n_seq=b, band_period=t // tm_p)
        mk, mv = _memory_kv(mem_prompt.reshape(b * n_mem, d), g_mem[l], w_kv_b, tm=n_mem)
        y3 = y.reshape(b, t, in_width)
        o_sb = _sb_prompt(y3, negu2)
        o_bd = _band_prompt(y3, bias_p)
        o_mm = _mem_attention(y3, mk.reshape(b, n_mem, MEM_WIDTH), mv.reshape(b, n_mem, MEM_WIDTH), tq=512)
        xp = _output_stage(xp, y, o_sb.reshape(b * t, -1), o_bd.reshape(b * t, -1),
                           o_mm.reshape(b * t, -1), w_sb_b, w_bd_b, w_mm_b, w_out_b, g_post[l], tm=256)
        outs[0].append(_head_major(sbk))
        outs[1].append(_head_major(sbv))
        outs[2].append(_head_major(bdk))
        outs[3].append(_head_major(bdv))
        outs[4].append(mk.reshape(b, n_mem, MEM_HEADS, HEAD_DIM))
        outs[5].append(mv.reshape(b, n_mem, MEM_HEADS, HEAD_DIM))

        ms = bd * n_new
        ys, sbk2, sbv2, bdk2, bdv2 = _in_projection(xs, g_pre[l], w_in_b, col_scale, tm=ms, seqs=bd,
                                                    n_seq=bd, band_period=1)
        ys3 = ys.reshape(bd, n_new, in_width)
        o_sb2 = _sb_decode(ys3, _head_major(cache_sb_k[l]), _head_major(cache_sb_v[l]), negu2)
        o_bd2 = _band_decode(ys3, _head_major(cache_band_k[l]), _head_major(cache_band_v[l]), bias_d)
        o_mm2 = _mem_attention(ys3, cache_mem_k[l].reshape(bd, n_mem, MEM_WIDTH),
                               cache_mem_v[l].reshape(bd, n_mem, MEM_WIDTH), tq=n_new)
        xs = _output_stage(xs, ys, o_sb2.reshape(ms, -1), o_bd2.reshape(ms, -1), o_mm2.reshape(ms, -1),
                           w_sb_b, w_bd_b, w_mm_b, w_out_b, g_post[l], tm=ms)
        outs[6].append(_head_major(sbk2))
        outs[7].append(_head_major(sbv2))
        outs[8].append(_head_major(bdk2))
        outs[9].append(_head_major(bdv2))

    return (xp.reshape(b, t, d), xs.reshape(bd, n_new, d)) + tuple(jnp.stack(o) for o in outs)
```

```python
import functools
import math

import jax
import jax.numpy as jnp
from jax import lax
from jax.experimental import pallas as pl
from jax.experimental.pallas import tpu as pltpu

F32 = jnp.float32
BF16 = jnp.bfloat16

HEAD_DIM = 128
SB_HEADS = 6
BAND_HEADS = 6
MEM_HEADS = 4
SB_WIDTH = SB_HEADS * HEAD_DIM
BAND_WIDTH = BAND_HEADS * HEAD_DIM
MEM_WIDTH = MEM_HEADS * HEAD_DIM
CHUNK = 64
CHUNK_SHIFT = 6
BAND_LEFT_CHUNKS = 8
BAND_ROWS = BAND_LEFT_CHUNKS * CHUNK
MAX_REL = 256
RMS_EPS = 1e-6
NEG_INF = -1e30
LOG2E = math.log2(math.e)
Q_SCALE = HEAD_DIM ** -0.5 * LOG2E

COL_SB_Q = 0
COL_SB_K = COL_SB_Q + SB_WIDTH
COL_SB_V = COL_SB_K + SB_WIDTH
COL_SB_G = COL_SB_V + SB_WIDTH
COL_BD_Q = COL_SB_G + SB_WIDTH
COL_BD_K = COL_BD_Q + BAND_WIDTH
COL_BD_V = COL_BD_K + BAND_WIDTH
COL_BD_G = COL_BD_V + BAND_WIDTH
COL_MM_Q = COL_BD_G + BAND_WIDTH
COL_MM_G = COL_MM_Q + MEM_WIDTH
COL_MG = COL_MM_G + MEM_WIDTH

KV_SB_K = 0
KV_SB_V = KV_SB_K + SB_WIDTH
KV_BD_K = KV_SB_V + SB_WIDTH
KV_BD_V = KV_BD_K + BAND_WIDTH
KV_WIDTH = KV_BD_V + BAND_WIDTH
R_SB_Q = 0
R_BD_Q = R_SB_Q + SB_WIDTH
R_SB_G = R_BD_Q + BAND_WIDTH
R_BD_G = R_SB_G + SB_WIDTH
R_MM_Q = R_BD_G + BAND_WIDTH
R_MM_G = R_MM_Q + MEM_WIDTH
R_MG = R_MM_G + MEM_WIDTH

VMEM_LIMIT_BYTES = 56 * 1024 * 1024
KV_TN = SB_WIDTH
ACT_TN = 1024
SB_TK = 256
SB_TQ = 4 * SB_TK
SB_DEAD = -160.0
BAND_TQ = 4 * CHUNK
BAND_WIN = BAND_TQ + BAND_ROWS
BAND_STEP_GROUPS = 4
BIAS_LANES = 1024
NEW_PAD = 128


def _params(n_axes, vmem=VMEM_LIMIT_BYTES):
    return pltpu.CompilerParams(dimension_semantics=("arbitrary",) * n_axes,
                                vmem_limit_bytes=vmem)


def _nt_dot(a, b):
    return lax.dot_general(a, b, (((1,), (1,)), ((), ())), preferred_element_type=F32)


def _dot(a, b):
    return jnp.dot(a, b, preferred_element_type=F32)


def _pre_norm_to(h_ref, x_ref, g_ref):
    x = x_ref[...]
    ms = jnp.mean(x * x, axis=-1, keepdims=True)
    h_ref[...] = ((x * lax.rsqrt(ms + RMS_EPS)) * g_ref[...]).astype(BF16)


def _kv_proj_kernel(x_ref, g_ref, w_ref, y_ref, sbk_ref, sbv_ref, bdk_ref, bdv_ref, h_ref, *, seqs, rows):
    j = pl.program_id(1)

    @pl.when(j == 0)
    def _():
        _pre_norm_to(h_ref, x_ref, g_ref)

    for group, dst_ref in enumerate((sbk_ref, sbv_ref, bdk_ref, bdv_ref)):
        @pl.when(j == group)
        def _(dst_ref=dst_ref):
            acc = _dot(h_ref[...], w_ref[...])
            y_ref[...] = acc.astype(BF16)
            for h in range(SB_HEADS):
                for s in range(seqs):
                    dst_ref[s, h] = acc[s * rows:(s + 1) * rows, h * HEAD_DIM:(h + 1) * HEAD_DIM]


def _kv_projection(x2d, g_pre, w_bf16, *, tm, seqs, n_seq):
    m, d = x2d.shape
    assert w_bf16.shape[1] == KV_WIDTH and KV_WIDTH == 4 * KV_TN and SB_HEADS == BAND_HEADS
    rows = tm // seqs
    seq_rows = m // n_seq
    blocks_per_seq = seq_rows // rows
    kern = functools.partial(_kv_proj_kernel, seqs=seqs, rows=rows)
    f32_spec = pl.BlockSpec((seqs, SB_HEADS, rows, HEAD_DIM),
                            lambda i, j: (i // blocks_per_seq, 0, i % blocks_per_seq, 0))
    f32_shape = jax.ShapeDtypeStruct((n_seq, SB_HEADS, seq_rows, HEAD_DIM), F32)
    return pl.pallas_call(
        kern,
        grid=(m // tm, KV_WIDTH // KV_TN),
        in_specs=[
            pl.BlockSpec((tm, d), lambda i, j: (i, 0)),
            pl.BlockSpec((1, d), lambda i, j: (0, 0)),
            pl.BlockSpec((d, KV_TN), lambda i, j: (0, j)),
        ],
        out_specs=[pl.BlockSpec((tm, KV_TN), lambda i, j: (i, j)), f32_spec, f32_spec, f32_spec, f32_spec],
        out_shape=[jax.ShapeDtypeStruct((m, KV_WIDTH), BF16), f32_shape, f32_shape, f32_shape, f32_shape],
        scratch_shapes=[pltpu.VMEM((tm, d), BF16)],
        compiler_params=_params(2),
        name="kv_projection",
    )(x2d, g_pre.reshape(1, d), w_bf16)


def _act_proj_kernel(x_ref, g_ref, w_ref, cs_ref, y_ref, h_ref):
    @pl.when(pl.program_id(1) == 0)
    def _():
        _pre_norm_to(h_ref, x_ref, g_ref)

    y_ref[...] = (_dot(h_ref[...], w_ref[...]) * cs_ref[...]).astype(BF16)


def _act_projection(x2d, g_pre, w_bf16, col_scale, *, tm):
    m, d = x2d.shape
    n = w_bf16.shape[1]
    assert n % ACT_TN == 0 and m % tm == 0
    return pl.pallas_call(
        _act_proj_kernel,
        grid=(m // tm, n // ACT_TN),
        in_specs=[
            pl.BlockSpec((tm, d), lambda i, j: (i, 0)),
            pl.BlockSpec((1, d), lambda i, j: (0, 0)),
            pl.BlockSpec((d, ACT_TN), lambda i, j: (0, j)),
            pl.BlockSpec((1, ACT_TN), lambda i, j: (0, j)),
        ],
        out_specs=pl.BlockSpec((tm, ACT_TN), lambda i, j: (i, j)),
        out_shape=jax.ShapeDtypeStruct((m, n), BF16),
        scratch_shapes=[pltpu.VMEM((tm, d), BF16)],
        compiler_params=_params(2),
        name="act_projection",
    )(x2d, g_pre.reshape(1, d), w_bf16, col_scale)


def _memkv_kernel(x_ref, g_ref, w_ref, mk_ref, mv_ref):
    x = x_ref[...]
    ms = jnp.mean(x * x, axis=-1, keepdims=True)
    h = ((x * lax.rsqrt(ms + RMS_EPS)) * g_ref[...]).astype(BF16)
    acc = _dot(h, w_ref[...])
    mk_ref[...] = acc[:, :MEM_WIDTH]
    mv_ref[...] = acc[:, MEM_WIDTH:]


def _memory_kv(mem2d, g_mem, w_bf16, *, tm):
    m, d = mem2d.shape
    return pl.pallas_call(
        _memkv_kernel,
        grid=(m // tm,),
        in_specs=[
            pl.BlockSpec((tm, d), lambda i: (i, 0)),
            pl.BlockSpec((1, d), lambda i: (0, 0)),
            pl.BlockSpec((d, 2 * MEM_WIDTH), lambda i: (0, 0)),
        ],
        out_specs=[pl.BlockSpec((tm, MEM_WIDTH), lambda i: (i, 0)),
                   pl.BlockSpec((tm, MEM_WIDTH), lambda i: (i, 0))],
        out_shape=[jax.ShapeDtypeStruct((m, MEM_WIDTH), F32),
                   jax.ShapeDtypeStruct((m, MEM_WIDTH), F32)],
        compiler_params=_params(1),
        name="memory_kv",
    )(mem2d, g_mem.reshape(1, d), w_bf16)


def _neg_suffix_matrix(n):
    row = lax.broadcasted_iota(jnp.int32, (2 * n, n), 0)
    col = lax.broadcasted_iota(jnp.int32, (2 * n, n), 1)
    row = jnp.where(row >= n, row - n, row)
    return jnp.where(row >= col, -1.0, 0.0).astype(BF16)


def _sb_weights(z2, carry2, negu2, mask):
    p = jnp.maximum(z2, 0.0) + jnp.log(1.0 + jnp.exp2(-jnp.abs(z2))) * LOG2E
    if mask is not None:
        p = jnp.where(mask, p, 0.0)
    p_hi = p.astype(BF16)
    p_lo = (p - p_hi.astype(F32)).astype(BF16)
    suffix = _dot(jnp.concatenate([p_hi, p_lo], axis=1), negu2)
    w = jnp.exp2(z2 + suffix + carry2)
    if mask is not None:
        w = jnp.where(mask, w, 0.0)
    return w, carry2 - jnp.sum(p, axis=-1, keepdims=True)


def _sb_prompt_kernel(q_ref, k_ref, v_ref, negu2_ref, o_ref, acc_ref, carry_ref, kpad_ref, vpad_ref, *, t):
    i = pl.program_id(2)
    n_sub = SB_TQ // SB_TK
    negu2 = negu2_ref[...]

    @pl.when(i == 0)
    def _():
        kpad_ref[0:SB_TK, :] = jnp.zeros((SB_TK, HEAD_DIM), BF16)
        vpad_ref[0:SB_TK, :] = jnp.zeros((SB_TK, HEAD_DIM), BF16)
        kpad_ref[SB_TK:SB_TK + t, :] = k_ref[...]
        vpad_ref[SB_TK:SB_TK + t, :] = v_ref[...]

    def kv_block(j):
        start = pl.multiple_of((j + 1) * SB_TK, SB_TK)
        return kpad_ref[pl.ds(start, SB_TK), :], vpad_ref[pl.ds(start, SB_TK), :]

    row = lax.broadcasted_iota(jnp.int32, (SB_TK, SB_TK), 0)
    col = lax.broadcasted_iota(jnp.int32, (SB_TK, SB_TK), 1)
    for r in range(n_sub):
        rows = slice(r * SB_TK, (r + 1) * SB_TK)
        s = i * n_sub + r
        q = q_ref[rows, :]
        kb, vb = kv_block(s)
        w, carry = _sb_weights(_nt_dot(q, kb), jnp.zeros((SB_TK, 1), F32), negu2, col < row)
        acc = _dot(w.astype(BF16), vb)
        kb, vb = kv_block(s - 1)
        prev_exists = None if r > 0 else (jnp.zeros((SB_TK, SB_TK), jnp.int32) + i) > 0
        w, carry = _sb_weights(_nt_dot(q, kb), carry, negu2, prev_exists)
        acc_ref[rows, :] = acc + _dot(w.astype(BF16), vb)
        carry_ref[rows, :] = carry

    row_q = lax.broadcasted_iota(jnp.int32, (SB_TQ, 1), 0)
    row_t = lax.broadcasted_iota(jnp.int32, (SB_TQ, SB_TK), 0)
    has_more = row_q >= (2 - n_sub * i) * SB_TK

    def any_alive(carry):
        return (jnp.max(jnp.where(has_more, carry, NEG_INF)) > SB_DEAD).astype(jnp.int32)

    def cond(state):
        j, alive = state
        return jnp.logical_and(j >= 0, alive > 0)

    def body(state):
        j, _ = state
        kb, vb = kv_block(j)
        visits = row_t >= (j - n_sub * i + 2) * SB_TK
        w, carry = _sb_weights(_nt_dot(q_ref[...], kb), carry_ref[...], negu2, visits)
        acc_ref[...] += _dot(w.astype(BF16), vb)
        carry_ref[...] = carry
        return j - 1, any_alive(carry)

    lax.while_loop(cond, body, (n_sub * i + n_sub - 3, any_alive(carry_ref[...])))
    o_ref[...] = acc_ref[...].astype(BF16)


def _sb_prompt(ya3, ykv3, negu2):
    b, t, _ = ya3.shape
    qb, kb, vb = R_SB_Q // HEAD_DIM, KV_SB_K // HEAD_DIM, KV_SB_V // HEAD_DIM
    assert SB_TQ // SB_TK >= 3
    kern = functools.partial(_sb_prompt_kernel, t=t)
    return pl.pallas_call(
        kern,
        grid=(b, SB_HEADS, t // SB_TQ),
        in_specs=[
            pl.BlockSpec((None, SB_TQ, HEAD_DIM), lambda b, h, i: (b, i, qb + h)),
            pl.BlockSpec((None, t, HEAD_DIM), lambda b, h, i: (b, 0, kb + h)),
            pl.BlockSpec((None, t, HEAD_DIM), lambda b, h, i: (b, 0, vb + h)),
            pl.BlockSpec((2 * SB_TK, SB_TK), lambda b, h, i: (0, 0)),
        ],
        out_specs=pl.BlockSpec((None, SB_TQ, HEAD_DIM), lambda b, h, i: (b, i, h)),
        out_shape=jax.ShapeDtypeStruct((b, t, SB_WIDTH), BF16),
        scratch_shapes=[pltpu.VMEM((SB_TQ, HEAD_DIM), F32), pltpu.VMEM((SB_TQ, 1), F32),
                        pltpu.VMEM((SB_TK + t, HEAD_DIM), BF16), pltpu.VMEM((SB_TK + t, HEAD_DIM), BF16)],
        compiler_params=_params(3),
        name="sb_prompt",
    )(ya3, ykv3, ykv3, negu2)


def _sb_decode_kernel(q_ref, kn_ref, vn_ref, kc_hbm, vc_hbm, negu2_ref, o_ref,
                      acc_ref, carry_ref, kpad_ref, vpad_ref, kbuf_ref, vbuf_ref, sem,
                      *, n_new, n_blocks):
    b = pl.program_id(0)
    heads = SB_HEADS

    def cache_copies(j, slot):
        rows = pl.ds(pl.multiple_of((n_blocks - 1 - j) * SB_TK, SB_TK), SB_TK)
        return (pltpu.make_async_copy(kc_hbm.at[b, :, rows, :], kbuf_ref.at[slot], sem.at[0, slot]),
                pltpu.make_async_copy(vc_hbm.at[b, :, rows, :], vbuf_ref.at[slot], sem.at[1, slot]))

    def start_fetch(j, slot):
        for cp in cache_copies(j, slot):
            cp.start()

    def wait_fetch(j, slot):
        for cp in cache_copies(j, slot):
            cp.wait()

    start_fetch(0, 0)

    def head_cols(h):
        return slice(h * HEAD_DIM, (h + 1) * HEAD_DIM)

    def head_rows(h):
        return slice(h * n_new, (h + 1) * n_new)

    def block(k_of, v_of, negu2, mask):
        z2 = jnp.concatenate([_nt_dot(q_ref[:, head_cols(h)], k_of(h)) for h in range(heads)], axis=0)
        w, carry = _sb_weights(z2, carry_ref[...], negu2, mask)
        wb = w.astype(BF16)
        for h in range(heads):
            acc_ref[head_rows(h), :] += _dot(wb[head_rows(h), :], v_of(h))
        carry_ref[...] = carry

    def any_alive():
        return (jnp.max(carry_ref[...]) > SB_DEAD).astype(jnp.int32)

    acc_ref[...] = jnp.zeros_like(acc_ref)
    carry_ref[...] = jnp.zeros_like(carry_ref)
    kpad_ref[...] = jnp.zeros_like(kpad_ref)
    vpad_ref[...] = jnp.zeros_like(vpad_ref)
    for h in range(heads):
        kpad_ref[h, 0:n_new, :] = kn_ref[:, head_cols(h)]
        vpad_ref[h, 0:n_new, :] = vn_ref[:, head_cols(h)]
    row = lax.broadcasted_iota(jnp.int32, (n_new, NEW_PAD), 0)
    col = lax.broadcasted_iota(jnp.int32, (n_new, NEW_PAD), 1)
    mask = jnp.concatenate([(col < row).astype(jnp.int32)] * heads, axis=0) == 1
    block(lambda h: kpad_ref[h], lambda h: vpad_ref[h], _neg_suffix_matrix(NEW_PAD), mask)

    negu2 = negu2_ref[...]

    def cond(state):
        j, alive = state
        return jnp.logical_and(j < n_blocks, alive > 0)

    def body(state):
        j, _ = state
        slot = j % 2
        wait_fetch(j, slot)

        @pl.when(j + 1 < n_blocks)
        def _():
            start_fetch(j + 1, 1 - slot)

        block(lambda h: kbuf_ref[slot, h].astype(BF16), lambda h: vbuf_ref[slot, h].astype(BF16),
              negu2, None)
        return j + 1, any_alive()

    j_end, _ = lax.while_loop(cond, body, (0, any_alive()))

    @pl.when(j_end < n_blocks)
    def _():
        wait_fetch(j_end, j_end % 2)

    for h in range(heads):
        o_ref[:, head_cols(h)] = acc_ref[head_rows(h), :].astype(BF16)


def _sb_decode(ya3, ykv3, cache_k, cache_v, negu2):
    bd, n_new, _ = ya3.shape
    past = cache_k.shape[2]
    assert past % SB_TK == 0 and n_new <= NEW_PAD and n_new % 16 == 0
    kern = functools.partial(_sb_decode_kernel, n_new=n_new, n_blocks=past // SB_TK)
    return pl.pallas_call(
        kern,
        grid=(bd,),
        in_specs=[
            pl.BlockSpec((None, n_new, SB_WIDTH), lambda b: (b, 0, R_SB_Q // SB_WIDTH)),
            pl.BlockSpec((None, n_new, SB_WIDTH), lambda b: (b, 0, KV_SB_K // SB_WIDTH)),
            pl.BlockSpec((None, n_new, SB_WIDTH), lambda b: (b, 0, KV_SB_V // SB_WIDTH)),
            pl.BlockSpec(memory_space=pl.ANY),
            pl.BlockSpec(memory_space=pl.ANY),
            pl.BlockSpec((2 * SB_TK, SB_TK), lambda b: (0, 0)),
        ],
        out_specs=pl.BlockSpec((None, n_new, SB_WIDTH), lambda b: (b, 0, 0)),
        out_shape=jax.ShapeDtypeStruct((bd, n_new, SB_WIDTH), BF16),
        scratch_shapes=[pltpu.VMEM((SB_HEADS * n_new, HEAD_DIM), F32),
                        pltpu.VMEM((SB_HEADS * n_new, 1), F32),
                        pltpu.VMEM((SB_HEADS, NEW_PAD, HEAD_DIM), BF16),
                        pltpu.VMEM((SB_HEADS, NEW_PAD, HEAD_DIM), BF16),
                        pltpu.VMEM((2, SB_HEADS, SB_TK, HEAD_DIM), F32),
                        pltpu.VMEM((2, SB_HEADS, SB_TK, HEAD_DIM), F32),
                        pltpu.SemaphoreType.DMA((2, 2))],
        compiler_params=_params(1),
        name="sb_decode",
    )(ya3, ykv3, ykv3, cache_k, cache_v, negu2)


def _softmax2_pv(parts):
    mx = functools.reduce(jnp.maximum, [jnp.max(s, axis=-1, keepdims=True) for s, _ in parts])
    num = None
    den = None
    for s, v in parts:
        p = jnp.exp2(s - mx)
        d = jnp.sum(p, axis=-1, keepdims=True)
        o = _dot(p.astype(BF16), v)
        num = o if num is None else num + o
        den = d if den is None else den + d
    return num / den


def _band_bias_kernel(g_ref, tp_ref, td_ref, *, n_new, r_band):
    x = jnp.broadcast_to(g_ref[...], (BAND_TQ, BIAS_LANES))
    row = lax.broadcasted_iota(jnp.int32, (BAND_TQ, BIAS_LANES), 0)
    x = pltpu.roll(x, BAND_TQ, 1)
    for bit in range(BAND_TQ.bit_length() - 1):
        x = jnp.where(((row >> bit) & 1) == 1, pltpu.roll(x, 1 << bit, 1), x)
    tbl = x[:, :BAND_WIN] * LOG2E
    r = lax.broadcasted_iota(jnp.int32, (BAND_TQ, BAND_WIN), 0)
    j = lax.broadcasted_iota(jnp.int32, (BAND_TQ, BAND_WIN), 1)
    dc = (j >> CHUNK_SHIFT) - (r >> CHUNK_SHIFT)
    tp_ref[...] = jnp.where(jnp.logical_and(dc >= 0, dc <= BAND_LEFT_CHUNKS), tbl, NEG_INF)
    jd = lax.broadcasted_iota(jnp.int32, (n_new, r_band + NEW_PAD), 1)
    td_ref[...] = jnp.where(jd < r_band + n_new, tbl[:n_new, :r_band + NEW_PAD], NEG_INF)


def _band_bias_tables(rel_bias, n_new, r_band):
    h = rel_bias.shape[0]
    assert BAND_ROWS == 2 * MAX_REL and r_band == BAND_ROWS and BIAS_LANES == 2 * BAND_ROWS
    rb = rel_bias.astype(F32)
    g = jnp.concatenate([rb[:, :0:-1], jnp.broadcast_to(rb[:, -1:], (h, BIAS_LANES - 2 * MAX_REL))], axis=1)
    kern = functools.partial(_band_bias_kernel, n_new=n_new, r_band=r_band)
    return pl.pallas_call(
        kern,
        grid=(h,),
        in_specs=[pl.BlockSpec((None, 1, BIAS_LANES), lambda i: (i, 0, 0))],
        out_specs=[pl.BlockSpec((None, BAND_TQ, BAND_WIN), lambda i: (i, 0, 0)),
                   pl.BlockSpec((None, n_new, r_band + NEW_PAD), lambda i: (i, 0, 0))],
        out_shape=[jax.ShapeDtypeStruct((h, BAND_TQ, BAND_WIN), F32),
                   jax.ShapeDtypeStruct((h, n_new, r_band + NEW_PAD), F32)],
        compiler_params=_params(1),
        name="band_bias",
    )(g.reshape(h, 1, BIAS_LANES))


def _band_prompt_kernel(q_ref, k_ref, v_ref, bias_ref, o_ref, kpad_ref, vpad_ref, *, t):
    s_idx = pl.program_id(2)

    @pl.when(s_idx == 0)
    def _():
        kpad_ref[0:BAND_ROWS, :] = jnp.zeros((BAND_ROWS, HEAD_DIM), BF16)
        vpad_ref[0:BAND_ROWS, :] = jnp.zeros((BAND_ROWS, HEAD_DIM), BF16)
        kpad_ref[BAND_ROWS:BAND_ROWS + t, :] = k_ref[...]
        vpad_ref[BAND_ROWS:BAND_ROWS + t, :] = v_ref[...]

    col = lax.broadcasted_iota(jnp.int32, (BAND_TQ, BAND_WIN), 1)
    for gg in range(BAND_STEP_GROUPS):
        g = s_idx * BAND_STEP_GROUPS + gg
        start = pl.multiple_of(g * BAND_TQ, BAND_TQ)
        rows = slice(gg * BAND_TQ, (gg + 1) * BAND_TQ)
        s = _nt_dot(q_ref[rows, :], kpad_ref[pl.ds(start, BAND_WIN), :]) + bias_ref[...]
        s = jnp.where(col + g * BAND_TQ >= BAND_ROWS, s, NEG_INF)
        o_ref[rows, :] = _softmax2_pv([(s, vpad_ref[pl.ds(start, BAND_WIN), :])]).astype(BF16)


def _band_prompt(ya3, ykv3, bias_tbl):
    b, t, _ = ya3.shape
    tq = BAND_TQ * BAND_STEP_GROUPS
    qb, kb, vb = R_BD_Q // HEAD_DIM, KV_BD_K // HEAD_DIM, KV_BD_V // HEAD_DIM
    kern = functools.partial(_band_prompt_kernel, t=t)
    return pl.pallas_call(
        kern,
        grid=(b, BAND_HEADS, t // tq),
        in_specs=[
            pl.BlockSpec((None, tq, HEAD_DIM), lambda b, h, g: (b, g, qb + h)),
            pl.BlockSpec((None, t, HEAD_DIM), lambda b, h, g: (b, 0, kb + h)),
            pl.BlockSpec((None, t, HEAD_DIM), lambda b, h, g: (b, 0, vb + h)),
            pl.BlockSpec((None, BAND_TQ, BAND_WIN), lambda b, h, g: (h, 0, 0)),
        ],
        out_specs=pl.BlockSpec((None, tq, HEAD_DIM), lambda b, h, g: (b, g, h)),
        out_shape=jax.ShapeDtypeStruct((b, t, BAND_WIDTH), BF16),
        scratch_shapes=[pltpu.VMEM((BAND_ROWS + t, HEAD_DIM), BF16),
                        pltpu.VMEM((BAND_ROWS + t, HEAD_DIM), BF16)],
        compiler_params=_params(3),
        name="band_prompt",
    )(ya3, ykv3, ykv3, bias_tbl)


def _band_decode_kernel(q_ref, kn_ref, vn_ref, kc_ref, vc_ref, bias_ref, o_ref, kpad_ref, vpad_ref,
                        *, n_new, r_band):
    kpad_ref[...] = jnp.zeros_like(kpad_ref)
    vpad_ref[...] = jnp.zeros_like(vpad_ref)
    for h in range(BAND_HEADS):
        cols = slice(h * HEAD_DIM, (h + 1) * HEAD_DIM)
        kpad_ref[h, 0:n_new, :] = kn_ref[:, cols]
        vpad_ref[h, 0:n_new, :] = vn_ref[:, cols]
    for h in range(BAND_HEADS):
        cols = slice(h * HEAD_DIM, (h + 1) * HEAD_DIM)
        q = q_ref[:, cols]
        s_cache = _nt_dot(q, kc_ref[h].astype(BF16)) + bias_ref[h, :, 0:r_band]
        s_new = _nt_dot(q, kpad_ref[h]) + bias_ref[h, :, r_band:r_band + NEW_PAD]
        o_ref[:, cols] = _softmax2_pv([(s_cache, vc_ref[h].astype(BF16)),
                                       (s_new, vpad_ref[h])]).astype(BF16)


def _band_decode(ya3, ykv3, cache_k, cache_v, bias_tbl):
    bd, n_new, _ = ya3.shape
    r_band = cache_k.shape[2]
    kern = functools.partial(_band_decode_kernel, n_new=n_new, r_band=r_band)
    cache_spec = pl.BlockSpec((None, BAND_HEADS, r_band, HEAD_DIM), lambda b: (b, 0, 0, 0))
    return pl.pallas_call(
        kern,
        grid=(bd,),
        in_specs=[
            pl.BlockSpec((None, n_new, BAND_WIDTH), lambda b: (b, 0, R_BD_Q // BAND_WIDTH)),
            pl.BlockSpec((None, n_new, BAND_WIDTH), lambda b: (b, 0, KV_BD_K // BAND_WIDTH)),
            pl.BlockSpec((None, n_new, BAND_WIDTH), lambda b: (b, 0, KV_BD_V // BAND_WIDTH)),
            cache_spec,
            cache_spec,
            pl.BlockSpec((BAND_HEADS, n_new, r_band + NEW_PAD), lambda b: (0, 0, 0)),
        ],
        out_specs=pl.BlockSpec((None, n_new, BAND_WIDTH), lambda b: (b, 0, 0)),
        out_shape=jax.ShapeDtypeStruct((bd, n_new, BAND_WIDTH), BF16),
        scratch_shapes=[pltpu.VMEM((BAND_HEADS, NEW_PAD, HEAD_DIM), BF16),
                        pltpu.VMEM((BAND_HEADS, NEW_PAD, HEAD_DIM), BF16)],
        compiler_params=_params(1),
        name="band_decode",
    )(ya3, ykv3, ykv3, cache_k, cache_v, bias_tbl)


def _mem_attn_kernel(q_ref, mk_ref, mv_ref, o_ref):
    for h in range(MEM_HEADS):
        sl = slice(h * HEAD_DIM, (h + 1) * HEAD_DIM)
        s = _nt_dot(q_ref[:, sl], mk_ref[:, sl].astype(BF16))
        o_ref[:, sl] = _softmax2_pv([(s, mv_ref[:, sl].astype(BF16))]).astype(BF16)


def _mem_attention(y3, mk, mv, *, tq):
    b, t, _ = y3.shape
    n_mem = mk.shape[1]
    qb = R_MM_Q // MEM_WIDTH
    return pl.pallas_call(
        _mem_attn_kernel,
        grid=(b, t // tq),
        in_specs=[
            pl.BlockSpec((None, tq, MEM_WIDTH), lambda b, i: (b, i, qb)),
            pl.BlockSpec((None, n_mem, MEM_WIDTH), lambda b, i: (b, 0, 0)),
            pl.BlockSpec((None, n_mem, MEM_WIDTH), lambda b, i: (b, 0, 0)),
        ],
        out_specs=pl.BlockSpec((None, tq, MEM_WIDTH), lambda b, i: (b, i, 0)),
        out_shape=jax.ShapeDtypeStruct((b, t, MEM_WIDTH), BF16),
        compiler_params=_params(2),
        name="mem_attention",
    )(y3, mk, mv)


def _silu(g):
    return g / (1.0 + jnp.exp(-g))


def _sigmoid(g):
    return 1.0 / (1.0 + jnp.exp(-g))


def _out_kernel(x_ref, osb_ref, obd_ref, omm_ref, gsb_ref, gbd_ref, gmm_ref,
                mgsb_ref, mgbd_ref, mgmm_ref,
                wsb_ref, wbd_ref, wmm_ref, wout_ref, gpost_ref, y_ref, merged_ref, *, half):
    u_sb = (osb_ref[...].astype(F32) * _silu(gsb_ref[...].astype(F32))).astype(BF16)
    u_bd = (obd_ref[...].astype(F32) * _silu(gbd_ref[...].astype(F32))).astype(BF16)
    u_mm = (omm_ref[...].astype(F32) * _silu(gmm_ref[...].astype(F32))).astype(BF16)
    for n in range(2):
        cols = slice(n * half, (n + 1) * half)
        merged = (_sigmoid(mgsb_ref[:, cols].astype(F32)) * _dot(u_sb, wsb_ref[:, cols])
                  + _sigmoid(mgbd_ref[:, cols].astype(F32)) * _dot(u_bd, wbd_ref[:, cols])
                  + _sigmoid(mgmm_ref[:, cols].astype(F32)) * _dot(u_mm, wmm_ref[:, cols]))
        merged_ref[:, cols] = merged.astype(BF16)
    y = _dot(merged_ref[...], wout_ref[...])
    ms = jnp.mean(y * y, axis=-1, keepdims=True)
    y_ref[...] = x_ref[...] + (y * lax.rsqrt(ms + RMS_EPS)) * gpost_ref[...]


def _output_stage(x2d, y2d, o_sb, o_bd, o_mm, w_sb, w_bd, w_mm, w_out, g_post, *, tm):
    m, d = x2d.shape
    half = d // 2
    assert R_MG % d == 0
    mgb = R_MG // d
    const = dict(pipeline_mode=pl.Buffered(1))
    kern = functools.partial(_out_kernel, half=half)
    return pl.pallas_call(
        kern,
        grid=(m // tm,),
        in_specs=[
            pl.BlockSpec((tm, d), lambda i: (i, 0)),
            pl.BlockSpec((tm, SB_WIDTH), lambda i: (i, 0)),
            pl.BlockSpec((tm, BAND_WIDTH), lambda i: (i, 0)),
            pl.BlockSpec((tm, MEM_WIDTH), lambda i: (i, 0)),
            pl.BlockSpec((tm, SB_WIDTH), lambda i: (i, R_SB_G // SB_WIDTH)),
            pl.BlockSpec((tm, BAND_WIDTH), lambda i: (i, R_BD_G // BAND_WIDTH)),
            pl.BlockSpec((tm, MEM_WIDTH), lambda i: (i, R_MM_G // MEM_WIDTH)),
            pl.BlockSpec((tm, d), lambda i: (i, mgb)),
            pl.BlockSpec((tm, d), lambda i: (i, mgb + 1)),
            pl.BlockSpec((tm, d), lambda i: (i, mgb + 2)),
            pl.BlockSpec((SB_WIDTH, d), lambda i: (0, 0), **const),
            pl.BlockSpec((BAND_WIDTH, d), lambda i: (0, 0), **const),
            pl.BlockSpec((MEM_WIDTH, d), lambda i: (0, 0), **const),
            pl.BlockSpec((d, d), lambda i: (0, 0), **const),
            pl.BlockSpec((1, d), lambda i: (0, 0)),
        ],
        out_specs=pl.BlockSpec((tm, d), lambda i: (i, 0)),
        out_shape=jax.ShapeDtypeStruct((m, d), F32),
        scratch_shapes=[pltpu.VMEM((tm, d), BF16)],
        compiler_params=_params(1),
        name="output_stage",
    )(x2d, o_sb, o_bd, o_mm, y2d, y2d, y2d, y2d, y2d, y2d, w_sb, w_bd, w_mm, w_out, g_post.reshape(1, d))


def _head_major(a):
    return jnp.transpose(a, (0, 2, 1, 3))


def kernel(x_prompt, x_sample, cache_sb_k, cache_sb_v, cache_band_k, cache_band_v, cache_mem_k, cache_mem_v, mem_prompt, g_pre, w_in, rel_bias, g_mem, w_mem_kv, w_up_sb, w_up_band, w_up_mem, w_out, g_post):
    depth = w_in.shape[0]
    b, t, d = x_prompt.shape
    bd, n_new, _ = x_sample.shape
    n_mem = mem_prompt.shape[1]
    r_band = cache_band_k.shape[2]
    in_width = w_in.shape[2]
    band_keep = min(BAND_ROWS, t)
    tm_p = 512
    assert COL_MG + 3 * d == in_width
    assert t % SB_TQ == 0 and t % (BAND_TQ * BAND_STEP_GROUPS) == 0 and t % tm_p == 0
    assert r_band == BAND_ROWS and n_new <= CHUNK

    negu2 = jnp.where(jnp.arange(2 * SB_TK)[:, None] % SB_TK >= jnp.arange(SB_TK)[None, :], -1.0, 0.0).astype(BF16)
    act_width = R_MG + 3 * d
    cols = jnp.arange(act_width)
    is_q = (cols < R_SB_G) | ((cols >= R_MM_Q) & (cols < R_MM_G))
    col_scale = jnp.where(is_q, Q_SCALE, 1.0).astype(F32).reshape(1, act_width)

    xp = x_prompt.reshape(b * t, d)
    xs = x_sample.reshape(bd * n_new, d)
    outs = [[] for _ in range(10)]
    for l in range(depth):
        w = w_in[l]
        w_kvp_b = jnp.concatenate([w[:, COL_SB_K:COL_SB_G], w[:, COL_BD_K:COL_BD_G]], axis=1).astype(BF16)
        w_act_b = jnp.concatenate([w[:, COL_SB_Q:COL_SB_K], w[:, COL_BD_Q:COL_BD_K], w[:, COL_SB_G:COL_BD_Q],
                                   w[:, COL_BD_G:COL_MM_Q], w[:, COL_MM_Q:]], axis=1).astype(BF16)
        w_kv_b = w_mem_kv[l].astype(BF16)
        w_sb_b = w_up_sb[l].astype(BF16)
        w_bd_b = w_up_band[l].astype(BF16)
        w_mm_b = w_up_mem[l].astype(BF16)
        w_out_b = w_out[l].astype(BF16)
        bias_p, bias_d = _band_bias_tables(rel_bias[l], n_new, r_band)

        ykv, sbk, sbv, bdk, bdv = _kv_projection(xp, g_pre[l], w_kvp_b, tm=tm_p, seqs=1, n_seq=b)
        bdk = bdk[:, :, t - band_keep:]
        bdv = bdv[:, :, t - band_keep:]
        ya = _act_projection(xp, g_pre[l], w_act_b, col_scale, tm=1024)
        mk, mv = _memory_kv(mem_prompt.reshape(b * n_mem, d), g_mem[l], w_kv_b, tm=n_mem)
        ya3 = ya.reshape(b, t, act_width)
        ykv3 = ykv.reshape(b, t, KV_WIDTH)
        o_sb = _sb_prompt(ya3, ykv3, negu2)
        o_bd = _band_prompt(ya3, ykv3, bias_p)
        o_mm = _mem_attention(ya3, mk.reshape(b, n_mem, MEM_WIDTH), mv.reshape(b, n_mem, MEM_WIDTH), tq=512)
        xp = _output_stage(xp, ya, o_sb.reshape(b * t, -1), o_bd.reshape(b * t, -1),
                           o_mm.reshape(b * t, -1), w_sb_b, w_bd_b, w_mm_b, w_out_b, g_post[l], tm=256)
        outs[0].append(_head_major(sbk))
        outs[1].append(_head_major(sbv))
        outs[2].append(_head_major(bdk))
        outs[3].append(_head_major(bdv))
        outs[4].append(mk.reshape(b, n_mem, MEM_HEADS, HEAD_DIM))
        outs[5].append(mv.reshape(b, n_mem, MEM_HEADS, HEAD_DIM))

        ms = bd * n_new
        ykv_s, sbk2, sbv2, bdk2, bdv2 = _kv_projection(xs, g_pre[l], w_kvp_b, tm=ms, seqs=bd, n_seq=bd)
        ya_s = _act_projection(xs, g_pre[l], w_act_b, col_scale, tm=ms)
        ya_s3 = ya_s.reshape(bd, n_new, act_width)
        ykv_s3 = ykv_s.reshape(bd, n_new, KV_WIDTH)
        o_sb2 = _sb_decode(ya_s3, ykv_s3, _head_major(cache_sb_k[l]), _head_major(cache_sb_v[l]), negu2)
        o_bd2 = _band_decode(ya_s3, ykv_s3, _head_major(cache_band_k[l]), _head_major(cache_band_v[l]), bias_d)
        o_mm2 = _mem_attention(ya_s3, cache_mem_k[l].reshape(bd, n_mem, MEM_WIDTH),
                               cache_mem_v[l].reshape(bd, n_mem, MEM_WIDTH), tq=n_new)
        xs = _output_stage(xs, ya_s, o_sb2.reshape(ms, -1), o_bd2.reshape(ms, -1), o_mm2.reshape(ms, -1),
                           w_sb_b, w_bd_b, w_mm_b, w_out_b, g_post[l], tm=ms)
        outs[6].append(_head_major(sbk2))
        outs[7].append(_head_major(sbv2))
        outs[8].append(_head_major(bdk2))
        outs[9].append(_head_major(bdv2))

    return (xp.reshape(b, t, d), xs.reshape(bd, n_new, d)) + tuple(jnp.stack(o) for o in outs)
```

```python
import functools
import math

import jax
import jax.numpy as jnp
from jax import lax
from jax.experimental import pallas as pl
from jax.experimental.pallas import tpu as pltpu

F32 = jnp.float32
BF16 = jnp.bfloat16

HEAD_DIM = 128
SB_HEADS = 6
BAND_HEADS = 6
MEM_HEADS = 4
SB_WIDTH = SB_HEADS * HEAD_DIM
BAND_WIDTH = BAND_HEADS * HEAD_DIM
MEM_WIDTH = MEM_HEADS * HEAD_DIM
CHUNK = 64
CHUNK_SHIFT = 6
BAND_LEFT_CHUNKS = 8
BAND_ROWS = BAND_LEFT_CHUNKS * CHUNK
MAX_REL = 256
RMS_EPS = 1e-6
NEG_INF = -1e30
LOG2E = math.log2(math.e)
Q_SCALE = HEAD_DIM ** -0.5 * LOG2E

COL_SB_Q = 0
COL_SB_K = COL_SB_Q + SB_WIDTH
COL_SB_V = COL_SB_K + SB_WIDTH
COL_SB_G = COL_SB_V + SB_WIDTH
COL_BD_Q = COL_SB_G + SB_WIDTH
COL_BD_K = COL_BD_Q + BAND_WIDTH
COL_BD_V = COL_BD_K + BAND_WIDTH
COL_BD_G = COL_BD_V + BAND_WIDTH
COL_MM_Q = COL_BD_G + BAND_WIDTH
COL_MM_G = COL_MM_Q + MEM_WIDTH
COL_MG = COL_MM_G + MEM_WIDTH

KV_SB_K = 0
KV_SB_V = KV_SB_K + SB_WIDTH
KV_BD_K = KV_SB_V + SB_WIDTH
KV_BD_V = KV_BD_K + BAND_WIDTH
KV_WIDTH = KV_BD_V + BAND_WIDTH
QG_SB_Q = 0
QG_SB_G = QG_SB_Q + SB_WIDTH
QG_BD_Q = QG_SB_G + SB_WIDTH
QG_BD_G = QG_BD_Q + BAND_WIDTH
QG_WIDTH = QG_BD_G + BAND_WIDTH
MG_MM_Q = 0
MG_MM_G = MG_MM_Q + MEM_WIDTH
MG_MG = MG_MM_G + MEM_WIDTH

VMEM_LIMIT_BYTES = 56 * 1024 * 1024
KV_TN = SB_WIDTH
MG_TN = 1024
SB_TK = 256
SB_TQ = 4 * SB_TK
SB_DEAD = -160.0
BAND_TQ = 4 * CHUNK
BAND_WIN = BAND_TQ + BAND_ROWS
BAND_STEP_GROUPS = 4
BIAS_LANES = 1024
NEW_PAD = 128


def _params(n_axes, vmem=VMEM_LIMIT_BYTES):
    return pltpu.CompilerParams(dimension_semantics=("arbitrary",) * n_axes,
                                vmem_limit_bytes=vmem)


def _nt_dot(a, b):
    return lax.dot_general(a, b, (((1,), (1,)), ((), ())), preferred_element_type=F32)


def _dot(a, b):
    return jnp.dot(a, b, preferred_element_type=F32)


def _pre_norm_to(h_ref, x_ref, g_ref):
    x = x_ref[...]
    ms = jnp.mean(x * x, axis=-1, keepdims=True)
    h_ref[...] = ((x * lax.rsqrt(ms + RMS_EPS)) * g_ref[...]).astype(BF16)


def _kv_proj_kernel(x_ref, g_ref, wsk_ref, wsv_ref, wbk_ref, wbv_ref,
                    y_ref, sbk_ref, sbv_ref, bdk_ref, bdv_ref, h_ref, *, seqs, rows):
    _pre_norm_to(h_ref, x_ref, g_ref)
    groups = ((wsk_ref, sbk_ref), (wsv_ref, sbv_ref), (wbk_ref, bdk_ref), (wbv_ref, bdv_ref))
    for group, (w_ref, dst_ref) in enumerate(groups):
        acc = _dot(h_ref[...], w_ref[...])
        y_ref[:, group * KV_TN:(group + 1) * KV_TN] = acc.astype(BF16)
        for h in range(SB_HEADS):
            for s in range(seqs):
                dst_ref[s, h] = acc[s * rows:(s + 1) * rows, h * HEAD_DIM:(h + 1) * HEAD_DIM]


def _kv_projection(x2d, g_pre, w_in_b, *, tm, seqs, n_seq, band_keep):
    m, d = x2d.shape
    assert KV_WIDTH == 4 * KV_TN and SB_HEADS == BAND_HEADS
    rows = tm // seqs
    seq_rows = m // n_seq
    blocks_per_seq = seq_rows // rows
    assert rows == band_keep
    kern = functools.partial(_kv_proj_kernel, seqs=seqs, rows=rows)
    sb_spec = pl.BlockSpec((seqs, SB_HEADS, rows, HEAD_DIM),
                           lambda i: (i // blocks_per_seq, 0, i % blocks_per_seq, 0))
    bd_spec = pl.BlockSpec((seqs, BAND_HEADS, rows, HEAD_DIM), lambda i: (i // blocks_per_seq, 0, 0, 0))
    sb_shape = jax.ShapeDtypeStruct((n_seq, SB_HEADS, seq_rows, HEAD_DIM), F32)
    bd_shape = jax.ShapeDtypeStruct((n_seq, BAND_HEADS, band_keep, HEAD_DIM), F32)

    def w_spec(col0):
        assert col0 % KV_TN == 0
        return pl.BlockSpec((d, KV_TN), functools.partial(lambda i, c: (0, c), c=col0 // KV_TN),
                            pipeline_mode=pl.Buffered(1))

    return pl.pallas_call(
        kern,
        grid=(m // tm,),
        in_specs=[
            pl.BlockSpec((tm, d), lambda i: (i, 0)),
            pl.BlockSpec((1, d), lambda i: (0, 0)),
            w_spec(COL_SB_K), w_spec(COL_SB_V), w_spec(COL_BD_K), w_spec(COL_BD_V),
        ],
        out_specs=[pl.BlockSpec((tm, KV_WIDTH), lambda i: (i, 0)), sb_spec, sb_spec, bd_spec, bd_spec],
        out_shape=[jax.ShapeDtypeStruct((m, KV_WIDTH), BF16), sb_shape, sb_shape, bd_shape, bd_shape],
        scratch_shapes=[pltpu.VMEM((tm, d), BF16)],
        compiler_params=_params(1),
        name="kv_projection",
    )(x2d, g_pre.reshape(1, d), w_in_b, w_in_b, w_in_b, w_in_b)


def _col_proj_kernel(x_ref, g_ref, w_ref, cs_ref, y_ref, h_ref):
    @pl.when(pl.program_id(1) == 0)
    def _():
        _pre_norm_to(h_ref, x_ref, g_ref)

    y_ref[...] = (_dot(h_ref[...], w_ref[...]) * cs_ref[...]).astype(BF16)


def _col_projection(x2d, g_pre, w_in_b, col_scale, src_block, *, tm, tn, name):
    m, d = x2d.shape
    n = col_scale.shape[1]
    assert n % tn == 0 and m % tm == 0
    return pl.pallas_call(
        _col_proj_kernel,
        grid=(m // tm, n // tn),
        in_specs=[
            pl.BlockSpec((tm, d), lambda i, j: (i, 0)),
            pl.BlockSpec((1, d), lambda i, j: (0, 0)),
            pl.BlockSpec((d, tn), lambda i, j: (0, src_block(j))),
            pl.BlockSpec((1, tn), lambda i, j: (0, j)),
        ],
        out_specs=pl.BlockSpec((tm, tn), lambda i, j: (i, j)),
        out_shape=jax.ShapeDtypeStruct((m, n), BF16),
        scratch_shapes=[pltpu.VMEM((tm, d), BF16)],
        compiler_params=_params(2),
        name=name,
    )(x2d, g_pre.reshape(1, d), w_in_b, col_scale)


def _memkv_kernel(x_ref, g_ref, w_ref, mk_ref, mv_ref):
    x = x_ref[...]
    ms = jnp.mean(x * x, axis=-1, keepdims=True)
    h = ((x * lax.rsqrt(ms + RMS_EPS)) * g_ref[...]).astype(BF16)
    acc = _dot(h, w_ref[...])
    mk_ref[...] = acc[:, :MEM_WIDTH]
    mv_ref[...] = acc[:, MEM_WIDTH:]


def _memory_kv(mem2d, g_mem, w_bf16, *, tm):
    m, d = mem2d.shape
    return pl.pallas_call(
        _memkv_kernel,
        grid=(m // tm,),
        in_specs=[
            pl.BlockSpec((tm, d), lambda i: (i, 0)),
            pl.BlockSpec((1, d), lambda i: (0, 0)),
            pl.BlockSpec((d, 2 * MEM_WIDTH), lambda i: (0, 0)),
        ],
        out_specs=[pl.BlockSpec((tm, MEM_WIDTH), lambda i: (i, 0)),
                   pl.BlockSpec((tm, MEM_WIDTH), lambda i: (i, 0))],
        out_shape=[jax.ShapeDtypeStruct((m, MEM_WIDTH), F32),
                   jax.ShapeDtypeStruct((m, MEM_WIDTH), F32)],
        compiler_params=_params(1),
        name="memory_kv",
    )(mem2d, g_mem.reshape(1, d), w_bf16)


def _neg_suffix_matrix(n):
    row = lax.broadcasted_iota(jnp.int32, (2 * n, n), 0)
    col = lax.broadcasted_iota(jnp.int32, (2 * n, n), 1)
    row = jnp.where(row >= n, row - n, row)
    return jnp.where(row >= col, -1.0, 0.0).astype(BF16)


def _sb_weights(z2, carry2, negu2, mask):
    p = jnp.maximum(z2, 0.0) + jnp.log(1.0 + jnp.exp2(-jnp.abs(z2))) * LOG2E
    if mask is not None:
        p = jnp.where(mask, p, 0.0)
    p_hi = p.astype(BF16)
    p_lo = (p - p_hi.astype(F32)).astype(BF16)
    suffix = _dot(jnp.concatenate([p_hi, p_lo], axis=1), negu2)
    w = jnp.exp2(z2 + suffix + carry2)
    if mask is not None:
        w = jnp.where(mask, w, 0.0)
    return w, carry2 - jnp.sum(p, axis=-1, keepdims=True)


def _sb_prompt_kernel(q_ref, k_ref, v_ref, negu2_ref, o_ref, acc_ref, carry_ref, kpad_ref, vpad_ref, *, t):
    i = pl.program_id(2)
    n_sub = SB_TQ // SB_TK
    negu2 = negu2_ref[...]

    @pl.when(i == 0)
    def _():
        kpad_ref[0:SB_TK, :] = jnp.zeros((SB_TK, HEAD_DIM), BF16)
        vpad_ref[0:SB_TK, :] = jnp.zeros((SB_TK, HEAD_DIM), BF16)
        kpad_ref[SB_TK:SB_TK + t, :] = k_ref[...]
        vpad_ref[SB_TK:SB_TK + t, :] = v_ref[...]

    def kv_block(j):
        start = pl.multiple_of((j + 1) * SB_TK, SB_TK)
        return kpad_ref[pl.ds(start, SB_TK), :], vpad_ref[pl.ds(start, SB_TK), :]

    row = lax.broadcasted_iota(jnp.int32, (SB_TK, SB_TK), 0)
    col = lax.broadcasted_iota(jnp.int32, (SB_TK, SB_TK), 1)
    for r in range(n_sub):
        rows = slice(r * SB_TK, (r + 1) * SB_TK)
        s = i * n_sub + r
        q = q_ref[rows, :]
        kb, vb = kv_block(s)
        w, carry = _sb_weights(_nt_dot(q, kb), jnp.zeros((SB_TK, 1), F32), negu2, col < row)
        acc = _dot(w.astype(BF16), vb)
        kb, vb = kv_block(s - 1)
        prev_exists = None if r > 0 else (jnp.zeros((SB_TK, SB_TK), jnp.int32) + i) > 0
        w, carry = _sb_weights(_nt_dot(q, kb), carry, negu2, prev_exists)
        acc_ref[rows, :] = acc + _dot(w.astype(BF16), vb)
        carry_ref[rows, :] = carry

    row_q = lax.broadcasted_iota(jnp.int32, (SB_TQ, 1), 0)
    row_t = lax.broadcasted_iota(jnp.int32, (SB_TQ, SB_TK), 0)
    has_more = row_q >= (2 - n_sub * i) * SB_TK

    def any_alive(carry):
        return (jnp.max(jnp.where(has_more, carry, NEG_INF)) > SB_DEAD).astype(jnp.int32)

    def cond(state):
        j, alive = state
        return jnp.logical_and(j >= 0, alive > 0)

    def body(state):
        j, _ = state
        kb, vb = kv_block(j)
        visits = row_t >= (j - n_sub * i + 2) * SB_TK
        w, carry = _sb_weights(_nt_dot(q_ref[...], kb), carry_ref[...], negu2, visits)
        acc_ref[...] += _dot(w.astype(BF16), vb)
        carry_ref[...] = carry
        return j - 1, any_alive(carry)

    lax.while_loop(cond, body, (n_sub * i + n_sub - 3, any_alive(carry_ref[...])))
    o_ref[...] = acc_ref[...].astype(BF16)


def _sb_prompt(ya3, ykv3, negu2):
    b, t, _ = ya3.shape
    qb, kb, vb = QG_SB_Q // HEAD_DIM, KV_SB_K // HEAD_DIM, KV_SB_V // HEAD_DIM
    assert SB_TQ // SB_TK >= 3
    kern = functools.partial(_sb_prompt_kernel, t=t)
    return pl.pallas_call(
        kern,
        grid=(b, SB_HEADS, t // SB_TQ),
        in_specs=[
            pl.BlockSpec((None, SB_TQ, HEAD_DIM), lambda b, h, i: (b, i, qb + h)),
            pl.BlockSpec((None, t, HEAD_DIM), lambda b, h, i: (b, 0, kb + h)),
            pl.BlockSpec((None, t, HEAD_DIM), lambda b, h, i: (b, 0, vb + h)),
            pl.BlockSpec((2 * SB_TK, SB_TK), lambda b, h, i: (0, 0)),
        ],
        out_specs=pl.BlockSpec((None, SB_TQ, HEAD_DIM), lambda b, h, i: (b, i, h)),
        out_shape=jax.ShapeDtypeStruct((b, t, SB_WIDTH), BF16),
        scratch_shapes=[pltpu.VMEM((SB_TQ, HEAD_DIM), F32), pltpu.VMEM((SB_TQ, 1), F32),
                        pltpu.VMEM((SB_TK + t, HEAD_DIM), BF16), pltpu.VMEM((SB_TK + t, HEAD_DIM), BF16)],
        compiler_params=_params(3),
        name="sb_prompt",
    )(ya3, ykv3, ykv3, negu2)


def _sb_decode_kernel(q_ref, kn_ref, vn_ref, kc_hbm, vc_hbm, negu2_ref, o_ref,
                      acc_ref, carry_ref, kpad_ref, vpad_ref, kbuf_ref, vbuf_ref, sem,
                      *, n_new, n_blocks):
    b = pl.program_id(0)
    heads = SB_HEADS

    def cache_copies(j, slot):
        rows = pl.ds(pl.multiple_of((n_blocks - 1 - j) * SB_TK, SB_TK), SB_TK)
        return (pltpu.make_async_copy(kc_hbm.at[b, :, rows, :], kbuf_ref.at[slot], sem.at[0, slot]),
                pltpu.make_async_copy(vc_hbm.at[b, :, rows, :], vbuf_ref.at[slot], sem.at[1, slot]))

    def start_fetch(j, slot):
        for cp in cache_copies(j, slot):
            cp.start()

    def wait_fetch(j, slot):
        for cp in cache_copies(j, slot):
            cp.wait()

    start_fetch(0, 0)

    def head_cols(h):
        return slice(h * HEAD_DIM, (h + 1) * HEAD_DIM)

    def head_rows(h):
        return slice(h * n_new, (h + 1) * n_new)

    def block(k_of, v_of, negu2, mask):
        z2 = jnp.concatenate([_nt_dot(q_ref[:, head_cols(h)], k_of(h)) for h in range(heads)], axis=0)
        w, carry = _sb_weights(z2, carry_ref[...], negu2, mask)
        wb = w.astype(BF16)
        for h in range(heads):
            acc_ref[head_rows(h), :] += _dot(wb[head_rows(h), :], v_of(h))
        carry_ref[...] = carry

    def any_alive():
        return (jnp.max(carry_ref[...]) > SB_DEAD).astype(jnp.int32)

    acc_ref[...] = jnp.zeros_like(acc_ref)
    carry_ref[...] = jnp.zeros_like(carry_ref)
    kpad_ref[...] = jnp.zeros_like(kpad_ref)
    vpad_ref[...] = jnp.zeros_like(vpad_ref)
    for h in range(heads):
        kpad_ref[h, 0:n_new, :] = kn_ref[:, head_cols(h)]
        vpad_ref[h, 0:n_new, :] = vn_ref[:, head_cols(h)]
    row = lax.broadcasted_iota(jnp.int32, (n_new, NEW_PAD), 0)
    col = lax.broadcasted_iota(jnp.int32, (n_new, NEW_PAD), 1)
    mask = jnp.concatenate([(col < row).astype(jnp.int32)] * heads, axis=0) == 1
    block(lambda h: kpad_ref[h], lambda h: vpad_ref[h], _neg_suffix_matrix(NEW_PAD), mask)

    negu2 = negu2_ref[...]

    def cond(state):
        j, alive = state
        return jnp.logical_and(j < n_blocks, alive > 0)

    def body(state):
        j, _ = state
        slot = j % 2
        wait_fetch(j, slot)

        @pl.when(j + 1 < n_blocks)
        def _():
            start_fetch(j + 1, 1 - slot)

        block(lambda h: kbuf_ref[slot, h].astype(BF16), lambda h: vbuf_ref[slot, h].astype(BF16),
              negu2, None)
        return j + 1, any_alive()

    j_end, _ = lax.while_loop(cond, body, (0, any_alive()))

    @pl.when(j_end < n_blocks)
    def _():
        wait_fetch(j_end, j_end % 2)

    for h in range(heads):
        o_ref[:, head_cols(h)] = acc_ref[head_rows(h), :].astype(BF16)


def _sb_decode(ya3, ykv3, cache_k, cache_v, negu2):
    bd, n_new, _ = ya3.shape
    past = cache_k.shape[2]
    assert past % SB_TK == 0 and n_new <= NEW_PAD and n_new % 16 == 0
    kern = functools.partial(_sb_decode_kernel, n_new=n_new, n_blocks=past // SB_TK)
    return pl.pallas_call(
        kern,
        grid=(bd,),
        in_specs=[
            pl.BlockSpec((None, n_new, SB_WIDTH), lambda b: (b, 0, QG_SB_Q // SB_WIDTH)),
            pl.BlockSpec((None, n_new, SB_WIDTH), lambda b: (b, 0, KV_SB_K // SB_WIDTH)),
            pl.BlockSpec((None, n_new, SB_WIDTH), lambda b: (b, 0, KV_SB_V // SB_WIDTH)),
            pl.BlockSpec(memory_space=pl.ANY),
            pl.BlockSpec(memory_space=pl.ANY),
            pl.BlockSpec((2 * SB_TK, SB_TK), lambda b: (0, 0)),
        ],
        out_specs=pl.BlockSpec((None, n_new, SB_WIDTH), lambda b: (b, 0, 0)),
        out_shape=jax.ShapeDtypeStruct((bd, n_new, SB_WIDTH), BF16),
        scratch_shapes=[pltpu.VMEM((SB_HEADS * n_new, HEAD_DIM), F32),
                        pltpu.VMEM((SB_HEADS * n_new, 1), F32),
                        pltpu.VMEM((SB_HEADS, NEW_PAD, HEAD_DIM), BF16),
                        pltpu.VMEM((SB_HEADS, NEW_PAD, HEAD_DIM), BF16),
                        pltpu.VMEM((2, SB_HEADS, SB_TK, HEAD_DIM), F32),
                        pltpu.VMEM((2, SB_HEADS, SB_TK, HEAD_DIM), F32),
                        pltpu.SemaphoreType.DMA((2, 2))],
        compiler_params=_params(1),
        name="sb_decode",
    )(ya3, ykv3, ykv3, cache_k, cache_v, negu2)


def _softmax2_pv(parts):
    mx = functools.reduce(jnp.maximum, [jnp.max(s, axis=-1, keepdims=True) for s, _ in parts])
    num = None
    den = None
    for s, v in parts:
        p = jnp.exp2(s - mx)
        d = jnp.sum(p, axis=-1, keepdims=True)
        o = _dot(p.astype(BF16), v)
        num = o if num is None else num + o
        den = d if den is None else den + d
    return num / den


def _band_bias_kernel(g_ref, tp_ref, td_ref, *, n_new, r_band):
    x = jnp.broadcast_to(g_ref[...], (BAND_TQ, BIAS_LANES))
    row = lax.broadcasted_iota(jnp.int32, (BAND_TQ, BIAS_LANES), 0)
    x = pltpu.roll(x, BAND_TQ, 1)
    for bit in range(BAND_TQ.bit_length() - 1):
        x = jnp.where(((row >> bit) & 1) == 1, pltpu.roll(x, 1 << bit, 1), x)
    tbl = x[:, :BAND_WIN] * LOG2E
    r = lax.broadcasted_iota(jnp.int32, (BAND_TQ, BAND_WIN), 0)
    j = lax.broadcasted_iota(jnp.int32, (BAND_TQ, BAND_WIN), 1)
    dc = (j >> CHUNK_SHIFT) - (r >> CHUNK_SHIFT)
    tp_ref[...] = jnp.where(jnp.logical_and(dc >= 0, dc <= BAND_LEFT_CHUNKS), tbl, NEG_INF)
    jd = lax.broadcasted_iota(jnp.int32, (n_new, r_band + NEW_PAD), 1)
    td_ref[...] = jnp.where(jd < r_band + n_new, tbl[:n_new, :r_band + NEW_PAD], NEG_INF)


def _band_bias_tables(rel_bias, n_new, r_band):
    h = rel_bias.shape[0]
    assert BAND_ROWS == 2 * MAX_REL and r_band == BAND_ROWS and BIAS_LANES == 2 * BAND_ROWS
    rb = rel_bias.astype(F32)
    g = jnp.concatenate([rb[:, :0:-1], jnp.broadcast_to(rb[:, -1:], (h, BIAS_LANES - 2 * MAX_REL))], axis=1)
    kern = functools.partial(_band_bias_kernel, n_new=n_new, r_band=r_band)
    return pl.pallas_call(
        kern,
        grid=(h,),
        in_specs=[pl.BlockSpec((None, 1, BIAS_LANES), lambda i: (i, 0, 0))],
        out_specs=[pl.BlockSpec((None, BAND_TQ, BAND_WIN), lambda i: (i, 0, 0)),
                   pl.BlockSpec((None, n_new, r_band + NEW_PAD), lambda i: (i, 0, 0))],
        out_shape=[jax.ShapeDtypeStruct((h, BAND_TQ, BAND_WIN), F32),
                   jax.ShapeDtypeStruct((h, n_new, r_band + NEW_PAD), F32)],
        compiler_params=_params(1),
        name="band_bias",
    )(g.reshape(h, 1, BIAS_LANES))


def _band_prompt_kernel(q_ref, k_ref, v_ref, bias_ref, o_ref, kpad_ref, vpad_ref, *, t):
    s_idx = pl.program_id(2)

    @pl.when(s_idx == 0)
    def _():
        kpad_ref[0:BAND_ROWS, :] = jnp.zeros((BAND_ROWS, HEAD_DIM), BF16)
        vpad_ref[0:BAND_ROWS, :] = jnp.zeros((BAND_ROWS, HEAD_DIM), BF16)
        kpad_ref[BAND_ROWS:BAND_ROWS + t, :] = k_ref[...]
        vpad_ref[BAND_ROWS:BAND_ROWS + t, :] = v_ref[...]

    col = lax.broadcasted_iota(jnp.int32, (BAND_TQ, BAND_WIN), 1)
    for gg in range(BAND_STEP_GROUPS):
        g = s_idx * BAND_STEP_GROUPS + gg
        start = pl.multiple_of(g * BAND_TQ, BAND_TQ)
        rows = slice(gg * BAND_TQ, (gg + 1) * BAND_TQ)
        s = _nt_dot(q_ref[rows, :], kpad_ref[pl.ds(start, BAND_WIN), :]) + bias_ref[...]
        s = jnp.where(col + g * BAND_TQ >= BAND_ROWS, s, NEG_INF)
        o_ref[rows, :] = _softmax2_pv([(s, vpad_ref[pl.ds(start, BAND_WIN), :])]).astype(BF16)


def _band_prompt(ya3, ykv3, bias_tbl):
    b, t, _ = ya3.shape
    tq = BAND_TQ * BAND_STEP_GROUPS
    qb, kb, vb = QG_BD_Q // HEAD_DIM, KV_BD_K // HEAD_DIM, KV_BD_V // HEAD_DIM
    kern = functools.partial(_band_prompt_kernel, t=t)
    return pl.pallas_call(
        kern,
        grid=(b, BAND_HEADS, t // tq),
        in_specs=[
            pl.BlockSpec((None, tq, HEAD_DIM), lambda b, h, g: (b, g, qb + h)),
            pl.BlockSpec((None, t, HEAD_DIM), lambda b, h, g: (b, 0, kb + h)),
            pl.BlockSpec((None, t, HEAD_DIM), lambda b, h, g: (b, 0, vb + h)),
            pl.BlockSpec((None, BAND_TQ, BAND_WIN), lambda b, h, g: (h, 0, 0)),
        ],
        out_specs=pl.BlockSpec((None, tq, HEAD_DIM), lambda b, h, g: (b, g, h)),
        out_shape=jax.ShapeDtypeStruct((b, t, BAND_WIDTH), BF16),
        scratch_shapes=[pltpu.VMEM((BAND_ROWS + t, HEAD_DIM), BF16),
                        pltpu.VMEM((BAND_ROWS + t, HEAD_DIM), BF16)],
        compiler_params=_params(3),
        name="band_prompt",
    )(ya3, ykv3, ykv3, bias_tbl)


def _band_decode_kernel(q_ref, kn_ref, vn_ref, kc_ref, vc_ref, bias_ref, o_ref, kpad_ref, vpad_ref,
                        *, n_new, r_band):
    kpad_ref[...] = jnp.zeros_like(kpad_ref)
    vpad_ref[...] = jnp.zeros_like(vpad_ref)
    for h in range(BAND_HEADS):
        cols = slice(h * HEAD_DIM, (h + 1) * HEAD_DIM)
        kpad_ref[h, 0:n_new, :] = kn_ref[:, cols]
        vpad_ref[h, 0:n_new, :] = vn_ref[:, cols]
    for h in range(BAND_HEADS):
        cols = slice(h * HEAD_DIM, (h + 1) * HEAD_DIM)
        q = q_ref[:, cols]
        s_cache = _nt_dot(q, kc_ref[h].astype(BF16)) + bias_ref[h, :, 0:r_band]
        s_new = _nt_dot(q, kpad_ref[h]) + bias_ref[h, :, r_band:r_band + NEW_PAD]
        o_ref[:, cols] = _softmax2_pv([(s_cache, vc_ref[h].astype(BF16)),
                                       (s_new, vpad_ref[h])]).astype(BF16)


def _band_decode(ya3, ykv3, cache_k, cache_v, bias_tbl):
    bd, n_new, _ = ya3.shape
    r_band = cache_k.shape[2]
    kern = functools.partial(_band_decode_kernel, n_new=n_new, r_band=r_band)
    cache_spec = pl.BlockSpec((None, BAND_HEADS, r_band, HEAD_DIM), lambda b: (b, 0, 0, 0))
    return pl.pallas_call(
        kern,
        grid=(bd,),
        in_specs=[
            pl.BlockSpec((None, n_new, BAND_WIDTH), lambda b: (b, 0, QG_BD_Q // BAND_WIDTH)),
            pl.BlockSpec((None, n_new, BAND_WIDTH), lambda b: (b, 0, KV_BD_K // BAND_WIDTH)),
            pl.BlockSpec((None, n_new, BAND_WIDTH), lambda b: (b, 0, KV_BD_V // BAND_WIDTH)),
            cache_spec,
            cache_spec,
            pl.BlockSpec((BAND_HEADS, n_new, r_band + NEW_PAD), lambda b: (0, 0, 0)),
        ],
        out_specs=pl.BlockSpec((None, n_new, BAND_WIDTH), lambda b: (b, 0, 0)),
        out_shape=jax.ShapeDtypeStruct((bd, n_new, BAND_WIDTH), BF16),
        scratch_shapes=[pltpu.VMEM((BAND_HEADS, NEW_PAD, HEAD_DIM), BF16),
                        pltpu.VMEM((BAND_HEADS, NEW_PAD, HEAD_DIM), BF16)],
        compiler_params=_params(1),
        name="band_decode",
    )(ya3, ykv3, ykv3, cache_k, cache_v, bias_tbl)


def _mem_attn_kernel(q_ref, mk_ref, mv_ref, o_ref):
    for h in range(MEM_HEADS):
        sl = slice(h * HEAD_DIM, (h + 1) * HEAD_DIM)
        s = _nt_dot(q_ref[:, sl], mk_ref[:, sl].astype(BF16))
        o_ref[:, sl] = _softmax2_pv([(s, mv_ref[:, sl].astype(BF16))]).astype(BF16)


def _mem_attention(y3, mk, mv, *, tq):
    b, t, _ = y3.shape
    n_mem = mk.shape[1]
    qb = MG_MM_Q // MEM_WIDTH
    return pl.pallas_call(
        _mem_attn_kernel,
        grid=(b, t // tq),
        in_specs=[
            pl.BlockSpec((None, tq, MEM_WIDTH), lambda b, i: (b, i, qb)),
            pl.BlockSpec((None, n_mem, MEM_WIDTH), lambda b, i: (b, 0, 0)),
            pl.BlockSpec((None, n_mem, MEM_WIDTH), lambda b, i: (b, 0, 0)),
        ],
        out_specs=pl.BlockSpec((None, tq, MEM_WIDTH), lambda b, i: (b, i, 0)),
        out_shape=jax.ShapeDtypeStruct((b, t, MEM_WIDTH), BF16),
        compiler_params=_params(2),
        name="mem_attention",
    )(y3, mk, mv)


def _silu_of_half(h):
    return h + h * jnp.tanh(h)


def _merge_kernel(osb_ref, obd_ref, omm_ref, gsb_ref, gbd_ref, gmm_ref,
                  mg0_ref, mg1_ref, mg2_ref, mg3_ref, mg4_ref, mg5_ref,
                  wsb_ref, wbd_ref, wmm_ref, merged_ref, *, half):
    u_sb = (osb_ref[...].astype(F32) * _silu_of_half(gsb_ref[...].astype(F32))).astype(BF16)
    u_bd = (obd_ref[...].astype(F32) * _silu_of_half(gbd_ref[...].astype(F32))).astype(BF16)
    u_mm = (omm_ref[...].astype(F32) * _silu_of_half(gmm_ref[...].astype(F32))).astype(BF16)
    mg = ((mg0_ref, mg2_ref, mg4_ref), (mg1_ref, mg3_ref, mg5_ref))
    for n in range(2):
        cols = slice(n * half, (n + 1) * half)
        merged = None
        for m_ref, u, w_ref in zip(mg[n], (u_sb, u_bd, u_mm), (wsb_ref, wbd_ref, wmm_ref)):
            a = _dot(u, w_ref[:, cols])
            term = a + a * jnp.tanh(m_ref[...].astype(F32))
            merged = term if merged is None else merged + term
        merged_ref[:, cols] = merged.astype(BF16)


def _merge_branches(yqg, ymg, o_sb, o_bd, o_mm, w_sb, w_bd, w_mm, *, tm):
    m = yqg.shape[0]
    d = w_sb.shape[1]
    half = d // 2
    assert MG_MG % half == 0
    mgb = MG_MG // half
    const = dict(pipeline_mode=pl.Buffered(1))
    kern = functools.partial(_merge_kernel, half=half)
    return pl.pallas_call(
        kern,
        grid=(m // tm,),
        in_specs=[
            pl.BlockSpec((tm, SB_WIDTH), lambda i: (i, 0)),
            pl.BlockSpec((tm, BAND_WIDTH), lambda i: (i, 0)),
            pl.BlockSpec((tm, MEM_WIDTH), lambda i: (i, 0)),
            pl.BlockSpec((tm, SB_WIDTH), lambda i: (i, QG_SB_G // SB_WIDTH)),
            pl.BlockSpec((tm, BAND_WIDTH), lambda i: (i, QG_BD_G // BAND_WIDTH)),
            pl.BlockSpec((tm, MEM_WIDTH), lambda i: (i, MG_MM_G // MEM_WIDTH)),
        ] + [pl.BlockSpec((tm, half), functools.partial(lambda i, c: (i, c), c=mgb + c)) for c in range(6)] + [
            pl.BlockSpec((SB_WIDTH, d), lambda i: (0, 0), **const),
            pl.BlockSpec((BAND_WIDTH, d), lambda i: (0, 0), **const),
            pl.BlockSpec((MEM_WIDTH, d), lambda i: (0, 0), **const),
        ],
        out_specs=pl.BlockSpec((tm, d), lambda i: (i, 0)),
        out_shape=jax.ShapeDtypeStruct((m, d), BF16),
        compiler_params=_params(1),
        name="merge_branches",
    )(o_sb, o_bd, o_mm, yqg, yqg, ymg, *([ymg] * 6), w_sb, w_bd, w_mm)


def _out_proj_kernel(x_ref, merged_ref, wout_ref, gpost_ref, y_ref):
    y = _dot(merged_ref[...], wout_ref[...])
    ms = jnp.mean(y * y, axis=-1, keepdims=True)
    y_ref[...] = x_ref[...] + (y * lax.rsqrt(ms + RMS_EPS)) * gpost_ref[...]


def _out_projection(x2d, merged, w_out, g_post, *, tm):
    m, d = x2d.shape
    return pl.pallas_call(
        _out_proj_kernel,
        grid=(m // tm,),
        in_specs=[
            pl.BlockSpec((tm, d), lambda i: (i, 0)),
            pl.BlockSpec((tm, d), lambda i: (i, 0)),
            pl.BlockSpec((d, d), lambda i: (0, 0), pipeline_mode=pl.Buffered(1)),
            pl.BlockSpec((1, d), lambda i: (0, 0)),
        ],
        out_specs=pl.BlockSpec((tm, d), lambda i: (i, 0)),
        out_shape=jax.ShapeDtypeStruct((m, d), F32),
        compiler_params=_params(1),
        name="out_projection",
    )(x2d, merged, w_out, g_post.reshape(1, d))


def _head_major(a):
    return jnp.transpose(a, (0, 2, 1, 3))


def kernel(x_prompt, x_sample, cache_sb_k, cache_sb_v, cache_band_k, cache_band_v, cache_mem_k, cache_mem_v, mem_prompt, g_pre, w_in, rel_bias, g_mem, w_mem_kv, w_up_sb, w_up_band, w_up_mem, w_out, g_post):
    depth = w_in.shape[0]
    b, t, d = x_prompt.shape
    bd, n_new, _ = x_sample.shape
    n_mem = mem_prompt.shape[1]
    r_band = cache_band_k.shape[2]
    in_width = w_in.shape[2]
    band_keep = min(BAND_ROWS, t)
    tm_p = 512
    assert COL_MG + 3 * d == in_width
    assert t % SB_TQ == 0 and t % (BAND_TQ * BAND_STEP_GROUPS) == 0 and t % tm_p == 0
    assert r_band == BAND_ROWS and n_new <= CHUNK

    negu2 = jnp.where(jnp.arange(2 * SB_TK)[:, None] % SB_TK >= jnp.arange(SB_TK)[None, :], -1.0, 0.0).astype(BF16)
    mg_width = in_width - COL_MM_Q
    assert COL_MM_Q % MG_TN == 0 and mg_width % MG_TN == 0 and COL_SB_G == 3 * KV_TN and COL_BD_Q == 4 * KV_TN
    cols = jnp.arange(QG_WIDTH)
    qg_scale = jnp.where((cols // SB_WIDTH) % 2 == 0, Q_SCALE, 0.5).astype(F32).reshape(1, QG_WIDTH)
    cols = jnp.arange(mg_width)
    mg_scale = jnp.where(cols < MG_MM_G, Q_SCALE, 0.5).astype(F32).reshape(1, mg_width)
    qg_block = lambda j: j + 2 * ((j + 1) // 2)
    mg_block = lambda j: j + COL_MM_Q // MG_TN

    def projections(x2d, l, w_in_b, *, tm_kv, tm_act, seqs, n_seq, band_keep):
        ykv, sbk, sbv, bdk, bdv = _kv_projection(x2d, g_pre[l], w_in_b, tm=tm_kv, seqs=seqs, n_seq=n_seq,
                                                 band_keep=band_keep)
        yqg = _col_projection(x2d, g_pre[l], w_in_b, qg_scale, qg_block, tm=tm_act, tn=KV_TN,
                              name="qg_projection")
        ymg = _col_projection(x2d, g_pre[l], w_in_b, mg_scale, mg_block, tm=tm_act, tn=MG_TN,
                              name="mg_projection")
        return ykv, yqg, ymg, sbk, sbv, bdk, bdv

    xp = x_prompt.reshape(b * t, d)
    xs = x_sample.reshape(bd * n_new, d)
    outs = [[] for _ in range(10)]
    for l in range(depth):
        w_in_b = w_in[l].astype(BF16)
        w_kv_b = w_mem_kv[l].astype(BF16)
        w_sb_b = (0.5 * w_up_sb[l]).astype(BF16)
        w_bd_b = (0.5 * w_up_band[l]).astype(BF16)
        w_mm_b = (0.5 * w_up_mem[l]).astype(BF16)
        w_out_b = w_out[l].astype(BF16)
        bias_p, bias_d = _band_bias_tables(rel_bias[l], n_new, r_band)

        ykv, yqg, ymg, sbk, sbv, bdk, bdv = projections(xp, l, w_in_b, tm_kv=band_keep, tm_act=1024,
                                                        seqs=1, n_seq=b, band_keep=band_keep)
        mk, mv = _memory_kv(mem_prompt.reshape(b * n_mem, d), g_mem[l], w_kv_b, tm=n_mem)
        yqg3 = yqg.reshape(b, t, QG_WIDTH)
        ykv3 = ykv.reshape(b, t, KV_WIDTH)
        o_sb = _sb_prompt(yqg3, ykv3, negu2)
        o_bd = _band_prompt(yqg3, ykv3, bias_p)
        o_mm = _mem_attention(ymg.reshape(b, t, mg_width), mk.reshape(b, n_mem, MEM_WIDTH),
                              mv.reshape(b, n_mem, MEM_WIDTH), tq=512)
        merged = _merge_branches(yqg, ymg, o_sb.reshape(b * t, -1), o_bd.reshape(b * t, -1),
                                 o_mm.reshape(b * t, -1), w_sb_b, w_bd_b, w_mm_b, tm=512)
        xp = _out_projection(xp, merged, w_out_b, g_post[l], tm=512)
        outs[0].append(_head_major(sbk))
        outs[1].append(_head_major(sbv))
        outs[2].append(_head_major(bdk))
        outs[3].append(_head_major(bdv))
        outs[4].append(mk.reshape(b, n_mem, MEM_HEADS, HEAD_DIM))
        outs[5].append(mv.reshape(b, n_mem, MEM_HEADS, HEAD_DIM))

        ms = bd * n_new
        ykv_s, yqg_s, ymg_s, sbk2, sbv2, bdk2, bdv2 = projections(xs, l, w_in_b, tm_kv=ms, tm_act=ms,
                                                                  seqs=bd, n_seq=bd, band_keep=n_new)
        yqg_s3 = yqg_s.reshape(bd, n_new, QG_WIDTH)
        ykv_s3 = ykv_s.reshape(bd, n_new, KV_WIDTH)
        o_sb2 = _sb_decode(yqg_s3, ykv_s3, _head_major(cache_sb_k[l]), _head_major(cache_sb_v[l]), negu2)
        o_bd2 = _band_decode(yqg_s3, ykv_s3, _head_major(cache_band_k[l]), _head_major(cache_band_v[l]), bias_d)
        o_mm2 = _mem_attention(ymg_s.reshape(bd, n_new, mg_width), cache_mem_k[l].reshape(bd, n_mem, MEM_WIDTH),
                               cache_mem_v[l].reshape(bd, n_mem, MEM_WIDTH), tq=n_new)
        merged_s = _merge_branches(yqg_s, ymg_s, o_sb2.reshape(ms, -1), o_bd2.reshape(ms, -1),
                                   o_mm2.reshape(ms, -1), w_sb_b, w_bd_b, w_mm_b, tm=ms)
        xs = _out_projection(xs, merged_s, w_out_b, g_post[l], tm=ms)
        outs[6].append(_head_major(sbk2))
        outs[7].append(_head_major(sbv2))
        outs[8].append(_head_major(bdk2))
        outs[9].append(_head_major(bdv2))

    return (xp.reshape(b, t, d), xs.reshape(bd, n_new, d)) + tuple(jnp.stack(o) for o in outs)
```

```python
import functools
import math

import jax
import jax.numpy as jnp
from jax import lax
from jax.experimental import pallas as pl
from jax.experimental.pallas import tpu as pltpu

F32 = jnp.float32
BF16 = jnp.bfloat16

HEAD_DIM = 128
SB_HEADS = 6
BAND_HEADS = 6
MEM_HEADS = 4
SB_WIDTH = SB_HEADS * HEAD_DIM
BAND_WIDTH = BAND_HEADS * HEAD_DIM
MEM_WIDTH = MEM_HEADS * HEAD_DIM
CHUNK = 64
CHUNK_SHIFT = 6
BAND_LEFT_CHUNKS = 8
BAND_ROWS = BAND_LEFT_CHUNKS * CHUNK
MAX_REL = 256
RMS_EPS = 1e-6
NEG_INF = -1e30
LOG2E = math.log2(math.e)
Q_SCALE = HEAD_DIM ** -0.5 * LOG2E

COL_SB_Q = 0
COL_SB_K = COL_SB_Q + SB_WIDTH
COL_SB_V = COL_SB_K + SB_WIDTH
COL_SB_G = COL_SB_V + SB_WIDTH
COL_BD_Q = COL_SB_G + SB_WIDTH
COL_BD_K = COL_BD_Q + BAND_WIDTH
COL_BD_V = COL_BD_K + BAND_WIDTH
COL_BD_G = COL_BD_V + BAND_WIDTH
COL_MM_Q = COL_BD_G + BAND_WIDTH
COL_MM_G = COL_MM_Q + MEM_WIDTH
COL_MG = COL_MM_G + MEM_WIDTH

KV_SB_K = 0
KV_SB_V = KV_SB_K + SB_WIDTH
KV_BD_K = KV_SB_V + SB_WIDTH
KV_BD_V = KV_BD_K + BAND_WIDTH
KV_WIDTH = KV_BD_V + BAND_WIDTH
QG_SB_Q = 0
QG_SB_G = QG_SB_Q + SB_WIDTH
QG_BD_Q = QG_SB_G + SB_WIDTH
QG_BD_G = QG_BD_Q + BAND_WIDTH
QG_WIDTH = QG_BD_G + BAND_WIDTH
MG_MM_Q = 0
MG_MM_G = MG_MM_Q + MEM_WIDTH
MG_MG = MG_MM_G + MEM_WIDTH

VMEM_LIMIT_BYTES = 56 * 1024 * 1024
KV_TN = SB_WIDTH
MG_TN = 1024
SB_TK = 256
SB_TQ = 4 * SB_TK
SB_DEAD = -160.0
BAND_TQ = 4 * CHUNK
BAND_WIN = BAND_TQ + BAND_ROWS
BAND_STEP_GROUPS = 4
BIAS_LANES = 1024
NEW_PAD = 128


def _params(n_axes, vmem=VMEM_LIMIT_BYTES):
    return pltpu.CompilerParams(dimension_semantics=("arbitrary",) * n_axes,
                                vmem_limit_bytes=vmem)


def _nt_dot(a, b):
    return lax.dot_general(a, b, (((1,), (1,)), ((), ())), preferred_element_type=F32)


def _dot(a, b):
    return jnp.dot(a, b, preferred_element_type=F32)


def _pre_norm_to(h_ref, x_ref, g_ref):
    x = x_ref[...]
    ms = jnp.mean(x * x, axis=-1, keepdims=True)
    h_ref[...] = ((x * lax.rsqrt(ms + RMS_EPS)) * g_ref[...]).astype(BF16)


def _kv_proj_kernel(x_ref, g_ref, wsk_ref, wsv_ref, wbk_ref, wbv_ref,
                    y_ref, h_ref, sbk_ref, sbv_ref, bdk_ref, bdv_ref, *, seqs, rows):
    _pre_norm_to(h_ref, x_ref, g_ref)
    groups = ((wsk_ref, sbk_ref), (wsv_ref, sbv_ref), (wbk_ref, bdk_ref), (wbv_ref, bdv_ref))
    for group, (w_ref, dst_ref) in enumerate(groups):
        acc = _dot(h_ref[...], w_ref[...])
        y_ref[:, group * KV_TN:(group + 1) * KV_TN] = acc.astype(BF16)
        for h in range(SB_HEADS):
            for s in range(seqs):
                dst_ref[s, h] = acc[s * rows:(s + 1) * rows, h * HEAD_DIM:(h + 1) * HEAD_DIM]


def _kv_projection(x2d, g_pre, w_kv_b, *, tm, seqs, n_seq, band_keep):
    m, d = x2d.shape
    assert KV_WIDTH == 4 * KV_TN and SB_HEADS == BAND_HEADS
    rows = tm // seqs
    seq_rows = m // n_seq
    blocks_per_seq = seq_rows // rows
    assert rows == band_keep
    kern = functools.partial(_kv_proj_kernel, seqs=seqs, rows=rows)
    sb_spec = pl.BlockSpec((seqs, SB_HEADS, rows, HEAD_DIM),
                           lambda i: (i // blocks_per_seq, 0, i % blocks_per_seq, 0))
    bd_spec = pl.BlockSpec((seqs, BAND_HEADS, rows, HEAD_DIM), lambda i: (i // blocks_per_seq, 0, 0, 0))
    sb_shape = jax.ShapeDtypeStruct((n_seq, SB_HEADS, seq_rows, HEAD_DIM), F32)
    bd_shape = jax.ShapeDtypeStruct((n_seq, BAND_HEADS, band_keep, HEAD_DIM), F32)

    def w_spec(group):
        return pl.BlockSpec((d, KV_TN), functools.partial(lambda i, c: (0, c), c=group),
                            pipeline_mode=pl.Buffered(1))

    return pl.pallas_call(
        kern,
        grid=(m // tm,),
        in_specs=[
            pl.BlockSpec((tm, d), lambda i: (i, 0)),
            pl.BlockSpec((1, d), lambda i: (0, 0)),
            w_spec(0), w_spec(1), w_spec(2), w_spec(3),
        ],
        out_specs=[pl.BlockSpec((tm, KV_WIDTH), lambda i: (i, 0)), pl.BlockSpec((tm, d), lambda i: (i, 0)),
                   sb_spec, sb_spec, bd_spec, bd_spec],
        out_shape=[jax.ShapeDtypeStruct((m, KV_WIDTH), BF16), jax.ShapeDtypeStruct((m, d), BF16),
                   sb_shape, sb_shape, bd_shape, bd_shape],
        compiler_params=_params(1),
        name="kv_projection",
    )(x2d, g_pre.reshape(1, d), w_kv_b, w_kv_b, w_kv_b, w_kv_b)


def _col_proj_kernel(h_ref, w_ref, cs_ref, y_ref):
    y_ref[...] = (_dot(h_ref[...], w_ref[...].astype(BF16)) * cs_ref[...]).astype(BF16)


def _col_projection(h2d, w_in, col_scale, src_block, *, tm, tn, name):
    m, d = h2d.shape
    n = col_scale.shape[1]
    assert n % tn == 0 and m % tm == 0
    return pl.pallas_call(
        _col_proj_kernel,
        grid=(m // tm, n // tn),
        in_specs=[
            pl.BlockSpec((tm, d), lambda i, j: (i, 0)),
            pl.BlockSpec((d, tn), lambda i, j: (0, src_block(j))),
            pl.BlockSpec((1, tn), lambda i, j: (0, j)),
        ],
        out_specs=pl.BlockSpec((tm, tn), lambda i, j: (i, j)),
        out_shape=jax.ShapeDtypeStruct((m, n), BF16),
        compiler_params=_params(2),
        name=name,
    )(h2d, w_in, col_scale)


def _memkv_kernel(x_ref, g_ref, w_ref, mk_ref, mv_ref):
    x = x_ref[...]
    ms = jnp.mean(x * x, axis=-1, keepdims=True)
    h = ((x * lax.rsqrt(ms + RMS_EPS)) * g_ref[...]).astype(BF16)
    acc = _dot(h, w_ref[...])
    mk_ref[...] = acc[:, :MEM_WIDTH]
    mv_ref[...] = acc[:, MEM_WIDTH:]


def _memory_kv(mem2d, g_mem, w_bf16, *, tm):
    m, d = mem2d.shape
    return pl.pallas_call(
        _memkv_kernel,
        grid=(m // tm,),
        in_specs=[
            pl.BlockSpec((tm, d), lambda i: (i, 0)),
            pl.BlockSpec((1, d), lambda i: (0, 0)),
            pl.BlockSpec((d, 2 * MEM_WIDTH), lambda i: (0, 0)),
        ],
        out_specs=[pl.BlockSpec((tm, MEM_WIDTH), lambda i: (i, 0)),
                   pl.BlockSpec((tm, MEM_WIDTH), lambda i: (i, 0))],
        out_shape=[jax.ShapeDtypeStruct((m, MEM_WIDTH), F32),
                   jax.ShapeDtypeStruct((m, MEM_WIDTH), F32)],
        compiler_params=_params(1),
        name="memory_kv",
    )(mem2d, g_mem.reshape(1, d), w_bf16)


def _neg_suffix_matrix(n):
    row = lax.broadcasted_iota(jnp.int32, (2 * n, n), 0)
    col = lax.broadcasted_iota(jnp.int32, (2 * n, n), 1)
    row = jnp.where(row >= n, row - n, row)
    return jnp.where(row >= col, -1.0, 0.0).astype(BF16)


def _sb_weights(z2, carry2, negu2, mask):
    p = jnp.maximum(z2, 0.0) + jnp.log(1.0 + jnp.exp2(-jnp.abs(z2))) * LOG2E
    if mask is not None:
        p = jnp.where(mask, p, 0.0)
    p_hi = p.astype(BF16)
    p_lo = (p - p_hi.astype(F32)).astype(BF16)
    suffix = _dot(jnp.concatenate([p_hi, p_lo], axis=1), negu2)
    w = jnp.exp2(z2 + suffix + carry2)
    if mask is not None:
        w = jnp.where(mask, w, 0.0)
    return w, carry2 - jnp.sum(p, axis=-1, keepdims=True)


def _sb_prompt_kernel(q_ref, k_ref, v_ref, negu2_ref, o_ref, acc_ref, carry_ref, kpad_ref, vpad_ref, *, t):
    i = pl.program_id(2)
    n_sub = SB_TQ // SB_TK
    negu2 = negu2_ref[...]

    @pl.when(i == 0)
    def _():
        kpad_ref[0:SB_TK, :] = jnp.zeros((SB_TK, HEAD_DIM), BF16)
        vpad_ref[0:SB_TK, :] = jnp.zeros((SB_TK, HEAD_DIM), BF16)
        kpad_ref[SB_TK:SB_TK + t, :] = k_ref[...]
        vpad_ref[SB_TK:SB_TK + t, :] = v_ref[...]

    def kv_block(j):
        start = pl.multiple_of((j + 1) * SB_TK, SB_TK)
        return kpad_ref[pl.ds(start, SB_TK), :], vpad_ref[pl.ds(start, SB_TK), :]

    row = lax.broadcasted_iota(jnp.int32, (SB_TK, SB_TK), 0)
    col = lax.broadcasted_iota(jnp.int32, (SB_TK, SB_TK), 1)
    for r in range(n_sub):
        rows = slice(r * SB_TK, (r + 1) * SB_TK)
        s = i * n_sub + r
        q = q_ref[rows, :]
        kb, vb = kv_block(s)
        w, carry = _sb_weights(_nt_dot(q, kb), jnp.zeros((SB_TK, 1), F32), negu2, col < row)
        acc = _dot(w.astype(BF16), vb)
        kb, vb = kv_block(s - 1)
        prev_exists = None if r > 0 else (jnp.zeros((SB_TK, SB_TK), jnp.int32) + i) > 0
        w, carry = _sb_weights(_nt_dot(q, kb), carry, negu2, prev_exists)
        acc_ref[rows, :] = acc + _dot(w.astype(BF16), vb)
        carry_ref[rows, :] = carry

    row_q = lax.broadcasted_iota(jnp.int32, (SB_TQ, 1), 0)
    row_t = lax.broadcasted_iota(jnp.int32, (SB_TQ, SB_TK), 0)
    has_more = row_q >= (2 - n_sub * i) * SB_TK

    def any_alive(carry):
        return (jnp.max(jnp.where(has_more, carry, NEG_INF)) > SB_DEAD).astype(jnp.int32)

    def cond(state):
        j, alive = state
        return jnp.logical_and(j >= 0, alive > 0)

    def body(state):
        j, _ = state
        kb, vb = kv_block(j)
        visits = row_t >= (j - n_sub * i + 2) * SB_TK
        w, carry = _sb_weights(_nt_dot(q_ref[...], kb), carry_ref[...], negu2, visits)
        acc_ref[...] += _dot(w.astype(BF16), vb)
        carry_ref[...] = carry
        return j - 1, any_alive(carry)

    lax.while_loop(cond, body, (n_sub * i + n_sub - 3, any_alive(carry_ref[...])))
    o_ref[...] = acc_ref[...].astype(BF16)


def _sb_prompt(ya3, ykv3, negu2):
    b, t, _ = ya3.shape
    qb, kb, vb = QG_SB_Q // HEAD_DIM, KV_SB_K // HEAD_DIM, KV_SB_V // HEAD_DIM
    assert SB_TQ // SB_TK >= 3
    kern = functools.partial(_sb_prompt_kernel, t=t)
    return pl.pallas_call(
        kern,
        grid=(b, SB_HEADS, t // SB_TQ),
        in_specs=[
            pl.BlockSpec((None, SB_TQ, HEAD_DIM), lambda b, h, i: (b, i, qb + h)),
            pl.BlockSpec((None, t, HEAD_DIM), lambda b, h, i: (b, 0, kb + h)),
            pl.BlockSpec((None, t, HEAD_DIM), lambda b, h, i: (b, 0, vb + h)),
            pl.BlockSpec((2 * SB_TK, SB_TK), lambda b, h, i: (0, 0)),
        ],
        out_specs=pl.BlockSpec((None, SB_TQ, HEAD_DIM), lambda b, h, i: (b, i, h)),
        out_shape=jax.ShapeDtypeStruct((b, t, SB_WIDTH), BF16),
        scratch_shapes=[pltpu.VMEM((SB_TQ, HEAD_DIM), F32), pltpu.VMEM((SB_TQ, 1), F32),
                        pltpu.VMEM((SB_TK + t, HEAD_DIM), BF16), pltpu.VMEM((SB_TK + t, HEAD_DIM), BF16)],
        compiler_params=_params(3),
        name="sb_prompt",
    )(ya3, ykv3, ykv3, negu2)


def _sb_decode_kernel(q_ref, kn_ref, vn_ref, kc_hbm, vc_hbm, negu2_ref, o_ref,
                      acc_ref, carry_ref, kpad_ref, vpad_ref, kbuf_ref, vbuf_ref, sem,
                      *, n_new, n_blocks):
    b = pl.program_id(0)
    heads = SB_HEADS

    def cache_copies(j, slot):
        rows = pl.ds(pl.multiple_of((n_blocks - 1 - j) * SB_TK, SB_TK), SB_TK)
        return (pltpu.make_async_copy(kc_hbm.at[b, :, rows, :], kbuf_ref.at[slot], sem.at[0, slot]),
                pltpu.make_async_copy(vc_hbm.at[b, :, rows, :], vbuf_ref.at[slot], sem.at[1, slot]))

    def start_fetch(j, slot):
        for cp in cache_copies(j, slot):
            cp.start()

    def wait_fetch(j, slot):
        for cp in cache_copies(j, slot):
            cp.wait()

    start_fetch(0, 0)

    def head_cols(h):
        return slice(h * HEAD_DIM, (h + 1) * HEAD_DIM)

    def head_rows(h):
        return slice(h * n_new, (h + 1) * n_new)

    def block(k_of, v_of, negu2, mask):
        z2 = jnp.concatenate([_nt_dot(q_ref[:, head_cols(h)], k_of(h)) for h in range(heads)], axis=0)
        w, carry = _sb_weights(z2, carry_ref[...], negu2, mask)
        wb = w.astype(BF16)
        for h in range(heads):
            acc_ref[head_rows(h), :] += _dot(wb[head_rows(h), :], v_of(h))
        carry_ref[...] = carry

    def any_alive():
        return (jnp.max(carry_ref[...]) > SB_DEAD).astype(jnp.int32)

    acc_ref[...] = jnp.zeros_like(acc_ref)
    carry_ref[...] = jnp.zeros_like(carry_ref)
    kpad_ref[...] = jnp.zeros_like(kpad_ref)
    vpad_ref[...] = jnp.zeros_like(vpad_ref)
    for h in range(heads):
        kpad_ref[h, 0:n_new, :] = kn_ref[:, head_cols(h)]
        vpad_ref[h, 0:n_new, :] = vn_ref[:, head_cols(h)]
    row = lax.broadcasted_iota(jnp.int32, (n_new, NEW_PAD), 0)
    col = lax.broadcasted_iota(jnp.int32, (n_new, NEW_PAD), 1)
    mask = jnp.concatenate([(col < row).astype(jnp.int32)] * heads, axis=0) == 1
    block(lambda h: kpad_ref[h], lambda h: vpad_ref[h], _neg_suffix_matrix(NEW_PAD), mask)

    negu2 = negu2_ref[...]

    def cond(state):
        j, alive = state
        return jnp.logical_and(j < n_blocks, alive > 0)

    def body(state):
        j, _ = state
        slot = j % 2
        wait_fetch(j, slot)

        @pl.when(j + 1 < n_blocks)
        def _():
            start_fetch(j + 1, 1 - slot)

        block(lambda h: kbuf_ref[slot, h].astype(BF16), lambda h: vbuf_ref[slot, h].astype(BF16),
              negu2, None)
        return j + 1, any_alive()

    j_end, _ = lax.while_loop(cond, body, (0, any_alive()))

    @pl.when(j_end < n_blocks)
    def _():
        wait_fetch(j_end, j_end % 2)

    for h in range(heads):
        o_ref[:, head_cols(h)] = acc_ref[head_rows(h), :].astype(BF16)


def _sb_decode(ya3, ykv3, cache_k, cache_v, negu2):
    bd, n_new, _ = ya3.shape
    past = cache_k.shape[2]
    assert past % SB_TK == 0 and n_new <= NEW_PAD and n_new % 16 == 0
    kern = functools.partial(_sb_decode_kernel, n_new=n_new, n_blocks=past // SB_TK)
    return pl.pallas_call(
        kern,
        grid=(bd,),
        in_specs=[
            pl.BlockSpec((None, n_new, SB_WIDTH), lambda b: (b, 0, QG_SB_Q // SB_WIDTH)),
            pl.BlockSpec((None, n_new, SB_WIDTH), lambda b: (b, 0, KV_SB_K // SB_WIDTH)),
            pl.BlockSpec((None, n_new, SB_WIDTH), lambda b: (b, 0, KV_SB_V // SB_WIDTH)),
            pl.BlockSpec(memory_space=pl.ANY),
            pl.BlockSpec(memory_space=pl.ANY),
            pl.BlockSpec((2 * SB_TK, SB_TK), lambda b: (0, 0)),
        ],
        out_specs=pl.BlockSpec((None, n_new, SB_WIDTH), lambda b: (b, 0, 0)),
        out_shape=jax.ShapeDtypeStruct((bd, n_new, SB_WIDTH), BF16),
        scratch_shapes=[pltpu.VMEM((SB_HEADS * n_new, HEAD_DIM), F32),
                        pltpu.VMEM((SB_HEADS * n_new, 1), F32),
                        pltpu.VMEM((SB_HEADS, NEW_PAD, HEAD_DIM), BF16),
                        pltpu.VMEM((SB_HEADS, NEW_PAD, HEAD_DIM), BF16),
                        pltpu.VMEM((2, SB_HEADS, SB_TK, HEAD_DIM), F32),
                        pltpu.VMEM((2, SB_HEADS, SB_TK, HEAD_DIM), F32),
                        pltpu.SemaphoreType.DMA((2, 2))],
        compiler_params=_params(1),
        name="sb_decode",
    )(ya3, ykv3, ykv3, cache_k, cache_v, negu2)


def _softmax2_pv(parts):
    mx = functools.reduce(jnp.maximum, [jnp.max(s, axis=-1, keepdims=True) for s, _ in parts])
    num = None
    den = None
    for s, v in parts:
        p = jnp.exp2(s - mx)
        d = jnp.sum(p, axis=-1, keepdims=True)
        o = _dot(p.astype(BF16), v)
        num = o if num is None else num + o
        den = d if den is None else den + d
    return num / den


def _band_bias_kernel(g_ref, tp_ref, td_ref, *, n_new, r_band):
    x = jnp.broadcast_to(g_ref[...], (BAND_TQ, BIAS_LANES))
    row = lax.broadcasted_iota(jnp.int32, (BAND_TQ, BIAS_LANES), 0)
    x = pltpu.roll(x, BAND_TQ, 1)
    for bit in range(BAND_TQ.bit_length() - 1):
        x = jnp.where(((row >> bit) & 1) == 1, pltpu.roll(x, 1 << bit, 1), x)
    tbl = x[:, :BAND_WIN] * LOG2E
    r = lax.broadcasted_iota(jnp.int32, (BAND_TQ, BAND_WIN), 0)
    j = lax.broadcasted_iota(jnp.int32, (BAND_TQ, BAND_WIN), 1)
    dc = (j >> CHUNK_SHIFT) - (r >> CHUNK_SHIFT)
    tp_ref[...] = jnp.where(jnp.logical_and(dc >= 0, dc <= BAND_LEFT_CHUNKS), tbl, NEG_INF)
    jd = lax.broadcasted_iota(jnp.int32, (n_new, r_band + NEW_PAD), 1)
    td_ref[...] = jnp.where(jd < r_band + n_new, tbl[:n_new, :r_band + NEW_PAD], NEG_INF)


def _band_bias_tables(rel_bias, n_new, r_band):
    h = rel_bias.shape[0]
    assert BAND_ROWS == 2 * MAX_REL and r_band == BAND_ROWS and BIAS_LANES == 2 * BAND_ROWS
    rb = rel_bias.astype(F32)
    g = jnp.concatenate([rb[:, :0:-1], jnp.broadcast_to(rb[:, -1:], (h, BIAS_LANES - 2 * MAX_REL))], axis=1)
    kern = functools.partial(_band_bias_kernel, n_new=n_new, r_band=r_band)
    return pl.pallas_call(
        kern,
        grid=(h,),
        in_specs=[pl.BlockSpec((None, 1, BIAS_LANES), lambda i: (i, 0, 0))],
        out_specs=[pl.BlockSpec((None, BAND_TQ, BAND_WIN), lambda i: (i, 0, 0)),
                   pl.BlockSpec((None, n_new, r_band + NEW_PAD), lambda i: (i, 0, 0))],
        out_shape=[jax.ShapeDtypeStruct((h, BAND_TQ, BAND_WIN), F32),
                   jax.ShapeDtypeStruct((h, n_new, r_band + NEW_PAD), F32)],
        compiler_params=_params(1),
        name="band_bias",
    )(g.reshape(h, 1, BIAS_LANES))


def _band_prompt_kernel(q_ref, k_ref, v_ref, bias_ref, o_ref, kpad_ref, vpad_ref, *, t):
    s_idx = pl.program_id(2)

    @pl.when(s_idx == 0)
    def _():
        kpad_ref[0:BAND_ROWS, :] = jnp.zeros((BAND_ROWS, HEAD_DIM), BF16)
        vpad_ref[0:BAND_ROWS, :] = jnp.zeros((BAND_ROWS, HEAD_DIM), BF16)
        kpad_ref[BAND_ROWS:BAND_ROWS + t, :] = k_ref[...]
        vpad_ref[BAND_ROWS:BAND_ROWS + t, :] = v_ref[...]

    col = lax.broadcasted_iota(jnp.int32, (BAND_TQ, BAND_WIN), 1)
    for gg in range(BAND_STEP_GROUPS):
        g = s_idx * BAND_STEP_GROUPS + gg
        start = pl.multiple_of(g * BAND_TQ, BAND_TQ)
        rows = slice(gg * BAND_TQ, (gg + 1) * BAND_TQ)
        s = _nt_dot(q_ref[rows, :], kpad_ref[pl.ds(start, BAND_WIN), :]) + bias_ref[...]
        s = jnp.where(col + g * BAND_TQ >= BAND_ROWS, s, NEG_INF)
        o_ref[rows, :] = _softmax2_pv([(s, vpad_ref[pl.ds(start, BAND_WIN), :])]).astype(BF16)


def _band_prompt(ya3, ykv3, bias_tbl):
    b, t, _ = ya3.shape
    tq = BAND_TQ * BAND_STEP_GROUPS
    qb, kb, vb = QG_BD_Q // HEAD_DIM, KV_BD_K // HEAD_DIM, KV_BD_V // HEAD_DIM
    kern = functools.partial(_band_prompt_kernel, t=t)
    return pl.pallas_call(
        kern,
        grid=(b, BAND_HEADS, t // tq),
        in_specs=[
            pl.BlockSpec((None, tq, HEAD_DIM), lambda b, h, g: (b, g, qb + h)),
            pl.BlockSpec((None, t, HEAD_DIM), lambda b, h, g: (b, 0, kb + h)),
            pl.BlockSpec((None, t, HEAD_DIM), lambda b, h, g: (b, 0, vb + h)),
            pl.BlockSpec((None, BAND_TQ, BAND_WIN), lambda b, h, g: (h, 0, 0)),
        ],
        out_specs=pl.BlockSpec((None, tq, HEAD_DIM), lambda b, h, g: (b, g, h)),
        out_shape=jax.ShapeDtypeStruct((b, t, BAND_WIDTH), BF16),
        scratch_shapes=[pltpu.VMEM((BAND_ROWS + t, HEAD_DIM), BF16),
                        pltpu.VMEM((BAND_ROWS + t, HEAD_DIM), BF16)],
        compiler_params=_params(3),
        name="band_prompt",
    )(ya3, ykv3, ykv3, bias_tbl)


def _band_decode_kernel(q_ref, kn_ref, vn_ref, kc_ref, vc_ref, bias_ref, o_ref, kpad_ref, vpad_ref,
                        *, n_new, r_band):
    kpad_ref[...] = jnp.zeros_like(kpad_ref)
    vpad_ref[...] = jnp.zeros_like(vpad_ref)
    for h in range(BAND_HEADS):
        cols = slice(h * HEAD_DIM, (h + 1) * HEAD_DIM)
        kpad_ref[h, 0:n_new, :] = kn_ref[:, cols]
        vpad_ref[h, 0:n_new, :] = vn_ref[:, cols]
    for h in range(BAND_HEADS):
        cols = slice(h * HEAD_DIM, (h + 1) * HEAD_DIM)
        q = q_ref[:, cols]
        s_cache = _nt_dot(q, kc_ref[h].astype(BF16)) + bias_ref[h, :, 0:r_band]
        s_new = _nt_dot(q, kpad_ref[h]) + bias_ref[h, :, r_band:r_band + NEW_PAD]
        o_ref[:, cols] = _softmax2_pv([(s_cache, vc_ref[h].astype(BF16)),
                                       (s_new, vpad_ref[h])]).astype(BF16)


def _band_decode(ya3, ykv3, cache_k, cache_v, bias_tbl):
    bd, n_new, _ = ya3.shape
    r_band = cache_k.shape[2]
    kern = functools.partial(_band_decode_kernel, n_new=n_new, r_band=r_band)
    cache_spec = pl.BlockSpec((None, BAND_HEADS, r_band, HEAD_DIM), lambda b: (b, 0, 0, 0))
    return pl.pallas_call(
        kern,
        grid=(bd,),
        in_specs=[
            pl.BlockSpec((None, n_new, BAND_WIDTH), lambda b: (b, 0, QG_BD_Q // BAND_WIDTH)),
            pl.BlockSpec((None, n_new, BAND_WIDTH), lambda b: (b, 0, KV_BD_K // BAND_WIDTH)),
            pl.BlockSpec((None, n_new, BAND_WIDTH), lambda b: (b, 0, KV_BD_V // BAND_WIDTH)),
            cache_spec,
            cache_spec,
            pl.BlockSpec((BAND_HEADS, n_new, r_band + NEW_PAD), lambda b: (0, 0, 0)),
        ],
        out_specs=pl.BlockSpec((None, n_new, BAND_WIDTH), lambda b: (b, 0, 0)),
        out_shape=jax.ShapeDtypeStruct((bd, n_new, BAND_WIDTH), BF16),
        scratch_shapes=[pltpu.VMEM((BAND_HEADS, NEW_PAD, HEAD_DIM), BF16),
                        pltpu.VMEM((BAND_HEADS, NEW_PAD, HEAD_DIM), BF16)],
        compiler_params=_params(1),
        name="band_decode",
    )(ya3, ykv3, ykv3, cache_k, cache_v, bias_tbl)


def _mem_attn_kernel(q_ref, mk_ref, mv_ref, o_ref):
    for h in range(MEM_HEADS):
        sl = slice(h * HEAD_DIM, (h + 1) * HEAD_DIM)
        s = _nt_dot(q_ref[:, sl], mk_ref[:, sl].astype(BF16))
        o_ref[:, sl] = _softmax2_pv([(s, mv_ref[:, sl].astype(BF16))]).astype(BF16)


def _mem_attention(y3, mk, mv, *, tq):
    b, t, _ = y3.shape
    n_mem = mk.shape[1]
    qb = MG_MM_Q // MEM_WIDTH
    return pl.pallas_call(
        _mem_attn_kernel,
        grid=(b, t // tq),
        in_specs=[
            pl.BlockSpec((None, tq, MEM_WIDTH), lambda b, i: (b, i, qb)),
            pl.BlockSpec((None, n_mem, MEM_WIDTH), lambda b, i: (b, 0, 0)),
            pl.BlockSpec((None, n_mem, MEM_WIDTH), lambda b, i: (b, 0, 0)),
        ],
        out_specs=pl.BlockSpec((None, tq, MEM_WIDTH), lambda b, i: (b, i, 0)),
        out_shape=jax.ShapeDtypeStruct((b, t, MEM_WIDTH), BF16),
        compiler_params=_params(2),
        name="mem_attention",
    )(y3, mk, mv)


def _silu_of_half(h):
    return h + h * jnp.tanh(h)


def _merge_kernel(osb_ref, obd_ref, omm_ref, gsb_ref, gbd_ref, gmm_ref,
                  mg0_ref, mg1_ref, mg2_ref, mg3_ref, mg4_ref, mg5_ref,
                  wsb_ref, wbd_ref, wmm_ref, merged_ref, *, half):
    u_sb = (osb_ref[...].astype(F32) * _silu_of_half(gsb_ref[...].astype(F32))).astype(BF16)
    u_bd = (obd_ref[...].astype(F32) * _silu_of_half(gbd_ref[...].astype(F32))).astype(BF16)
    u_mm = (omm_ref[...].astype(F32) * _silu_of_half(gmm_ref[...].astype(F32))).astype(BF16)
    mg = ((mg0_ref, mg2_ref, mg4_ref), (mg1_ref, mg3_ref, mg5_ref))
    for n in range(2):
        cols = slice(n * half, (n + 1) * half)
        merged = None
        for m_ref, u, w_ref in zip(mg[n], (u_sb, u_bd, u_mm), (wsb_ref, wbd_ref, wmm_ref)):
            a = _dot(u, w_ref[:, cols])
            term = a + a * jnp.tanh(m_ref[...].astype(F32))
            merged = term if merged is None else merged + term
        merged_ref[:, cols] = merged.astype(BF16)


def _merge_branches(yqg, ymg, o_sb, o_bd, o_mm, w_sb, w_bd, w_mm, *, tm):
    m = yqg.shape[0]
    d = w_sb.shape[1]
    half = d // 2
    assert MG_MG % half == 0
    mgb = MG_MG // half
    const = dict(pipeline_mode=pl.Buffered(1))
    kern = functools.partial(_merge_kernel, half=half)
    return pl.pallas_call(
        kern,
        grid=(m // tm,),
        in_specs=[
            pl.BlockSpec((tm, SB_WIDTH), lambda i: (i, 0)),
            pl.BlockSpec((tm, BAND_WIDTH), lambda i: (i, 0)),
            pl.BlockSpec((tm, MEM_WIDTH), lambda i: (i, 0)),
            pl.BlockSpec((tm, SB_WIDTH), lambda i: (i, QG_SB_G // SB_WIDTH)),
            pl.BlockSpec((tm, BAND_WIDTH), lambda i: (i, QG_BD_G // BAND_WIDTH)),
            pl.BlockSpec((tm, MEM_WIDTH), lambda i: (i, MG_MM_G // MEM_WIDTH)),
        ] + [pl.BlockSpec((tm, half), functools.partial(lambda i, c: (i, c), c=mgb + c)) for c in range(6)] + [
            pl.BlockSpec((SB_WIDTH, d), lambda i: (0, 0), **const),
            pl.BlockSpec((BAND_WIDTH, d), lambda i: (0, 0), **const),
            pl.BlockSpec((MEM_WIDTH, d), lambda i: (0, 0), **const),
        ],
        out_specs=pl.BlockSpec((tm, d), lambda i: (i, 0)),
        out_shape=jax.ShapeDtypeStruct((m, d), BF16),
        compiler_params=_params(1),
        name="merge_branches",
    )(o_sb, o_bd, o_mm, yqg, yqg, ymg, *([ymg] * 6), w_sb, w_bd, w_mm)


def _out_proj_kernel(x_ref, merged_ref, wout_ref, gpost_ref, y_ref):
    y = _dot(merged_ref[...], wout_ref[...])
    ms = jnp.mean(y * y, axis=-1, keepdims=True)
    y_ref[...] = x_ref[...] + (y * lax.rsqrt(ms + RMS_EPS)) * gpost_ref[...]


def _out_projection(x2d, merged, w_out, g_post, *, tm):
    m, d = x2d.shape
    return pl.pallas_call(
        _out_proj_kernel,
        grid=(m // tm,),
        in_specs=[
            pl.BlockSpec((tm, d), lambda i: (i, 0)),
            pl.BlockSpec((tm, d), lambda i: (i, 0)),
            pl.BlockSpec((d, d), lambda i: (0, 0), pipeline_mode=pl.Buffered(1)),
            pl.BlockSpec((1, d), lambda i: (0, 0)),
        ],
        out_specs=pl.BlockSpec((tm, d), lambda i: (i, 0)),
        out_shape=jax.ShapeDtypeStruct((m, d), F32),
        compiler_params=_params(1),
        name="out_projection",
    )(x2d, merged, w_out, g_post.reshape(1, d))


def _head_major(a):
    return jnp.transpose(a, (0, 2, 1, 3))


def kernel(x_prompt, x_sample, cache_sb_k, cache_sb_v, cache_band_k, cache_band_v, cache_mem_k, cache_mem_v, mem_prompt, g_pre, w_in, rel_bias, g_mem, w_mem_kv, w_up_sb, w_up_band, w_up_mem, w_out, g_post):
    depth = w_in.shape[0]
    b, t, d = x_prompt.shape
    bd, n_new, _ = x_sample.shape
    n_mem = mem_prompt.shape[1]
    r_band = cache_band_k.shape[2]
    in_width = w_in.shape[2]
    band_keep = min(BAND_ROWS, t)
    tm_p = 512
    assert COL_MG + 3 * d == in_width
    assert t % SB_TQ == 0 and t % (BAND_TQ * BAND_STEP_GROUPS) == 0 and t % tm_p == 0
    assert r_band == BAND_ROWS and n_new <= CHUNK

    negu2 = jnp.where(jnp.arange(2 * SB_TK)[:, None] % SB_TK >= jnp.arange(SB_TK)[None, :], -1.0, 0.0).astype(BF16)
    mg_width = in_width - COL_MM_Q
    assert COL_MM_Q % MG_TN == 0 and mg_width % MG_TN == 0 and COL_SB_G == 3 * KV_TN and COL_BD_Q == 4 * KV_TN
    cols = jnp.arange(QG_WIDTH)
    qg_scale = jnp.where((cols // SB_WIDTH) % 2 == 0, Q_SCALE, 0.5).astype(F32).reshape(1, QG_WIDTH)
    cols = jnp.arange(mg_width)
    mg_scale = jnp.where(cols < MG_MM_G, Q_SCALE, 0.5).astype(F32).reshape(1, mg_width)
    qg_block = lambda j: j + 2 * ((j + 1) // 2)
    mg_block = lambda j: j + COL_MM_Q // MG_TN

    def projections(x2d, l, w_kvp_b, *, tm_kv, tm_act, seqs, n_seq, band_keep):
        ykv, h2d, sbk, sbv, bdk, bdv = _kv_projection(x2d, g_pre[l], w_kvp_b, tm=tm_kv, seqs=seqs,
                                                      n_seq=n_seq, band_keep=band_keep)
        yqg = _col_projection(h2d, w_in[l], qg_scale, qg_block, tm=tm_act, tn=KV_TN, name="qg_projection")
        ymg = _col_projection(h2d, w_in[l], mg_scale, mg_block, tm=tm_act, tn=MG_TN, name="mg_projection")
        return ykv, yqg, ymg, sbk, sbv, bdk, bdv

    xp = x_prompt.reshape(b * t, d)
    xs = x_sample.reshape(bd * n_new, d)
    outs = [[] for _ in range(10)]
    for l in range(depth):
        w_kvp_b = jnp.concatenate([w_in[l][:, COL_SB_K:COL_SB_G], w_in[l][:, COL_BD_K:COL_BD_G]],
                                  axis=1).astype(BF16)
        w_kv_b = w_mem_kv[l].astype(BF16)
        w_sb_b = (0.5 * w_up_sb[l]).astype(BF16)
        w_bd_b = (0.5 * w_up_band[l]).astype(BF16)
        w_mm_b = (0.5 * w_up_mem[l]).astype(BF16)
        w_out_b = w_out[l].astype(BF16)
        bias_p, bias_d = _band_bias_tables(rel_bias[l], n_new, r_band)

        ykv, yqg, ymg, sbk, sbv, bdk, bdv = projections(xp, l, w_kvp_b, tm_kv=band_keep, tm_act=2048,
                                                        seqs=1, n_seq=b, band_keep=band_keep)
        mk, mv = _memory_kv(mem_prompt.reshape(b * n_mem, d), g_mem[l], w_kv_b, tm=n_mem)
        yqg3 = yqg.reshape(b, t, QG_WIDTH)
        ykv3 = ykv.reshape(b, t, KV_WIDTH)
        o_sb = _sb_prompt(yqg3, ykv3, negu2)
        o_bd = _band_prompt(yqg3, ykv3, bias_p)
        o_mm = _mem_attention(ymg.reshape(b, t, mg_width), mk.reshape(b, n_mem, MEM_WIDTH),
                              mv.reshape(b, n_mem, MEM_WIDTH), tq=512)
        merged = _merge_branches(yqg, ymg, o_sb.reshape(b * t, -1), o_bd.reshape(b * t, -1),
                                 o_mm.reshape(b * t, -1), w_sb_b, w_bd_b, w_mm_b, tm=512)
        xp = _out_projection(xp, merged, w_out_b, g_post[l], tm=512)
        outs[0].append(_head_major(sbk))
        outs[1].append(_head_major(sbv))
        outs[2].append(_head_major(bdk))
        outs[3].append(_head_major(bdv))
        outs[4].append(mk.reshape(b, n_mem, MEM_HEADS, HEAD_DIM))
        outs[5].append(mv.reshape(b, n_mem, MEM_HEADS, HEAD_DIM))

        ms = bd * n_new
        ykv_s, yqg_s, ymg_s, sbk2, sbv2, bdk2, bdv2 = projections(xs, l, w_kvp_b, tm_kv=ms, tm_act=ms,
                                                                  seqs=bd, n_seq=bd, band_keep=n_new)
        yqg_s3 = yqg_s.reshape(bd, n_new, QG_WIDTH)
        ykv_s3 = ykv_s.reshape(bd, n_new, KV_WIDTH)
        o_sb2 = _sb_decode(yqg_s3, ykv_s3, _head_major(cache_sb_k[l]), _head_major(cache_sb_v[l]), negu2)
        o_bd2 = _band_decode(yqg_s3, ykv_s3, _head_major(cache_band_k[l]), _head_major(cache_band_v[l]), bias_d)
        o_mm2 = _mem_attention(ymg_s.reshape(bd, n_new, mg_width), cache_mem_k[l].reshape(bd, n_mem, MEM_WIDTH),
                               cache_mem_v[l].reshape(bd, n_mem, MEM_WIDTH), tq=n_new)
        merged_s = _merge_branches(yqg_s, ymg_s, o_sb2.reshape(ms, -1), o_bd2.reshape(ms, -1),
                                   o_mm2.reshape(ms, -1), w_sb_b, w_bd_b, w_mm_b, tm=ms)
        xs = _out_projection(xs, merged_s, w_out_b, g_post[l], tm=ms)
        outs[6].append(_head_major(sbk2))
        outs[7].append(_head_major(sbv2))
        outs[8].append(_head_major(bdk2))
        outs[9].append(_head_major(bdv2))

    return (xp.reshape(b, t, d), xs.reshape(bd, n_new, d)) + tuple(jnp.stack(o) for o in outs)
```

```python
import functools
import math

import jax
import jax.numpy as jnp
from jax import lax
from jax.experimental import pallas as pl
from jax.experimental.pallas import tpu as pltpu

F32 = jnp.float32
BF16 = jnp.bfloat16

HEAD_DIM = 128
SB_HEADS = 6
BAND_HEADS = 6
MEM_HEADS = 4
SB_WIDTH = SB_HEADS * HEAD_DIM
BAND_WIDTH = BAND_HEADS * HEAD_DIM
MEM_WIDTH = MEM_HEADS * HEAD_DIM
CHUNK = 64
CHUNK_SHIFT = 6
BAND_LEFT_CHUNKS = 8
BAND_ROWS = BAND_LEFT_CHUNKS * CHUNK
MAX_REL = 256
RMS_EPS = 1e-6
NEG_INF = -1e30
LOG2E = math.log2(math.e)
Q_SCALE = HEAD_DIM ** -0.5 * LOG2E

COL_SB_Q = 0
COL_SB_K = COL_SB_Q + SB_WIDTH
COL_SB_V = COL_SB_K + SB_WIDTH
COL_SB_G = COL_SB_V + SB_WIDTH
COL_BD_Q = COL_SB_G + SB_WIDTH
COL_BD_K = COL_BD_Q + BAND_WIDTH
COL_BD_V = COL_BD_K + BAND_WIDTH
COL_BD_G = COL_BD_V + BAND_WIDTH
COL_MM_Q = COL_BD_G + BAND_WIDTH
COL_MM_G = COL_MM_Q + MEM_WIDTH
COL_MG = COL_MM_G + MEM_WIDTH

KV_SB_K = 0
KV_SB_V = KV_SB_K + SB_WIDTH
KV_BD_K = KV_SB_V + SB_WIDTH
KV_BD_V = KV_BD_K + BAND_WIDTH
KV_WIDTH = KV_BD_V + BAND_WIDTH
QG_SB_Q = 0
QG_SB_G = QG_SB_Q + SB_WIDTH
QG_BD_Q = QG_SB_G + SB_WIDTH
QG_BD_G = QG_BD_Q + BAND_WIDTH
QG_WIDTH = QG_BD_G + BAND_WIDTH
MG_MM_Q = 0
MG_MM_G = MG_MM_Q + MEM_WIDTH
MG_MG = MG_MM_G + MEM_WIDTH

VMEM_LIMIT_BYTES = 56 * 1024 * 1024
KV_TN = SB_WIDTH
MG_TN = 1024
SB_TK = 256
SB_TQ = 4 * SB_TK
SB_DEAD = -160.0
BAND_TQ = 4 * CHUNK
BAND_WIN = BAND_TQ + BAND_ROWS
BAND_STEP_GROUPS = 4
BIAS_LANES = 1024
NEW_PAD = 128


def _params(n_axes, vmem=VMEM_LIMIT_BYTES):
    return pltpu.CompilerParams(dimension_semantics=("arbitrary",) * n_axes,
                                vmem_limit_bytes=vmem)


def _nt_dot(a, b):
    return lax.dot_general(a, b, (((1,), (1,)), ((), ())), preferred_element_type=F32)


def _dot(a, b):
    return jnp.dot(a, b, preferred_element_type=F32)


def _pre_norm_to(h_ref, x_ref, g_ref):
    x = x_ref[...]
    ms = jnp.mean(x * x, axis=-1, keepdims=True)
    h_ref[...] = ((x * lax.rsqrt(ms + RMS_EPS)) * g_ref[...]).astype(BF16)


def _kv_proj_kernel(x_ref, g_ref, wsk_ref, wsv_ref, wbk_ref, wbv_ref,
                    y_ref, h_ref, sbk_ref, sbv_ref, bdk_ref, bdv_ref, *, seqs, rows):
    _pre_norm_to(h_ref, x_ref, g_ref)
    groups = ((wsk_ref, sbk_ref), (wsv_ref, sbv_ref), (wbk_ref, bdk_ref), (wbv_ref, bdv_ref))
    for group, (w_ref, dst_ref) in enumerate(groups):
        acc = _dot(h_ref[...], w_ref[...])
        y_ref[:, group * KV_TN:(group + 1) * KV_TN] = acc.astype(BF16)
        for h in range(SB_HEADS):
            for s in range(seqs):
                dst_ref[s, h] = acc[s * rows:(s + 1) * rows, h * HEAD_DIM:(h + 1) * HEAD_DIM]


def _kv_projection(x2d, g_pre, w_kv_b, *, tm, seqs, n_seq, band_keep):
    m, d = x2d.shape
    assert KV_WIDTH == 4 * KV_TN and SB_HEADS == BAND_HEADS
    rows = tm // seqs
    seq_rows = m // n_seq
    blocks_per_seq = seq_rows // rows
    assert rows == band_keep
    kern = functools.partial(_kv_proj_kernel, seqs=seqs, rows=rows)
    sb_spec = pl.BlockSpec((seqs, SB_HEADS, rows, HEAD_DIM),
                           lambda i: (i // blocks_per_seq, 0, i % blocks_per_seq, 0))
    bd_spec = pl.BlockSpec((seqs, BAND_HEADS, rows, HEAD_DIM), lambda i: (i // blocks_per_seq, 0, 0, 0))
    sb_shape = jax.ShapeDtypeStruct((n_seq, SB_HEADS, seq_rows, HEAD_DIM), F32)
    bd_shape = jax.ShapeDtypeStruct((n_seq, BAND_HEADS, band_keep, HEAD_DIM), F32)

    def w_spec(group):
        return pl.BlockSpec((d, KV_TN), functools.partial(lambda i, c: (0, c), c=group),
                            pipeline_mode=pl.Buffered(1))

    return pl.pallas_call(
        kern,
        grid=(m // tm,),
        in_specs=[
            pl.BlockSpec((tm, d), lambda i: (i, 0)),
            pl.BlockSpec((1, d), lambda i: (0, 0)),
            w_spec(0), w_spec(1), w_spec(2), w_spec(3),
        ],
        out_specs=[pl.BlockSpec((tm, KV_WIDTH), lambda i: (i, 0)), pl.BlockSpec((tm, d), lambda i: (i, 0)),
                   sb_spec, sb_spec, bd_spec, bd_spec],
        out_shape=[jax.ShapeDtypeStruct((m, KV_WIDTH), BF16), jax.ShapeDtypeStruct((m, d), BF16),
                   sb_shape, sb_shape, bd_shape, bd_shape],
        compiler_params=_params(1),
        name="kv_projection",
    )(x2d, g_pre.reshape(1, d), w_kv_b, w_kv_b, w_kv_b, w_kv_b)


def _col_proj_kernel(h_ref, hs_ref, w_ref, cs_ref, y_ref, ys_ref, wb_ref):
    @pl.when(pl.program_id(1) == 0)
    def _():
        wb_ref[...] = w_ref[...].astype(BF16)
        ys_ref[...] = (_dot(hs_ref[...], wb_ref[...]) * cs_ref[...]).astype(BF16)

    y_ref[...] = (_dot(h_ref[...], wb_ref[...]) * cs_ref[...]).astype(BF16)


def _col_projection(h2d, hs2d, w_in, col_scale, src_block, *, tm, tn, name):
    m, d = h2d.shape
    ms = hs2d.shape[0]
    n = col_scale.shape[1]
    assert n % tn == 0 and m % tm == 0
    return pl.pallas_call(
        _col_proj_kernel,
        grid=(n // tn, m // tm),
        in_specs=[
            pl.BlockSpec((tm, d), lambda j, i: (i, 0)),
            pl.BlockSpec((ms, d), lambda j, i: (0, 0)),
            pl.BlockSpec((d, tn), lambda j, i: (0, src_block(j))),
            pl.BlockSpec((1, tn), lambda j, i: (0, j)),
        ],
        out_specs=[pl.BlockSpec((tm, tn), lambda j, i: (i, j)), pl.BlockSpec((ms, tn), lambda j, i: (0, j))],
        out_shape=[jax.ShapeDtypeStruct((m, n), BF16), jax.ShapeDtypeStruct((ms, n), BF16)],
        scratch_shapes=[pltpu.VMEM((d, tn), BF16)],
        compiler_params=_params(2),
        name=name,
    )(h2d, hs2d, w_in, col_scale)


def _kv_weight_kernel(a_ref, b_ref, c_ref, d_ref, o_ref):
    for group, w_ref in enumerate((a_ref, b_ref, c_ref, d_ref)):
        o_ref[:, group * KV_TN:(group + 1) * KV_TN] = w_ref[...].astype(BF16)


def _kv_weight_bf16(w):
    d = w.shape[0]
    tr = 256
    assert d % tr == 0

    def spec(col0):
        assert col0 % KV_TN == 0
        return pl.BlockSpec((tr, KV_TN), functools.partial(lambda i, c: (i, c), c=col0 // KV_TN))

    return pl.pallas_call(
        _kv_weight_kernel,
        grid=(d // tr,),
        in_specs=[spec(COL_SB_K), spec(COL_SB_V), spec(COL_BD_K), spec(COL_BD_V)],
        out_specs=pl.BlockSpec((tr, KV_WIDTH), lambda i: (i, 0)),
        out_shape=jax.ShapeDtypeStruct((d, KV_WIDTH), BF16),
        compiler_params=_params(1),
        name="kv_weight_cast",
    )(w, w, w, w)


def _memkv_kernel(x_ref, g_ref, w_ref, mk_ref, mv_ref):
    x = x_ref[...]
    ms = jnp.mean(x * x, axis=-1, keepdims=True)
    h = ((x * lax.rsqrt(ms + RMS_EPS)) * g_ref[...]).astype(BF16)
    acc = _dot(h, w_ref[...])
    mk_ref[...] = acc[:, :MEM_WIDTH]
    mv_ref[...] = acc[:, MEM_WIDTH:]


def _memory_kv(mem2d, g_mem, w_bf16, *, tm):
    m, d = mem2d.shape
    return pl.pallas_call(
        _memkv_kernel,
        grid=(m // tm,),
        in_specs=[
            pl.BlockSpec((tm, d), lambda i: (i, 0)),
            pl.BlockSpec((1, d), lambda i: (0, 0)),
            pl.BlockSpec((d, 2 * MEM_WIDTH), lambda i: (0, 0)),
        ],
        out_specs=[pl.BlockSpec((tm, MEM_WIDTH), lambda i: (i, 0)),
                   pl.BlockSpec((tm, MEM_WIDTH), lambda i: (i, 0))],
        out_shape=[jax.ShapeDtypeStruct((m, MEM_WIDTH), F32),
                   jax.ShapeDtypeStruct((m, MEM_WIDTH), F32)],
        compiler_params=_params(1),
        name="memory_kv",
    )(mem2d, g_mem.reshape(1, d), w_bf16)


def _neg_suffix_matrix(n):
    row = lax.broadcasted_iota(jnp.int32, (2 * n, n), 0)
    col = lax.broadcasted_iota(jnp.int32, (2 * n, n), 1)
    row = jnp.where(row >= n, row - n, row)
    return jnp.where(row >= col, -1.0, 0.0).astype(BF16)


def _sb_weights(z2, carry2, negu2, mask):
    p = jnp.maximum(z2, 0.0) + jnp.log(1.0 + jnp.exp2(-jnp.abs(z2))) * LOG2E
    if mask is not None:
        p = jnp.where(mask, p, 0.0)
    p_hi = p.astype(BF16)
    p_lo = (p - p_hi.astype(F32)).astype(BF16)
    suffix = _dot(jnp.concatenate([p_hi, p_lo], axis=1), negu2)
    w = jnp.exp2(z2 + suffix + carry2)
    if mask is not None:
        w = jnp.where(mask, w, 0.0)
    return w, carry2 - jnp.sum(p, axis=-1, keepdims=True)


def _sb_prompt_kernel(q_ref, k_ref, v_ref, negu2_ref, o_ref, acc_ref, carry_ref, kpad_ref, vpad_ref, *, t):
    i = pl.program_id(2)
    n_sub = SB_TQ // SB_TK
    negu2 = negu2_ref[...]

    @pl.when(i == 0)
    def _():
        kpad_ref[0:SB_TK, :] = jnp.zeros((SB_TK, HEAD_DIM), BF16)
        vpad_ref[0:SB_TK, :] = jnp.zeros((SB_TK, HEAD_DIM), BF16)
        kpad_ref[SB_TK:SB_TK + t, :] = k_ref[...]
        vpad_ref[SB_TK:SB_TK + t, :] = v_ref[...]

    def kv_block(j):
        start = pl.multiple_of((j + 1) * SB_TK, SB_TK)
        return kpad_ref[pl.ds(start, SB_TK), :], vpad_ref[pl.ds(start, SB_TK), :]

    row = lax.broadcasted_iota(jnp.int32, (SB_TK, SB_TK), 0)
    col = lax.broadcasted_iota(jnp.int32, (SB_TK, SB_TK), 1)
    for r in range(n_sub):
        rows = slice(r * SB_TK, (r + 1) * SB_TK)
        s = i * n_sub + r
        q = q_ref[rows, :]
        kb, vb = kv_block(s)
        w, carry = _sb_weights(_nt_dot(q, kb), jnp.zeros((SB_TK, 1), F32), negu2, col < row)
        acc = _dot(w.astype(BF16), vb)
        kb, vb = kv_block(s - 1)
        prev_exists = None if r > 0 else (jnp.zeros((SB_TK, SB_TK), jnp.int32) + i) > 0
        w, carry = _sb_weights(_nt_dot(q, kb), carry, negu2, prev_exists)
        acc_ref[rows, :] = acc + _dot(w.astype(BF16), vb)
        carry_ref[rows, :] = carry

    row_q = lax.broadcasted_iota(jnp.int32, (SB_TQ, 1), 0)
    row_t = lax.broadcasted_iota(jnp.int32, (SB_TQ, SB_TK), 0)
    has_more = row_q >= (2 - n_sub * i) * SB_TK

    def any_alive(carry):
        return (jnp.max(jnp.where(has_more, carry, NEG_INF)) > SB_DEAD).astype(jnp.int32)

    def cond(state):
        j, alive = state
        return jnp.logical_and(j >= 0, alive > 0)

    def body(state):
        j, _ = state
        kb, vb = kv_block(j)
        visits = row_t >= (j - n_sub * i + 2) * SB_TK
        w, carry = _sb_weights(_nt_dot(q_ref[...], kb), carry_ref[...], negu2, visits)
        acc_ref[...] += _dot(w.astype(BF16), vb)
        carry_ref[...] = carry
        return j - 1, any_alive(carry)

    lax.while_loop(cond, body, (n_sub * i + n_sub - 3, any_alive(carry_ref[...])))
    o_ref[...] = acc_ref[...].astype(BF16)


def _sb_prompt(ya3, ykv3, negu2):
    b, t, _ = ya3.shape
    qb, kb, vb = QG_SB_Q // HEAD_DIM, KV_SB_K // HEAD_DIM, KV_SB_V // HEAD_DIM
    assert SB_TQ // SB_TK >= 3
    kern = functools.partial(_sb_prompt_kernel, t=t)
    return pl.pallas_call(
        kern,
        grid=(b, SB_HEADS, t // SB_TQ),
        in_specs=[
            pl.BlockSpec((None, SB_TQ, HEAD_DIM), lambda b, h, i: (b, i, qb + h)),
            pl.BlockSpec((None, t, HEAD_DIM), lambda b, h, i: (b, 0, kb + h)),
            pl.BlockSpec((None, t, HEAD_DIM), lambda b, h, i: (b, 0, vb + h)),
            pl.BlockSpec((2 * SB_TK, SB_TK), lambda b, h, i: (0, 0)),
        ],
        out_specs=pl.BlockSpec((None, SB_TQ, HEAD_DIM), lambda b, h, i: (b, i, h)),
        out_shape=jax.ShapeDtypeStruct((b, t, SB_WIDTH), BF16),
        scratch_shapes=[pltpu.VMEM((SB_TQ, HEAD_DIM), F32), pltpu.VMEM((SB_TQ, 1), F32),
                        pltpu.VMEM((SB_TK + t, HEAD_DIM), BF16), pltpu.VMEM((SB_TK + t, HEAD_DIM), BF16)],
        compiler_params=_params(3),
        name="sb_prompt",
    )(ya3, ykv3, ykv3, negu2)


def _sb_decode_kernel(q_ref, kn_ref, vn_ref, kc_hbm, vc_hbm, negu2_ref, o_ref,
                      acc_ref, carry_ref, kpad_ref, vpad_ref, kbuf_ref, vbuf_ref, sem,
                      *, n_new, n_blocks):
    b = pl.program_id(0)
    heads = SB_HEADS

    def cache_copies(j, slot):
        rows = pl.ds(pl.multiple_of((n_blocks - 1 - j) * SB_TK, SB_TK), SB_TK)
        return (pltpu.make_async_copy(kc_hbm.at[b, :, rows, :], kbuf_ref.at[slot], sem.at[0, slot]),
                pltpu.make_async_copy(vc_hbm.at[b, :, rows, :], vbuf_ref.at[slot], sem.at[1, slot]))

    def start_fetch(j, slot):
        for cp in cache_copies(j, slot):
            cp.start()

    def wait_fetch(j, slot):
        for cp in cache_copies(j, slot):
            cp.wait()

    start_fetch(0, 0)

    def head_cols(h):
        return slice(h * HEAD_DIM, (h + 1) * HEAD_DIM)

    def head_rows(h):
        return slice(h * n_new, (h + 1) * n_new)

    def block(k_of, v_of, negu2, mask):
        z2 = jnp.concatenate([_nt_dot(q_ref[:, head_cols(h)], k_of(h)) for h in range(heads)], axis=0)
        w, carry = _sb_weights(z2, carry_ref[...], negu2, mask)
        wb = w.astype(BF16)
        for h in range(heads):
            acc_ref[head_rows(h), :] += _dot(wb[head_rows(h), :], v_of(h))
        carry_ref[...] = carry

    def any_alive():
        return (jnp.max(carry_ref[...]) > SB_DEAD).astype(jnp.int32)

    acc_ref[...] = jnp.zeros_like(acc_ref)
    carry_ref[...] = jnp.zeros_like(carry_ref)
    kpad_ref[...] = jnp.zeros_like(kpad_ref)
    vpad_ref[...] = jnp.zeros_like(vpad_ref)
    for h in range(heads):
        kpad_ref[h, 0:n_new, :] = kn_ref[:, head_cols(h)]
        vpad_ref[h, 0:n_new, :] = vn_ref[:, head_cols(h)]
    row = lax.broadcasted_iota(jnp.int32, (n_new, NEW_PAD), 0)
    col = lax.broadcasted_iota(jnp.int32, (n_new, NEW_PAD), 1)
    mask = jnp.concatenate([(col < row).astype(jnp.int32)] * heads, axis=0) == 1
    block(lambda h: kpad_ref[h], lambda h: vpad_ref[h], _neg_suffix_matrix(NEW_PAD), mask)

    negu2 = negu2_ref[...]

    def cond(state):
        j, alive = state
        return jnp.logical_and(j < n_blocks, alive > 0)

    def body(state):
        j, _ = state
        slot = j % 2
        wait_fetch(j, slot)

        @pl.when(j + 1 < n_blocks)
        def _():
            start_fetch(j + 1, 1 - slot)

        block(lambda h: kbuf_ref[slot, h].astype(BF16), lambda h: vbuf_ref[slot, h].astype(BF16),
              negu2, None)
        return j + 1, any_alive()

    j_end, _ = lax.while_loop(cond, body, (0, any_alive()))

    @pl.when(j_end < n_blocks)
    def _():
        wait_fetch(j_end, j_end % 2)

    for h in range(heads):
        o_ref[:, head_cols(h)] = acc_ref[head_rows(h), :].astype(BF16)


def _sb_decode(ya3, ykv3, cache_k, cache_v, negu2):
    bd, n_new, _ = ya3.shape
    past = cache_k.shape[2]
    assert past % SB_TK == 0 and n_new <= NEW_PAD and n_new % 16 == 0
    kern = functools.partial(_sb_decode_kernel, n_new=n_new, n_blocks=past // SB_TK)
    return pl.pallas_call(
        kern,
        grid=(bd,),
        in_specs=[
            pl.BlockSpec((None, n_new, SB_WIDTH), lambda b: (b, 0, QG_SB_Q // SB_WIDTH)),
            pl.BlockSpec((None, n_new, SB_WIDTH), lambda b: (b, 0, KV_SB_K // SB_WIDTH)),
            pl.BlockSpec((None, n_new, SB_WIDTH), lambda b: (b, 0, KV_SB_V // SB_WIDTH)),
            pl.BlockSpec(memory_space=pl.ANY),
            pl.BlockSpec(memory_space=pl.ANY),
            pl.BlockSpec((2 * SB_TK, SB_TK), lambda b: (0, 0)),
        ],
        out_specs=pl.BlockSpec((None, n_new, SB_WIDTH), lambda b: (b, 0, 0)),
        out_shape=jax.ShapeDtypeStruct((bd, n_new, SB_WIDTH), BF16),
        scratch_shapes=[pltpu.VMEM((SB_HEADS * n_new, HEAD_DIM), F32),
                        pltpu.VMEM((SB_HEADS * n_new, 1), F32),
                        pltpu.VMEM((SB_HEADS, NEW_PAD, HEAD_DIM), BF16),
                        pltpu.VMEM((SB_HEADS, NEW_PAD, HEAD_DIM), BF16),
                        pltpu.VMEM((2, SB_HEADS, SB_TK, HEAD_DIM), F32),
                        pltpu.VMEM((2, SB_HEADS, SB_TK, HEAD_DIM), F32),
                        pltpu.SemaphoreType.DMA((2, 2))],
        compiler_params=_params(1),
        name="sb_decode",
    )(ya3, ykv3, ykv3, cache_k, cache_v, negu2)


def _softmax2_pv(parts):
    mx = functools.reduce(jnp.maximum, [jnp.max(s, axis=-1, keepdims=True) for s, _ in parts])
    num = None
    den = None
    for s, v in parts:
        p = jnp.exp2(s - mx)
        d = jnp.sum(p, axis=-1, keepdims=True)
        o = _dot(p.astype(BF16), v)
        num = o if num is None else num + o
        den = d if den is None else den + d
    return num / den


def _band_bias_kernel(g_ref, tp_ref, td_ref, *, n_new, r_band):
    x = jnp.broadcast_to(g_ref[...], (BAND_TQ, BIAS_LANES))
    x = pltpu.roll(x, BAND_TQ, 1, stride=1, stride_axis=0)
    tbl = x[:, :BAND_WIN] * LOG2E
    r = lax.broadcasted_iota(jnp.int32, (BAND_TQ, BAND_WIN), 0)
    j = lax.broadcasted_iota(jnp.int32, (BAND_TQ, BAND_WIN), 1)
    dc = (j >> CHUNK_SHIFT) - (r >> CHUNK_SHIFT)
    tp_ref[...] = jnp.where(jnp.logical_and(dc >= 0, dc <= BAND_LEFT_CHUNKS), tbl, NEG_INF)
    jd = lax.broadcasted_iota(jnp.int32, (n_new, r_band + NEW_PAD), 1)
    td_ref[...] = jnp.where(jd < r_band + n_new, tbl[:n_new, :r_band + NEW_PAD], NEG_INF)


def _band_bias_tables(rel_bias, n_new, r_band):
    h = rel_bias.shape[0]
    assert BAND_ROWS == 2 * MAX_REL and r_band == BAND_ROWS and BIAS_LANES == 2 * BAND_ROWS
    rb = rel_bias.astype(F32)
    g = jnp.concatenate([rb[:, :0:-1], jnp.broadcast_to(rb[:, -1:], (h, BIAS_LANES - 2 * MAX_REL))], axis=1)
    kern = functools.partial(_band_bias_kernel, n_new=n_new, r_band=r_band)
    return pl.pallas_call(
        kern,
        grid=(h,),
        in_specs=[pl.BlockSpec((None, 1, BIAS_LANES), lambda i: (i, 0, 0))],
        out_specs=[pl.BlockSpec((None, BAND_TQ, BAND_WIN), lambda i: (i, 0, 0)),
                   pl.BlockSpec((None, n_new, r_band + NEW_PAD), lambda i: (i, 0, 0))],
        out_shape=[jax.ShapeDtypeStruct((h, BAND_TQ, BAND_WIN), F32),
                   jax.ShapeDtypeStruct((h, n_new, r_band + NEW_PAD), F32)],
        compiler_params=_params(1),
        name="band_bias",
    )(g.reshape(h, 1, BIAS_LANES))


def _band_prompt_kernel(q_ref, k_ref, v_ref, bias_ref, o_ref, kpad_ref, vpad_ref, *, t):
    s_idx = pl.program_id(2)

    @pl.when(s_idx == 0)
    def _():
        kpad_ref[0:BAND_ROWS, :] = jnp.zeros((BAND_ROWS, HEAD_DIM), BF16)
        vpad_ref[0:BAND_ROWS, :] = jnp.zeros((BAND_ROWS, HEAD_DIM), BF16)
        kpad_ref[BAND_ROWS:BAND_ROWS + t, :] = k_ref[...]
        vpad_ref[BAND_ROWS:BAND_ROWS + t, :] = v_ref[...]

    col = lax.broadcasted_iota(jnp.int32, (BAND_TQ, BAND_WIN), 1)
    for gg in range(BAND_STEP_GROUPS):
        g = s_idx * BAND_STEP_GROUPS + gg
        start = pl.multiple_of(g * BAND_TQ, BAND_TQ)
        rows = slice(gg * BAND_TQ, (gg + 1) * BAND_TQ)
        s = _nt_dot(q_ref[rows, :], kpad_ref[pl.ds(start, BAND_WIN), :]) + bias_ref[...]
        s = jnp.where(col + g * BAND_TQ >= BAND_ROWS, s, NEG_INF)
        o_ref[rows, :] = _softmax2_pv([(s, vpad_ref[pl.ds(start, BAND_WIN), :])]).astype(BF16)


def _band_prompt(ya3, ykv3, bias_tbl):
    b, t, _ = ya3.shape
    tq = BAND_TQ * BAND_STEP_GROUPS
    qb, kb, vb = QG_BD_Q // HEAD_DIM, KV_BD_K // HEAD_DIM, KV_BD_V // HEAD_DIM
    kern = functools.partial(_band_prompt_kernel, t=t)
    return pl.pallas_call(
        kern,
        grid=(b, BAND_HEADS, t // tq),
        in_specs=[
            pl.BlockSpec((None, tq, HEAD_DIM), lambda b, h, g: (b, g, qb + h)),
            pl.BlockSpec((None, t, HEAD_DIM), lambda b, h, g: (b, 0, kb + h)),
            pl.BlockSpec((None, t, HEAD_DIM), lambda b, h, g: (b, 0, vb + h)),
            pl.BlockSpec((None, BAND_TQ, BAND_WIN), lambda b, h, g: (h, 0, 0)),
        ],
        out_specs=pl.BlockSpec((None, tq, HEAD_DIM), lambda b, h, g: (b, g, h)),
        out_shape=jax.ShapeDtypeStruct((b, t, BAND_WIDTH), BF16),
        scratch_shapes=[pltpu.VMEM((BAND_ROWS + t, HEAD_DIM), BF16),
                        pltpu.VMEM((BAND_ROWS + t, HEAD_DIM), BF16)],
        compiler_params=_params(3),
        name="band_prompt",
    )(ya3, ykv3, ykv3, bias_tbl)


def _band_decode_kernel(q_ref, kn_ref, vn_ref, kc_ref, vc_ref, bias_ref, o_ref, kpad_ref, vpad_ref,
                        *, n_new, r_band):
    kpad_ref[...] = jnp.zeros_like(kpad_ref)
    vpad_ref[...] = jnp.zeros_like(vpad_ref)
    for h in range(BAND_HEADS):
        cols = slice(h * HEAD_DIM, (h + 1) * HEAD_DIM)
        kpad_ref[h, 0:n_new, :] = kn_ref[:, cols]
        vpad_ref[h, 0:n_new, :] = vn_ref[:, cols]
    for h in range(BAND_HEADS):
        cols = slice(h * HEAD_DIM, (h + 1) * HEAD_DIM)
        q = q_ref[:, cols]
        s_cache = _nt_dot(q, kc_ref[h].astype(BF16)) + bias_ref[h, :, 0:r_band]
        s_new = _nt_dot(q, kpad_ref[h]) + bias_ref[h, :, r_band:r_band + NEW_PAD]
        o_ref[:, cols] = _softmax2_pv([(s_cache, vc_ref[h].astype(BF16)),
                                       (s_new, vpad_ref[h])]).astype(BF16)


def _band_decode(ya3, ykv3, cache_k, cache_v, bias_tbl):
    bd, n_new, _ = ya3.shape
    r_band = cache_k.shape[2]
    kern = functools.partial(_band_decode_kernel, n_new=n_new, r_band=r_band)
    cache_spec = pl.BlockSpec((None, BAND_HEADS, r_band, HEAD_DIM), lambda b: (b, 0, 0, 0))
    return pl.pallas_call(
        kern,
        grid=(bd,),
        in_specs=[
            pl.BlockSpec((None, n_new, BAND_WIDTH), lambda b: (b, 0, QG_BD_Q // BAND_WIDTH)),
            pl.BlockSpec((None, n_new, BAND_WIDTH), lambda b: (b, 0, KV_BD_K // BAND_WIDTH)),
            pl.BlockSpec((None, n_new, BAND_WIDTH), lambda b: (b, 0, KV_BD_V // BAND_WIDTH)),
            cache_spec,
            cache_spec,
            pl.BlockSpec((BAND_HEADS, n_new, r_band + NEW_PAD), lambda b: (0, 0, 0)),
        ],
        out_specs=pl.BlockSpec((None, n_new, BAND_WIDTH), lambda b: (b, 0, 0)),
        out_shape=jax.ShapeDtypeStruct((bd, n_new, BAND_WIDTH), BF16),
        scratch_shapes=[pltpu.VMEM((BAND_HEADS, NEW_PAD, HEAD_DIM), BF16),
                        pltpu.VMEM((BAND_HEADS, NEW_PAD, HEAD_DIM), BF16)],
        compiler_params=_params(1),
        name="band_decode",
    )(ya3, ykv3, ykv3, cache_k, cache_v, bias_tbl)


def _mem_attn_kernel(q_ref, mk_ref, mv_ref, o_ref):
    for h in range(MEM_HEADS):
        sl = slice(h * HEAD_DIM, (h + 1) * HEAD_DIM)
        s = _nt_dot(q_ref[:, sl], mk_ref[:, sl].astype(BF16))
        o_ref[:, sl] = _softmax2_pv([(s, mv_ref[:, sl].astype(BF16))]).astype(BF16)


def _mem_attention(y3, mk, mv, *, tq):
    b, t, _ = y3.shape
    n_mem = mk.shape[1]
    qb = MG_MM_Q // MEM_WIDTH
    return pl.pallas_call(
        _mem_attn_kernel,
        grid=(b, t // tq),
        in_specs=[
            pl.BlockSpec((None, tq, MEM_WIDTH), lambda b, i: (b, i, qb)),
            pl.BlockSpec((None, n_mem, MEM_WIDTH), lambda b, i: (b, 0, 0)),
            pl.BlockSpec((None, n_mem, MEM_WIDTH), lambda b, i: (b, 0, 0)),
        ],
        out_specs=pl.BlockSpec((None, tq, MEM_WIDTH), lambda b, i: (b, i, 0)),
        out_shape=jax.ShapeDtypeStruct((b, t, MEM_WIDTH), BF16),
        compiler_params=_params(2),
        name="mem_attention",
    )(y3, mk, mv)


def _silu_of_half(h):
    return h + h * jnp.tanh(h)


def _merge_kernel(osb_ref, obd_ref, omm_ref, gsb_ref, gbd_ref, gmm_ref,
                  mg0_ref, mg1_ref, mg2_ref, mg3_ref, mg4_ref, mg5_ref,
                  wsb_ref, wbd_ref, wmm_ref, merged_ref, *, half):
    u_sb = (osb_ref[...].astype(F32) * _silu_of_half(gsb_ref[...].astype(F32))).astype(BF16)
    u_bd = (obd_ref[...].astype(F32) * _silu_of_half(gbd_ref[...].astype(F32))).astype(BF16)
    u_mm = (omm_ref[...].astype(F32) * _silu_of_half(gmm_ref[...].astype(F32))).astype(BF16)
    mg = ((mg0_ref, mg2_ref, mg4_ref), (mg1_ref, mg3_ref, mg5_ref))
    for n in range(2):
        cols = slice(n * half, (n + 1) * half)
        merged = None
        for m_ref, u, w_ref in zip(mg[n], (u_sb, u_bd, u_mm), (wsb_ref, wbd_ref, wmm_ref)):
            a = _dot(u, w_ref[:, cols])
            term = a + a * jnp.tanh(m_ref[...].astype(F32))
            merged = term if merged is None else merged + term
        merged_ref[:, cols] = merged.astype(BF16)


def _merge_branches(yqg, ymg, o_sb, o_bd, o_mm, w_sb, w_bd, w_mm, *, tm):
    m = yqg.shape[0]
    d = w_sb.shape[1]
    half = d // 2
    assert MG_MG % half == 0
    mgb = MG_MG // half
    const = dict(pipeline_mode=pl.Buffered(1))
    kern = functools.partial(_merge_kernel, half=half)
    return pl.pallas_call(
        kern,
        grid=(m // tm,),
        in_specs=[
            pl.BlockSpec((tm, SB_WIDTH), lambda i: (i, 0)),
            pl.BlockSpec((tm, BAND_WIDTH), lambda i: (i, 0)),
            pl.BlockSpec((tm, MEM_WIDTH), lambda i: (i, 0)),
            pl.BlockSpec((tm, SB_WIDTH), lambda i: (i, QG_SB_G // SB_WIDTH)),
            pl.BlockSpec((tm, BAND_WIDTH), lambda i: (i, QG_BD_G // BAND_WIDTH)),
            pl.BlockSpec((tm, MEM_WIDTH), lambda i: (i, MG_MM_G // MEM_WIDTH)),
        ] + [pl.BlockSpec((tm, half), functools.partial(lambda i, c: (i, c), c=mgb + c)) for c in range(6)] + [
            pl.BlockSpec((SB_WIDTH, d), lambda i: (0, 0), **const),
            pl.BlockSpec((BAND_WIDTH, d), lambda i: (0, 0), **const),
            pl.BlockSpec((MEM_WIDTH, d), lambda i: (0, 0), **const),
        ],
        out_specs=pl.BlockSpec((tm, d), lambda i: (i, 0)),
        out_shape=jax.ShapeDtypeStruct((m, d), BF16),
        compiler_params=_params(1),
        name="merge_branches",
    )(o_sb, o_bd, o_mm, yqg, yqg, ymg, *([ymg] * 6), w_sb, w_bd, w_mm)


def _out_proj_kernel(x_ref, merged_ref, wout_ref, gpost_ref, y_ref):
    y = _dot(merged_ref[...], wout_ref[...])
    ms = jnp.mean(y * y, axis=-1, keepdims=True)
    y_ref[...] = x_ref[...] + (y * lax.rsqrt(ms + RMS_EPS)) * gpost_ref[...]


def _out_projection(x2d, merged, w_out, g_post, *, tm):
    m, d = x2d.shape
    return pl.pallas_call(
        _out_proj_kernel,
        grid=(m // tm,),
        in_specs=[
            pl.BlockSpec((tm, d), lambda i: (i, 0)),
            pl.BlockSpec((tm, d), lambda i: (i, 0)),
            pl.BlockSpec((d, d), lambda i: (0, 0), pipeline_mode=pl.Buffered(1)),
            pl.BlockSpec((1, d), lambda i: (0, 0)),
        ],
        out_specs=pl.BlockSpec((tm, d), lambda i: (i, 0)),
        out_shape=jax.ShapeDtypeStruct((m, d), F32),
        compiler_params=_params(1),
        name="out_projection",
    )(x2d, merged, w_out, g_post.reshape(1, d))


def _head_major(a):
    return jnp.transpose(a, (0, 2, 1, 3))


def kernel(x_prompt, x_sample, cache_sb_k, cache_sb_v, cache_band_k, cache_band_v, cache_mem_k, cache_mem_v, mem_prompt, g_pre, w_in, rel_bias, g_mem, w_mem_kv, w_up_sb, w_up_band, w_up_mem, w_out, g_post):
    depth = w_in.shape[0]
    b, t, d = x_prompt.shape
    bd, n_new, _ = x_sample.shape
    n_mem = mem_prompt.shape[1]
    r_band = cache_band_k.shape[2]
    in_width = w_in.shape[2]
    band_keep = min(BAND_ROWS, t)
    tm_p = 512
    assert COL_MG + 3 * d == in_width
    assert t % SB_TQ == 0 and t % (BAND_TQ * BAND_STEP_GROUPS) == 0 and t % tm_p == 0
    assert r_band == BAND_ROWS and n_new <= CHUNK

    negu2 = jnp.where(jnp.arange(2 * SB_TK)[:, None] % SB_TK >= jnp.arange(SB_TK)[None, :], -1.0, 0.0).astype(BF16)
    mg_width = in_width - COL_MM_Q
    assert COL_MM_Q % MG_TN == 0 and mg_width % MG_TN == 0 and COL_SB_G == 3 * KV_TN and COL_BD_Q == 4 * KV_TN
    cols = jnp.arange(QG_WIDTH)
    qg_scale = jnp.where((cols // SB_WIDTH) % 2 == 0, Q_SCALE, 0.5).astype(F32).reshape(1, QG_WIDTH)
    cols = jnp.arange(mg_width)
    mg_scale = jnp.where(cols < MG_MM_G, Q_SCALE, 0.5).astype(F32).reshape(1, mg_width)
    qg_block = lambda j: j + 2 * ((j + 1) // 2)
    mg_block = lambda j: j + COL_MM_Q // MG_TN

    xp = x_prompt.reshape(b * t, d)
    xs = x_sample.reshape(bd * n_new, d)
    ms = bd * n_new
    outs = [[] for _ in range(10)]
    for l in range(depth):
        w_kvp_b = _kv_weight_bf16(w_in[l])
        w_kv_b = w_mem_kv[l].astype(BF16)
        w_sb_b = (0.5 * w_up_sb[l]).astype(BF16)
        w_bd_b = (0.5 * w_up_band[l]).astype(BF16)
        w_mm_b = (0.5 * w_up_mem[l]).astype(BF16)
        w_out_b = w_out[l].astype(BF16)
        bias_p, bias_d = _band_bias_tables(rel_bias[l], n_new, r_band)

        ykv, hp, sbk, sbv, bdk, bdv = _kv_projection(xp, g_pre[l], w_kvp_b, tm=band_keep, seqs=1, n_seq=b,
                                                     band_keep=band_keep)
        ykv_s, hs, sbk2, sbv2, bdk2, bdv2 = _kv_projection(xs, g_pre[l], w_kvp_b, tm=ms, seqs=bd, n_seq=bd,
                                                           band_keep=n_new)
        yqg, yqg_s = _col_projection(hp, hs, w_in[l], qg_scale, qg_block, tm=1024, tn=KV_TN,
                                     name="qg_projection")
        ymg, ymg_s = _col_projection(hp, hs, w_in[l], mg_scale, mg_block, tm=1024, tn=MG_TN,
                                     name="mg_projection")

        mk, mv = _memory_kv(mem_prompt.reshape(b * n_mem, d), g_mem[l], w_kv_b, tm=n_mem)
        yqg3 = yqg.reshape(b, t, QG_WIDTH)
        ykv3 = ykv.reshape(b, t, KV_WIDTH)
        o_sb = _sb_prompt(yqg3, ykv3, negu2)
        o_bd = _band_prompt(yqg3, ykv3, bias_p)
        o_mm = _mem_attention(ymg.reshape(b, t, mg_width), mk.reshape(b, n_mem, MEM_WIDTH),
                              mv.reshape(b, n_mem, MEM_WIDTH), tq=512)
        merged = _merge_branches(yqg, ymg, o_sb.reshape(b * t, -1), o_bd.reshape(b * t, -1),
                                 o_mm.reshape(b * t, -1), w_sb_b, w_bd_b, w_mm_b, tm=512)
        xp = _out_projection(xp, merged, w_out_b, g_post[l], tm=512)
        outs[0].append(_head_major(sbk))
        outs[1].append(_head_major(sbv))
        outs[2].append(_head_major(bdk))
        outs[3].append(_head_major(bdv))
        outs[4].append(mk.reshape(b, n_mem, MEM_HEADS, HEAD_DIM))
        outs[5].append(mv.reshape(b, n_mem, MEM_HEADS, HEAD_DIM))

        yqg_s3 = yqg_s.reshape(bd, n_new, QG_WIDTH)
        ykv_s3 = ykv_s.reshape(bd, n_new, KV_WIDTH)
        o_sb2 = _sb_decode(yqg_s3, ykv_s3, _head_major(cache_sb_k[l]), _head_major(cache_sb_v[l]), negu2)
        o_bd2 = _band_decode(yqg_s3, ykv_s3, _head_major(cache_band_k[l]), _head_major(cache_band_v[l]), bias_d)
        o_mm2 = _mem_attention(ymg_s.reshape(bd, n_new, mg_width), cache_mem_k[l].reshape(bd, n_mem, MEM_WIDTH),
                               cache_mem_v[l].reshape(bd, n_mem, MEM_WIDTH), tq=n_new)
        merged_s = _merge_branches(yqg_s, ymg_s, o_sb2.reshape(ms, -1), o_bd2.reshape(ms, -1),
                                   o_mm2.reshape(ms, -1), w_sb_b, w_bd_b, w_mm_b, tm=ms)
        xs = _out_projection(xs, merged_s, w_out_b, g_post[l], tm=ms)
        outs[6].append(_head_major(sbk2))
        outs[7].append(_head_major(sbv2))
        outs[8].append(_head_major(bdk2))
        outs[9].append(_head_major(bdv2))

    return (xp.reshape(b, t, d), xs.reshape(bd, n_new, d)) + tuple(jnp.stack(o) for o in outs)
```

```python
import functools
import math

import jax
import jax.numpy as jnp
from jax import lax
from jax.experimental import pallas as pl
from jax.experimental.pallas import tpu as pltpu

F32 = jnp.float32
BF16 = jnp.bfloat16

HEAD_DIM = 128
SB_HEADS = 6
BAND_HEADS = 6
MEM_HEADS = 4
SB_WIDTH = SB_HEADS * HEAD_DIM
BAND_WIDTH = BAND_HEADS * HEAD_DIM
MEM_WIDTH = MEM_HEADS * HEAD_DIM
CHUNK = 64
CHUNK_SHIFT = 6
BAND_LEFT_CHUNKS = 8
BAND_ROWS = BAND_LEFT_CHUNKS * CHUNK
MAX_REL = 256
RMS_EPS = 1e-6
NEG_INF = -1e30
LOG2E = math.log2(math.e)
Q_SCALE = HEAD_DIM ** -0.5 * LOG2E

COL_SB_Q = 0
COL_SB_K = COL_SB_Q + SB_WIDTH
COL_SB_V = COL_SB_K + SB_WIDTH
COL_SB_G = COL_SB_V + SB_WIDTH
COL_BD_Q = COL_SB_G + SB_WIDTH
COL_BD_K = COL_BD_Q + BAND_WIDTH
COL_BD_V = COL_BD_K + BAND_WIDTH
COL_BD_G = COL_BD_V + BAND_WIDTH
COL_MM_Q = COL_BD_G + BAND_WIDTH
COL_MM_G = COL_MM_Q + MEM_WIDTH
COL_MG = COL_MM_G + MEM_WIDTH

KV_SB_K = 0
KV_SB_V = KV_SB_K + SB_WIDTH
KV_BD_K = KV_SB_V + SB_WIDTH
KV_BD_V = KV_BD_K + BAND_WIDTH
KV_WIDTH = KV_BD_V + BAND_WIDTH
QG_SB_Q = 0
QG_SB_G = QG_SB_Q + SB_WIDTH
QG_BD_Q = QG_SB_G + SB_WIDTH
QG_BD_G = QG_BD_Q + BAND_WIDTH
QG_WIDTH = QG_BD_G + BAND_WIDTH
MG_MM_Q = 0
MG_MM_G = MG_MM_Q + MEM_WIDTH
MG_MG = MG_MM_G + MEM_WIDTH

VMEM_LIMIT_BYTES = 56 * 1024 * 1024
MAX_VMEM_LIMIT_BYTES = 58 * 1024 * 1024
COMPILER_TEMP_BYTES = 2 * 1024 * 1024
PROJ_TM = 2048
KV_TN = SB_WIDTH
MG_TN = 1024
SB_TK = 256
SB_TQ = 8 * SB_TK
SB_DEAD = -160.0
BAND_TQ = 4 * CHUNK
BAND_WIN = BAND_TQ + BAND_ROWS
BAND_STEP_GROUPS = 8
BIAS_LANES = 1024
NEW_PAD = 128


def _params(n_axes, vmem=VMEM_LIMIT_BYTES):
    return pltpu.CompilerParams(dimension_semantics=("arbitrary",) * n_axes,
                                vmem_limit_bytes=vmem)


def _nt_dot(a, b):
    return lax.dot_general(a, b, (((1,), (1,)), ((), ())), preferred_element_type=F32)


def _dot(a, b):
    return jnp.dot(a, b, preferred_element_type=F32)


def _pre_norm_to(h_ref, x_ref, g_ref):
    x = x_ref[...]
    ms = jnp.mean(x * x, axis=-1, keepdims=True)
    h_ref[...] = ((x * lax.rsqrt(ms + RMS_EPS)) * g_ref[...]).astype(BF16)


def _kv_proj_kernel(x_ref, g_ref, wsk_ref, wsv_ref, wbk_ref, wbv_ref,
                    y_ref, h_ref, sbk_ref, sbv_ref, bdk_ref, bdv_ref, *, seqs, rows):
    _pre_norm_to(h_ref, x_ref, g_ref)
    groups = ((wsk_ref, sbk_ref), (wsv_ref, sbv_ref), (wbk_ref, bdk_ref), (wbv_ref, bdv_ref))
    for group, (w_ref, dst_ref) in enumerate(groups):
        acc = _dot(h_ref[...], w_ref[...])
        y_ref[:, group * KV_TN:(group + 1) * KV_TN] = acc.astype(BF16)
        for h in range(SB_HEADS):
            for s in range(seqs):
                dst_ref[s, h] = acc[s * rows:(s + 1) * rows, h * HEAD_DIM:(h + 1) * HEAD_DIM]


def _kv_projection(x2d, g_pre, w_kv_b, *, tm, seqs, n_seq, band_keep):
    m, d = x2d.shape
    assert KV_WIDTH == 4 * KV_TN and SB_HEADS == BAND_HEADS
    rows = tm // seqs
    seq_rows = m // n_seq
    blocks_per_seq = seq_rows // rows
    assert rows == band_keep
    kern = functools.partial(_kv_proj_kernel, seqs=seqs, rows=rows)
    sb_spec = pl.BlockSpec((seqs, SB_HEADS, rows, HEAD_DIM),
                           lambda i: (i // blocks_per_seq, 0, i % blocks_per_seq, 0))
    bd_spec = pl.BlockSpec((seqs, BAND_HEADS, rows, HEAD_DIM), lambda i: (i // blocks_per_seq, 0, 0, 0))
    sb_shape = jax.ShapeDtypeStruct((n_seq, SB_HEADS, seq_rows, HEAD_DIM), F32)
    bd_shape = jax.ShapeDtypeStruct((n_seq, BAND_HEADS, band_keep, HEAD_DIM), F32)

    def w_spec(group):
        return pl.BlockSpec((d, KV_TN), functools.partial(lambda i, c: (0, c), c=group),
                            pipeline_mode=pl.Buffered(1))

    return pl.pallas_call(
        kern,
        grid=(m // tm,),
        in_specs=[
            pl.BlockSpec((tm, d), lambda i: (i, 0)),
            pl.BlockSpec((1, d), lambda i: (0, 0)),
            w_spec(0), w_spec(1), w_spec(2), w_spec(3),
        ],
        out_specs=[pl.BlockSpec((tm, KV_WIDTH), lambda i: (i, 0)), pl.BlockSpec((tm, d), lambda i: (i, 0)),
                   sb_spec, sb_spec, bd_spec, bd_spec],
        out_shape=[jax.ShapeDtypeStruct((m, KV_WIDTH), BF16), jax.ShapeDtypeStruct((m, d), BF16),
                   sb_shape, sb_shape, bd_shape, bd_shape],
        compiler_params=_params(1),
        name="kv_projection",
    )(x2d, g_pre.reshape(1, d), w_kv_b, w_kv_b, w_kv_b, w_kv_b)


def _col_proj_kernel(h_ref, hs_ref, w_ref, cs_ref, y_ref, ys_ref, wb_ref):
    @pl.when(pl.program_id(1) == 0)
    def _():
        wb_ref[...] = w_ref[...].astype(BF16)
        ys_ref[...] = (_dot(hs_ref[...], wb_ref[...]) * cs_ref[...]).astype(BF16)

    y_ref[...] = (_dot(h_ref[...], wb_ref[...]) * cs_ref[...]).astype(BF16)


def _col_projection(h2d, hs2d, w_in, col_scale, src_block, *, tm, tn, name):
    m, d = h2d.shape
    ms = hs2d.shape[0]
    n = col_scale.shape[1]
    assert n % tn == 0 and m % tm == 0
    vmem = (2 * tm * d * 2 + 2 * d * tn * 4 + 2 * tm * tn * 2 + d * tn * 2 + 2 * tm * tn * 4
            + 2 * ms * (d + tn) * 2 + COMPILER_TEMP_BYTES)
    vmem = min(vmem, MAX_VMEM_LIMIT_BYTES)
    return pl.pallas_call(
        _col_proj_kernel,
        grid=(n // tn, m // tm),
        in_specs=[
            pl.BlockSpec((tm, d), lambda j, i: (i, 0)),
            pl.BlockSpec((ms, d), lambda j, i: (0, 0)),
            pl.BlockSpec((d, tn), lambda j, i: (0, src_block(j))),
            pl.BlockSpec((1, tn), lambda j, i: (0, j)),
        ],
        out_specs=[pl.BlockSpec((tm, tn), lambda j, i: (i, j)), pl.BlockSpec((ms, tn), lambda j, i: (0, j))],
        out_shape=[jax.ShapeDtypeStruct((m, n), BF16), jax.ShapeDtypeStruct((ms, n), BF16)],
        scratch_shapes=[pltpu.VMEM((d, tn), BF16)],
        compiler_params=_params(2, vmem=vmem),
        name=name,
    )(h2d, hs2d, w_in, col_scale)


def _kv_weight_kernel(a_ref, b_ref, c_ref, d_ref, o_ref):
    for group, w_ref in enumerate((a_ref, b_ref, c_ref, d_ref)):
        o_ref[:, group * KV_TN:(group + 1) * KV_TN] = w_ref[...].astype(BF16)


def _kv_weight_bf16(w):
    d = w.shape[0]
    tr = 256
    assert d % tr == 0

    def spec(col0):
        assert col0 % KV_TN == 0
        return pl.BlockSpec((tr, KV_TN), functools.partial(lambda i, c: (i, c), c=col0 // KV_TN))

    return pl.pallas_call(
        _kv_weight_kernel,
        grid=(d // tr,),
        in_specs=[spec(COL_SB_K), spec(COL_SB_V), spec(COL_BD_K), spec(COL_BD_V)],
        out_specs=pl.BlockSpec((tr, KV_WIDTH), lambda i: (i, 0)),
        out_shape=jax.ShapeDtypeStruct((d, KV_WIDTH), BF16),
        compiler_params=_params(1),
        name="kv_weight_cast",
    )(w, w, w, w)


def _memkv_kernel(x_ref, g_ref, w_ref, mk_ref, mv_ref):
    x = x_ref[...]
    ms = jnp.mean(x * x, axis=-1, keepdims=True)
    h = ((x * lax.rsqrt(ms + RMS_EPS)) * g_ref[...]).astype(BF16)
    acc = _dot(h, w_ref[...])
    mk_ref[...] = acc[:, :MEM_WIDTH]
    mv_ref[...] = acc[:, MEM_WIDTH:]


def _memory_kv(mem2d, g_mem, w_bf16, *, tm):
    m, d = mem2d.shape
    return pl.pallas_call(
        _memkv_kernel,
        grid=(m // tm,),
        in_specs=[
            pl.BlockSpec((tm, d), lambda i: (i, 0)),
            pl.BlockSpec((1, d), lambda i: (0, 0)),
            pl.BlockSpec((d, 2 * MEM_WIDTH), lambda i: (0, 0)),
        ],
        out_specs=[pl.BlockSpec((tm, MEM_WIDTH), lambda i: (i, 0)),
                   pl.BlockSpec((tm, MEM_WIDTH), lambda i: (i, 0))],
        out_shape=[jax.ShapeDtypeStruct((m, MEM_WIDTH), F32),
                   jax.ShapeDtypeStruct((m, MEM_WIDTH), F32)],
        compiler_params=_params(1),
        name="memory_kv",
    )(mem2d, g_mem.reshape(1, d), w_bf16)


def _neg_suffix_matrix(n):
    row = lax.broadcasted_iota(jnp.int32, (2 * n, n), 0)
    col = lax.broadcasted_iota(jnp.int32, (2 * n, n), 1)
    row = jnp.where(row >= n, row - n, row)
    return jnp.where(row >= col, -1.0, 0.0).astype(BF16)


def _sb_weights(z2, carry2, negu2, mask):
    p = jnp.maximum(z2, 0.0) + jnp.log(1.0 + jnp.exp2(-jnp.abs(z2))) * LOG2E
    if mask is not None:
        p = jnp.where(mask, p, 0.0)
    p_hi = p.astype(BF16)
    p_lo = (p - p_hi.astype(F32)).astype(BF16)
    suffix = _dot(jnp.concatenate([p_hi, p_lo], axis=1), negu2)
    w = jnp.exp2(z2 + suffix + carry2)
    if mask is not None:
        w = jnp.where(mask, w, 0.0)
    return w, carry2 - jnp.sum(p, axis=-1, keepdims=True)


def _sb_prompt_kernel(q_ref, k_ref, v_ref, negu2_ref, o_ref, acc_ref, carry_ref, kpad_ref, vpad_ref, *, t):
    i = pl.program_id(2)
    n_sub = SB_TQ // SB_TK
    negu2 = negu2_ref[...]

    @pl.when(i == 0)
    def _():
        kpad_ref[0:SB_TK, :] = jnp.zeros((SB_TK, HEAD_DIM), BF16)
        vpad_ref[0:SB_TK, :] = jnp.zeros((SB_TK, HEAD_DIM), BF16)
        kpad_ref[SB_TK:SB_TK + t, :] = k_ref[...]
        vpad_ref[SB_TK:SB_TK + t, :] = v_ref[...]

    def kv_block(j):
        start = pl.multiple_of((j + 1) * SB_TK, SB_TK)
        return kpad_ref[pl.ds(start, SB_TK), :], vpad_ref[pl.ds(start, SB_TK), :]

    row = lax.broadcasted_iota(jnp.int32, (SB_TK, SB_TK), 0)
    col = lax.broadcasted_iota(jnp.int32, (SB_TK, SB_TK), 1)
    for r in range(n_sub):
        rows = slice(r * SB_TK, (r + 1) * SB_TK)
        s = i * n_sub + r
        q = q_ref[rows, :]
        kb, vb = kv_block(s)
        w, carry = _sb_weights(_nt_dot(q, kb), jnp.zeros((SB_TK, 1), F32), negu2, col < row)
        acc = _dot(w.astype(BF16), vb)
        kb, vb = kv_block(s - 1)
        prev_exists = None if r > 0 else (jnp.zeros((SB_TK, SB_TK), jnp.int32) + i) > 0
        w, carry = _sb_weights(_nt_dot(q, kb), carry, negu2, prev_exists)
        acc_ref[rows, :] = acc + _dot(w.astype(BF16), vb)
        carry_ref[rows, :] = carry

    row_q = lax.broadcasted_iota(jnp.int32, (SB_TQ, 1), 0)
    row_t = lax.broadcasted_iota(jnp.int32, (SB_TQ, SB_TK), 0)
    has_more = row_q >= (2 - n_sub * i) * SB_TK

    def any_alive(carry):
        return (jnp.max(jnp.where(has_more, carry, NEG_INF)) > SB_DEAD).astype(jnp.int32)

    def cond(state):
        j, alive = state
        return jnp.logical_and(j >= 0, alive > 0)

    def body(state):
        j, _ = state
        kb, vb = kv_block(j)
        visits = row_t >= (j - n_sub * i + 2) * SB_TK
        w, carry = _sb_weights(_nt_dot(q_ref[...], kb), carry_ref[...], negu2, visits)
        acc_ref[...] += _dot(w.astype(BF16), vb)
        carry_ref[...] = carry
        return j - 1, any_alive(carry)

    lax.while_loop(cond, body, (n_sub * i + n_sub - 3, any_alive(carry_ref[...])))
    o_ref[...] = acc_ref[...].astype(BF16)


def _sb_prompt(ya3, ykv3, negu2):
    b, t, _ = ya3.shape
    qb, kb, vb = QG_SB_Q // HEAD_DIM, KV_SB_K // HEAD_DIM, KV_SB_V // HEAD_DIM
    assert SB_TQ // SB_TK >= 3
    kern = functools.partial(_sb_prompt_kernel, t=t)
    return pl.pallas_call(
        kern,
        grid=(b, SB_HEADS, t // SB_TQ),
        in_specs=[
            pl.BlockSpec((None, SB_TQ, HEAD_DIM), lambda b, h, i: (b, i, qb + h)),
            pl.BlockSpec((None, t, HEAD_DIM), lambda b, h, i: (b, 0, kb + h)),
            pl.BlockSpec((None, t, HEAD_DIM), lambda b, h, i: (b, 0, vb + h)),
            pl.BlockSpec((2 * SB_TK, SB_TK), lambda b, h, i: (0, 0)),
        ],
        out_specs=pl.BlockSpec((None, SB_TQ, HEAD_DIM), lambda b, h, i: (b, i, h)),
        out_shape=jax.ShapeDtypeStruct((b, t, SB_WIDTH), BF16),
        scratch_shapes=[pltpu.VMEM((SB_TQ, HEAD_DIM), F32), pltpu.VMEM((SB_TQ, 1), F32),
                        pltpu.VMEM((SB_TK + t, HEAD_DIM), BF16), pltpu.VMEM((SB_TK + t, HEAD_DIM), BF16)],
        compiler_params=_params(3),
        name="sb_prompt",
    )(ya3, ykv3, ykv3, negu2)


def _sb_decode_kernel(q_ref, kn_ref, vn_ref, kc_hbm, vc_hbm, negu2_ref, o_ref,
                      acc_ref, carry_ref, kpad_ref, vpad_ref, kbuf_ref, vbuf_ref, sem,
                      *, n_new, n_blocks):
    b = pl.program_id(0)
    heads = SB_HEADS

    def cache_copies(j, slot):
        rows = pl.ds(pl.multiple_of((n_blocks - 1 - j) * SB_TK, SB_TK), SB_TK)
        return (pltpu.make_async_copy(kc_hbm.at[b, :, rows, :], kbuf_ref.at[slot], sem.at[0, slot]),
                pltpu.make_async_copy(vc_hbm.at[b, :, rows, :], vbuf_ref.at[slot], sem.at[1, slot]))

    def start_fetch(j, slot):
        for cp in cache_copies(j, slot):
            cp.start()

    def wait_fetch(j, slot):
        for cp in cache_copies(j, slot):
            cp.wait()

    start_fetch(0, 0)

    def head_cols(h):
        return slice(h * HEAD_DIM, (h + 1) * HEAD_DIM)

    def head_rows(h):
        return slice(h * n_new, (h + 1) * n_new)

    def block(k_of, v_of, negu2, mask):
        z2 = jnp.concatenate([_nt_dot(q_ref[:, head_cols(h)], k_of(h)) for h in range(heads)], axis=0)
        w, carry = _sb_weights(z2, carry_ref[...], negu2, mask)
        wb = w.astype(BF16)
        for h in range(heads):
            acc_ref[head_rows(h), :] += _dot(wb[head_rows(h), :], v_of(h))
        carry_ref[...] = carry

    def any_alive():
        return (jnp.max(carry_ref[...]) > SB_DEAD).astype(jnp.int32)

    acc_ref[...] = jnp.zeros_like(acc_ref)
    carry_ref[...] = jnp.zeros_like(carry_ref)
    kpad_ref[...] = jnp.zeros_like(kpad_ref)
    vpad_ref[...] = jnp.zeros_like(vpad_ref)
    for h in range(heads):
        kpad_ref[h, 0:n_new, :] = kn_ref[:, head_cols(h)]
        vpad_ref[h, 0:n_new, :] = vn_ref[:, head_cols(h)]
    row = lax.broadcasted_iota(jnp.int32, (n_new, NEW_PAD), 0)
    col = lax.broadcasted_iota(jnp.int32, (n_new, NEW_PAD), 1)
    mask = jnp.concatenate([(col < row).astype(jnp.int32)] * heads, axis=0) == 1
    block(lambda h: kpad_ref[h], lambda h: vpad_ref[h], _neg_suffix_matrix(NEW_PAD), mask)

    negu2 = negu2_ref[...]

    def cond(state):
        j, alive = state
        return jnp.logical_and(j < n_blocks, alive > 0)

    def body(state):
        j, _ = state
        slot = j % 2
        wait_fetch(j, slot)

        @pl.when(j + 1 < n_blocks)
        def _():
            start_fetch(j + 1, 1 - slot)

        block(lambda h: kbuf_ref[slot, h].astype(BF16), lambda h: vbuf_ref[slot, h].astype(BF16),
              negu2, None)
        return j + 1, any_alive()

    j_end, _ = lax.while_loop(cond, body, (0, any_alive()))

    @pl.when(j_end < n_blocks)
    def _():
        wait_fetch(j_end, j_end % 2)

    for h in range(heads):
        o_ref[:, head_cols(h)] = acc_ref[head_rows(h), :].astype(BF16)


def _sb_decode(ya3, ykv3, cache_k, cache_v, negu2):
    bd, n_new, _ = ya3.shape
    past = cache_k.shape[2]
    assert past % SB_TK == 0 and n_new <= NEW_PAD and n_new % 16 == 0
    kern = functools.partial(_sb_decode_kernel, n_new=n_new, n_blocks=past // SB_TK)
    return pl.pallas_call(
        kern,
        grid=(bd,),
        in_specs=[
            pl.BlockSpec((None, n_new, SB_WIDTH), lambda b: (b, 0, QG_SB_Q // SB_WIDTH)),
            pl.BlockSpec((None, n_new, SB_WIDTH), lambda b: (b, 0, KV_SB_K // SB_WIDTH)),
            pl.BlockSpec((None, n_new, SB_WIDTH), lambda b: (b, 0, KV_SB_V // SB_WIDTH)),
            pl.BlockSpec(memory_space=pl.ANY),
            pl.BlockSpec(memory_space=pl.ANY),
            pl.BlockSpec((2 * SB_TK, SB_TK), lambda b: (0, 0)),
        ],
        out_specs=pl.BlockSpec((None, n_new, SB_WIDTH), lambda b: (b, 0, 0)),
        out_shape=jax.ShapeDtypeStruct((bd, n_new, SB_WIDTH), BF16),
        scratch_shapes=[pltpu.VMEM((SB_HEADS * n_new, HEAD_DIM), F32),
                        pltpu.VMEM((SB_HEADS * n_new, 1), F32),
                        pltpu.VMEM((SB_HEADS, NEW_PAD, HEAD_DIM), BF16),
                        pltpu.VMEM((SB_HEADS, NEW_PAD, HEAD_DIM), BF16),
                        pltpu.VMEM((2, SB_HEADS, SB_TK, HEAD_DIM), F32),
                        pltpu.VMEM((2, SB_HEADS, SB_TK, HEAD_DIM), F32),
                        pltpu.SemaphoreType.DMA((2, 2))],
        compiler_params=_params(1),
        name="sb_decode",
    )(ya3, ykv3, ykv3, cache_k, cache_v, negu2)


def _softmax2_pv(parts):
    mx = functools.reduce(jnp.maximum, [jnp.max(s, axis=-1, keepdims=True) for s, _ in parts])
    num = None
    den = None
    for s, v in parts:
        p = jnp.exp2(s - mx)
        d = jnp.sum(p, axis=-1, keepdims=True)
        o = _dot(p.astype(BF16), v)
        num = o if num is None else num + o
        den = d if den is None else den + d
    return num / den


def _band_bias_kernel(g_ref, tp_ref, td_ref, *, n_new, r_band):
    x = jnp.broadcast_to(g_ref[...], (BAND_TQ, BIAS_LANES))
    x = pltpu.roll(x, BAND_TQ, 1, stride=1, stride_axis=0)
    tbl = x[:, :BAND_WIN] * LOG2E
    r = lax.broadcasted_iota(jnp.int32, (BAND_TQ, BAND_WIN), 0)
    j = lax.broadcasted_iota(jnp.int32, (BAND_TQ, BAND_WIN), 1)
    dc = (j >> CHUNK_SHIFT) - (r >> CHUNK_SHIFT)
    tp_ref[...] = jnp.where(jnp.logical_and(dc >= 0, dc <= BAND_LEFT_CHUNKS), tbl, NEG_INF)
    jd = lax.broadcasted_iota(jnp.int32, (n_new, r_band + NEW_PAD), 1)
    td_ref[...] = jnp.where(jd < r_band + n_new, tbl[:n_new, :r_band + NEW_PAD], NEG_INF)


def _band_bias_tables(rel_bias, n_new, r_band):
    h = rel_bias.shape[0]
    assert BAND_ROWS == 2 * MAX_REL and r_band == BAND_ROWS and BIAS_LANES == 2 * BAND_ROWS
    rb = rel_bias.astype(F32)
    g = jnp.concatenate([rb[:, :0:-1], jnp.broadcast_to(rb[:, -1:], (h, BIAS_LANES - 2 * MAX_REL))], axis=1)
    kern = functools.partial(_band_bias_kernel, n_new=n_new, r_band=r_band)
    return pl.pallas_call(
        kern,
        grid=(h,),
        in_specs=[pl.BlockSpec((None, 1, BIAS_LANES), lambda i: (i, 0, 0))],
        out_specs=[pl.BlockSpec((None, BAND_TQ, BAND_WIN), lambda i: (i, 0, 0)),
                   pl.BlockSpec((None, n_new, r_band + NEW_PAD), lambda i: (i, 0, 0))],
        out_shape=[jax.ShapeDtypeStruct((h, BAND_TQ, BAND_WIN), F32),
                   jax.ShapeDtypeStruct((h, n_new, r_band + NEW_PAD), F32)],
        compiler_params=_params(1),
        name="band_bias",
    )(g.reshape(h, 1, BIAS_LANES))


def _band_prompt_kernel(q_ref, k_ref, v_ref, bias_ref, o_ref, kpad_ref, vpad_ref, *, t):
    s_idx = pl.program_id(2)

    @pl.when(s_idx == 0)
    def _():
        kpad_ref[0:BAND_ROWS, :] = jnp.zeros((BAND_ROWS, HEAD_DIM), BF16)
        vpad_ref[0:BAND_ROWS, :] = jnp.zeros((BAND_ROWS, HEAD_DIM), BF16)
        kpad_ref[BAND_ROWS:BAND_ROWS + t, :] = k_ref[...]
        vpad_ref[BAND_ROWS:BAND_ROWS + t, :] = v_ref[...]

    col = lax.broadcasted_iota(jnp.int32, (BAND_TQ, BAND_WIN), 1)
    for gg in range(BAND_STEP_GROUPS):
        g = s_idx * BAND_STEP_GROUPS + gg
        start = pl.multiple_of(g * BAND_TQ, BAND_TQ)
        rows = slice(gg * BAND_TQ, (gg + 1) * BAND_TQ)
        s = _nt_dot(q_ref[rows, :], kpad_ref[pl.ds(start, BAND_WIN), :]) + bias_ref[...]
        s = jnp.where(col + g * BAND_TQ >= BAND_ROWS, s, NEG_INF)
        o_ref[rows, :] = _softmax2_pv([(s, vpad_ref[pl.ds(start, BAND_WIN), :])]).astype(BF16)


def _band_prompt(ya3, ykv3, bias_tbl):
    b, t, _ = ya3.shape
    tq = BAND_TQ * BAND_STEP_GROUPS
    qb, kb, vb = QG_BD_Q // HEAD_DIM, KV_BD_K // HEAD_DIM, KV_BD_V // HEAD_DIM
    kern = functools.partial(_band_prompt_kernel, t=t)
    return pl.pallas_call(
        kern,
        grid=(b, BAND_HEADS, t // tq),
        in_specs=[
            pl.BlockSpec((None, tq, HEAD_DIM), lambda b, h, g: (b, g, qb + h)),
            pl.BlockSpec((None, t, HEAD_DIM), lambda b, h, g: (b, 0, kb + h)),
            pl.BlockSpec((None, t, HEAD_DIM), lambda b, h, g: (b, 0, vb + h)),
            pl.BlockSpec((None, BAND_TQ, BAND_WIN), lambda b, h, g: (h, 0, 0)),
        ],
        out_specs=pl.BlockSpec((None, tq, HEAD_DIM), lambda b, h, g: (b, g, h)),
        out_shape=jax.ShapeDtypeStruct((b, t, BAND_WIDTH), BF16),
        scratch_shapes=[pltpu.VMEM((BAND_ROWS + t, HEAD_DIM), BF16),
                        pltpu.VMEM((BAND_ROWS + t, HEAD_DIM), BF16)],
        compiler_params=_params(3),
        name="band_prompt",
    )(ya3, ykv3, ykv3, bias_tbl)


def _band_decode_kernel(q_ref, kn_ref, vn_ref, kc_ref, vc_ref, bias_ref, o_ref, kpad_ref, vpad_ref,
                        *, n_new, r_band):
    kpad_ref[...] = jnp.zeros_like(kpad_ref)
    vpad_ref[...] = jnp.zeros_like(vpad_ref)
    for h in range(BAND_HEADS):
        cols = slice(h * HEAD_DIM, (h + 1) * HEAD_DIM)
        kpad_ref[h, 0:n_new, :] = kn_ref[:, cols]
        vpad_ref[h, 0:n_new, :] = vn_ref[:, cols]
    for h in range(BAND_HEADS):
        cols = slice(h * HEAD_DIM, (h + 1) * HEAD_DIM)
        q = q_ref[:, cols]
        s_cache = _nt_dot(q, kc_ref[h].astype(BF16)) + bias_ref[h, :, 0:r_band]
        s_new = _nt_dot(q, kpad_ref[h]) + bias_ref[h, :, r_band:r_band + NEW_PAD]
        o_ref[:, cols] = _softmax2_pv([(s_cache, vc_ref[h].astype(BF16)),
                                       (s_new, vpad_ref[h])]).astype(BF16)


def _band_decode(ya3, ykv3, cache_k, cache_v, bias_tbl):
    bd, n_new, _ = ya3.shape
    r_band = cache_k.shape[2]
    kern = functools.partial(_band_decode_kernel, n_new=n_new, r_band=r_band)
    cache_spec = pl.BlockSpec((None, BAND_HEADS, r_band, HEAD_DIM), lambda b: (b, 0, 0, 0))
    return pl.pallas_call(
        kern,
        grid=(bd,),
        in_specs=[
            pl.BlockSpec((None, n_new, BAND_WIDTH), lambda b: (b, 0, QG_BD_Q // BAND_WIDTH)),
            pl.BlockSpec((None, n_new, BAND_WIDTH), lambda b: (b, 0, KV_BD_K // BAND_WIDTH)),
            pl.BlockSpec((None, n_new, BAND_WIDTH), lambda b: (b, 0, KV_BD_V // BAND_WIDTH)),
            cache_spec,
            cache_spec,
            pl.BlockSpec((BAND_HEADS, n_new, r_band + NEW_PAD), lambda b: (0, 0, 0)),
        ],
        out_specs=pl.BlockSpec((None, n_new, BAND_WIDTH), lambda b: (b, 0, 0)),
        out_shape=jax.ShapeDtypeStruct((bd, n_new, BAND_WIDTH), BF16),
        scratch_shapes=[pltpu.VMEM((BAND_HEADS, NEW_PAD, HEAD_DIM), BF16),
                        pltpu.VMEM((BAND_HEADS, NEW_PAD, HEAD_DIM), BF16)],
        compiler_params=_params(1),
        name="band_decode",
    )(ya3, ykv3, ykv3, cache_k, cache_v, bias_tbl)


def _mem_attn_kernel(q_ref, mk_ref, mv_ref, o_ref):
    for h in range(MEM_HEADS):
        sl = slice(h * HEAD_DIM, (h + 1) * HEAD_DIM)
        s = _nt_dot(q_ref[:, sl], mk_ref[:, sl].astype(BF16))
        o_ref[:, sl] = _softmax2_pv([(s, mv_ref[:, sl].astype(BF16))]).astype(BF16)


def _mem_attention(y3, mk, mv, *, tq):
    b, t, _ = y3.shape
    n_mem = mk.shape[1]
    qb = MG_MM_Q // MEM_WIDTH
    return pl.pallas_call(
        _mem_attn_kernel,
        grid=(b, t // tq),
        in_specs=[
            pl.BlockSpec((None, tq, MEM_WIDTH), lambda b, i: (b, i, qb)),
            pl.BlockSpec((None, n_mem, MEM_WIDTH), lambda b, i: (b, 0, 0)),
            pl.BlockSpec((None, n_mem, MEM_WIDTH), lambda b, i: (b, 0, 0)),
        ],
        out_specs=pl.BlockSpec((None, tq, MEM_WIDTH), lambda b, i: (b, i, 0)),
        out_shape=jax.ShapeDtypeStruct((b, t, MEM_WIDTH), BF16),
        compiler_params=_params(2),
        name="mem_attention",
    )(y3, mk, mv)


def _silu_of_half(h):
    return h + h * jnp.tanh(h)


def _merge_kernel(osb_ref, obd_ref, omm_ref, gsb_ref, gbd_ref, gmm_ref,
                  mg0_ref, mg1_ref, mg2_ref, mg3_ref, mg4_ref, mg5_ref,
                  wsb_ref, wbd_ref, wmm_ref, merged_ref, *, half):
    u_sb = (osb_ref[...].astype(F32) * _silu_of_half(gsb_ref[...].astype(F32))).astype(BF16)
    u_bd = (obd_ref[...].astype(F32) * _silu_of_half(gbd_ref[...].astype(F32))).astype(BF16)
    u_mm = (omm_ref[...].astype(F32) * _silu_of_half(gmm_ref[...].astype(F32))).astype(BF16)
    mg = ((mg0_ref, mg2_ref, mg4_ref), (mg1_ref, mg3_ref, mg5_ref))
    for n in range(2):
        cols = slice(n * half, (n + 1) * half)
        merged = None
        for m_ref, u, w_ref in zip(mg[n], (u_sb, u_bd, u_mm), (wsb_ref, wbd_ref, wmm_ref)):
            a = _dot(u, w_ref[:, cols])
            term = a + a * jnp.tanh(m_ref[...].astype(F32))
            merged = term if merged is None else merged + term
        merged_ref[:, cols] = merged.astype(BF16)


def _merge_branches(yqg, ymg, o_sb, o_bd, o_mm, w_sb, w_bd, w_mm, *, tm):
    m = yqg.shape[0]
    d = w_sb.shape[1]
    half = d // 2
    assert MG_MG % half == 0
    mgb = MG_MG // half
    const = dict(pipeline_mode=pl.Buffered(1))
    kern = functools.partial(_merge_kernel, half=half)
    return pl.pallas_call(
        kern,
        grid=(m // tm,),
        in_specs=[
            pl.BlockSpec((tm, SB_WIDTH), lambda i: (i, 0)),
            pl.BlockSpec((tm, BAND_WIDTH), lambda i: (i, 0)),
            pl.BlockSpec((tm, MEM_WIDTH), lambda i: (i, 0)),
            pl.BlockSpec((tm, SB_WIDTH), lambda i: (i, QG_SB_G // SB_WIDTH)),
            pl.BlockSpec((tm, BAND_WIDTH), lambda i: (i, QG_BD_G // BAND_WIDTH)),
            pl.BlockSpec((tm, MEM_WIDTH), lambda i: (i, MG_MM_G // MEM_WIDTH)),
        ] + [pl.BlockSpec((tm, half), functools.partial(lambda i, c: (i, c), c=mgb + c)) for c in range(6)] + [
            pl.BlockSpec((SB_WIDTH, d), lambda i: (0, 0), **const),
            pl.BlockSpec((BAND_WIDTH, d), lambda i: (0, 0), **const),
            pl.BlockSpec((MEM_WIDTH, d), lambda i: (0, 0), **const),
        ],
        out_specs=pl.BlockSpec((tm, d), lambda i: (i, 0)),
        out_shape=jax.ShapeDtypeStruct((m, d), BF16),
        compiler_params=_params(1),
        name="merge_branches",
    )(o_sb, o_bd, o_mm, yqg, yqg, ymg, *([ymg] * 6), w_sb, w_bd, w_mm)


def _out_proj_kernel(x_ref, merged_ref, wout_ref, gpost_ref, y_ref):
    y = _dot(merged_ref[...], wout_ref[...])
    ms = jnp.mean(y * y, axis=-1, keepdims=True)
    y_ref[...] = x_ref[...] + (y * lax.rsqrt(ms + RMS_EPS)) * gpost_ref[...]


def _out_projection(x2d, merged, w_out, g_post, *, tm):
    m, d = x2d.shape
    return pl.pallas_call(
        _out_proj_kernel,
        grid=(m // tm,),
        in_specs=[
            pl.BlockSpec((tm, d), lambda i: (i, 0)),
            pl.BlockSpec((tm, d), lambda i: (i, 0)),
            pl.BlockSpec((d, d), lambda i: (0, 0), pipeline_mode=pl.Buffered(1)),
            pl.BlockSpec((1, d), lambda i: (0, 0)),
        ],
        out_specs=pl.BlockSpec((tm, d), lambda i: (i, 0)),
        out_shape=jax.ShapeDtypeStruct((m, d), F32),
        compiler_params=_params(1),
        name="out_projection",
    )(x2d, merged, w_out, g_post.reshape(1, d))


def _head_major(a):
    return jnp.transpose(a, (0, 2, 1, 3))


def kernel(x_prompt, x_sample, cache_sb_k, cache_sb_v, cache_band_k, cache_band_v, cache_mem_k, cache_mem_v, mem_prompt, g_pre, w_in, rel_bias, g_mem, w_mem_kv, w_up_sb, w_up_band, w_up_mem, w_out, g_post):
    depth = w_in.shape[0]
    b, t, d = x_prompt.shape
    bd, n_new, _ = x_sample.shape
    n_mem = mem_prompt.shape[1]
    r_band = cache_band_k.shape[2]
    in_width = w_in.shape[2]
    band_keep = min(BAND_ROWS, t)
    tm_p = 512
    assert COL_MG + 3 * d == in_width
    assert t % SB_TQ == 0 and t % (BAND_TQ * BAND_STEP_GROUPS) == 0 and t % tm_p == 0
    assert r_band == BAND_ROWS and n_new <= CHUNK

    negu2 = jnp.where(jnp.arange(2 * SB_TK)[:, None] % SB_TK >= jnp.arange(SB_TK)[None, :], -1.0, 0.0).astype(BF16)
    mg_width = in_width - COL_MM_Q
    assert COL_MM_Q % MG_TN == 0 and mg_width % MG_TN == 0 and COL_SB_G == 3 * KV_TN and COL_BD_Q == 4 * KV_TN
    cols = jnp.arange(QG_WIDTH)
    qg_scale = jnp.where((cols // SB_WIDTH) % 2 == 0, Q_SCALE, 0.5).astype(F32).reshape(1, QG_WIDTH)
    cols = jnp.arange(mg_width)
    mg_scale = jnp.where(cols < MG_MM_G, Q_SCALE, 0.5).astype(F32).reshape(1, mg_width)
    qg_block = lambda j: j + 2 * ((j + 1) // 2)
    mg_block = lambda j: j + COL_MM_Q // MG_TN

    xp = x_prompt.reshape(b * t, d)
    xs = x_sample.reshape(bd * n_new, d)
    ms = bd * n_new
    outs = [[] for _ in range(10)]
    for l in range(depth):
        w_kvp_b = _kv_weight_bf16(w_in[l])
        w_kv_b = w_mem_kv[l].astype(BF16)
        w_sb_b = (0.5 * w_up_sb[l]).astype(BF16)
        w_bd_b = (0.5 * w_up_band[l]).astype(BF16)
        w_mm_b = (0.5 * w_up_mem[l]).astype(BF16)
        w_out_b = w_out[l].astype(BF16)
        bias_p, bias_d = _band_bias_tables(rel_bias[l], n_new, r_band)

        ykv, hp, sbk, sbv, bdk, bdv = _kv_projection(xp, g_pre[l], w_kvp_b, tm=band_keep, seqs=1, n_seq=b,
                                                     band_keep=band_keep)
        ykv_s, hs, sbk2, sbv2, bdk2, bdv2 = _kv_projection(xs, g_pre[l], w_kvp_b, tm=ms, seqs=bd, n_seq=bd,
                                                           band_keep=n_new)
        yqg, yqg_s = _col_projection(hp, hs, w_in[l], qg_scale, qg_block, tm=PROJ_TM, tn=KV_TN,
                                     name="qg_projection")
        ymg, ymg_s = _col_projection(hp, hs, w_in[l], mg_scale, mg_block, tm=PROJ_TM, tn=MG_TN,
                                     name="mg_projection")

        mk, mv = _memory_kv(mem_prompt.reshape(b * n_mem, d), g_mem[l], w_kv_b, tm=n_mem)
        yqg3 = yqg.reshape(b, t, QG_WIDTH)
        ykv3 = ykv.reshape(b, t, KV_WIDTH)
        o_sb = _sb_prompt(yqg3, ykv3, negu2)
        o_bd = _band_prompt(yqg3, ykv3, bias_p)
        o_mm = _mem_attention(ymg.reshape(b, t, mg_width), mk.reshape(b, n_mem, MEM_WIDTH),
                              mv.reshape(b, n_mem, MEM_WIDTH), tq=1024)
        merged = _merge_branches(yqg, ymg, o_sb.reshape(b * t, -1), o_bd.reshape(b * t, -1),
                                 o_mm.reshape(b * t, -1), w_sb_b, w_bd_b, w_mm_b, tm=512)
        xp = _out_projection(xp, merged, w_out_b, g_post[l], tm=512)
        outs[0].append(_head_major(sbk))
        outs[1].append(_head_major(sbv))
        outs[2].append(_head_major(bdk))
        outs[3].append(_head_major(bdv))
        outs[4].append(mk.reshape(b, n_mem, MEM_HEADS, HEAD_DIM))
        outs[5].append(mv.reshape(b, n_mem, MEM_HEADS, HEAD_DIM))

        yqg_s3 = yqg_s.reshape(bd, n_new, QG_WIDTH)
        ykv_s3 = ykv_s.reshape(bd, n_new, KV_WIDTH)
        o_sb2 = _sb_decode(yqg_s3, ykv_s3, _head_major(cache_sb_k[l]), _head_major(cache_sb_v[l]), negu2)
        o_bd2 = _band_decode(yqg_s3, ykv_s3, _head_major(cache_band_k[l]), _head_major(cache_band_v[l]), bias_d)
        o_mm2 = _mem_attention(ymg_s.reshape(bd, n_new, mg_width), cache_mem_k[l].reshape(bd, n_mem, MEM_WIDTH),
                               cache_mem_v[l].reshape(bd, n_mem, MEM_WIDTH), tq=n_new)
        merged_s = _merge_branches(yqg_s, ymg_s, o_sb2.reshape(ms, -1), o_bd2.reshape(ms, -1),
                                   o_mm2.reshape(ms, -1), w_sb_b, w_bd_b, w_mm_b, tm=ms)
        xs = _out_projection(xs, merged_s, w_out_b, g_post[l], tm=ms)
        outs[6].append(_head_major(sbk2))
        outs[7].append(_head_major(sbv2))
        outs[8].append(_head_major(bdk2))
        outs[9].append(_head_major(bdv2))

    return (xp.reshape(b, t, d), xs.reshape(bd, n_new, d)) + tuple(jnp.stack(o) for o in outs)
```

```python
import functools
import math

import jax
import jax.numpy as jnp
from jax import lax
from jax.experimental import pallas as pl
from jax.experimental.pallas import tpu as pltpu

F32 = jnp.float32
BF16 = jnp.bfloat16

HEAD_DIM = 128
SB_HEADS = 6
BAND_HEADS = 6
MEM_HEADS = 4
SB_WIDTH = SB_HEADS * HEAD_DIM
BAND_WIDTH = BAND_HEADS * HEAD_DIM
MEM_WIDTH = MEM_HEADS * HEAD_DIM
CHUNK = 64
CHUNK_SHIFT = 6
BAND_LEFT_CHUNKS = 8
BAND_ROWS = BAND_LEFT_CHUNKS * CHUNK
MAX_REL = 256
RMS_EPS = 1e-6
NEG_INF = -1e30
LOG2E = math.log2(math.e)
Q_SCALE = HEAD_DIM ** -0.5 * LOG2E

COL_SB_Q = 0
COL_SB_K = COL_SB_Q + SB_WIDTH
COL_SB_V = COL_SB_K + SB_WIDTH
COL_SB_G = COL_SB_V + SB_WIDTH
COL_BD_Q = COL_SB_G + SB_WIDTH
COL_BD_K = COL_BD_Q + BAND_WIDTH
COL_BD_V = COL_BD_K + BAND_WIDTH
COL_BD_G = COL_BD_V + BAND_WIDTH
COL_MM_Q = COL_BD_G + BAND_WIDTH
COL_MM_G = COL_MM_Q + MEM_WIDTH
COL_MG = COL_MM_G + MEM_WIDTH

KV_SB_K = 0
KV_SB_V = KV_SB_K + SB_WIDTH
KV_BD_K = KV_SB_V + SB_WIDTH
KV_BD_V = KV_BD_K + BAND_WIDTH
KV_WIDTH = KV_BD_V + BAND_WIDTH
QG_SB_Q = 0
QG_SB_G = QG_SB_Q + SB_WIDTH
QG_BD_Q = QG_SB_G + SB_WIDTH
QG_BD_G = QG_BD_Q + BAND_WIDTH
QG_WIDTH = QG_BD_G + BAND_WIDTH
MG_MM_Q = 0
MG_MM_G = MG_MM_Q + MEM_WIDTH
MG_MG = MG_MM_G + MEM_WIDTH

VMEM_LIMIT_BYTES = 56 * 1024 * 1024
MAX_VMEM_LIMIT_BYTES = 58 * 1024 * 1024
COMPILER_TEMP_BYTES = 2 * 1024 * 1024
PROJ_TM = 2048
KV_TN = SB_WIDTH
MG_TN = 1024
SB_TK = 256
SB_TQ = 16 * SB_TK
SB_DEAD = -160.0
BAND_TQ = 4 * CHUNK
BAND_WIN = BAND_TQ + BAND_ROWS
BAND_STEP_GROUPS = 16
BIAS_LANES = 1024
NEW_PAD = 128


def _params(n_axes, vmem=VMEM_LIMIT_BYTES):
    return pltpu.CompilerParams(dimension_semantics=("arbitrary",) * n_axes,
                                vmem_limit_bytes=vmem)


def _nt_dot(a, b):
    return lax.dot_general(a, b, (((1,), (1,)), ((), ())), preferred_element_type=F32)


def _dot(a, b):
    return jnp.dot(a, b, preferred_element_type=F32)


def _pre_norm_to(h_ref, x_ref, g_ref):
    x = x_ref[...]
    ms = jnp.mean(x * x, axis=-1, keepdims=True)
    h_ref[...] = ((x * lax.rsqrt(ms + RMS_EPS)) * g_ref[...]).astype(BF16)


def _kv_proj_kernel(x_ref, g_ref, wsk_ref, wsv_ref, wbk_ref, wbv_ref,
                    y_ref, h_ref, sbk_ref, sbv_ref, bdk_ref, bdv_ref, *, seqs, rows):
    _pre_norm_to(h_ref, x_ref, g_ref)
    groups = ((wsk_ref, sbk_ref), (wsv_ref, sbv_ref), (wbk_ref, bdk_ref), (wbv_ref, bdv_ref))
    for group, (w_ref, dst_ref) in enumerate(groups):
        acc = _dot(h_ref[...], w_ref[...])
        y_ref[:, group * KV_TN:(group + 1) * KV_TN] = acc.astype(BF16)
        for h in range(SB_HEADS):
            for s in range(seqs):
                dst_ref[s, h] = acc[s * rows:(s + 1) * rows, h * HEAD_DIM:(h + 1) * HEAD_DIM]


def _kv_projection(x2d, g_pre, w_kv_b, *, tm, seqs, n_seq, band_keep):
    m, d = x2d.shape
    assert KV_WIDTH == 4 * KV_TN and SB_HEADS == BAND_HEADS
    rows = tm // seqs
    seq_rows = m // n_seq
    blocks_per_seq = seq_rows // rows
    assert rows == band_keep
    kern = functools.partial(_kv_proj_kernel, seqs=seqs, rows=rows)
    sb_spec = pl.BlockSpec((seqs, SB_HEADS, rows, HEAD_DIM),
                           lambda i: (i // blocks_per_seq, 0, i % blocks_per_seq, 0))
    bd_spec = pl.BlockSpec((seqs, BAND_HEADS, rows, HEAD_DIM), lambda i: (i // blocks_per_seq, 0, 0, 0))
    sb_shape = jax.ShapeDtypeStruct((n_seq, SB_HEADS, seq_rows, HEAD_DIM), F32)
    bd_shape = jax.ShapeDtypeStruct((n_seq, BAND_HEADS, band_keep, HEAD_DIM), F32)

    def w_spec(group):
        return pl.BlockSpec((d, KV_TN), functools.partial(lambda i, c: (0, c), c=group),
                            pipeline_mode=pl.Buffered(1))

    return pl.pallas_call(
        kern,
        grid=(m // tm,),
        in_specs=[
            pl.BlockSpec((tm, d), lambda i: (i, 0)),
            pl.BlockSpec((1, d), lambda i: (0, 0)),
            w_spec(0), w_spec(1), w_spec(2), w_spec(3),
        ],
        out_specs=[pl.BlockSpec((tm, KV_WIDTH), lambda i: (i, 0)), pl.BlockSpec((tm, d), lambda i: (i, 0)),
                   sb_spec, sb_spec, bd_spec, bd_spec],
        out_shape=[jax.ShapeDtypeStruct((m, KV_WIDTH), BF16), jax.ShapeDtypeStruct((m, d), BF16),
                   sb_shape, sb_shape, bd_shape, bd_shape],
        compiler_params=_params(1),
        name="kv_projection",
    )(x2d, g_pre.reshape(1, d), w_kv_b, w_kv_b, w_kv_b, w_kv_b)


def _col_proj_kernel(h_ref, hs_ref, w_ref, cs_ref, y_ref, ys_ref, wb_ref):
    @pl.when(pl.program_id(1) == 0)
    def _():
        wb_ref[...] = w_ref[...].astype(BF16)
        ys_ref[...] = (_dot(hs_ref[...], wb_ref[...]) * cs_ref[...]).astype(BF16)

    y_ref[...] = (_dot(h_ref[...], wb_ref[...]) * cs_ref[...]).astype(BF16)


def _col_projection(h2d, hs2d, w_in, col_scale, src_block, *, tm, tn, name):
    m, d = h2d.shape
    ms = hs2d.shape[0]
    n = col_scale.shape[1]
    assert n % tn == 0 and m % tm == 0
    vmem = (2 * tm * d * 2 + 2 * d * tn * 4 + 2 * tm * tn * 2 + d * tn * 2 + 2 * tm * tn * 4
            + 2 * ms * (d + tn) * 2 + COMPILER_TEMP_BYTES)
    vmem = min(vmem, MAX_VMEM_LIMIT_BYTES)
    return pl.pallas_call(
        _col_proj_kernel,
        grid=(n // tn, m // tm),
        in_specs=[
            pl.BlockSpec((tm, d), lambda j, i: (i, 0)),
            pl.BlockSpec((ms, d), lambda j, i: (0, 0)),
            pl.BlockSpec((d, tn), lambda j, i: (0, src_block(j))),
            pl.BlockSpec((1, tn), lambda j, i: (0, j)),
        ],
        out_specs=[pl.BlockSpec((tm, tn), lambda j, i: (i, j)), pl.BlockSpec((ms, tn), lambda j, i: (0, j))],
        out_shape=[jax.ShapeDtypeStruct((m, n), BF16), jax.ShapeDtypeStruct((ms, n), BF16)],
        scratch_shapes=[pltpu.VMEM((d, tn), BF16)],
        compiler_params=_params(2, vmem=vmem),
        name=name,
    )(h2d, hs2d, w_in, col_scale)


def _kv_weight_kernel(a_ref, b_ref, c_ref, d_ref, o_ref):
    for group, w_ref in enumerate((a_ref, b_ref, c_ref, d_ref)):
        o_ref[:, group * KV_TN:(group + 1) * KV_TN] = w_ref[...].astype(BF16)


def _kv_weight_bf16(w):
    d = w.shape[0]
    tr = 256
    assert d % tr == 0

    def spec(col0):
        assert col0 % KV_TN == 0
        return pl.BlockSpec((tr, KV_TN), functools.partial(lambda i, c: (i, c), c=col0 // KV_TN))

    return pl.pallas_call(
        _kv_weight_kernel,
        grid=(d // tr,),
        in_specs=[spec(COL_SB_K), spec(COL_SB_V), spec(COL_BD_K), spec(COL_BD_V)],
        out_specs=pl.BlockSpec((tr, KV_WIDTH), lambda i: (i, 0)),
        out_shape=jax.ShapeDtypeStruct((d, KV_WIDTH), BF16),
        compiler_params=_params(1),
        name="kv_weight_cast",
    )(w, w, w, w)


def _memkv_kernel(x_ref, g_ref, w_ref, mk_ref, mv_ref):
    x = x_ref[...]
    ms = jnp.mean(x * x, axis=-1, keepdims=True)
    h = ((x * lax.rsqrt(ms + RMS_EPS)) * g_ref[...]).astype(BF16)
    acc = _dot(h, w_ref[...])
    mk_ref[...] = acc[:, :MEM_WIDTH]
    mv_ref[...] = acc[:, MEM_WIDTH:]


def _memory_kv(mem2d, g_mem, w_bf16, *, tm):
    m, d = mem2d.shape
    return pl.pallas_call(
        _memkv_kernel,
        grid=(m // tm,),
        in_specs=[
            pl.BlockSpec((tm, d), lambda i: (i, 0)),
            pl.BlockSpec((1, d), lambda i: (0, 0)),
            pl.BlockSpec((d, 2 * MEM_WIDTH), lambda i: (0, 0)),
        ],
        out_specs=[pl.BlockSpec((tm, MEM_WIDTH), lambda i: (i, 0)),
                   pl.BlockSpec((tm, MEM_WIDTH), lambda i: (i, 0))],
        out_shape=[jax.ShapeDtypeStruct((m, MEM_WIDTH), F32),
                   jax.ShapeDtypeStruct((m, MEM_WIDTH), F32)],
        compiler_params=_params(1),
        name="memory_kv",
    )(mem2d, g_mem.reshape(1, d), w_bf16)


def _neg_suffix_matrix(n):
    row = lax.broadcasted_iota(jnp.int32, (2 * n, n), 0)
    col = lax.broadcasted_iota(jnp.int32, (2 * n, n), 1)
    row = jnp.where(row >= n, row - n, row)
    return jnp.where(row >= col, -1.0, 0.0).astype(BF16)


def _sb_weights(z2, carry2, negu2, mask):
    p = jnp.maximum(z2, 0.0) + jnp.log(1.0 + jnp.exp2(-jnp.abs(z2))) * LOG2E
    if mask is not None:
        p = jnp.where(mask, p, 0.0)
    p_hi = p.astype(BF16)
    p_lo = (p - p_hi.astype(F32)).astype(BF16)
    suffix = _dot(jnp.concatenate([p_hi, p_lo], axis=1), negu2)
    w = jnp.exp2(z2 + suffix + carry2)
    if mask is not None:
        w = jnp.where(mask, w, 0.0)
    return w, carry2 - jnp.sum(p, axis=-1, keepdims=True)


def _sb_prompt_kernel(q_ref, k_ref, v_ref, negu2_ref, o_ref, acc_ref, carry_ref, kpad_ref, vpad_ref, *, t):
    i = pl.program_id(2)
    n_sub = SB_TQ // SB_TK
    negu2 = negu2_ref[...]

    @pl.when(i == 0)
    def _():
        kpad_ref[0:SB_TK, :] = jnp.zeros((SB_TK, HEAD_DIM), BF16)
        vpad_ref[0:SB_TK, :] = jnp.zeros((SB_TK, HEAD_DIM), BF16)
        kpad_ref[SB_TK:SB_TK + t, :] = k_ref[...]
        vpad_ref[SB_TK:SB_TK + t, :] = v_ref[...]

    def kv_block(j):
        start = pl.multiple_of((j + 1) * SB_TK, SB_TK)
        return kpad_ref[pl.ds(start, SB_TK), :], vpad_ref[pl.ds(start, SB_TK), :]

    row = lax.broadcasted_iota(jnp.int32, (SB_TK, SB_TK), 0)
    col = lax.broadcasted_iota(jnp.int32, (SB_TK, SB_TK), 1)
    for r in range(n_sub):
        rows = slice(r * SB_TK, (r + 1) * SB_TK)
        s = i * n_sub + r
        q = q_ref[rows, :]
        kb, vb = kv_block(s)
        w, carry = _sb_weights(_nt_dot(q, kb), jnp.zeros((SB_TK, 1), F32), negu2, col < row)
        acc = _dot(w.astype(BF16), vb)
        kb, vb = kv_block(s - 1)
        prev_exists = None if r > 0 else (jnp.zeros((SB_TK, SB_TK), jnp.int32) + i) > 0
        w, carry = _sb_weights(_nt_dot(q, kb), carry, negu2, prev_exists)
        acc_ref[rows, :] = acc + _dot(w.astype(BF16), vb)
        carry_ref[rows, :] = carry

    row_q = lax.broadcasted_iota(jnp.int32, (SB_TQ, 1), 0)
    row_t = lax.broadcasted_iota(jnp.int32, (SB_TQ, SB_TK), 0)
    has_more = row_q >= (2 - n_sub * i) * SB_TK

    def any_alive(carry):
        return (jnp.max(jnp.where(has_more, carry, NEG_INF)) > SB_DEAD).astype(jnp.int32)

    def cond(state):
        j, alive = state
        return jnp.logical_and(j >= 0, alive > 0)

    def body(state):
        j, _ = state
        kb, vb = kv_block(j)
        visits = row_t >= (j - n_sub * i + 2) * SB_TK
        w, carry = _sb_weights(_nt_dot(q_ref[...], kb), carry_ref[...], negu2, visits)
        acc_ref[...] += _dot(w.astype(BF16), vb)
        carry_ref[...] = carry
        return j - 1, any_alive(carry)

    lax.while_loop(cond, body, (n_sub * i + n_sub - 3, any_alive(carry_ref[...])))
    o_ref[...] = acc_ref[...].astype(BF16)


def _sb_prompt(ya3, ykv3, negu2):
    b, t, _ = ya3.shape
    qb, kb, vb = QG_SB_Q // HEAD_DIM, KV_SB_K // HEAD_DIM, KV_SB_V // HEAD_DIM
    assert SB_TQ // SB_TK >= 3
    kern = functools.partial(_sb_prompt_kernel, t=t)
    return pl.pallas_call(
        kern,
        grid=(b, SB_HEADS, t // SB_TQ),
        in_specs=[
            pl.BlockSpec((None, SB_TQ, HEAD_DIM), lambda b, h, i: (b, i, qb + h)),
            pl.BlockSpec((None, t, HEAD_DIM), lambda b, h, i: (b, 0, kb + h)),
            pl.BlockSpec((None, t, HEAD_DIM), lambda b, h, i: (b, 0, vb + h)),
            pl.BlockSpec((2 * SB_TK, SB_TK), lambda b, h, i: (0, 0)),
        ],
        out_specs=pl.BlockSpec((None, SB_TQ, HEAD_DIM), lambda b, h, i: (b, i, h)),
        out_shape=jax.ShapeDtypeStruct((b, t, SB_WIDTH), BF16),
        scratch_shapes=[pltpu.VMEM((SB_TQ, HEAD_DIM), F32), pltpu.VMEM((SB_TQ, 1), F32),
                        pltpu.VMEM((SB_TK + t, HEAD_DIM), BF16), pltpu.VMEM((SB_TK + t, HEAD_DIM), BF16)],
        compiler_params=_params(3),
        name="sb_prompt",
    )(ya3, ykv3, ykv3, negu2)


def _sb_decode_body(q_ref, kn_ref, vn_ref, kc_hbm, vc_hbm, negu2_ref, o_ref,
                    acc_ref, carry_ref, kpad_ref, vpad_ref, kbuf_ref, vbuf_ref, sem,
                    *, n_new, n_blocks, overlap):
    b = pl.program_id(0)
    heads = SB_HEADS

    def cache_copies(j, slot):
        rows = pl.ds(pl.multiple_of((n_blocks - 1 - j) * SB_TK, SB_TK), SB_TK)
        return (pltpu.make_async_copy(kc_hbm.at[b, :, rows, :], kbuf_ref.at[slot], sem.at[0, slot]),
                pltpu.make_async_copy(vc_hbm.at[b, :, rows, :], vbuf_ref.at[slot], sem.at[1, slot]))

    def start_fetch(j, slot):
        for cp in cache_copies(j, slot):
            cp.start()

    def wait_fetch(j, slot):
        for cp in cache_copies(j, slot):
            cp.wait()

    start_fetch(0, 0)
    overlap()

    def head_cols(h):
        return slice(h * HEAD_DIM, (h + 1) * HEAD_DIM)

    def head_rows(h):
        return slice(h * n_new, (h + 1) * n_new)

    def block(k_of, v_of, negu2, mask):
        z2 = jnp.concatenate([_nt_dot(q_ref[:, head_cols(h)], k_of(h)) for h in range(heads)], axis=0)
        w, carry = _sb_weights(z2, carry_ref[...], negu2, mask)
        wb = w.astype(BF16)
        for h in range(heads):
            acc_ref[head_rows(h), :] += _dot(wb[head_rows(h), :], v_of(h))
        carry_ref[...] = carry

    def any_alive():
        return (jnp.max(carry_ref[...]) > SB_DEAD).astype(jnp.int32)

    acc_ref[...] = jnp.zeros_like(acc_ref)
    carry_ref[...] = jnp.zeros_like(carry_ref)
    kpad_ref[...] = jnp.zeros_like(kpad_ref)
    vpad_ref[...] = jnp.zeros_like(vpad_ref)
    for h in range(heads):
        kpad_ref[h, 0:n_new, :] = kn_ref[:, head_cols(h)]
        vpad_ref[h, 0:n_new, :] = vn_ref[:, head_cols(h)]
    row = lax.broadcasted_iota(jnp.int32, (n_new, NEW_PAD), 0)
    col = lax.broadcasted_iota(jnp.int32, (n_new, NEW_PAD), 1)
    mask = jnp.concatenate([(col < row).astype(jnp.int32)] * heads, axis=0) == 1
    block(lambda h: kpad_ref[h], lambda h: vpad_ref[h], _neg_suffix_matrix(NEW_PAD), mask)

    negu2 = negu2_ref[...]

    def cond(state):
        j, alive = state
        return jnp.logical_and(j < n_blocks, alive > 0)

    def body(state):
        j, _ = state
        slot = j % 2
        wait_fetch(j, slot)

        @pl.when(j + 1 < n_blocks)
        def _():
            start_fetch(j + 1, 1 - slot)

        block(lambda h: kbuf_ref[slot, h].astype(BF16), lambda h: vbuf_ref[slot, h].astype(BF16),
              negu2, None)
        return j + 1, any_alive()

    j_end, _ = lax.while_loop(cond, body, (0, any_alive()))

    @pl.when(j_end < n_blocks)
    def _():
        wait_fetch(j_end, j_end % 2)

    for h in range(heads):
        o_ref[:, head_cols(h)] = acc_ref[head_rows(h), :].astype(BF16)


def _softmax2_pv(parts):
    mx = functools.reduce(jnp.maximum, [jnp.max(s, axis=-1, keepdims=True) for s, _ in parts])
    num = None
    den = None
    for s, v in parts:
        p = jnp.exp2(s - mx)
        d = jnp.sum(p, axis=-1, keepdims=True)
        o = _dot(p.astype(BF16), v)
        num = o if num is None else num + o
        den = d if den is None else den + d
    return num / den


def _band_bias_kernel(g_ref, tp_ref, td_ref, *, n_new, r_band):
    x = jnp.broadcast_to(g_ref[...], (BAND_TQ, BIAS_LANES))
    x = pltpu.roll(x, BAND_TQ, 1, stride=1, stride_axis=0)
    tbl = x[:, :BAND_WIN] * LOG2E
    r = lax.broadcasted_iota(jnp.int32, (BAND_TQ, BAND_WIN), 0)
    j = lax.broadcasted_iota(jnp.int32, (BAND_TQ, BAND_WIN), 1)
    dc = (j >> CHUNK_SHIFT) - (r >> CHUNK_SHIFT)
    tp_ref[...] = jnp.where(jnp.logical_and(dc >= 0, dc <= BAND_LEFT_CHUNKS), tbl, NEG_INF)
    jd = lax.broadcasted_iota(jnp.int32, (n_new, r_band + NEW_PAD), 1)
    td_ref[...] = jnp.where(jd < r_band + n_new, tbl[:n_new, :r_band + NEW_PAD], NEG_INF)


def _band_bias_tables(rel_bias, n_new, r_band):
    h = rel_bias.shape[0]
    assert BAND_ROWS == 2 * MAX_REL and r_band == BAND_ROWS and BIAS_LANES == 2 * BAND_ROWS
    rb = rel_bias.astype(F32)
    g = jnp.concatenate([rb[:, :0:-1], jnp.broadcast_to(rb[:, -1:], (h, BIAS_LANES - 2 * MAX_REL))], axis=1)
    kern = functools.partial(_band_bias_kernel, n_new=n_new, r_band=r_band)
    return pl.pallas_call(
        kern,
        grid=(h,),
        in_specs=[pl.BlockSpec((None, 1, BIAS_LANES), lambda i: (i, 0, 0))],
        out_specs=[pl.BlockSpec((None, BAND_TQ, BAND_WIN), lambda i: (i, 0, 0)),
                   pl.BlockSpec((None, n_new, r_band + NEW_PAD), lambda i: (i, 0, 0))],
        out_shape=[jax.ShapeDtypeStruct((h, BAND_TQ, BAND_WIN), F32),
                   jax.ShapeDtypeStruct((h, n_new, r_band + NEW_PAD), F32)],
        compiler_params=_params(1),
        name="band_bias",
    )(g.reshape(h, 1, BIAS_LANES))


def _band_prompt_kernel(q_ref, k_ref, v_ref, bias_ref, o_ref, kpad_ref, vpad_ref, *, t):
    s_idx = pl.program_id(2)

    @pl.when(s_idx == 0)
    def _():
        kpad_ref[0:BAND_ROWS, :] = jnp.zeros((BAND_ROWS, HEAD_DIM), BF16)
        vpad_ref[0:BAND_ROWS, :] = jnp.zeros((BAND_ROWS, HEAD_DIM), BF16)
        kpad_ref[BAND_ROWS:BAND_ROWS + t, :] = k_ref[...]
        vpad_ref[BAND_ROWS:BAND_ROWS + t, :] = v_ref[...]

    col = lax.broadcasted_iota(jnp.int32, (BAND_TQ, BAND_WIN), 1)
    for gg in range(BAND_STEP_GROUPS):
        g = s_idx * BAND_STEP_GROUPS + gg
        start = pl.multiple_of(g * BAND_TQ, BAND_TQ)
        rows = slice(gg * BAND_TQ, (gg + 1) * BAND_TQ)
        s = _nt_dot(q_ref[rows, :], kpad_ref[pl.ds(start, BAND_WIN), :]) + bias_ref[...]
        s = jnp.where(col + g * BAND_TQ >= BAND_ROWS, s, NEG_INF)
        o_ref[rows, :] = _softmax2_pv([(s, vpad_ref[pl.ds(start, BAND_WIN), :])]).astype(BF16)


def _band_prompt(ya3, ykv3, bias_tbl):
    b, t, _ = ya3.shape
    tq = BAND_TQ * BAND_STEP_GROUPS
    qb, kb, vb = QG_BD_Q // HEAD_DIM, KV_BD_K // HEAD_DIM, KV_BD_V // HEAD_DIM
    kern = functools.partial(_band_prompt_kernel, t=t)
    return pl.pallas_call(
        kern,
        grid=(b, BAND_HEADS, t // tq),
        in_specs=[
            pl.BlockSpec((None, tq, HEAD_DIM), lambda b, h, g: (b, g, qb + h)),
            pl.BlockSpec((None, t, HEAD_DIM), lambda b, h, g: (b, 0, kb + h)),
            pl.BlockSpec((None, t, HEAD_DIM), lambda b, h, g: (b, 0, vb + h)),
            pl.BlockSpec((None, BAND_TQ, BAND_WIN), lambda b, h, g: (h, 0, 0)),
        ],
        out_specs=pl.BlockSpec((None, tq, HEAD_DIM), lambda b, h, g: (b, g, h)),
        out_shape=jax.ShapeDtypeStruct((b, t, BAND_WIDTH), BF16),
        scratch_shapes=[pltpu.VMEM((BAND_ROWS + t, HEAD_DIM), BF16),
                        pltpu.VMEM((BAND_ROWS + t, HEAD_DIM), BF16)],
        compiler_params=_params(3),
        name="band_prompt",
    )(ya3, ykv3, ykv3, bias_tbl)


def _band_decode_kernel(q_ref, kn_ref, vn_ref, kc_ref, vc_ref, bias_ref, o_ref, kpad_ref, vpad_ref,
                        *, n_new, r_band):
    kpad_ref[...] = jnp.zeros_like(kpad_ref)
    vpad_ref[...] = jnp.zeros_like(vpad_ref)
    for h in range(BAND_HEADS):
        cols = slice(h * HEAD_DIM, (h + 1) * HEAD_DIM)
        kpad_ref[h, 0:n_new, :] = kn_ref[:, cols]
        vpad_ref[h, 0:n_new, :] = vn_ref[:, cols]
    for h in range(BAND_HEADS):
        cols = slice(h * HEAD_DIM, (h + 1) * HEAD_DIM)
        q = q_ref[:, cols]
        s_cache = _nt_dot(q, kc_ref[h].astype(BF16)) + bias_ref[h, :, 0:r_band]
        s_new = _nt_dot(q, kpad_ref[h]) + bias_ref[h, :, r_band:r_band + NEW_PAD]
        o_ref[:, cols] = _softmax2_pv([(s_cache, vc_ref[h].astype(BF16)),
                                       (s_new, vpad_ref[h])]).astype(BF16)


def _mem_attn_kernel(q_ref, mk_ref, mv_ref, o_ref):
    for h in range(MEM_HEADS):
        sl = slice(h * HEAD_DIM, (h + 1) * HEAD_DIM)
        s = _nt_dot(q_ref[:, sl], mk_ref[:, sl].astype(BF16))
        o_ref[:, sl] = _softmax2_pv([(s, mv_ref[:, sl].astype(BF16))]).astype(BF16)


def _mem_attention(y3, mk, mv, *, tq):
    b, t, _ = y3.shape
    n_mem = mk.shape[1]
    qb = MG_MM_Q // MEM_WIDTH
    return pl.pallas_call(
        _mem_attn_kernel,
        grid=(b, t // tq),
        in_specs=[
            pl.BlockSpec((None, tq, MEM_WIDTH), lambda b, i: (b, i, qb)),
            pl.BlockSpec((None, n_mem, MEM_WIDTH), lambda b, i: (b, 0, 0)),
            pl.BlockSpec((None, n_mem, MEM_WIDTH), lambda b, i: (b, 0, 0)),
        ],
        out_specs=pl.BlockSpec((None, tq, MEM_WIDTH), lambda b, i: (b, i, 0)),
        out_shape=jax.ShapeDtypeStruct((b, t, MEM_WIDTH), BF16),
        compiler_params=_params(2),
        name="mem_attention",
    )(y3, mk, mv)


def _decode_attn_kernel(qs_ref, kns_ref, vns_ref, kcs_hbm, vcs_hbm, negu2_ref,
                        qb_ref, knb_ref, vnb_ref, kcb_ref, vcb_ref, bias_ref, qm_ref, mk_ref, mv_ref,
                        osb_ref, obd_ref, omm_ref,
                        acc_ref, carry_ref, kpad_ref, vpad_ref, kbuf_ref, vbuf_ref, sem, kpadb_ref, vpadb_ref,
                        *, n_new, n_blocks, r_band):
    def band_and_memory():
        _band_decode_kernel(qb_ref, knb_ref, vnb_ref, kcb_ref, vcb_ref, bias_ref, obd_ref,
                            kpadb_ref, vpadb_ref, n_new=n_new, r_band=r_band)
        _mem_attn_kernel(qm_ref, mk_ref, mv_ref, omm_ref)

    _sb_decode_body(qs_ref, kns_ref, vns_ref, kcs_hbm, vcs_hbm, negu2_ref, osb_ref,
                    acc_ref, carry_ref, kpad_ref, vpad_ref, kbuf_ref, vbuf_ref, sem,
                    n_new=n_new, n_blocks=n_blocks, overlap=band_and_memory)


def _decode_attention(yqg3, ykv3, ymg3, cache_sb_k, cache_sb_v, cache_bd_k, cache_bd_v, mk, mv, bias_tbl, negu2):
    bd, n_new, _ = yqg3.shape
    past = cache_sb_k.shape[2]
    r_band = cache_bd_k.shape[2]
    n_mem = mk.shape[1]
    assert past % SB_TK == 0 and n_new <= NEW_PAD and n_new % 16 == 0
    kern = functools.partial(_decode_attn_kernel, n_new=n_new, n_blocks=past // SB_TK, r_band=r_band)
    slab = lambda width, col0: pl.BlockSpec((None, n_new, width), functools.partial(lambda b, c: (b, 0, c), c=col0 // width))
    band_cache = pl.BlockSpec((None, BAND_HEADS, r_band, HEAD_DIM), lambda b: (b, 0, 0, 0))
    mem_cache = pl.BlockSpec((None, n_mem, MEM_WIDTH), lambda b: (b, 0, 0))
    return pl.pallas_call(
        kern,
        grid=(bd,),
        in_specs=[
            slab(SB_WIDTH, QG_SB_Q), slab(SB_WIDTH, KV_SB_K), slab(SB_WIDTH, KV_SB_V),
            pl.BlockSpec(memory_space=pl.ANY), pl.BlockSpec(memory_space=pl.ANY),
            pl.BlockSpec((2 * SB_TK, SB_TK), lambda b: (0, 0)),
            slab(BAND_WIDTH, QG_BD_Q), slab(BAND_WIDTH, KV_BD_K), slab(BAND_WIDTH, KV_BD_V),
            band_cache, band_cache,
            pl.BlockSpec((BAND_HEADS, n_new, r_band + NEW_PAD), lambda b: (0, 0, 0)),
            slab(MEM_WIDTH, MG_MM_Q), mem_cache, mem_cache,
        ],
        out_specs=[pl.BlockSpec((None, n_new, SB_WIDTH), lambda b: (b, 0, 0)),
                   pl.BlockSpec((None, n_new, BAND_WIDTH), lambda b: (b, 0, 0)),
                   pl.BlockSpec((None, n_new, MEM_WIDTH), lambda b: (b, 0, 0))],
        out_shape=[jax.ShapeDtypeStruct((bd, n_new, SB_WIDTH), BF16),
                   jax.ShapeDtypeStruct((bd, n_new, BAND_WIDTH), BF16),
                   jax.ShapeDtypeStruct((bd, n_new, MEM_WIDTH), BF16)],
        scratch_shapes=[pltpu.VMEM((SB_HEADS * n_new, HEAD_DIM), F32),
                        pltpu.VMEM((SB_HEADS * n_new, 1), F32),
                        pltpu.VMEM((SB_HEADS, NEW_PAD, HEAD_DIM), BF16),
                        pltpu.VMEM((SB_HEADS, NEW_PAD, HEAD_DIM), BF16),
                        pltpu.VMEM((2, SB_HEADS, SB_TK, HEAD_DIM), F32),
                        pltpu.VMEM((2, SB_HEADS, SB_TK, HEAD_DIM), F32),
                        pltpu.SemaphoreType.DMA((2, 2)),
                        pltpu.VMEM((BAND_HEADS, NEW_PAD, HEAD_DIM), BF16),
                        pltpu.VMEM((BAND_HEADS, NEW_PAD, HEAD_DIM), BF16)],
        compiler_params=_params(1),
        name="decode_attention",
    )(yqg3, ykv3, ykv3, cache_sb_k, cache_sb_v, negu2, yqg3, ykv3, ykv3, cache_bd_k, cache_bd_v, bias_tbl,
      ymg3, mk, mv)


def _silu_of_half(h):
    return h + h * jnp.tanh(h)


def _merge_kernel(osb_ref, obd_ref, omm_ref, gsb_ref, gbd_ref, gmm_ref,
                  mg0_ref, mg1_ref, mg2_ref, mg3_ref, mg4_ref, mg5_ref,
                  wsb_ref, wbd_ref, wmm_ref, merged_ref, *, half):
    u_sb = (osb_ref[...].astype(F32) * _silu_of_half(gsb_ref[...].astype(F32))).astype(BF16)
    u_bd = (obd_ref[...].astype(F32) * _silu_of_half(gbd_ref[...].astype(F32))).astype(BF16)
    u_mm = (omm_ref[...].astype(F32) * _silu_of_half(gmm_ref[...].astype(F32))).astype(BF16)
    mg = ((mg0_ref, mg2_ref, mg4_ref), (mg1_ref, mg3_ref, mg5_ref))
    for n in range(2):
        cols = slice(n * half, (n + 1) * half)
        merged = None
        for m_ref, u, w_ref in zip(mg[n], (u_sb, u_bd, u_mm), (wsb_ref, wbd_ref, wmm_ref)):
            a = _dot(u, w_ref[:, cols])
            term = a + a * jnp.tanh(m_ref[...].astype(F32))
            merged = term if merged is None else merged + term
        merged_ref[:, cols] = merged.astype(BF16)


def _merge_branches(yqg, ymg, o_sb, o_bd, o_mm, w_sb, w_bd, w_mm, *, tm):
    m = yqg.shape[0]
    d = w_sb.shape[1]
    half = d // 2
    assert MG_MG % half == 0
    mgb = MG_MG // half
    const = dict(pipeline_mode=pl.Buffered(1))
    kern = functools.partial(_merge_kernel, half=half)
    return pl.pallas_call(
        kern,
        grid=(m // tm,),
        in_specs=[
            pl.BlockSpec((tm, SB_WIDTH), lambda i: (i, 0)),
            pl.BlockSpec((tm, BAND_WIDTH), lambda i: (i, 0)),
            pl.BlockSpec((tm, MEM_WIDTH), lambda i: (i, 0)),
            pl.BlockSpec((tm, SB_WIDTH), lambda i: (i, QG_SB_G // SB_WIDTH)),
            pl.BlockSpec((tm, BAND_WIDTH), lambda i: (i, QG_BD_G // BAND_WIDTH)),
            pl.BlockSpec((tm, MEM_WIDTH), lambda i: (i, MG_MM_G // MEM_WIDTH)),
        ] + [pl.BlockSpec((tm, half), functools.partial(lambda i, c: (i, c), c=mgb + c)) for c in range(6)] + [
            pl.BlockSpec((SB_WIDTH, d), lambda i: (0, 0), **const),
            pl.BlockSpec((BAND_WIDTH, d), lambda i: (0, 0), **const),
            pl.BlockSpec((MEM_WIDTH, d), lambda i: (0, 0), **const),
        ],
        out_specs=pl.BlockSpec((tm, d), lambda i: (i, 0)),
        out_shape=jax.ShapeDtypeStruct((m, d), BF16),
        compiler_params=_params(1),
        name="merge_branches",
    )(o_sb, o_bd, o_mm, yqg, yqg, ymg, *([ymg] * 6), w_sb, w_bd, w_mm)


def _out_proj_kernel(x_ref, merged_ref, wout_ref, gpost_ref, y_ref):
    y = _dot(merged_ref[...], wout_ref[...])
    ms = jnp.mean(y * y, axis=-1, keepdims=True)
    y_ref[...] = x_ref[...] + (y * lax.rsqrt(ms + RMS_EPS)) * gpost_ref[...]


def _merge_out_kernel(x_ref, osb_ref, obd_ref, omm_ref, gsb_ref, gbd_ref, gmm_ref,
                      mg0_ref, mg1_ref, mg2_ref, mg3_ref, mg4_ref, mg5_ref,
                      wsb_ref, wbd_ref, wmm_ref, wout_ref, gpost_ref, y_ref, merged_ref, *, half):
    _merge_kernel(osb_ref, obd_ref, omm_ref, gsb_ref, gbd_ref, gmm_ref,
                  mg0_ref, mg1_ref, mg2_ref, mg3_ref, mg4_ref, mg5_ref,
                  wsb_ref, wbd_ref, wmm_ref, merged_ref, half=half)
    _out_proj_kernel(x_ref, merged_ref, wout_ref, gpost_ref, y_ref)


def _merge_out(x2d, yqg, ymg, o_sb, o_bd, o_mm, w_sb, w_bd, w_mm, w_out, g_post):
    m, d = x2d.shape
    half = d // 2
    mgb = MG_MG // half
    whole = lambda shape: pl.BlockSpec(shape, lambda i: (0,) * len(shape))
    col = lambda width, c: pl.BlockSpec((m, width), functools.partial(lambda i, c: (0, c), c=c))
    kern = functools.partial(_merge_out_kernel, half=half)
    return pl.pallas_call(
        kern,
        grid=(1,),
        in_specs=[
            whole((m, d)), whole((m, SB_WIDTH)), whole((m, BAND_WIDTH)), whole((m, MEM_WIDTH)),
            col(SB_WIDTH, QG_SB_G // SB_WIDTH), col(BAND_WIDTH, QG_BD_G // BAND_WIDTH),
            col(MEM_WIDTH, MG_MM_G // MEM_WIDTH),
        ] + [col(half, mgb + c) for c in range(6)] + [
            whole((SB_WIDTH, d)), whole((BAND_WIDTH, d)), whole((MEM_WIDTH, d)), whole((d, d)), whole((1, d)),
        ],
        out_specs=whole((m, d)),
        out_shape=jax.ShapeDtypeStruct((m, d), F32),
        scratch_shapes=[pltpu.VMEM((m, d), BF16)],
        compiler_params=_params(1),
        name="merge_out",
    )(x2d, o_sb, o_bd, o_mm, yqg, yqg, ymg, *([ymg] * 6), w_sb, w_bd, w_mm, w_out, g_post.reshape(1, d))


def _out_projection(x2d, merged, w_out, g_post, *, tm):
    m, d = x2d.shape
    return pl.pallas_call(
        _out_proj_kernel,
        grid=(m // tm,),
        in_specs=[
            pl.BlockSpec((tm, d), lambda i: (i, 0)),
            pl.BlockSpec((tm, d), lambda i: (i, 0)),
            pl.BlockSpec((d, d), lambda i: (0, 0), pipeline_mode=pl.Buffered(1)),
            pl.BlockSpec((1, d), lambda i: (0, 0)),
        ],
        out_specs=pl.BlockSpec((tm, d), lambda i: (i, 0)),
        out_shape=jax.ShapeDtypeStruct((m, d), F32),
        compiler_params=_params(1),
        name="out_projection",
    )(x2d, merged, w_out, g_post.reshape(1, d))


def _head_major(a):
    return jnp.transpose(a, (0, 2, 1, 3))


def kernel(x_prompt, x_sample, cache_sb_k, cache_sb_v, cache_band_k, cache_band_v, cache_mem_k, cache_mem_v, mem_prompt, g_pre, w_in, rel_bias, g_mem, w_mem_kv, w_up_sb, w_up_band, w_up_mem, w_out, g_post):
    depth = w_in.shape[0]
    b, t, d = x_prompt.shape
    bd, n_new, _ = x_sample.shape
    n_mem = mem_prompt.shape[1]
    r_band = cache_band_k.shape[2]
    in_width = w_in.shape[2]
    band_keep = min(BAND_ROWS, t)
    tm_p = 512
    assert COL_MG + 3 * d == in_width
    assert t % SB_TQ == 0 and t % (BAND_TQ * BAND_STEP_GROUPS) == 0 and t % tm_p == 0
    assert r_band == BAND_ROWS and n_new <= CHUNK

    negu2 = jnp.where(jnp.arange(2 * SB_TK)[:, None] % SB_TK >= jnp.arange(SB_TK)[None, :], -1.0, 0.0).astype(BF16)
    mg_width = in_width - COL_MM_Q
    assert COL_MM_Q % MG_TN == 0 and mg_width % MG_TN == 0 and COL_SB_G == 3 * KV_TN and COL_BD_Q == 4 * KV_TN
    cols = jnp.arange(QG_WIDTH)
    qg_scale = jnp.where((cols // SB_WIDTH) % 2 == 0, Q_SCALE, 0.5).astype(F32).reshape(1, QG_WIDTH)
    cols = jnp.arange(mg_width)
    mg_scale = jnp.where(cols < MG_MM_G, Q_SCALE, 0.5).astype(F32).reshape(1, mg_width)
    qg_block = lambda j: j + 2 * ((j + 1) // 2)
    mg_block = lambda j: j + COL_MM_Q // MG_TN

    xp = x_prompt.reshape(b * t, d)
    xs = x_sample.reshape(bd * n_new, d)
    ms = bd * n_new
    outs = [[] for _ in range(10)]
    for l in range(depth):
        w_kvp_b = _kv_weight_bf16(w_in[l])
        w_kv_b = w_mem_kv[l].astype(BF16)
        w_sb_b = (0.5 * w_up_sb[l]).astype(BF16)
        w_bd_b = (0.5 * w_up_band[l]).astype(BF16)
        w_mm_b = (0.5 * w_up_mem[l]).astype(BF16)
        w_out_b = w_out[l].astype(BF16)
        bias_p, bias_d = _band_bias_tables(rel_bias[l], n_new, r_band)

        ykv, hp, sbk, sbv, bdk, bdv = _kv_projection(xp, g_pre[l], w_kvp_b, tm=band_keep, seqs=1, n_seq=b,
                                                     band_keep=band_keep)
        ykv_s, hs, sbk2, sbv2, bdk2, bdv2 = _kv_projection(xs, g_pre[l], w_kvp_b, tm=ms, seqs=bd, n_seq=bd,
                                                           band_keep=n_new)
        yqg, yqg_s = _col_projection(hp, hs, w_in[l], qg_scale, qg_block, tm=PROJ_TM, tn=KV_TN,
                                     name="qg_projection")
        ymg, ymg_s = _col_projection(hp, hs, w_in[l], mg_scale, mg_block, tm=PROJ_TM, tn=MG_TN,
                                     name="mg_projection")

        mk, mv = _memory_kv(mem_prompt.reshape(b * n_mem, d), g_mem[l], w_kv_b, tm=n_mem)
        yqg3 = yqg.reshape(b, t, QG_WIDTH)
        ykv3 = ykv.reshape(b, t, KV_WIDTH)
        o_sb = _sb_prompt(yqg3, ykv3, negu2)
        o_bd = _band_prompt(yqg3, ykv3, bias_p)
        o_mm = _mem_attention(ymg.reshape(b, t, mg_width), mk.reshape(b, n_mem, MEM_WIDTH),
                              mv.reshape(b, n_mem, MEM_WIDTH), tq=1024)
        merged = _merge_branches(yqg, ymg, o_sb.reshape(b * t, -1), o_bd.reshape(b * t, -1),
                                 o_mm.reshape(b * t, -1), w_sb_b, w_bd_b, w_mm_b, tm=512)
        xp = _out_projection(xp, merged, w_out_b, g_post[l], tm=512)
        outs[0].append(_head_major(sbk))
        outs[1].append(_head_major(sbv))
        outs[2].append(_head_major(bdk))
        outs[3].append(_head_major(bdv))
        outs[4].append(mk.reshape(b, n_mem, MEM_HEADS, HEAD_DIM))
        outs[5].append(mv.reshape(b, n_mem, MEM_HEADS, HEAD_DIM))

        o_sb2, o_bd2, o_mm2 = _decode_attention(
            yqg_s.reshape(bd, n_new, QG_WIDTH), ykv_s.reshape(bd, n_new, KV_WIDTH),
            ymg_s.reshape(bd, n_new, mg_width),
            _head_major(cache_sb_k[l]), _head_major(cache_sb_v[l]),
            _head_major(cache_band_k[l]), _head_major(cache_band_v[l]),
            cache_mem_k[l].reshape(bd, n_mem, MEM_WIDTH), cache_mem_v[l].reshape(bd, n_mem, MEM_WIDTH),
            bias_d, negu2)
        xs = _merge_out(xs, yqg_s, ymg_s, o_sb2.reshape(ms, -1), o_bd2.reshape(ms, -1), o_mm2.reshape(ms, -1),
                        w_sb_b, w_bd_b, w_mm_b, w_out_b, g_post[l])
        outs[6].append(_head_major(sbk2))
        outs[7].append(_head_major(sbv2))
        outs[8].append(_head_major(bdk2))
        outs[9].append(_head_major(bdv2))

    return (xp.reshape(b, t, d), xs.reshape(bd, n_new, d)) + tuple(jnp.stack(o) for o in outs)
```

```python
import functools
import math

import jax
import jax.numpy as jnp
from jax import lax
from jax.experimental import pallas as pl
from jax.experimental.pallas import tpu as pltpu

F32 = jnp.float32
BF16 = jnp.bfloat16

HEAD_DIM = 128
SB_HEADS = 6
BAND_HEADS = 6
MEM_HEADS = 4
SB_WIDTH = SB_HEADS * HEAD_DIM
BAND_WIDTH = BAND_HEADS * HEAD_DIM
MEM_WIDTH = MEM_HEADS * HEAD_DIM
CHUNK = 64
CHUNK_SHIFT = 6
BAND_LEFT_CHUNKS = 8
BAND_ROWS = BAND_LEFT_CHUNKS * CHUNK
MAX_REL = 256
RMS_EPS = 1e-6
NEG_INF = -1e30
LOG2E = math.log2(math.e)
Q_SCALE = HEAD_DIM ** -0.5 * LOG2E

COL_SB_Q = 0
COL_SB_K = COL_SB_Q + SB_WIDTH
COL_SB_V = COL_SB_K + SB_WIDTH
COL_SB_G = COL_SB_V + SB_WIDTH
COL_BD_Q = COL_SB_G + SB_WIDTH
COL_BD_K = COL_BD_Q + BAND_WIDTH
COL_BD_V = COL_BD_K + BAND_WIDTH
COL_BD_G = COL_BD_V + BAND_WIDTH
COL_MM_Q = COL_BD_G + BAND_WIDTH
COL_MM_G = COL_MM_Q + MEM_WIDTH
COL_MG = COL_MM_G + MEM_WIDTH

KV_SB_K = 0
KV_SB_V = KV_SB_K + SB_WIDTH
KV_BD_K = KV_SB_V + SB_WIDTH
KV_BD_V = KV_BD_K + BAND_WIDTH
KV_WIDTH = KV_BD_V + BAND_WIDTH
QG_SB_Q = 0
QG_SB_G = QG_SB_Q + SB_WIDTH
QG_BD_Q = QG_SB_G + SB_WIDTH
QG_BD_G = QG_BD_Q + BAND_WIDTH
QG_WIDTH = QG_BD_G + BAND_WIDTH
MG_MM_Q = 0
MG_MM_G = MG_MM_Q + MEM_WIDTH
MG_MG = MG_MM_G + MEM_WIDTH

VMEM_LIMIT_BYTES = 56 * 1024 * 1024
MAX_VMEM_LIMIT_BYTES = 58 * 1024 * 1024
COMPILER_TEMP_BYTES = 2 * 1024 * 1024
PROJ_TM = 2048
KV_TN = SB_WIDTH
MG_TN = 1024
SB_TK = 256
SB_TQ = 16 * SB_TK
SB_DEAD = -160.0
BAND_TQ = 4 * CHUNK
BAND_WIN = BAND_TQ + BAND_ROWS
BAND_STEP_GROUPS = 8
BIAS_LANES = 1024
NEW_PAD = 128


def _params(n_axes, vmem=VMEM_LIMIT_BYTES):
    return pltpu.CompilerParams(dimension_semantics=("arbitrary",) * n_axes,
                                vmem_limit_bytes=vmem)


def _nt_dot(a, b):
    return lax.dot_general(a, b, (((1,), (1,)), ((), ())), preferred_element_type=F32)


def _dot(a, b):
    return jnp.dot(a, b, preferred_element_type=F32)


def _pre_norm_to(h_ref, x_ref, g_ref):
    x = x_ref[...]
    ms = jnp.mean(x * x, axis=-1, keepdims=True)
    h_ref[...] = ((x * lax.rsqrt(ms + RMS_EPS)) * g_ref[...]).astype(BF16)


def _kv_proj_kernel(x_ref, g_ref, wsk_ref, wsv_ref, wbk_ref, wbv_ref,
                    y_ref, h_ref, sbk_ref, sbv_ref, bdk_ref, bdv_ref, *, seqs, rows):
    _pre_norm_to(h_ref, x_ref, g_ref)
    groups = ((wsk_ref, sbk_ref), (wsv_ref, sbv_ref), (wbk_ref, bdk_ref), (wbv_ref, bdv_ref))
    for group, (w_ref, dst_ref) in enumerate(groups):
        acc = _dot(h_ref[...], w_ref[...])
        y_ref[:, group * KV_TN:(group + 1) * KV_TN] = acc.astype(BF16)
        for h in range(SB_HEADS):
            for s in range(seqs):
                dst_ref[s, h] = acc[s * rows:(s + 1) * rows, h * HEAD_DIM:(h + 1) * HEAD_DIM]


def _kv_projection(x2d, g_pre, w_kv_b, *, tm, seqs, n_seq, band_keep):
    m, d = x2d.shape
    assert KV_WIDTH == 4 * KV_TN and SB_HEADS == BAND_HEADS
    rows = tm // seqs
    seq_rows = m // n_seq
    blocks_per_seq = seq_rows // rows
    assert rows == band_keep
    kern = functools.partial(_kv_proj_kernel, seqs=seqs, rows=rows)
    sb_spec = pl.BlockSpec((seqs, SB_HEADS, rows, HEAD_DIM),
                           lambda i: (i // blocks_per_seq, 0, i % blocks_per_seq, 0))
    bd_spec = pl.BlockSpec((seqs, BAND_HEADS, rows, HEAD_DIM), lambda i: (i // blocks_per_seq, 0, 0, 0))
    sb_shape = jax.ShapeDtypeStruct((n_seq, SB_HEADS, seq_rows, HEAD_DIM), F32)
    bd_shape = jax.ShapeDtypeStruct((n_seq, BAND_HEADS, band_keep, HEAD_DIM), F32)

    def w_spec(group):
        return pl.BlockSpec((d, KV_TN), functools.partial(lambda i, c: (0, c), c=group),
                            pipeline_mode=pl.Buffered(1))

    return pl.pallas_call(
        kern,
        grid=(m // tm,),
        in_specs=[
            pl.BlockSpec((tm, d), lambda i: (i, 0)),
            pl.BlockSpec((1, d), lambda i: (0, 0)),
            w_spec(0), w_spec(1), w_spec(2), w_spec(3),
        ],
        out_specs=[pl.BlockSpec((tm, KV_WIDTH), lambda i: (i, 0)), pl.BlockSpec((tm, d), lambda i: (i, 0)),
                   sb_spec, sb_spec, bd_spec, bd_spec],
        out_shape=[jax.ShapeDtypeStruct((m, KV_WIDTH), BF16), jax.ShapeDtypeStruct((m, d), BF16),
                   sb_shape, sb_shape, bd_shape, bd_shape],
        compiler_params=_params(1),
        name="kv_projection",
    )(x2d, g_pre.reshape(1, d), w_kv_b, w_kv_b, w_kv_b, w_kv_b)


def _col_proj_kernel(h_ref, hs_ref, w_ref, cs_ref, y_ref, ys_ref, wb_ref):
    @pl.when(pl.program_id(1) == 0)
    def _():
        wb_ref[...] = w_ref[...].astype(BF16)
        ys_ref[...] = (_dot(hs_ref[...], wb_ref[...]) * cs_ref[...]).astype(BF16)

    y_ref[...] = (_dot(h_ref[...], wb_ref[...]) * cs_ref[...]).astype(BF16)


def _col_projection(h2d, hs2d, w_in, col_scale, src_block, *, tm, tn, name):
    m, d = h2d.shape
    ms = hs2d.shape[0]
    n = col_scale.shape[1]
    assert n % tn == 0 and m % tm == 0
    vmem = (2 * tm * d * 2 + 2 * d * tn * 4 + 2 * tm * tn * 2 + d * tn * 2 + 2 * tm * tn * 4
            + 2 * ms * (d + tn) * 2 + COMPILER_TEMP_BYTES)
    vmem = min(vmem, MAX_VMEM_LIMIT_BYTES)
    return pl.pallas_call(
        _col_proj_kernel,
        grid=(n // tn, m // tm),
        in_specs=[
            pl.BlockSpec((tm, d), lambda j, i: (i, 0)),
            pl.BlockSpec((ms, d), lambda j, i: (0, 0)),
            pl.BlockSpec((d, tn), lambda j, i: (0, src_block(j))),
            pl.BlockSpec((1, tn), lambda j, i: (0, j)),
        ],
        out_specs=[pl.BlockSpec((tm, tn), lambda j, i: (i, j)), pl.BlockSpec((ms, tn), lambda j, i: (0, j))],
        out_shape=[jax.ShapeDtypeStruct((m, n), BF16), jax.ShapeDtypeStruct((ms, n), BF16)],
        scratch_shapes=[pltpu.VMEM((d, tn), BF16)],
        compiler_params=_params(2, vmem=vmem),
        name=name,
    )(h2d, hs2d, w_in, col_scale)


def _kv_weight_kernel(a_ref, b_ref, c_ref, d_ref, o_ref):
    for group, w_ref in enumerate((a_ref, b_ref, c_ref, d_ref)):
        o_ref[:, group * KV_TN:(group + 1) * KV_TN] = w_ref[...].astype(BF16)


def _kv_weight_bf16(w):
    d = w.shape[0]
    tr = 256
    assert d % tr == 0

    def spec(col0):
        assert col0 % KV_TN == 0
        return pl.BlockSpec((tr, KV_TN), functools.partial(lambda i, c: (i, c), c=col0 // KV_TN))

    return pl.pallas_call(
        _kv_weight_kernel,
        grid=(d // tr,),
        in_specs=[spec(COL_SB_K), spec(COL_SB_V), spec(COL_BD_K), spec(COL_BD_V)],
        out_specs=pl.BlockSpec((tr, KV_WIDTH), lambda i: (i, 0)),
        out_shape=jax.ShapeDtypeStruct((d, KV_WIDTH), BF16),
        compiler_params=_params(1),
        name="kv_weight_cast",
    )(w, w, w, w)


def _memkv_kernel(x_ref, g_ref, w_ref, mk_ref, mv_ref):
    x = x_ref[...]
    ms = jnp.mean(x * x, axis=-1, keepdims=True)
    h = ((x * lax.rsqrt(ms + RMS_EPS)) * g_ref[...]).astype(BF16)
    acc = _dot(h, w_ref[...].astype(BF16))
    mk_ref[...] = acc[:, :MEM_WIDTH]
    mv_ref[...] = acc[:, MEM_WIDTH:]


def _memory_kv(mem2d, g_mem, w_bf16, *, tm):
    m, d = mem2d.shape
    return pl.pallas_call(
        _memkv_kernel,
        grid=(m // tm,),
        in_specs=[
            pl.BlockSpec((tm, d), lambda i: (i, 0)),
            pl.BlockSpec((1, d), lambda i: (0, 0)),
            pl.BlockSpec((d, 2 * MEM_WIDTH), lambda i: (0, 0)),
        ],
        out_specs=[pl.BlockSpec((tm, MEM_WIDTH), lambda i: (i, 0)),
                   pl.BlockSpec((tm, MEM_WIDTH), lambda i: (i, 0))],
        out_shape=[jax.ShapeDtypeStruct((m, MEM_WIDTH), F32),
                   jax.ShapeDtypeStruct((m, MEM_WIDTH), F32)],
        compiler_params=_params(1),
        name="memory_kv",
    )(mem2d, g_mem.reshape(1, d), w_bf16)


def _neg_suffix_matrix(n):
    row = lax.broadcasted_iota(jnp.int32, (2 * n, n), 0)
    col = lax.broadcasted_iota(jnp.int32, (2 * n, n), 1)
    row = jnp.where(row >= n, row - n, row)
    return jnp.where(row >= col, -1.0, 0.0).astype(BF16)


def _sb_weights(z2, carry2, negu2, mask):
    p = jnp.maximum(z2, 0.0) + jnp.log(1.0 + jnp.exp2(-jnp.abs(z2))) * LOG2E
    if mask is not None:
        p = jnp.where(mask, p, 0.0)
    p_hi = p.astype(BF16)
    p_lo = (p - p_hi.astype(F32)).astype(BF16)
    suffix = _dot(jnp.concatenate([p_hi, p_lo], axis=1), negu2)
    w = jnp.exp2(z2 + suffix + carry2)
    if mask is not None:
        w = jnp.where(mask, w, 0.0)
    return w, carry2 - jnp.sum(p, axis=-1, keepdims=True)


def _sb_prompt_kernel(q_ref, k_ref, v_ref, negu2_ref, o_ref, acc_ref, carry_ref, kpad_ref, vpad_ref, *, t):
    i = pl.program_id(2)
    n_sub = SB_TQ // SB_TK
    negu2 = negu2_ref[...]

    @pl.when(i == 0)
    def _():
        kpad_ref[0:SB_TK, :] = jnp.zeros((SB_TK, HEAD_DIM), BF16)
        vpad_ref[0:SB_TK, :] = jnp.zeros((SB_TK, HEAD_DIM), BF16)
        kpad_ref[SB_TK:SB_TK + t, :] = k_ref[...]
        vpad_ref[SB_TK:SB_TK + t, :] = v_ref[...]

    def kv_block(j):
        start = pl.multiple_of((j + 1) * SB_TK, SB_TK)
        return kpad_ref[pl.ds(start, SB_TK), :], vpad_ref[pl.ds(start, SB_TK), :]

    row = lax.broadcasted_iota(jnp.int32, (SB_TK, SB_TK), 0)
    col = lax.broadcasted_iota(jnp.int32, (SB_TK, SB_TK), 1)
    for r in range(n_sub):
        rows = slice(r * SB_TK, (r + 1) * SB_TK)
        s = i * n_sub + r
        q = q_ref[rows, :]
        kb, vb = kv_block(s)
        w, carry = _sb_weights(_nt_dot(q, kb), jnp.zeros((SB_TK, 1), F32), negu2, col < row)
        acc = _dot(w.astype(BF16), vb)
        kb, vb = kv_block(s - 1)
        prev_exists = None if r > 0 else (jnp.zeros((SB_TK, SB_TK), jnp.int32) + i) > 0
        w, carry = _sb_weights(_nt_dot(q, kb), carry, negu2, prev_exists)
        acc_ref[rows, :] = acc + _dot(w.astype(BF16), vb)
        carry_ref[rows, :] = carry

    row_q = lax.broadcasted_iota(jnp.int32, (SB_TQ, 1), 0)
    row_t = lax.broadcasted_iota(jnp.int32, (SB_TQ, SB_TK), 0)
    has_more = row_q >= (2 - n_sub * i) * SB_TK

    def any_alive(carry):
        return (jnp.max(jnp.where(has_more, carry, NEG_INF)) > SB_DEAD).astype(jnp.int32)

    def cond(state):
        j, alive = state
        return jnp.logical_and(j >= 0, alive > 0)

    def body(state):
        j, _ = state
        kb, vb = kv_block(j)
        visits = row_t >= (j - n_sub * i + 2) * SB_TK
        w, carry = _sb_weights(_nt_dot(q_ref[...], kb), carry_ref[...], negu2, visits)
        acc_ref[...] += _dot(w.astype(BF16), vb)
        carry_ref[...] = carry
        return j - 1, any_alive(carry)

    lax.while_loop(cond, body, (n_sub * i + n_sub - 3, any_alive(carry_ref[...])))
    o_ref[...] = acc_ref[...].astype(BF16)


def _sb_prompt(ya3, ykv3, negu2):
    b, t, _ = ya3.shape
    qb, kb, vb = QG_SB_Q // HEAD_DIM, KV_SB_K // HEAD_DIM, KV_SB_V // HEAD_DIM
    assert SB_TQ // SB_TK >= 3
    kern = functools.partial(_sb_prompt_kernel, t=t)
    return pl.pallas_call(
        kern,
        grid=(b, SB_HEADS, t // SB_TQ),
        in_specs=[
            pl.BlockSpec((None, SB_TQ, HEAD_DIM), lambda b, h, i: (b, i, qb + h)),
            pl.BlockSpec((None, t, HEAD_DIM), lambda b, h, i: (b, 0, kb + h)),
            pl.BlockSpec((None, t, HEAD_DIM), lambda b, h, i: (b, 0, vb + h)),
            pl.BlockSpec((2 * SB_TK, SB_TK), lambda b, h, i: (0, 0)),
        ],
        out_specs=pl.BlockSpec((None, SB_TQ, HEAD_DIM), lambda b, h, i: (b, i, h)),
        out_shape=jax.ShapeDtypeStruct((b, t, SB_WIDTH), BF16),
        scratch_shapes=[pltpu.VMEM((SB_TQ, HEAD_DIM), F32), pltpu.VMEM((SB_TQ, 1), F32),
                        pltpu.VMEM((SB_TK + t, HEAD_DIM), BF16), pltpu.VMEM((SB_TK + t, HEAD_DIM), BF16)],
        compiler_params=_params(3),
        name="sb_prompt",
    )(ya3, ykv3, ykv3, negu2)


def _sb_decode_body(q_ref, kn_ref, vn_ref, kc_hbm, vc_hbm, negu2_ref, o_ref,
                    acc_ref, carry_ref, kpad_ref, vpad_ref, kbuf_ref, vbuf_ref, sem,
                    *, n_new, n_blocks, overlap):
    b = pl.program_id(0)
    heads = SB_HEADS

    def cache_copies(j, slot):
        rows = pl.ds(pl.multiple_of((n_blocks - 1 - j) * SB_TK, SB_TK), SB_TK)
        return (pltpu.make_async_copy(kc_hbm.at[b, :, rows, :], kbuf_ref.at[slot], sem.at[0, slot]),
                pltpu.make_async_copy(vc_hbm.at[b, :, rows, :], vbuf_ref.at[slot], sem.at[1, slot]))

    def start_fetch(j, slot):
        for cp in cache_copies(j, slot):
            cp.start()

    def wait_fetch(j, slot):
        for cp in cache_copies(j, slot):
            cp.wait()

    start_fetch(0, 0)
    overlap()

    def head_cols(h):
        return slice(h * HEAD_DIM, (h + 1) * HEAD_DIM)

    def head_rows(h):
        return slice(h * n_new, (h + 1) * n_new)

    def block(k_of, v_of, negu2, mask):
        z2 = jnp.concatenate([_nt_dot(q_ref[:, head_cols(h)], k_of(h)) for h in range(heads)], axis=0)
        w, carry = _sb_weights(z2, carry_ref[...], negu2, mask)
        wb = w.astype(BF16)
        for h in range(heads):
            acc_ref[head_rows(h), :] += _dot(wb[head_rows(h), :], v_of(h))
        carry_ref[...] = carry

    def any_alive():
        return (jnp.max(carry_ref[...]) > SB_DEAD).astype(jnp.int32)

    acc_ref[...] = jnp.zeros_like(acc_ref)
    carry_ref[...] = jnp.zeros_like(carry_ref)
    kpad_ref[...] = jnp.zeros_like(kpad_ref)
    vpad_ref[...] = jnp.zeros_like(vpad_ref)
    for h in range(heads):
        kpad_ref[h, 0:n_new, :] = kn_ref[:, head_cols(h)]
        vpad_ref[h, 0:n_new, :] = vn_ref[:, head_cols(h)]
    row = lax.broadcasted_iota(jnp.int32, (n_new, NEW_PAD), 0)
    col = lax.broadcasted_iota(jnp.int32, (n_new, NEW_PAD), 1)
    mask = jnp.concatenate([(col < row).astype(jnp.int32)] * heads, axis=0) == 1
    block(lambda h: kpad_ref[h], lambda h: vpad_ref[h], _neg_suffix_matrix(NEW_PAD), mask)

    negu2 = negu2_ref[...]

    def cond(state):
        j, alive = state
        return jnp.logical_and(j < n_blocks, alive > 0)

    def body(state):
        j, _ = state
        slot = j % 2
        wait_fetch(j, slot)

        @pl.when(j + 1 < n_blocks)
        def _():
            start_fetch(j + 1, 1 - slot)

        block(lambda h: kbuf_ref[slot, h].astype(BF16), lambda h: vbuf_ref[slot, h].astype(BF16),
              negu2, None)
        return j + 1, any_alive()

    j_end, _ = lax.while_loop(cond, body, (0, any_alive()))

    @pl.when(j_end < n_blocks)
    def _():
        wait_fetch(j_end, j_end % 2)

    for h in range(heads):
        o_ref[:, head_cols(h)] = acc_ref[head_rows(h), :].astype(BF16)


def _softmax2_pv(parts):
    mx = functools.reduce(jnp.maximum, [jnp.max(s, axis=-1, keepdims=True) for s, _ in parts])
    num = None
    den = None
    for s, v in parts:
        p = jnp.exp2(s - mx)
        d = jnp.sum(p, axis=-1, keepdims=True)
        o = _dot(p.astype(BF16), v)
        num = o if num is None else num + o
        den = d if den is None else den + d
    return num / den


def _band_bias_kernel(g_ref, tp_ref, td_ref, *, n_new, r_band):
    x = jnp.broadcast_to(g_ref[...], (BAND_TQ, BIAS_LANES))
    x = pltpu.roll(x, BAND_TQ, 1, stride=1, stride_axis=0)
    tbl = x[:, :BAND_WIN] * LOG2E
    r = lax.broadcasted_iota(jnp.int32, (BAND_TQ, BAND_WIN), 0)
    j = lax.broadcasted_iota(jnp.int32, (BAND_TQ, BAND_WIN), 1)
    dc = (j >> CHUNK_SHIFT) - (r >> CHUNK_SHIFT)
    tp_ref[...] = jnp.where(jnp.logical_and(dc >= 0, dc <= BAND_LEFT_CHUNKS), tbl, NEG_INF)
    jd = lax.broadcasted_iota(jnp.int32, (n_new, r_band + NEW_PAD), 1)
    td_ref[...] = jnp.where(jd < r_band + n_new, tbl[:n_new, :r_band + NEW_PAD], NEG_INF)


def _band_bias_tables(rel_bias, n_new, r_band):
    h = rel_bias.shape[0]
    assert BAND_ROWS == 2 * MAX_REL and r_band == BAND_ROWS and BIAS_LANES == 2 * BAND_ROWS
    rb = rel_bias.astype(F32)
    g = jnp.concatenate([rb[:, :0:-1], jnp.broadcast_to(rb[:, -1:], (h, BIAS_LANES - 2 * MAX_REL))], axis=1)
    kern = functools.partial(_band_bias_kernel, n_new=n_new, r_band=r_band)
    return pl.pallas_call(
        kern,
        grid=(h,),
        in_specs=[pl.BlockSpec((None, 1, BIAS_LANES), lambda i: (i, 0, 0))],
        out_specs=[pl.BlockSpec((None, BAND_TQ, BAND_WIN), lambda i: (i, 0, 0)),
                   pl.BlockSpec((None, n_new, r_band + NEW_PAD), lambda i: (i, 0, 0))],
        out_shape=[jax.ShapeDtypeStruct((h, BAND_TQ, BAND_WIN), F32),
                   jax.ShapeDtypeStruct((h, n_new, r_band + NEW_PAD), F32)],
        compiler_params=_params(1),
        name="band_bias",
    )(g.reshape(h, 1, BIAS_LANES))


def _band_prompt_kernel(q_ref, k_ref, v_ref, bias_ref, o_ref, kpad_ref, vpad_ref, *, t):
    s_idx = pl.program_id(2)

    @pl.when(s_idx == 0)
    def _():
        kpad_ref[0:BAND_ROWS, :] = jnp.zeros((BAND_ROWS, HEAD_DIM), BF16)
        vpad_ref[0:BAND_ROWS, :] = jnp.zeros((BAND_ROWS, HEAD_DIM), BF16)
        kpad_ref[BAND_ROWS:BAND_ROWS + t, :] = k_ref[...]
        vpad_ref[BAND_ROWS:BAND_ROWS + t, :] = v_ref[...]

    col = lax.broadcasted_iota(jnp.int32, (BAND_TQ, BAND_WIN), 1)
    for gg in range(BAND_STEP_GROUPS):
        g = s_idx * BAND_STEP_GROUPS + gg
        start = pl.multiple_of(g * BAND_TQ, BAND_TQ)
        rows = slice(gg * BAND_TQ, (gg + 1) * BAND_TQ)
        s = _nt_dot(q_ref[rows, :], kpad_ref[pl.ds(start, BAND_WIN), :]) + bias_ref[...]
        s = jnp.where(col + g * BAND_TQ >= BAND_ROWS, s, NEG_INF)
        o_ref[rows, :] = _softmax2_pv([(s, vpad_ref[pl.ds(start, BAND_WIN), :])]).astype(BF16)


def _band_prompt(ya3, ykv3, bias_tbl):
    b, t, _ = ya3.shape
    tq = BAND_TQ * BAND_STEP_GROUPS
    qb, kb, vb = QG_BD_Q // HEAD_DIM, KV_BD_K // HEAD_DIM, KV_BD_V // HEAD_DIM
    kern = functools.partial(_band_prompt_kernel, t=t)
    return pl.pallas_call(
        kern,
        grid=(b, BAND_HEADS, t // tq),
        in_specs=[
            pl.BlockSpec((None, tq, HEAD_DIM), lambda b, h, g: (b, g, qb + h)),
            pl.BlockSpec((None, t, HEAD_DIM), lambda b, h, g: (b, 0, kb + h)),
            pl.BlockSpec((None, t, HEAD_DIM), lambda b, h, g: (b, 0, vb + h)),
            pl.BlockSpec((None, BAND_TQ, BAND_WIN), lambda b, h, g: (h, 0, 0)),
        ],
        out_specs=pl.BlockSpec((None, tq, HEAD_DIM), lambda b, h, g: (b, g, h)),
        out_shape=jax.ShapeDtypeStruct((b, t, BAND_WIDTH), BF16),
        scratch_shapes=[pltpu.VMEM((BAND_ROWS + t, HEAD_DIM), BF16),
                        pltpu.VMEM((BAND_ROWS + t, HEAD_DIM), BF16)],
        compiler_params=_params(3),
        name="band_prompt",
    )(ya3, ykv3, ykv3, bias_tbl)


def _band_decode_kernel(q_ref, kn_ref, vn_ref, kc_ref, vc_ref, bias_ref, o_ref, kpad_ref, vpad_ref,
                        *, n_new, r_band):
    kpad_ref[...] = jnp.zeros_like(kpad_ref)
    vpad_ref[...] = jnp.zeros_like(vpad_ref)
    for h in range(BAND_HEADS):
        cols = slice(h * HEAD_DIM, (h + 1) * HEAD_DIM)
        kpad_ref[h, 0:n_new, :] = kn_ref[:, cols]
        vpad_ref[h, 0:n_new, :] = vn_ref[:, cols]
    for h in range(BAND_HEADS):
        cols = slice(h * HEAD_DIM, (h + 1) * HEAD_DIM)
        q = q_ref[:, cols]
        s_cache = _nt_dot(q, kc_ref[h].astype(BF16)) + bias_ref[h, :, 0:r_band]
        s_new = _nt_dot(q, kpad_ref[h]) + bias_ref[h, :, r_band:r_band + NEW_PAD]
        o_ref[:, cols] = _softmax2_pv([(s_cache, vc_ref[h].astype(BF16)),
                                       (s_new, vpad_ref[h])]).astype(BF16)


def _mem_attn_kernel(q_ref, mk_ref, mv_ref, o_ref):
    for h in range(MEM_HEADS):
        sl = slice(h * HEAD_DIM, (h + 1) * HEAD_DIM)
        s = _nt_dot(q_ref[:, sl], mk_ref[:, sl].astype(BF16))
        o_ref[:, sl] = _softmax2_pv([(s, mv_ref[:, sl].astype(BF16))]).astype(BF16)


def _mem_attention(y3, mk, mv, *, tq):
    b, t, _ = y3.shape
    n_mem = mk.shape[1]
    qb = MG_MM_Q // MEM_WIDTH
    return pl.pallas_call(
        _mem_attn_kernel,
        grid=(b, t // tq),
        in_specs=[
            pl.BlockSpec((None, tq, MEM_WIDTH), lambda b, i: (b, i, qb)),
            pl.BlockSpec((None, n_mem, MEM_WIDTH), lambda b, i: (b, 0, 0)),
            pl.BlockSpec((None, n_mem, MEM_WIDTH), lambda b, i: (b, 0, 0)),
        ],
        out_specs=pl.BlockSpec((None, tq, MEM_WIDTH), lambda b, i: (b, i, 0)),
        out_shape=jax.ShapeDtypeStruct((b, t, MEM_WIDTH), BF16),
        compiler_params=_params(2),
        name="mem_attention",
    )(y3, mk, mv)


def _decode_attn_kernel(qs_ref, kns_ref, vns_ref, kcs_hbm, vcs_hbm, negu2_ref,
                        qb_ref, knb_ref, vnb_ref, kcb_ref, vcb_ref, bias_ref, qm_ref, mk_ref, mv_ref,
                        osb_ref, obd_ref, omm_ref,
                        acc_ref, carry_ref, kpad_ref, vpad_ref, kbuf_ref, vbuf_ref, sem, kpadb_ref, vpadb_ref,
                        *, n_new, n_blocks, r_band):
    def band_and_memory():
        _band_decode_kernel(qb_ref, knb_ref, vnb_ref, kcb_ref, vcb_ref, bias_ref, obd_ref,
                            kpadb_ref, vpadb_ref, n_new=n_new, r_band=r_band)
        _mem_attn_kernel(qm_ref, mk_ref, mv_ref, omm_ref)

    _sb_decode_body(qs_ref, kns_ref, vns_ref, kcs_hbm, vcs_hbm, negu2_ref, osb_ref,
                    acc_ref, carry_ref, kpad_ref, vpad_ref, kbuf_ref, vbuf_ref, sem,
                    n_new=n_new, n_blocks=n_blocks, overlap=band_and_memory)


def _decode_attention(yqg3, ykv3, ymg3, cache_sb_k, cache_sb_v, cache_bd_k, cache_bd_v, mk, mv, bias_tbl, negu2):
    bd, n_new, _ = yqg3.shape
    past = cache_sb_k.shape[2]
    r_band = cache_bd_k.shape[2]
    n_mem = mk.shape[1]
    assert past % SB_TK == 0 and n_new <= NEW_PAD and n_new % 16 == 0
    kern = functools.partial(_decode_attn_kernel, n_new=n_new, n_blocks=past // SB_TK, r_band=r_band)
    slab = lambda width, col0: pl.BlockSpec((None, n_new, width), functools.partial(lambda b, c: (b, 0, c), c=col0 // width))
    band_cache = pl.BlockSpec((None, BAND_HEADS, r_band, HEAD_DIM), lambda b: (b, 0, 0, 0))
    mem_cache = pl.BlockSpec((None, n_mem, MEM_WIDTH), lambda b: (b, 0, 0))
    return pl.pallas_call(
        kern,
        grid=(bd,),
        in_specs=[
            slab(SB_WIDTH, QG_SB_Q), slab(SB_WIDTH, KV_SB_K), slab(SB_WIDTH, KV_SB_V),
            pl.BlockSpec(memory_space=pl.ANY), pl.BlockSpec(memory_space=pl.ANY),
            pl.BlockSpec((2 * SB_TK, SB_TK), lambda b: (0, 0)),
            slab(BAND_WIDTH, QG_BD_Q), slab(BAND_WIDTH, KV_BD_K), slab(BAND_WIDTH, KV_BD_V),
            band_cache, band_cache,
            pl.BlockSpec((BAND_HEADS, n_new, r_band + NEW_PAD), lambda b: (0, 0, 0)),
            slab(MEM_WIDTH, MG_MM_Q), mem_cache, mem_cache,
        ],
        out_specs=[pl.BlockSpec((None, n_new, SB_WIDTH), lambda b: (b, 0, 0)),
                   pl.BlockSpec((None, n_new, BAND_WIDTH), lambda b: (b, 0, 0)),
                   pl.BlockSpec((None, n_new, MEM_WIDTH), lambda b: (b, 0, 0))],
        out_shape=[jax.ShapeDtypeStruct((bd, n_new, SB_WIDTH), BF16),
                   jax.ShapeDtypeStruct((bd, n_new, BAND_WIDTH), BF16),
                   jax.ShapeDtypeStruct((bd, n_new, MEM_WIDTH), BF16)],
        scratch_shapes=[pltpu.VMEM((SB_HEADS * n_new, HEAD_DIM), F32),
                        pltpu.VMEM((SB_HEADS * n_new, 1), F32),
                        pltpu.VMEM((SB_HEADS, NEW_PAD, HEAD_DIM), BF16),
                        pltpu.VMEM((SB_HEADS, NEW_PAD, HEAD_DIM), BF16),
                        pltpu.VMEM((2, SB_HEADS, SB_TK, HEAD_DIM), F32),
                        pltpu.VMEM((2, SB_HEADS, SB_TK, HEAD_DIM), F32),
                        pltpu.SemaphoreType.DMA((2, 2)),
                        pltpu.VMEM((BAND_HEADS, NEW_PAD, HEAD_DIM), BF16),
                        pltpu.VMEM((BAND_HEADS, NEW_PAD, HEAD_DIM), BF16)],
        compiler_params=_params(1),
        name="decode_attention",
    )(yqg3, ykv3, ykv3, cache_sb_k, cache_sb_v, negu2, yqg3, ykv3, ykv3, cache_bd_k, cache_bd_v, bias_tbl,
      ymg3, mk, mv)


def _silu_of_half(h):
    return h + h * jnp.tanh(h)


def _merge_kernel(osb_ref, obd_ref, omm_ref, gsb_ref, gbd_ref, gmm_ref,
                  mg0_ref, mg1_ref, mg2_ref, mg3_ref, mg4_ref, mg5_ref,
                  wsb_ref, wbd_ref, wmm_ref, merged_ref, *, half):
    u_sb = (osb_ref[...].astype(F32) * _silu_of_half(gsb_ref[...].astype(F32))).astype(BF16)
    u_bd = (obd_ref[...].astype(F32) * _silu_of_half(gbd_ref[...].astype(F32))).astype(BF16)
    u_mm = (omm_ref[...].astype(F32) * _silu_of_half(gmm_ref[...].astype(F32))).astype(BF16)
    mg = ((mg0_ref, mg2_ref, mg4_ref), (mg1_ref, mg3_ref, mg5_ref))
    for n in range(2):
        cols = slice(n * half, (n + 1) * half)
        merged = None
        for m_ref, u, w_ref in zip(mg[n], (u_sb, u_bd, u_mm), (wsb_ref, wbd_ref, wmm_ref)):
            a = _dot(u, w_ref[:, cols])
            term = a + a * jnp.tanh(m_ref[...].astype(F32))
            merged = term if merged is None else merged + term
        merged_ref[:, cols] = merged.astype(BF16)


def _merge_branches(yqg, ymg, o_sb, o_bd, o_mm, w_sb, w_bd, w_mm, *, tm):
    m = yqg.shape[0]
    d = w_sb.shape[1]
    half = d // 2
    assert MG_MG % half == 0
    mgb = MG_MG // half
    const = dict(pipeline_mode=pl.Buffered(1))
    kern = functools.partial(_merge_kernel, half=half)
    return pl.pallas_call(
        kern,
        grid=(m // tm,),
        in_specs=[
            pl.BlockSpec((tm, SB_WIDTH), lambda i: (i, 0)),
            pl.BlockSpec((tm, BAND_WIDTH), lambda i: (i, 0)),
            pl.BlockSpec((tm, MEM_WIDTH), lambda i: (i, 0)),
            pl.BlockSpec((tm, SB_WIDTH), lambda i: (i, QG_SB_G // SB_WIDTH)),
            pl.BlockSpec((tm, BAND_WIDTH), lambda i: (i, QG_BD_G // BAND_WIDTH)),
            pl.BlockSpec((tm, MEM_WIDTH), lambda i: (i, MG_MM_G // MEM_WIDTH)),
        ] + [pl.BlockSpec((tm, half), functools.partial(lambda i, c: (i, c), c=mgb + c)) for c in range(6)] + [
            pl.BlockSpec((SB_WIDTH, d), lambda i: (0, 0), **const),
            pl.BlockSpec((BAND_WIDTH, d), lambda i: (0, 0), **const),
            pl.BlockSpec((MEM_WIDTH, d), lambda i: (0, 0), **const),
        ],
        out_specs=pl.BlockSpec((tm, d), lambda i: (i, 0)),
        out_shape=jax.ShapeDtypeStruct((m, d), BF16),
        compiler_params=_params(1),
        name="merge_branches",
    )(o_sb, o_bd, o_mm, yqg, yqg, ymg, *([ymg] * 6), w_sb, w_bd, w_mm)


def _out_proj_kernel(x_ref, merged_ref, wout_ref, gpost_ref, y_ref):
    y = _dot(merged_ref[...], wout_ref[...])
    ms = jnp.mean(y * y, axis=-1, keepdims=True)
    y_ref[...] = x_ref[...] + (y * lax.rsqrt(ms + RMS_EPS)) * gpost_ref[...]


def _merge_out_kernel(x_ref, osb_ref, obd_ref, omm_ref, gsb_ref, gbd_ref, gmm_ref,
                      mg0_ref, mg1_ref, mg2_ref, mg3_ref, mg4_ref, mg5_ref,
                      wsb_ref, wbd_ref, wmm_ref, wout_ref, gpost_ref, y_ref, merged_ref, *, half):
    _merge_kernel(osb_ref, obd_ref, omm_ref, gsb_ref, gbd_ref, gmm_ref,
                  mg0_ref, mg1_ref, mg2_ref, mg3_ref, mg4_ref, mg5_ref,
                  wsb_ref, wbd_ref, wmm_ref, merged_ref, half=half)
    _out_proj_kernel(x_ref, merged_ref, wout_ref, gpost_ref, y_ref)


def _merge_out(x2d, yqg, ymg, o_sb, o_bd, o_mm, w_sb, w_bd, w_mm, w_out, g_post):
    m, d = x2d.shape
    half = d // 2
    mgb = MG_MG // half
    whole = lambda shape: pl.BlockSpec(shape, lambda i: (0,) * len(shape))
    col = lambda width, c: pl.BlockSpec((m, width), functools.partial(lambda i, c: (0, c), c=c))
    kern = functools.partial(_merge_out_kernel, half=half)
    return pl.pallas_call(
        kern,
        grid=(1,),
        in_specs=[
            whole((m, d)), whole((m, SB_WIDTH)), whole((m, BAND_WIDTH)), whole((m, MEM_WIDTH)),
            col(SB_WIDTH, QG_SB_G // SB_WIDTH), col(BAND_WIDTH, QG_BD_G // BAND_WIDTH),
            col(MEM_WIDTH, MG_MM_G // MEM_WIDTH),
        ] + [col(half, mgb + c) for c in range(6)] + [
            whole((SB_WIDTH, d)), whole((BAND_WIDTH, d)), whole((MEM_WIDTH, d)), whole((d, d)), whole((1, d)),
        ],
        out_specs=whole((m, d)),
        out_shape=jax.ShapeDtypeStruct((m, d), F32),
        scratch_shapes=[pltpu.VMEM((m, d), BF16)],
        compiler_params=_params(1),
        name="merge_out",
    )(x2d, o_sb, o_bd, o_mm, yqg, yqg, ymg, *([ymg] * 6), w_sb, w_bd, w_mm, w_out, g_post.reshape(1, d))


def _out_projection(x2d, merged, w_out, g_post, *, tm):
    m, d = x2d.shape
    return pl.pallas_call(
        _out_proj_kernel,
        grid=(m // tm,),
        in_specs=[
            pl.BlockSpec((tm, d), lambda i: (i, 0)),
            pl.BlockSpec((tm, d), lambda i: (i, 0)),
            pl.BlockSpec((d, d), lambda i: (0, 0), pipeline_mode=pl.Buffered(1)),
            pl.BlockSpec((1, d), lambda i: (0, 0)),
        ],
        out_specs=pl.BlockSpec((tm, d), lambda i: (i, 0)),
        out_shape=jax.ShapeDtypeStruct((m, d), F32),
        compiler_params=_params(1),
        name="out_projection",
    )(x2d, merged, w_out, g_post.reshape(1, d))


def _head_major(a):
    return jnp.transpose(a, (0, 2, 1, 3))


def kernel(x_prompt, x_sample, cache_sb_k, cache_sb_v, cache_band_k, cache_band_v, cache_mem_k, cache_mem_v, mem_prompt, g_pre, w_in, rel_bias, g_mem, w_mem_kv, w_up_sb, w_up_band, w_up_mem, w_out, g_post):
    depth = w_in.shape[0]
    b, t, d = x_prompt.shape
    bd, n_new, _ = x_sample.shape
    n_mem = mem_prompt.shape[1]
    r_band = cache_band_k.shape[2]
    in_width = w_in.shape[2]
    band_keep = min(BAND_ROWS, t)
    tm_p = 512
    assert COL_MG + 3 * d == in_width
    assert t % SB_TQ == 0 and t % (BAND_TQ * BAND_STEP_GROUPS) == 0 and t % tm_p == 0
    assert r_band == BAND_ROWS and n_new <= CHUNK

    negu2 = jnp.where(jnp.arange(2 * SB_TK)[:, None] % SB_TK >= jnp.arange(SB_TK)[None, :], -1.0, 0.0).astype(BF16)
    mg_width = in_width - COL_MM_Q
    assert COL_MM_Q % MG_TN == 0 and mg_width % MG_TN == 0 and COL_SB_G == 3 * KV_TN and COL_BD_Q == 4 * KV_TN
    cols = jnp.arange(QG_WIDTH)
    qg_scale = jnp.where((cols // SB_WIDTH) % 2 == 0, Q_SCALE, 0.5).astype(F32).reshape(1, QG_WIDTH)
    cols = jnp.arange(mg_width)
    mg_scale = jnp.where(cols < MG_MM_G, Q_SCALE, 0.5).astype(F32).reshape(1, mg_width)
    qg_block = lambda j: j + 2 * ((j + 1) // 2)
    mg_block = lambda j: j + COL_MM_Q // MG_TN

    xp = x_prompt.reshape(b * t, d)
    xs = x_sample.reshape(bd * n_new, d)
    ms = bd * n_new
    outs = [[] for _ in range(10)]
    for l in range(depth):
        w_kvp_b = _kv_weight_bf16(w_in[l])
        w_sb_b = (0.5 * w_up_sb[l]).astype(BF16)
        w_bd_b = (0.5 * w_up_band[l]).astype(BF16)
        w_mm_b = (0.5 * w_up_mem[l]).astype(BF16)
        w_out_b = w_out[l].astype(BF16)
        bias_p, bias_d = _band_bias_tables(rel_bias[l], n_new, r_band)

        ykv, hp, sbk, sbv, bdk, bdv = _kv_projection(xp, g_pre[l], w_kvp_b, tm=band_keep, seqs=1, n_seq=b,
                                                     band_keep=band_keep)
        ykv_s, hs, sbk2, sbv2, bdk2, bdv2 = _kv_projection(xs, g_pre[l], w_kvp_b, tm=ms, seqs=bd, n_seq=bd,
                                                           band_keep=n_new)
        yqg, yqg_s = _col_projection(hp, hs, w_in[l], qg_scale, qg_block, tm=PROJ_TM, tn=KV_TN,
                                     name="qg_projection")
        ymg, ymg_s = _col_projection(hp, hs, w_in[l], mg_scale, mg_block, tm=PROJ_TM, tn=MG_TN,
                                     name="mg_projection")

        mk, mv = _memory_kv(mem_prompt.reshape(b * n_mem, d), g_mem[l], w_mem_kv[l], tm=n_mem)
        yqg3 = yqg.reshape(b, t, QG_WIDTH)
        ykv3 = ykv.reshape(b, t, KV_WIDTH)
        o_sb = _sb_prompt(yqg3, ykv3, negu2)
        o_bd = _band_prompt(yqg3, ykv3, bias_p)
        o_mm = _mem_attention(ymg.reshape(b, t, mg_width), mk.reshape(b, n_mem, MEM_WIDTH),
                              mv.reshape(b, n_mem, MEM_WIDTH), tq=1024)
        merged = _merge_branches(yqg, ymg, o_sb.reshape(b * t, -1), o_bd.reshape(b * t, -1),
                                 o_mm.reshape(b * t, -1), w_sb_b, w_bd_b, w_mm_b, tm=512)
        xp = _out_projection(xp, merged, w_out_b, g_post[l], tm=512)
        outs[0].append(_head_major(sbk))
        outs[1].append(_head_major(sbv))
        outs[2].append(_head_major(bdk))
        outs[3].append(_head_major(bdv))
        outs[4].append(mk.reshape(b, n_mem, MEM_HEADS, HEAD_DIM))
        outs[5].append(mv.reshape(b, n_mem, MEM_HEADS, HEAD_DIM))

        o_sb2, o_bd2, o_mm2 = _decode_attention(
            yqg_s.reshape(bd, n_new, QG_WIDTH), ykv_s.reshape(bd, n_new, KV_WIDTH),
            ymg_s.reshape(bd, n_new, mg_width),
            _head_major(cache_sb_k[l]), _head_major(cache_sb_v[l]),
            _head_major(cache_band_k[l]), _head_major(cache_band_v[l]),
            cache_mem_k[l].reshape(bd, n_mem, MEM_WIDTH), cache_mem_v[l].reshape(bd, n_mem, MEM_WIDTH),
            bias_d, negu2)
        xs = _merge_out(xs, yqg_s, ymg_s, o_sb2.reshape(ms, -1), o_bd2.reshape(ms, -1), o_mm2.reshape(ms, -1),
                        w_sb_b, w_bd_b, w_mm_b, w_out_b, g_post[l])
        outs[6].append(_head_major(sbk2))
        outs[7].append(_head_major(sbv2))
        outs[8].append(_head_major(bdk2))
        outs[9].append(_head_major(bdv2))

    return (xp.reshape(b, t, d), xs.reshape(bd, n_new, d)) + tuple(jnp.stack(o) for o in outs)
```

```python
import functools
import math

import jax
import jax.numpy as jnp
from jax import lax
from jax.experimental import pallas as pl
from jax.experimental.pallas import tpu as pltpu

F32 = jnp.float32
BF16 = jnp.bfloat16

HEAD_DIM = 128
SB_HEADS = 6
BAND_HEADS = 6
MEM_HEADS = 4
SB_WIDTH = SB_HEADS * HEAD_DIM
BAND_WIDTH = BAND_HEADS * HEAD_DIM
MEM_WIDTH = MEM_HEADS * HEAD_DIM
CHUNK = 64
CHUNK_SHIFT = 6
BAND_LEFT_CHUNKS = 8
BAND_ROWS = BAND_LEFT_CHUNKS * CHUNK
MAX_REL = 256
RMS_EPS = 1e-6
NEG_INF = -1e30
LOG2E = math.log2(math.e)
Q_SCALE = HEAD_DIM ** -0.5 * LOG2E

COL_SB_Q = 0
COL_SB_K = COL_SB_Q + SB_WIDTH
COL_SB_V = COL_SB_K + SB_WIDTH
COL_SB_G = COL_SB_V + SB_WIDTH
COL_BD_Q = COL_SB_G + SB_WIDTH
COL_BD_K = COL_BD_Q + BAND_WIDTH
COL_BD_V = COL_BD_K + BAND_WIDTH
COL_BD_G = COL_BD_V + BAND_WIDTH
COL_MM_Q = COL_BD_G + BAND_WIDTH
COL_MM_G = COL_MM_Q + MEM_WIDTH
COL_MG = COL_MM_G + MEM_WIDTH

KV_SB_K = 0
KV_SB_V = KV_SB_K + SB_WIDTH
KV_BD_K = KV_SB_V + SB_WIDTH
KV_BD_V = KV_BD_K + BAND_WIDTH
KV_WIDTH = KV_BD_V + BAND_WIDTH
QG_SB_Q = 0
QG_SB_G = QG_SB_Q + SB_WIDTH
QG_BD_Q = QG_SB_G + SB_WIDTH
QG_BD_G = QG_BD_Q + BAND_WIDTH
QG_WIDTH = QG_BD_G + BAND_WIDTH
MG_MM_Q = 0
MG_MM_G = MG_MM_Q + MEM_WIDTH
MG_MG = MG_MM_G + MEM_WIDTH

VMEM_LIMIT_BYTES = 56 * 1024 * 1024
MAX_VMEM_LIMIT_BYTES = 58 * 1024 * 1024
COMPILER_TEMP_BYTES = 2 * 1024 * 1024
PROJ_TM = 2048
KV_TN = SB_WIDTH
MG_TN = 1024
SB_TK = 256
SB_TQ = 16 * SB_TK
SB_DEAD = -160.0
BAND_TQ = 4 * CHUNK
BAND_WIN = BAND_TQ + BAND_ROWS
BAND_STEP_GROUPS = 8
BIAS_LANES = 1024
NEW_PAD = 128


def _params(n_axes, vmem=VMEM_LIMIT_BYTES):
    return pltpu.CompilerParams(dimension_semantics=("arbitrary",) * n_axes,
                                vmem_limit_bytes=vmem)


def _nt_dot(a, b):
    return lax.dot_general(a, b, (((1,), (1,)), ((), ())), preferred_element_type=F32)


def _dot(a, b):
    return jnp.dot(a, b, preferred_element_type=F32)


def _pre_norm_to(h_ref, x_ref, g_ref):
    x = x_ref[...]
    ms = jnp.mean(x * x, axis=-1, keepdims=True)
    h_ref[...] = ((x * lax.rsqrt(ms + RMS_EPS)) * g_ref[...]).astype(BF16)


def _kv_proj_kernel(x_ref, g_ref, wsk_ref, wsv_ref, wbk_ref, wbv_ref,
                    y_ref, h_ref, sbk_ref, sbv_ref, bdk_ref, bdv_ref, *, seqs, rows):
    _pre_norm_to(h_ref, x_ref, g_ref)
    groups = ((wsk_ref, sbk_ref), (wsv_ref, sbv_ref), (wbk_ref, bdk_ref), (wbv_ref, bdv_ref))
    for group, (w_ref, dst_ref) in enumerate(groups):
        acc = _dot(h_ref[...], w_ref[...])
        y_ref[:, group * KV_TN:(group + 1) * KV_TN] = acc.astype(BF16)
        for h in range(SB_HEADS):
            for s in range(seqs):
                dst_ref[s, h] = acc[s * rows:(s + 1) * rows, h * HEAD_DIM:(h + 1) * HEAD_DIM]


def _kv_projection(x2d, g_pre, w_kv_b, *, tm, seqs, n_seq, band_keep):
    m, d = x2d.shape
    assert KV_WIDTH == 4 * KV_TN and SB_HEADS == BAND_HEADS
    rows = tm // seqs
    seq_rows = m // n_seq
    blocks_per_seq = seq_rows // rows
    assert rows == band_keep
    kern = functools.partial(_kv_proj_kernel, seqs=seqs, rows=rows)
    sb_spec = pl.BlockSpec((seqs, SB_HEADS, rows, HEAD_DIM),
                           lambda i: (i // blocks_per_seq, 0, i % blocks_per_seq, 0))
    bd_spec = pl.BlockSpec((seqs, BAND_HEADS, rows, HEAD_DIM), lambda i: (i // blocks_per_seq, 0, 0, 0))
    sb_shape = jax.ShapeDtypeStruct((n_seq, SB_HEADS, seq_rows, HEAD_DIM), F32)
    bd_shape = jax.ShapeDtypeStruct((n_seq, BAND_HEADS, band_keep, HEAD_DIM), F32)

    def w_spec(group):
        return pl.BlockSpec((d, KV_TN), functools.partial(lambda i, c: (0, c), c=group),
                            pipeline_mode=pl.Buffered(1))

    return pl.pallas_call(
        kern,
        grid=(m // tm,),
        in_specs=[
            pl.BlockSpec((tm, d), lambda i: (i, 0)),
            pl.BlockSpec((1, d), lambda i: (0, 0)),
            w_spec(0), w_spec(1), w_spec(2), w_spec(3),
        ],
        out_specs=[pl.BlockSpec((tm, KV_WIDTH), lambda i: (i, 0)), pl.BlockSpec((tm, d), lambda i: (i, 0)),
                   sb_spec, sb_spec, bd_spec, bd_spec],
        out_shape=[jax.ShapeDtypeStruct((m, KV_WIDTH), BF16), jax.ShapeDtypeStruct((m, d), BF16),
                   sb_shape, sb_shape, bd_shape, bd_shape],
        compiler_params=_params(1),
        name="kv_projection",
    )(x2d, g_pre.reshape(1, d), w_kv_b, w_kv_b, w_kv_b, w_kv_b)


def _col_proj_kernel(h_ref, hs_ref, w_ref, cs_ref, y_ref, ys_ref, wb_ref):
    @pl.when(pl.program_id(1) == 0)
    def _():
        wb_ref[...] = w_ref[...].astype(BF16)
        ys_ref[...] = (_dot(hs_ref[...], wb_ref[...]) * cs_ref[...]).astype(BF16)

    y_ref[...] = (_dot(h_ref[...], wb_ref[...]) * cs_ref[...]).astype(BF16)


def _col_projection(h2d, hs2d, w_in, col_scale, src_block, *, tm, tn, name):
    m, d = h2d.shape
    ms = hs2d.shape[0]
    n = col_scale.shape[1]
    assert n % tn == 0 and m % tm == 0
    vmem = (2 * tm * d * 2 + 2 * d * tn * 4 + 2 * tm * tn * 2 + d * tn * 2 + 2 * tm * tn * 4
            + 2 * ms * (d + tn) * 2 + COMPILER_TEMP_BYTES)
    vmem = min(vmem, MAX_VMEM_LIMIT_BYTES)
    return pl.pallas_call(
        _col_proj_kernel,
        grid=(n // tn, m // tm),
        in_specs=[
            pl.BlockSpec((tm, d), lambda j, i: (i, 0)),
            pl.BlockSpec((ms, d), lambda j, i: (0, 0)),
            pl.BlockSpec((d, tn), lambda j, i: (0, src_block(j))),
            pl.BlockSpec((1, tn), lambda j, i: (0, j)),
        ],
        out_specs=[pl.BlockSpec((tm, tn), lambda j, i: (i, j)), pl.BlockSpec((ms, tn), lambda j, i: (0, j))],
        out_shape=[jax.ShapeDtypeStruct((m, n), BF16), jax.ShapeDtypeStruct((ms, n), BF16)],
        scratch_shapes=[pltpu.VMEM((d, tn), BF16)],
        compiler_params=_params(2, vmem=vmem),
        name=name,
    )(h2d, hs2d, w_in, col_scale)


def _kv_weight_kernel(a_ref, b_ref, c_ref, d_ref, o_ref):
    for group, w_ref in enumerate((a_ref, b_ref, c_ref, d_ref)):
        o_ref[:, group * KV_TN:(group + 1) * KV_TN] = w_ref[...].astype(BF16)


def _kv_weight_bf16(w):
    d = w.shape[0]
    tr = 256
    assert d % tr == 0

    def spec(col0):
        assert col0 % KV_TN == 0
        return pl.BlockSpec((tr, KV_TN), functools.partial(lambda i, c: (i, c), c=col0 // KV_TN))

    return pl.pallas_call(
        _kv_weight_kernel,
        grid=(d // tr,),
        in_specs=[spec(COL_SB_K), spec(COL_SB_V), spec(COL_BD_K), spec(COL_BD_V)],
        out_specs=pl.BlockSpec((tr, KV_WIDTH), lambda i: (i, 0)),
        out_shape=jax.ShapeDtypeStruct((d, KV_WIDTH), BF16),
        compiler_params=_params(1),
        name="kv_weight_cast",
    )(w, w, w, w)


def _memkv_kernel(x_ref, g_ref, w_ref, mk_ref, mv_ref):
    x = x_ref[...]
    ms = jnp.mean(x * x, axis=-1, keepdims=True)
    h = ((x * lax.rsqrt(ms + RMS_EPS)) * g_ref[...]).astype(BF16)
    acc = _dot(h, w_ref[...].astype(BF16))
    mk_ref[...] = acc[:, :MEM_WIDTH]
    mv_ref[...] = acc[:, MEM_WIDTH:]


def _memory_kv(mem2d, g_mem, w_bf16, *, tm):
    m, d = mem2d.shape
    return pl.pallas_call(
        _memkv_kernel,
        grid=(m // tm,),
        in_specs=[
            pl.BlockSpec((tm, d), lambda i: (i, 0)),
            pl.BlockSpec((1, d), lambda i: (0, 0)),
            pl.BlockSpec((d, 2 * MEM_WIDTH), lambda i: (0, 0)),
        ],
        out_specs=[pl.BlockSpec((tm, MEM_WIDTH), lambda i: (i, 0)),
                   pl.BlockSpec((tm, MEM_WIDTH), lambda i: (i, 0))],
        out_shape=[jax.ShapeDtypeStruct((m, MEM_WIDTH), F32),
                   jax.ShapeDtypeStruct((m, MEM_WIDTH), F32)],
        compiler_params=_params(1),
        name="memory_kv",
    )(mem2d, g_mem.reshape(1, d), w_bf16)


def _neg_suffix_matrix(n):
    row = lax.broadcasted_iota(jnp.int32, (2 * n, n), 0)
    col = lax.broadcasted_iota(jnp.int32, (2 * n, n), 1)
    row = jnp.where(row >= n, row - n, row)
    return jnp.where(row >= col, -1.0, 0.0).astype(BF16)


def _sb_weights(z2, carry2, negu2, mask):
    p = jnp.maximum(z2, 0.0) + jnp.log(1.0 + jnp.exp2(-jnp.abs(z2))) * LOG2E
    if mask is not None:
        p = jnp.where(mask, p, 0.0)
    p_hi = p.astype(BF16)
    p_lo = (p - p_hi.astype(F32)).astype(BF16)
    suffix = _dot(jnp.concatenate([p_hi, p_lo], axis=1), negu2)
    w = jnp.exp2(z2 + suffix + carry2)
    if mask is not None:
        w = jnp.where(mask, w, 0.0)
    return w, carry2 - jnp.sum(p, axis=-1, keepdims=True)


def _sb_prompt_kernel(q_ref, k_ref, v_ref, negu2_ref, o_ref, acc_ref, carry_ref, kpad_ref, vpad_ref, *, t):
    i = pl.program_id(2)
    n_sub = SB_TQ // SB_TK
    negu2 = negu2_ref[...]

    @pl.when(i == 0)
    def _():
        kpad_ref[0:SB_TK, :] = jnp.zeros((SB_TK, HEAD_DIM), BF16)
        vpad_ref[0:SB_TK, :] = jnp.zeros((SB_TK, HEAD_DIM), BF16)
        kpad_ref[SB_TK:SB_TK + t, :] = k_ref[...]
        vpad_ref[SB_TK:SB_TK + t, :] = v_ref[...]

    def kv_block(j):
        start = pl.multiple_of((j + 1) * SB_TK, SB_TK)
        return kpad_ref[pl.ds(start, SB_TK), :], vpad_ref[pl.ds(start, SB_TK), :]

    row = lax.broadcasted_iota(jnp.int32, (SB_TK, SB_TK), 0)
    col = lax.broadcasted_iota(jnp.int32, (SB_TK, SB_TK), 1)
    for r in range(n_sub):
        rows = slice(r * SB_TK, (r + 1) * SB_TK)
        s = i * n_sub + r
        q = q_ref[rows, :]
        kb, vb = kv_block(s)
        w, carry = _sb_weights(_nt_dot(q, kb), jnp.zeros((SB_TK, 1), F32), negu2, col < row)
        acc = _dot(w.astype(BF16), vb)
        kb, vb = kv_block(s - 1)
        prev_exists = None if r > 0 else (jnp.zeros((SB_TK, SB_TK), jnp.int32) + i) > 0
        w, carry = _sb_weights(_nt_dot(q, kb), carry, negu2, prev_exists)
        acc_ref[rows, :] = acc + _dot(w.astype(BF16), vb)
        carry_ref[rows, :] = carry

    row_q = lax.broadcasted_iota(jnp.int32, (SB_TQ, 1), 0)
    row_t = lax.broadcasted_iota(jnp.int32, (SB_TQ, SB_TK), 0)
    has_more = row_q >= (2 - n_sub * i) * SB_TK

    def any_alive(carry):
        return (jnp.max(jnp.where(has_more, carry, NEG_INF)) > SB_DEAD).astype(jnp.int32)

    def cond(state):
        j, alive = state
        return jnp.logical_and(j >= 0, alive > 0)

    def body(state):
        j, _ = state
        kb, vb = kv_block(j)
        visits = row_t >= (j - n_sub * i + 2) * SB_TK
        w, carry = _sb_weights(_nt_dot(q_ref[...], kb), carry_ref[...], negu2, visits)
        acc_ref[...] += _dot(w.astype(BF16), vb)
        carry_ref[...] = carry
        return j - 1, any_alive(carry)

    lax.while_loop(cond, body, (n_sub * i + n_sub - 3, any_alive(carry_ref[...])))
    o_ref[...] = acc_ref[...].astype(BF16)


def _sb_prompt(ya3, ykv3, negu2):
    b, t, _ = ya3.shape
    qb, kb, vb = QG_SB_Q // HEAD_DIM, KV_SB_K // HEAD_DIM, KV_SB_V // HEAD_DIM
    assert SB_TQ // SB_TK >= 3
    kern = functools.partial(_sb_prompt_kernel, t=t)
    return pl.pallas_call(
        kern,
        grid=(b, SB_HEADS, t // SB_TQ),
        in_specs=[
            pl.BlockSpec((None, SB_TQ, HEAD_DIM), lambda b, h, i: (b, i, qb + h)),
            pl.BlockSpec((None, t, HEAD_DIM), lambda b, h, i: (b, 0, kb + h)),
            pl.BlockSpec((None, t, HEAD_DIM), lambda b, h, i: (b, 0, vb + h)),
            pl.BlockSpec((2 * SB_TK, SB_TK), lambda b, h, i: (0, 0)),
        ],
        out_specs=pl.BlockSpec((None, SB_TQ, HEAD_DIM), lambda b, h, i: (b, i, h)),
        out_shape=jax.ShapeDtypeStruct((b, t, SB_WIDTH), BF16),
        scratch_shapes=[pltpu.VMEM((SB_TQ, HEAD_DIM), F32), pltpu.VMEM((SB_TQ, 1), F32),
                        pltpu.VMEM((SB_TK + t, HEAD_DIM), BF16), pltpu.VMEM((SB_TK + t, HEAD_DIM), BF16)],
        compiler_params=_params(3),
        name="sb_prompt",
    )(ya3, ykv3, ykv3, negu2)


def _sb_decode_body(q_ref, kn_ref, vn_ref, kc_hbm, vc_hbm, negu2_ref, o_ref,
                    acc_ref, carry_ref, kpad_ref, vpad_ref, kbuf_ref, vbuf_ref, sem,
                    *, n_new, n_blocks, overlap):
    b = pl.program_id(0)
    heads = SB_HEADS

    def cache_copies(j, slot):
        rows = pl.ds(pl.multiple_of((n_blocks - 1 - j) * SB_TK, SB_TK), SB_TK)
        return (pltpu.make_async_copy(kc_hbm.at[b, :, rows, :], kbuf_ref.at[slot], sem.at[0, slot]),
                pltpu.make_async_copy(vc_hbm.at[b, :, rows, :], vbuf_ref.at[slot], sem.at[1, slot]))

    def start_fetch(j, slot):
        for cp in cache_copies(j, slot):
            cp.start()

    def wait_fetch(j, slot):
        for cp in cache_copies(j, slot):
            cp.wait()

    start_fetch(0, 0)
    overlap()

    def head_cols(h):
        return slice(h * HEAD_DIM, (h + 1) * HEAD_DIM)

    def head_rows(h):
        return slice(h * n_new, (h + 1) * n_new)

    def block(k_of, v_of, negu2, mask):
        z2 = jnp.concatenate([_nt_dot(q_ref[:, head_cols(h)], k_of(h)) for h in range(heads)], axis=0)
        w, carry = _sb_weights(z2, carry_ref[...], negu2, mask)
        wb = w.astype(BF16)
        for h in range(heads):
            acc_ref[head_rows(h), :] += _dot(wb[head_rows(h), :], v_of(h))
        carry_ref[...] = carry

    def any_alive():
        return (jnp.max(carry_ref[...]) > SB_DEAD).astype(jnp.int32)

    acc_ref[...] = jnp.zeros_like(acc_ref)
    carry_ref[...] = jnp.zeros_like(carry_ref)
    kpad_ref[...] = jnp.zeros_like(kpad_ref)
    vpad_ref[...] = jnp.zeros_like(vpad_ref)
    for h in range(heads):
        kpad_ref[h, 0:n_new, :] = kn_ref[:, head_cols(h)]
        vpad_ref[h, 0:n_new, :] = vn_ref[:, head_cols(h)]
    row = lax.broadcasted_iota(jnp.int32, (n_new, NEW_PAD), 0)
    col = lax.broadcasted_iota(jnp.int32, (n_new, NEW_PAD), 1)
    mask = jnp.concatenate([(col < row).astype(jnp.int32)] * heads, axis=0) == 1
    block(lambda h: kpad_ref[h], lambda h: vpad_ref[h], _neg_suffix_matrix(NEW_PAD), mask)

    negu2 = negu2_ref[...]

    def cond(state):
        j, alive = state
        return jnp.logical_and(j < n_blocks, alive > 0)

    def body(state):
        j, _ = state
        slot = j % 2
        wait_fetch(j, slot)

        @pl.when(j + 1 < n_blocks)
        def _():
            start_fetch(j + 1, 1 - slot)

        block(lambda h: kbuf_ref[slot, h].astype(BF16), lambda h: vbuf_ref[slot, h].astype(BF16),
              negu2, None)
        return j + 1, any_alive()

    j_end, _ = lax.while_loop(cond, body, (0, any_alive()))

    @pl.when(j_end < n_blocks)
    def _():
        wait_fetch(j_end, j_end % 2)

    for h in range(heads):
        o_ref[:, head_cols(h)] = acc_ref[head_rows(h), :].astype(BF16)


def _softmax2_pv(parts):
    mx = functools.reduce(jnp.maximum, [jnp.max(s, axis=-1, keepdims=True) for s, _ in parts])
    num = None
    den = None
    for s, v in parts:
        p = jnp.exp2(s - mx)
        d = jnp.sum(p, axis=-1, keepdims=True)
        o = _dot(p.astype(BF16), v)
        num = o if num is None else num + o
        den = d if den is None else den + d
    return num / den


def _band_bias_kernel(g_ref, tp_ref, td_ref, *, n_new, r_band):
    x = jnp.broadcast_to(g_ref[...], (BAND_TQ, BIAS_LANES))
    x = pltpu.roll(x, BAND_TQ, 1, stride=1, stride_axis=0)
    tbl = x[:, :BAND_WIN] * LOG2E
    r = lax.broadcasted_iota(jnp.int32, (BAND_TQ, BAND_WIN), 0)
    j = lax.broadcasted_iota(jnp.int32, (BAND_TQ, BAND_WIN), 1)
    dc = (j >> CHUNK_SHIFT) - (r >> CHUNK_SHIFT)
    tp_ref[...] = jnp.where(jnp.logical_and(dc >= 0, dc <= BAND_LEFT_CHUNKS), tbl, NEG_INF)
    jd = lax.broadcasted_iota(jnp.int32, (n_new, r_band + NEW_PAD), 1)
    td_ref[...] = jnp.where(jd < r_band + n_new, tbl[:n_new, :r_band + NEW_PAD], NEG_INF)


def _band_bias_tables(rel_bias, n_new, r_band):
    h = rel_bias.shape[0]
    assert BAND_ROWS == 2 * MAX_REL and r_band == BAND_ROWS and BIAS_LANES == 2 * BAND_ROWS
    rb = rel_bias.astype(F32)
    g = jnp.concatenate([rb[:, :0:-1], jnp.broadcast_to(rb[:, -1:], (h, BIAS_LANES - 2 * MAX_REL))], axis=1)
    kern = functools.partial(_band_bias_kernel, n_new=n_new, r_band=r_band)
    return pl.pallas_call(
        kern,
        grid=(h,),
        in_specs=[pl.BlockSpec((None, 1, BIAS_LANES), lambda i: (i, 0, 0))],
        out_specs=[pl.BlockSpec((None, BAND_TQ, BAND_WIN), lambda i: (i, 0, 0)),
                   pl.BlockSpec((None, n_new, r_band + NEW_PAD), lambda i: (i, 0, 0))],
        out_shape=[jax.ShapeDtypeStruct((h, BAND_TQ, BAND_WIN), F32),
                   jax.ShapeDtypeStruct((h, n_new, r_band + NEW_PAD), F32)],
        compiler_params=_params(1),
        name="band_bias",
    )(g.reshape(h, 1, BIAS_LANES))


def _band_prompt_kernel(q_ref, k_ref, v_ref, bias_ref, o_ref, kpad_ref, vpad_ref, *, t):
    s_idx = pl.program_id(2)

    @pl.when(s_idx == 0)
    def _():
        kpad_ref[0:BAND_ROWS, :] = jnp.zeros((BAND_ROWS, HEAD_DIM), BF16)
        vpad_ref[0:BAND_ROWS, :] = jnp.zeros((BAND_ROWS, HEAD_DIM), BF16)
        kpad_ref[BAND_ROWS:BAND_ROWS + t, :] = k_ref[...]
        vpad_ref[BAND_ROWS:BAND_ROWS + t, :] = v_ref[...]

    col = lax.broadcasted_iota(jnp.int32, (BAND_TQ, BAND_WIN), 1)
    for gg in range(BAND_STEP_GROUPS):
        g = s_idx * BAND_STEP_GROUPS + gg
        start = pl.multiple_of(g * BAND_TQ, BAND_TQ)
        rows = slice(gg * BAND_TQ, (gg + 1) * BAND_TQ)
        s = _nt_dot(q_ref[rows, :], kpad_ref[pl.ds(start, BAND_WIN), :]) + bias_ref[...]
        s = jnp.where(col + g * BAND_TQ >= BAND_ROWS, s, NEG_INF)
        o_ref[rows, :] = _softmax2_pv([(s, vpad_ref[pl.ds(start, BAND_WIN), :])]).astype(BF16)


def _band_prompt(ya3, ykv3, bias_tbl):
    b, t, _ = ya3.shape
    tq = BAND_TQ * BAND_STEP_GROUPS
    qb, kb, vb = QG_BD_Q // HEAD_DIM, KV_BD_K // HEAD_DIM, KV_BD_V // HEAD_DIM
    kern = functools.partial(_band_prompt_kernel, t=t)
    return pl.pallas_call(
        kern,
        grid=(b, BAND_HEADS, t // tq),
        in_specs=[
            pl.BlockSpec((None, tq, HEAD_DIM), lambda b, h, g: (b, g, qb + h)),
            pl.BlockSpec((None, t, HEAD_DIM), lambda b, h, g: (b, 0, kb + h)),
            pl.BlockSpec((None, t, HEAD_DIM), lambda b, h, g: (b, 0, vb + h)),
            pl.BlockSpec((None, BAND_TQ, BAND_WIN), lambda b, h, g: (h, 0, 0)),
        ],
        out_specs=pl.BlockSpec((None, tq, HEAD_DIM), lambda b, h, g: (b, g, h)),
        out_shape=jax.ShapeDtypeStruct((b, t, BAND_WIDTH), BF16),
        scratch_shapes=[pltpu.VMEM((BAND_ROWS + t, HEAD_DIM), BF16),
                        pltpu.VMEM((BAND_ROWS + t, HEAD_DIM), BF16)],
        compiler_params=_params(3),
        name="band_prompt",
    )(ya3, ykv3, ykv3, bias_tbl)


def _band_decode_kernel(q_ref, kn_ref, vn_ref, kc_ref, vc_ref, bias_ref, o_ref, kpad_ref, vpad_ref,
                        *, n_new, r_band):
    kpad_ref[...] = jnp.zeros_like(kpad_ref)
    vpad_ref[...] = jnp.zeros_like(vpad_ref)
    for h in range(BAND_HEADS):
        cols = slice(h * HEAD_DIM, (h + 1) * HEAD_DIM)
        kpad_ref[h, 0:n_new, :] = kn_ref[:, cols]
        vpad_ref[h, 0:n_new, :] = vn_ref[:, cols]
    for h in range(BAND_HEADS):
        cols = slice(h * HEAD_DIM, (h + 1) * HEAD_DIM)
        q = q_ref[:, cols]
        s_cache = _nt_dot(q, kc_ref[h].astype(BF16)) + bias_ref[h, :, 0:r_band]
        s_new = _nt_dot(q, kpad_ref[h]) + bias_ref[h, :, r_band:r_band + NEW_PAD]
        o_ref[:, cols] = _softmax2_pv([(s_cache, vc_ref[h].astype(BF16)),
                                       (s_new, vpad_ref[h])]).astype(BF16)


def _mem_attn_kernel(q_ref, mk_ref, mv_ref, o_ref):
    per_head = len(mk_ref.shape) == 3
    for h in range(MEM_HEADS):
        sl = slice(h * HEAD_DIM, (h + 1) * HEAD_DIM)
        mk = mk_ref[:, h, :] if per_head else mk_ref[:, sl]
        mv = mv_ref[:, h, :] if per_head else mv_ref[:, sl]
        s = _nt_dot(q_ref[:, sl], mk.astype(BF16))
        o_ref[:, sl] = _softmax2_pv([(s, mv.astype(BF16))]).astype(BF16)


def _mem_attention(y3, mk, mv, *, tq):
    b, t, _ = y3.shape
    n_mem = mk.shape[1]
    qb = MG_MM_Q // MEM_WIDTH
    return pl.pallas_call(
        _mem_attn_kernel,
        grid=(b, t // tq),
        in_specs=[
            pl.BlockSpec((None, tq, MEM_WIDTH), lambda b, i: (b, i, qb)),
            pl.BlockSpec((None, n_mem, MEM_WIDTH), lambda b, i: (b, 0, 0)),
            pl.BlockSpec((None, n_mem, MEM_WIDTH), lambda b, i: (b, 0, 0)),
        ],
        out_specs=pl.BlockSpec((None, tq, MEM_WIDTH), lambda b, i: (b, i, 0)),
        out_shape=jax.ShapeDtypeStruct((b, t, MEM_WIDTH), BF16),
        compiler_params=_params(2),
        name="mem_attention",
    )(y3, mk, mv)


def _decode_attn_kernel(qs_ref, kns_ref, vns_ref, kcs_hbm, vcs_hbm, negu2_ref,
                        qb_ref, knb_ref, vnb_ref, kcb_ref, vcb_ref, bias_ref, qm_ref, mk_ref, mv_ref,
                        osb_ref, obd_ref, omm_ref,
                        acc_ref, carry_ref, kpad_ref, vpad_ref, kbuf_ref, vbuf_ref, sem, kpadb_ref, vpadb_ref,
                        *, n_new, n_blocks, r_band):
    def band_and_memory():
        _band_decode_kernel(qb_ref, knb_ref, vnb_ref, kcb_ref, vcb_ref, bias_ref, obd_ref,
                            kpadb_ref, vpadb_ref, n_new=n_new, r_band=r_band)
        _mem_attn_kernel(qm_ref, mk_ref, mv_ref, omm_ref)

    _sb_decode_body(qs_ref, kns_ref, vns_ref, kcs_hbm, vcs_hbm, negu2_ref, osb_ref,
                    acc_ref, carry_ref, kpad_ref, vpad_ref, kbuf_ref, vbuf_ref, sem,
                    n_new=n_new, n_blocks=n_blocks, overlap=band_and_memory)


def _decode_attention(yqg3, ykv3, ymg3, cache_sb_k, cache_sb_v, cache_bd_k, cache_bd_v, mk, mv, bias_tbl, negu2):
    bd, n_new, _ = yqg3.shape
    past = cache_sb_k.shape[2]
    r_band = cache_bd_k.shape[2]
    n_mem = mk.shape[1]
    assert past % SB_TK == 0 and n_new <= NEW_PAD and n_new % 16 == 0
    kern = functools.partial(_decode_attn_kernel, n_new=n_new, n_blocks=past // SB_TK, r_band=r_band)
    slab = lambda width, col0: pl.BlockSpec((None, n_new, width), functools.partial(lambda b, c: (b, 0, c), c=col0 // width))
    band_cache = pl.BlockSpec((None, BAND_HEADS, r_band, HEAD_DIM), lambda b: (b, 0, 0, 0))
    mem_cache = pl.BlockSpec((None, n_mem, MEM_HEADS, HEAD_DIM), lambda b: (b, 0, 0, 0))
    return pl.pallas_call(
        kern,
        grid=(bd,),
        in_specs=[
            slab(SB_WIDTH, QG_SB_Q), slab(SB_WIDTH, KV_SB_K), slab(SB_WIDTH, KV_SB_V),
            pl.BlockSpec(memory_space=pl.ANY), pl.BlockSpec(memory_space=pl.ANY),
            pl.BlockSpec((2 * SB_TK, SB_TK), lambda b: (0, 0)),
            slab(BAND_WIDTH, QG_BD_Q), slab(BAND_WIDTH, KV_BD_K), slab(BAND_WIDTH, KV_BD_V),
            band_cache, band_cache,
            pl.BlockSpec((BAND_HEADS, n_new, r_band + NEW_PAD), lambda b: (0, 0, 0)),
            slab(MEM_WIDTH, MG_MM_Q), mem_cache, mem_cache,
        ],
        out_specs=[pl.BlockSpec((None, n_new, SB_WIDTH), lambda b: (b, 0, 0)),
                   pl.BlockSpec((None, n_new, BAND_WIDTH), lambda b: (b, 0, 0)),
                   pl.BlockSpec((None, n_new, MEM_WIDTH), lambda b: (b, 0, 0))],
        out_shape=[jax.ShapeDtypeStruct((bd, n_new, SB_WIDTH), BF16),
                   jax.ShapeDtypeStruct((bd, n_new, BAND_WIDTH), BF16),
                   jax.ShapeDtypeStruct((bd, n_new, MEM_WIDTH), BF16)],
        scratch_shapes=[pltpu.VMEM((SB_HEADS * n_new, HEAD_DIM), F32),
                        pltpu.VMEM((SB_HEADS * n_new, 1), F32),
                        pltpu.VMEM((SB_HEADS, NEW_PAD, HEAD_DIM), BF16),
                        pltpu.VMEM((SB_HEADS, NEW_PAD, HEAD_DIM), BF16),
                        pltpu.VMEM((2, SB_HEADS, SB_TK, HEAD_DIM), F32),
                        pltpu.VMEM((2, SB_HEADS, SB_TK, HEAD_DIM), F32),
                        pltpu.SemaphoreType.DMA((2, 2)),
                        pltpu.VMEM((BAND_HEADS, NEW_PAD, HEAD_DIM), BF16),
                        pltpu.VMEM((BAND_HEADS, NEW_PAD, HEAD_DIM), BF16)],
        compiler_params=_params(1),
        name="decode_attention",
    )(yqg3, ykv3, ykv3, cache_sb_k, cache_sb_v, negu2, yqg3, ykv3, ykv3, cache_bd_k, cache_bd_v, bias_tbl,
      ymg3, mk, mv)


def _silu_of_half(h):
    return h + h * jnp.tanh(h)


def _merge_kernel(osb_ref, obd_ref, omm_ref, gsb_ref, gbd_ref, gmm_ref,
                  mg0_ref, mg1_ref, mg2_ref, mg3_ref, mg4_ref, mg5_ref,
                  wsb_ref, wbd_ref, wmm_ref, merged_ref, *, half):
    u_sb = (osb_ref[...].astype(F32) * _silu_of_half(gsb_ref[...].astype(F32))).astype(BF16)
    u_bd = (obd_ref[...].astype(F32) * _silu_of_half(gbd_ref[...].astype(F32))).astype(BF16)
    u_mm = (omm_ref[...].astype(F32) * _silu_of_half(gmm_ref[...].astype(F32))).astype(BF16)
    mg = ((mg0_ref, mg2_ref, mg4_ref), (mg1_ref, mg3_ref, mg5_ref))
    for n in range(2):
        cols = slice(n * half, (n + 1) * half)
        merged = None
        for m_ref, u, w_ref in zip(mg[n], (u_sb, u_bd, u_mm), (wsb_ref, wbd_ref, wmm_ref)):
            a = _dot(u, w_ref[:, cols])
            term = a + a * jnp.tanh(m_ref[...].astype(F32))
            merged = term if merged is None else merged + term
        merged_ref[:, cols] = merged.astype(BF16)


def _merge_branches(yqg, ymg, o_sb, o_bd, o_mm, w_sb, w_bd, w_mm, *, tm):
    m = yqg.shape[0]
    d = w_sb.shape[1]
    half = d // 2
    assert MG_MG % half == 0
    mgb = MG_MG // half
    const = dict(pipeline_mode=pl.Buffered(1))
    kern = functools.partial(_merge_kernel, half=half)
    return pl.pallas_call(
        kern,
        grid=(m // tm,),
        in_specs=[
            pl.BlockSpec((tm, SB_WIDTH), lambda i: (i, 0)),
            pl.BlockSpec((tm, BAND_WIDTH), lambda i: (i, 0)),
            pl.BlockSpec((tm, MEM_WIDTH), lambda i: (i, 0)),
            pl.BlockSpec((tm, SB_WIDTH), lambda i: (i, QG_SB_G // SB_WIDTH)),
            pl.BlockSpec((tm, BAND_WIDTH), lambda i: (i, QG_BD_G // BAND_WIDTH)),
            pl.BlockSpec((tm, MEM_WIDTH), lambda i: (i, MG_MM_G // MEM_WIDTH)),
        ] + [pl.BlockSpec((tm, half), functools.partial(lambda i, c: (i, c), c=mgb + c)) for c in range(6)] + [
            pl.BlockSpec((SB_WIDTH, d), lambda i: (0, 0), **const),
            pl.BlockSpec((BAND_WIDTH, d), lambda i: (0, 0), **const),
            pl.BlockSpec((MEM_WIDTH, d), lambda i: (0, 0), **const),
        ],
        out_specs=pl.BlockSpec((tm, d), lambda i: (i, 0)),
        out_shape=jax.ShapeDtypeStruct((m, d), BF16),
        compiler_params=_params(1),
        name="merge_branches",
    )(o_sb, o_bd, o_mm, yqg, yqg, ymg, *([ymg] * 6), w_sb, w_bd, w_mm)


def _out_proj_kernel(x_ref, merged_ref, wout_ref, gpost_ref, y_ref):
    y = _dot(merged_ref[...], wout_ref[...])
    ms = jnp.mean(y * y, axis=-1, keepdims=True)
    y_ref[...] = x_ref[...] + (y * lax.rsqrt(ms + RMS_EPS)) * gpost_ref[...]


def _merge_out_kernel(x_ref, osb_ref, obd_ref, omm_ref, gsb_ref, gbd_ref, gmm_ref,
                      mg0_ref, mg1_ref, mg2_ref, mg3_ref, mg4_ref, mg5_ref,
                      wsb_ref, wbd_ref, wmm_ref, wout_ref, gpost_ref, y_ref, merged_ref, *, half):
    _merge_kernel(osb_ref, obd_ref, omm_ref, gsb_ref, gbd_ref, gmm_ref,
                  mg0_ref, mg1_ref, mg2_ref, mg3_ref, mg4_ref, mg5_ref,
                  wsb_ref, wbd_ref, wmm_ref, merged_ref, half=half)
    _out_proj_kernel(x_ref, merged_ref, wout_ref, gpost_ref, y_ref)


def _merge_out(x2d, yqg, ymg, o_sb, o_bd, o_mm, w_sb, w_bd, w_mm, w_out, g_post):
    m, d = x2d.shape
    half = d // 2
    mgb = MG_MG // half
    whole = lambda shape: pl.BlockSpec(shape, lambda i: (0,) * len(shape))
    col = lambda width, c: pl.BlockSpec((m, width), functools.partial(lambda i, c: (0, c), c=c))
    kern = functools.partial(_merge_out_kernel, half=half)
    return pl.pallas_call(
        kern,
        grid=(1,),
        in_specs=[
            whole((m, d)), whole((m, SB_WIDTH)), whole((m, BAND_WIDTH)), whole((m, MEM_WIDTH)),
            col(SB_WIDTH, QG_SB_G // SB_WIDTH), col(BAND_WIDTH, QG_BD_G // BAND_WIDTH),
            col(MEM_WIDTH, MG_MM_G // MEM_WIDTH),
        ] + [col(half, mgb + c) for c in range(6)] + [
            whole((SB_WIDTH, d)), whole((BAND_WIDTH, d)), whole((MEM_WIDTH, d)), whole((d, d)), whole((1, d)),
        ],
        out_specs=whole((m, d)),
        out_shape=jax.ShapeDtypeStruct((m, d), F32),
        scratch_shapes=[pltpu.VMEM((m, d), BF16)],
        compiler_params=_params(1),
        name="merge_out",
    )(x2d, o_sb, o_bd, o_mm, yqg, yqg, ymg, *([ymg] * 6), w_sb, w_bd, w_mm, w_out, g_post.reshape(1, d))


def _out_projection(x2d, merged, w_out, g_post, *, tm):
    m, d = x2d.shape
    return pl.pallas_call(
        _out_proj_kernel,
        grid=(m // tm,),
        in_specs=[
            pl.BlockSpec((tm, d), lambda i: (i, 0)),
            pl.BlockSpec((tm, d), lambda i: (i, 0)),
            pl.BlockSpec((d, d), lambda i: (0, 0), pipeline_mode=pl.Buffered(1)),
            pl.BlockSpec((1, d), lambda i: (0, 0)),
        ],
        out_specs=pl.BlockSpec((tm, d), lambda i: (i, 0)),
        out_shape=jax.ShapeDtypeStruct((m, d), F32),
        compiler_params=_params(1),
        name="out_projection",
    )(x2d, merged, w_out, g_post.reshape(1, d))


def _head_major(a):
    return jnp.transpose(a, (0, 2, 1, 3))


def kernel(x_prompt, x_sample, cache_sb_k, cache_sb_v, cache_band_k, cache_band_v, cache_mem_k, cache_mem_v, mem_prompt, g_pre, w_in, rel_bias, g_mem, w_mem_kv, w_up_sb, w_up_band, w_up_mem, w_out, g_post):
    depth = w_in.shape[0]
    b, t, d = x_prompt.shape
    bd, n_new, _ = x_sample.shape
    n_mem = mem_prompt.shape[1]
    r_band = cache_band_k.shape[2]
    in_width = w_in.shape[2]
    band_keep = min(BAND_ROWS, t)
    assert COL_MG + 3 * d == in_width
    assert t % SB_TQ == 0 and t % (BAND_TQ * BAND_STEP_GROUPS) == 0 and t % band_keep == 0
    assert r_band == BAND_ROWS and n_new <= CHUNK

    negu2 = jnp.where(jnp.arange(2 * SB_TK)[:, None] % SB_TK >= jnp.arange(SB_TK)[None, :], -1.0, 0.0).astype(BF16)
    mg_width = in_width - COL_MM_Q
    assert COL_MM_Q % MG_TN == 0 and mg_width % MG_TN == 0 and COL_SB_G == 3 * KV_TN and COL_BD_Q == 4 * KV_TN
    cols = jnp.arange(QG_WIDTH)
    qg_scale = jnp.where((cols // SB_WIDTH) % 2 == 0, Q_SCALE, 0.5).astype(F32).reshape(1, QG_WIDTH)
    cols = jnp.arange(mg_width)
    mg_scale = jnp.where(cols < MG_MM_G, Q_SCALE, 0.5).astype(F32).reshape(1, mg_width)
    qg_block = lambda j: j + 2 * ((j + 1) // 2)
    mg_block = lambda j: j + COL_MM_Q // MG_TN

    xp = x_prompt.reshape(b * t, d)
    xs = x_sample.reshape(bd * n_new, d)
    ms = bd * n_new
    outs = [[] for _ in range(10)]
    for l in range(depth):
        w_kvp_b = _kv_weight_bf16(w_in[l])
        w_sb_b = (0.5 * w_up_sb[l]).astype(BF16)
        w_bd_b = (0.5 * w_up_band[l]).astype(BF16)
        w_mm_b = (0.5 * w_up_mem[l]).astype(BF16)
        w_out_b = w_out[l].astype(BF16)
        bias_p, bias_d = _band_bias_tables(rel_bias[l], n_new, r_band)

        ykv, hp, sbk, sbv, bdk, bdv = _kv_projection(xp, g_pre[l], w_kvp_b, tm=band_keep, seqs=1, n_seq=b,
                                                     band_keep=band_keep)
        ykv_s, hs, sbk2, sbv2, bdk2, bdv2 = _kv_projection(xs, g_pre[l], w_kvp_b, tm=ms, seqs=bd, n_seq=bd,
                                                           band_keep=n_new)
        yqg, yqg_s = _col_projection(hp, hs, w_in[l], qg_scale, qg_block, tm=PROJ_TM, tn=KV_TN,
                                     name="qg_projection")
        ymg, ymg_s = _col_projection(hp, hs, w_in[l], mg_scale, mg_block, tm=PROJ_TM, tn=MG_TN,
                                     name="mg_projection")

        mk, mv = _memory_kv(mem_prompt.reshape(b * n_mem, d), g_mem[l], w_mem_kv[l], tm=n_mem)
        yqg3 = yqg.reshape(b, t, QG_WIDTH)
        ykv3 = ykv.reshape(b, t, KV_WIDTH)
        o_sb = _sb_prompt(yqg3, ykv3, negu2)
        o_bd = _band_prompt(yqg3, ykv3, bias_p)
        o_mm = _mem_attention(ymg.reshape(b, t, mg_width), mk.reshape(b, n_mem, MEM_WIDTH),
                              mv.reshape(b, n_mem, MEM_WIDTH), tq=1024)
        merged = _merge_branches(yqg, ymg, o_sb.reshape(b * t, -1), o_bd.reshape(b * t, -1),
                                 o_mm.reshape(b * t, -1), w_sb_b, w_bd_b, w_mm_b, tm=512)
        xp = _out_projection(xp, merged, w_out_b, g_post[l], tm=512)
        outs[0].append(_head_major(sbk))
        outs[1].append(_head_major(sbv))
        outs[2].append(_head_major(bdk))
        outs[3].append(_head_major(bdv))
        outs[4].append(mk.reshape(b, n_mem, MEM_HEADS, HEAD_DIM))
        outs[5].append(mv.reshape(b, n_mem, MEM_HEADS, HEAD_DIM))

        o_sb2, o_bd2, o_mm2 = _decode_attention(
            yqg_s.reshape(bd, n_new, QG_WIDTH), ykv_s.reshape(bd, n_new, KV_WIDTH),
            ymg_s.reshape(bd, n_new, mg_width),
            _head_major(cache_sb_k[l]), _head_major(cache_sb_v[l]),
            _head_major(cache_band_k[l]), _head_major(cache_band_v[l]),
            cache_mem_k[l], cache_mem_v[l],
            bias_d, negu2)
        xs = _merge_out(xs, yqg_s, ymg_s, o_sb2.reshape(ms, -1), o_bd2.reshape(ms, -1), o_mm2.reshape(ms, -1),
                        w_sb_b, w_bd_b, w_mm_b, w_out_b, g_post[l])
        outs[6].append(_head_major(sbk2))
        outs[7].append(_head_major(sbv2))
        outs[8].append(_head_major(bdk2))
        outs[9].append(_head_major(bdv2))

    return (xp.reshape(b, t, d), xs.reshape(bd, n_new, d)) + tuple(jnp.stack(o) for o in outs)
```

```python
import functools
import math

import jax
import jax.numpy as jnp
from jax import lax
from jax.experimental import pallas as pl
from jax.experimental.pallas import tpu as pltpu

F32 = jnp.float32
BF16 = jnp.bfloat16

HEAD_DIM = 128
SB_HEADS = 6
BAND_HEADS = 6
MEM_HEADS = 4
SB_WIDTH = SB_HEADS * HEAD_DIM
BAND_WIDTH = BAND_HEADS * HEAD_DIM
MEM_WIDTH = MEM_HEADS * HEAD_DIM
CHUNK = 64
CHUNK_SHIFT = 6
BAND_LEFT_CHUNKS = 8
BAND_ROWS = BAND_LEFT_CHUNKS * CHUNK
MAX_REL = 256
RMS_EPS = 1e-6
NEG_INF = -1e30
LOG2E = math.log2(math.e)
Q_SCALE = HEAD_DIM ** -0.5 * LOG2E

COL_SB_Q = 0
COL_SB_K = COL_SB_Q + SB_WIDTH
COL_SB_V = COL_SB_K + SB_WIDTH
COL_SB_G = COL_SB_V + SB_WIDTH
COL_BD_Q = COL_SB_G + SB_WIDTH
COL_BD_K = COL_BD_Q + BAND_WIDTH
COL_BD_V = COL_BD_K + BAND_WIDTH
COL_BD_G = COL_BD_V + BAND_WIDTH
COL_MM_Q = COL_BD_G + BAND_WIDTH
COL_MM_G = COL_MM_Q + MEM_WIDTH
COL_MG = COL_MM_G + MEM_WIDTH

KV_SB_K = 0
KV_SB_V = KV_SB_K + SB_WIDTH
KV_BD_K = KV_SB_V + SB_WIDTH
KV_BD_V = KV_BD_K + BAND_WIDTH
KV_WIDTH = KV_BD_V + BAND_WIDTH
QG_SB_Q = 0
QG_SB_G = QG_SB_Q + SB_WIDTH
QG_BD_Q = QG_SB_G + SB_WIDTH
QG_BD_G = QG_BD_Q + BAND_WIDTH
QG_WIDTH = QG_BD_G + BAND_WIDTH
MG_MM_Q = 0
MG_MM_G = MG_MM_Q + MEM_WIDTH
MG_MG = MG_MM_G + MEM_WIDTH

VMEM_LIMIT_BYTES = 56 * 1024 * 1024
MAX_VMEM_LIMIT_BYTES = 58 * 1024 * 1024
COMPILER_TEMP_BYTES = 2 * 1024 * 1024
PROJ_TM = 2048
OUT_TM = 512
MEM_TQ = 1024
CAST_ROWS = 256
KV_TN = SB_WIDTH
MG_TN = 1024
SB_TK = 256
SB_TQ = 8 * SB_TK
SB_DEAD = -160.0
BAND_TQ = 4 * CHUNK
BAND_WIN = BAND_TQ + BAND_ROWS
BAND_STEP_GROUPS = 8
BIAS_LANES = 1024
NEW_PAD = 128


def _params(n_axes, vmem=VMEM_LIMIT_BYTES):
    return pltpu.CompilerParams(dimension_semantics=("arbitrary",) * n_axes,
                                vmem_limit_bytes=vmem)


def _nt_dot(a, b):
    return lax.dot_general(a, b, (((1,), (1,)), ((), ())), preferred_element_type=F32)


def _dot(a, b):
    return jnp.dot(a, b, preferred_element_type=F32)


def _pre_norm_to(h_ref, x_ref, g_ref):
    x = x_ref[...]
    ms = jnp.mean(x * x, axis=-1, keepdims=True)
    h_ref[...] = ((x * lax.rsqrt(ms + RMS_EPS)) * g_ref[...]).astype(BF16)


def _kv_proj_kernel(x_ref, g_ref, wsk_ref, wsv_ref, wbk_ref, wbv_ref,
                    y_ref, h_ref, sbk_ref, sbv_ref, bdk_ref, bdv_ref, *, seqs, rows):
    _pre_norm_to(h_ref, x_ref, g_ref)
    groups = ((wsk_ref, sbk_ref), (wsv_ref, sbv_ref), (wbk_ref, bdk_ref), (wbv_ref, bdv_ref))
    for group, (w_ref, dst_ref) in enumerate(groups):
        acc = _dot(h_ref[...], w_ref[...])
        y_ref[:, group * KV_TN:(group + 1) * KV_TN] = acc.astype(BF16)
        for h in range(SB_HEADS):
            for s in range(seqs):
                dst_ref[s, h] = acc[s * rows:(s + 1) * rows, h * HEAD_DIM:(h + 1) * HEAD_DIM]


def _kv_projection(x2d, g_pre, w_kv_b, *, tm, seqs, n_seq, band_keep):
    m, d = x2d.shape
    assert KV_WIDTH == 4 * KV_TN and SB_HEADS == BAND_HEADS
    rows = tm // seqs
    seq_rows = m // n_seq
    blocks_per_seq = seq_rows // rows
    assert rows == band_keep
    kern = functools.partial(_kv_proj_kernel, seqs=seqs, rows=rows)
    sb_spec = pl.BlockSpec((seqs, SB_HEADS, rows, HEAD_DIM),
                           lambda i: (i // blocks_per_seq, 0, i % blocks_per_seq, 0))
    bd_spec = pl.BlockSpec((seqs, BAND_HEADS, rows, HEAD_DIM), lambda i: (i // blocks_per_seq, 0, 0, 0))
    sb_shape = jax.ShapeDtypeStruct((n_seq, SB_HEADS, seq_rows, HEAD_DIM), F32)
    bd_shape = jax.ShapeDtypeStruct((n_seq, BAND_HEADS, band_keep, HEAD_DIM), F32)

    def w_spec(group):
        return pl.BlockSpec((d, KV_TN), functools.partial(lambda i, c: (0, c), c=group),
                            pipeline_mode=pl.Buffered(1))

    return pl.pallas_call(
        kern,
        grid=(m // tm,),
        in_specs=[
            pl.BlockSpec((tm, d), lambda i: (i, 0)),
            pl.BlockSpec((1, d), lambda i: (0, 0)),
            w_spec(0), w_spec(1), w_spec(2), w_spec(3),
        ],
        out_specs=[pl.BlockSpec((tm, KV_WIDTH), lambda i: (i, 0)), pl.BlockSpec((tm, d), lambda i: (i, 0)),
                   sb_spec, sb_spec, bd_spec, bd_spec],
        out_shape=[jax.ShapeDtypeStruct((m, KV_WIDTH), BF16), jax.ShapeDtypeStruct((m, d), BF16),
                   sb_shape, sb_shape, bd_shape, bd_shape],
        compiler_params=_params(1),
        name="kv_projection",
    )(x2d, g_pre.reshape(1, d), w_kv_b, w_kv_b, w_kv_b, w_kv_b)


def _col_proj_kernel(h_ref, hs_ref, w_ref, cs_ref, y_ref, ys_ref, wb_ref):
    @pl.when(pl.program_id(1) == 0)
    def _():
        wb_ref[...] = w_ref[...].astype(BF16)
        ys_ref[...] = (_dot(hs_ref[...], wb_ref[...]) * cs_ref[...]).astype(BF16)

    y_ref[...] = (_dot(h_ref[...], wb_ref[...]) * cs_ref[...]).astype(BF16)


def _col_projection(h2d, hs2d, w_in, col_scale, src_block, *, tm, tn, name):
    m, d = h2d.shape
    ms = hs2d.shape[0]
    n = col_scale.shape[1]
    assert n % tn == 0 and m % tm == 0
    vmem = (2 * tm * d * 2 + 2 * d * tn * 4 + 2 * tm * tn * 2 + d * tn * 2 + 2 * tm * tn * 4
            + 2 * ms * (d + tn) * 2 + COMPILER_TEMP_BYTES)
    vmem = min(vmem, MAX_VMEM_LIMIT_BYTES)
    return pl.pallas_call(
        _col_proj_kernel,
        grid=(n // tn, m // tm),
        in_specs=[
            pl.BlockSpec((tm, d), lambda j, i: (i, 0)),
            pl.BlockSpec((ms, d), lambda j, i: (0, 0)),
            pl.BlockSpec((d, tn), lambda j, i: (0, src_block(j))),
            pl.BlockSpec((1, tn), lambda j, i: (0, j)),
        ],
        out_specs=[pl.BlockSpec((tm, tn), lambda j, i: (i, j)), pl.BlockSpec((ms, tn), lambda j, i: (0, j))],
        out_shape=[jax.ShapeDtypeStruct((m, n), BF16), jax.ShapeDtypeStruct((ms, n), BF16)],
        scratch_shapes=[pltpu.VMEM((d, tn), BF16)],
        compiler_params=_params(2, vmem=vmem),
        name=name,
    )(h2d, hs2d, w_in, col_scale)


def _kv_weight_kernel(a_ref, b_ref, c_ref, d_ref, o_ref):
    for group, w_ref in enumerate((a_ref, b_ref, c_ref, d_ref)):
        o_ref[:, group * KV_TN:(group + 1) * KV_TN] = w_ref[...].astype(BF16)


def _kv_weight_bf16(w):
    d = w.shape[0]
    tr = CAST_ROWS
    assert d % tr == 0

    def spec(col0):
        assert col0 % KV_TN == 0
        return pl.BlockSpec((tr, KV_TN), functools.partial(lambda i, c: (i, c), c=col0 // KV_TN))

    return pl.pallas_call(
        _kv_weight_kernel,
        grid=(d // tr,),
        in_specs=[spec(COL_SB_K), spec(COL_SB_V), spec(COL_BD_K), spec(COL_BD_V)],
        out_specs=pl.BlockSpec((tr, KV_WIDTH), lambda i: (i, 0)),
        out_shape=jax.ShapeDtypeStruct((d, KV_WIDTH), BF16),
        compiler_params=_params(1),
        name="kv_weight_cast",
    )(w, w, w, w)


def _memkv_kernel(x_ref, g_ref, w_ref, mk_ref, mv_ref):
    x = x_ref[...]
    ms = jnp.mean(x * x, axis=-1, keepdims=True)
    h = ((x * lax.rsqrt(ms + RMS_EPS)) * g_ref[...]).astype(BF16)
    acc = _dot(h, w_ref[...].astype(BF16))
    mk_ref[...] = acc[:, :MEM_WIDTH]
    mv_ref[...] = acc[:, MEM_WIDTH:]


def _memory_kv(mem2d, g_mem, w_bf16, *, tm):
    m, d = mem2d.shape
    return pl.pallas_call(
        _memkv_kernel,
        grid=(m // tm,),
        in_specs=[
            pl.BlockSpec((tm, d), lambda i: (i, 0)),
            pl.BlockSpec((1, d), lambda i: (0, 0)),
            pl.BlockSpec((d, 2 * MEM_WIDTH), lambda i: (0, 0)),
        ],
        out_specs=[pl.BlockSpec((tm, MEM_WIDTH), lambda i: (i, 0)),
                   pl.BlockSpec((tm, MEM_WIDTH), lambda i: (i, 0))],
        out_shape=[jax.ShapeDtypeStruct((m, MEM_WIDTH), F32),
                   jax.ShapeDtypeStruct((m, MEM_WIDTH), F32)],
        compiler_params=_params(1),
        name="memory_kv",
    )(mem2d, g_mem.reshape(1, d), w_bf16)


def _neg_suffix_matrix(n):
    row = lax.broadcasted_iota(jnp.int32, (2 * n, n), 0)
    col = lax.broadcasted_iota(jnp.int32, (2 * n, n), 1)
    row = jnp.where(row >= n, row - n, row)
    return jnp.where(row >= col, -1.0, 0.0).astype(BF16)


def _sb_weights(z2, carry2, negu2, mask):
    p = jnp.maximum(z2, 0.0) + jnp.log(1.0 + jnp.exp2(-jnp.abs(z2))) * LOG2E
    if mask is not None:
        p = jnp.where(mask, p, 0.0)
    p_hi = p.astype(BF16)
    p_lo = (p - p_hi.astype(F32)).astype(BF16)
    suffix = _dot(jnp.concatenate([p_hi, p_lo], axis=1), negu2)
    w = jnp.exp2(z2 + suffix + carry2)
    if mask is not None:
        w = jnp.where(mask, w, 0.0)
    return w, carry2 - jnp.sum(p, axis=-1, keepdims=True)


def _sb_prompt_body(q_ref, k_ref, v_ref, negu2_ref, o_ref, acc_ref, carry_ref, kpad_ref, vpad_ref,
                    *, t, at_first_step, beside_diagonal):
    i = pl.program_id(2)
    n_sub = SB_TQ // SB_TK
    negu2 = negu2_ref[...]

    @pl.when(i == 0)
    def _():
        kpad_ref[0:SB_TK, :] = jnp.zeros((SB_TK, HEAD_DIM), BF16)
        vpad_ref[0:SB_TK, :] = jnp.zeros((SB_TK, HEAD_DIM), BF16)
        kpad_ref[SB_TK:SB_TK + t, :] = k_ref[...]
        vpad_ref[SB_TK:SB_TK + t, :] = v_ref[...]
        at_first_step()

    def kv_block(j):
        start = pl.multiple_of((j + 1) * SB_TK, SB_TK)
        return kpad_ref[pl.ds(start, SB_TK), :], vpad_ref[pl.ds(start, SB_TK), :]

    row = lax.broadcasted_iota(jnp.int32, (SB_TK, SB_TK), 0)
    col = lax.broadcasted_iota(jnp.int32, (SB_TK, SB_TK), 1)
    for r in range(n_sub):
        rows = slice(r * SB_TK, (r + 1) * SB_TK)
        s = i * n_sub + r
        q = q_ref[rows, :]
        kb, vb = kv_block(s)
        w, carry = _sb_weights(_nt_dot(q, kb), jnp.zeros((SB_TK, 1), F32), negu2, col < row)
        acc = _dot(w.astype(BF16), vb)
        kb, vb = kv_block(s - 1)
        prev_exists = None if r > 0 else (jnp.zeros((SB_TK, SB_TK), jnp.int32) + i) > 0
        w, carry = _sb_weights(_nt_dot(q, kb), carry, negu2, prev_exists)
        acc_ref[rows, :] = acc + _dot(w.astype(BF16), vb)
        carry_ref[rows, :] = carry
    beside_diagonal()

    row_q = lax.broadcasted_iota(jnp.int32, (SB_TQ, 1), 0)
    row_t = lax.broadcasted_iota(jnp.int32, (SB_TQ, SB_TK), 0)
    has_more = row_q >= (2 - n_sub * i) * SB_TK

    def any_alive(carry):
        return (jnp.max(jnp.where(has_more, carry, NEG_INF)) > SB_DEAD).astype(jnp.int32)

    def cond(state):
        j, alive = state
        return jnp.logical_and(j >= 0, alive > 0)

    def body(state):
        j, _ = state
        kb, vb = kv_block(j)
        visits = row_t >= (j - n_sub * i + 2) * SB_TK
        w, carry = _sb_weights(_nt_dot(q_ref[...], kb), carry_ref[...], negu2, visits)
        acc_ref[...] += _dot(w.astype(BF16), vb)
        carry_ref[...] = carry
        return j - 1, any_alive(carry)

    lax.while_loop(cond, body, (n_sub * i + n_sub - 3, any_alive(carry_ref[...])))
    o_ref[...] = acc_ref[...].astype(BF16)


def _sb_decode_body(q_ref, kn_ref, vn_ref, kc_hbm, vc_hbm, negu2_ref, o_ref,
                    acc_ref, carry_ref, kpad_ref, vpad_ref, kbuf_ref, vbuf_ref, sem,
                    *, n_new, n_blocks, overlap):
    b = pl.program_id(0)
    heads = SB_HEADS

    def cache_copies(j, slot):
        rows = pl.ds(pl.multiple_of((n_blocks - 1 - j) * SB_TK, SB_TK), SB_TK)
        return (pltpu.make_async_copy(kc_hbm.at[b, :, rows, :], kbuf_ref.at[slot], sem.at[0, slot]),
                pltpu.make_async_copy(vc_hbm.at[b, :, rows, :], vbuf_ref.at[slot], sem.at[1, slot]))

    def start_fetch(j, slot):
        for cp in cache_copies(j, slot):
            cp.start()

    def wait_fetch(j, slot):
        for cp in cache_copies(j, slot):
            cp.wait()

    start_fetch(0, 0)
    overlap()

    def head_cols(h):
        return slice(h * HEAD_DIM, (h + 1) * HEAD_DIM)

    def head_rows(h):
        return slice(h * n_new, (h + 1) * n_new)

    def block(k_of, v_of, negu2, mask):
        z2 = jnp.concatenate([_nt_dot(q_ref[:, head_cols(h)], k_of(h)) for h in range(heads)], axis=0)
        w, carry = _sb_weights(z2, carry_ref[...], negu2, mask)
        wb = w.astype(BF16)
        for h in range(heads):
            acc_ref[head_rows(h), :] += _dot(wb[head_rows(h), :], v_of(h))
        carry_ref[...] = carry

    def any_alive():
        return (jnp.max(carry_ref[...]) > SB_DEAD).astype(jnp.int32)

    acc_ref[...] = jnp.zeros_like(acc_ref)
    carry_ref[...] = jnp.zeros_like(carry_ref)
    kpad_ref[...] = jnp.zeros_like(kpad_ref)
    vpad_ref[...] = jnp.zeros_like(vpad_ref)
    for h in range(heads):
        kpad_ref[h, 0:n_new, :] = kn_ref[:, head_cols(h)]
        vpad_ref[h, 0:n_new, :] = vn_ref[:, head_cols(h)]
    row = lax.broadcasted_iota(jnp.int32, (n_new, NEW_PAD), 0)
    col = lax.broadcasted_iota(jnp.int32, (n_new, NEW_PAD), 1)
    mask = jnp.concatenate([(col < row).astype(jnp.int32)] * heads, axis=0) == 1
    block(lambda h: kpad_ref[h], lambda h: vpad_ref[h], _neg_suffix_matrix(NEW_PAD), mask)

    negu2 = negu2_ref[...]

    def cond(state):
        j, alive = state
        return jnp.logical_and(j < n_blocks, alive > 0)

    def body(state):
        j, _ = state
        slot = j % 2
        wait_fetch(j, slot)

        @pl.when(j + 1 < n_blocks)
        def _():
            start_fetch(j + 1, 1 - slot)

        block(lambda h: kbuf_ref[slot, h].astype(BF16), lambda h: vbuf_ref[slot, h].astype(BF16),
              negu2, None)
        return j + 1, any_alive()

    j_end, _ = lax.while_loop(cond, body, (0, any_alive()))

    @pl.when(j_end < n_blocks)
    def _():
        wait_fetch(j_end, j_end % 2)

    for h in range(heads):
        o_ref[:, head_cols(h)] = acc_ref[head_rows(h), :].astype(BF16)


def _softmax2_pv(parts):
    mx = functools.reduce(jnp.maximum, [jnp.max(s, axis=-1, keepdims=True) for s, _ in parts])
    num = None
    den = None
    for s, v in parts:
        p = jnp.exp2(s - mx)
        d = jnp.sum(p, axis=-1, keepdims=True)
        o = _dot(p.astype(BF16), v)
        num = o if num is None else num + o
        den = d if den is None else den + d
    return num / den


def _band_bias_kernel(g_ref, tp_ref, td_ref, *, n_new, r_band):
    x = jnp.broadcast_to(g_ref[...], (BAND_TQ, BIAS_LANES))
    x = pltpu.roll(x, BAND_TQ, 1, stride=1, stride_axis=0)
    tbl = x[:, :BAND_WIN] * LOG2E
    r = lax.broadcasted_iota(jnp.int32, (BAND_TQ, BAND_WIN), 0)
    j = lax.broadcasted_iota(jnp.int32, (BAND_TQ, BAND_WIN), 1)
    dc = (j >> CHUNK_SHIFT) - (r >> CHUNK_SHIFT)
    tp_ref[...] = jnp.where(jnp.logical_and(dc >= 0, dc <= BAND_LEFT_CHUNKS), tbl, NEG_INF)
    jd = lax.broadcasted_iota(jnp.int32, (n_new, r_band + NEW_PAD), 1)
    td_ref[...] = jnp.where(jd < r_band + n_new, tbl[:n_new, :r_band + NEW_PAD], NEG_INF)


def _band_bias_tables(rel_bias, n_new, r_band):
    h = rel_bias.shape[0]
    assert BAND_ROWS == 2 * MAX_REL and r_band == BAND_ROWS and BIAS_LANES == 2 * BAND_ROWS
    rb = rel_bias.astype(F32)
    g = jnp.concatenate([rb[:, :0:-1], jnp.broadcast_to(rb[:, -1:], (h, BIAS_LANES - 2 * MAX_REL))], axis=1)
    kern = functools.partial(_band_bias_kernel, n_new=n_new, r_band=r_band)
    return pl.pallas_call(
        kern,
        grid=(h,),
        in_specs=[pl.BlockSpec((None, 1, BIAS_LANES), lambda i: (i, 0, 0))],
        out_specs=[pl.BlockSpec((None, BAND_TQ, BAND_WIN), lambda i: (i, 0, 0)),
                   pl.BlockSpec((None, n_new, r_band + NEW_PAD), lambda i: (i, 0, 0))],
        out_shape=[jax.ShapeDtypeStruct((h, BAND_TQ, BAND_WIN), F32),
                   jax.ShapeDtypeStruct((h, n_new, r_band + NEW_PAD), F32)],
        compiler_params=_params(1),
        name="band_bias",
    )(g.reshape(h, 1, BIAS_LANES))


def _band_pad(k_ref, v_ref, kpad_ref, vpad_ref, t):
    kpad_ref[0:BAND_ROWS, :] = jnp.zeros((BAND_ROWS, HEAD_DIM), BF16)
    vpad_ref[0:BAND_ROWS, :] = jnp.zeros((BAND_ROWS, HEAD_DIM), BF16)
    kpad_ref[BAND_ROWS:BAND_ROWS + t, :] = k_ref[...]
    vpad_ref[BAND_ROWS:BAND_ROWS + t, :] = v_ref[...]


def _band_groups(q_ref, bias_ref, o_ref, kpad_ref, vpad_ref):
    s_idx = pl.program_id(2)
    col = lax.broadcasted_iota(jnp.int32, (BAND_TQ, BAND_WIN), 1)
    for gg in range(BAND_STEP_GROUPS):
        g = s_idx * BAND_STEP_GROUPS + gg
        start = pl.multiple_of(g * BAND_TQ, BAND_TQ)
        rows = slice(gg * BAND_TQ, (gg + 1) * BAND_TQ)
        s = _nt_dot(q_ref[rows, :], kpad_ref[pl.ds(start, BAND_WIN), :]) + bias_ref[...]
        s = jnp.where(col + g * BAND_TQ >= BAND_ROWS, s, NEG_INF)
        o_ref[rows, :] = _softmax2_pv([(s, vpad_ref[pl.ds(start, BAND_WIN), :])]).astype(BF16)


def _prompt_attn_kernel(qs_ref, ks_ref, vs_ref, negu2_ref, qb_ref, kb_ref, vb_ref, bias_ref,
                        osb_ref, obd_ref, acc_ref, carry_ref, kpads_ref, vpads_ref, kpadb_ref, vpadb_ref, *, t):
    _sb_prompt_body(qs_ref, ks_ref, vs_ref, negu2_ref, osb_ref, acc_ref, carry_ref, kpads_ref, vpads_ref, t=t,
                    at_first_step=lambda: _band_pad(kb_ref, vb_ref, kpadb_ref, vpadb_ref, t),
                    beside_diagonal=lambda: _band_groups(qb_ref, bias_ref, obd_ref, kpadb_ref, vpadb_ref))


def _prompt_attention(yqg3, ykv3, bias_tbl, negu2):
    b, t, _ = yqg3.shape
    tq = SB_TQ
    assert SB_TQ == BAND_TQ * BAND_STEP_GROUPS and SB_TQ // SB_TK >= 3 and SB_HEADS == BAND_HEADS
    q_spec = lambda c0: pl.BlockSpec((None, tq, HEAD_DIM),
                                     functools.partial(lambda b, h, i, c: (b, i, c + h), c=c0 // HEAD_DIM))
    kv_spec = lambda c0: pl.BlockSpec((None, t, HEAD_DIM),
                                      functools.partial(lambda b, h, i, c: (b, 0, c + h), c=c0 // HEAD_DIM))
    out_spec = pl.BlockSpec((None, tq, HEAD_DIM), lambda b, h, i: (b, i, h))
    kern = functools.partial(_prompt_attn_kernel, t=t)
    return pl.pallas_call(
        kern,
        grid=(b, SB_HEADS, t // tq),
        in_specs=[
            q_spec(QG_SB_Q), kv_spec(KV_SB_K), kv_spec(KV_SB_V),
            pl.BlockSpec((2 * SB_TK, SB_TK), lambda b, h, i: (0, 0)),
            q_spec(QG_BD_Q), kv_spec(KV_BD_K), kv_spec(KV_BD_V),
            pl.BlockSpec((None, BAND_TQ, BAND_WIN), lambda b, h, i: (h, 0, 0)),
        ],
        out_specs=[out_spec, out_spec],
        out_shape=[jax.ShapeDtypeStruct((b, t, SB_WIDTH), BF16), jax.ShapeDtypeStruct((b, t, BAND_WIDTH), BF16)],
        scratch_shapes=[pltpu.VMEM((SB_TQ, HEAD_DIM), F32), pltpu.VMEM((SB_TQ, 1), F32),
                        pltpu.VMEM((SB_TK + t, HEAD_DIM), BF16), pltpu.VMEM((SB_TK + t, HEAD_DIM), BF16),
                        pltpu.VMEM((BAND_ROWS + t, HEAD_DIM), BF16), pltpu.VMEM((BAND_ROWS + t, HEAD_DIM), BF16)],
        compiler_params=_params(3),
        name="prompt_attention",
    )(yqg3, ykv3, ykv3, negu2, yqg3, ykv3, ykv3, bias_tbl)


def _band_decode_kernel(q_ref, kn_ref, vn_ref, kc_ref, vc_ref, bias_ref, o_ref, kpad_ref, vpad_ref,
                        *, n_new, r_band):
    kpad_ref[...] = jnp.zeros_like(kpad_ref)
    vpad_ref[...] = jnp.zeros_like(vpad_ref)
    for h in range(BAND_HEADS):
        cols = slice(h * HEAD_DIM, (h + 1) * HEAD_DIM)
        kpad_ref[h, 0:n_new, :] = kn_ref[:, cols]
        vpad_ref[h, 0:n_new, :] = vn_ref[:, cols]
    for h in range(BAND_HEADS):
        cols = slice(h * HEAD_DIM, (h + 1) * HEAD_DIM)
        q = q_ref[:, cols]
        s_cache = _nt_dot(q, kc_ref[h].astype(BF16)) + bias_ref[h, :, 0:r_band]
        s_new = _nt_dot(q, kpad_ref[h]) + bias_ref[h, :, r_band:r_band + NEW_PAD]
        o_ref[:, cols] = _softmax2_pv([(s_cache, vc_ref[h].astype(BF16)),
                                       (s_new, vpad_ref[h])]).astype(BF16)


def _mem_attn_kernel(q_ref, mk_ref, mv_ref, o_ref):
    per_head = len(mk_ref.shape) == 3
    for h in range(MEM_HEADS):
        sl = slice(h * HEAD_DIM, (h + 1) * HEAD_DIM)
        mk = mk_ref[:, h, :] if per_head else mk_ref[:, sl]
        mv = mv_ref[:, h, :] if per_head else mv_ref[:, sl]
        s = _nt_dot(q_ref[:, sl], mk.astype(BF16))
        o_ref[:, sl] = _softmax2_pv([(s, mv.astype(BF16))]).astype(BF16)


def _mem_attention(y3, mk, mv, *, tq):
    b, t, _ = y3.shape
    n_mem = mk.shape[1]
    qb = MG_MM_Q // MEM_WIDTH
    return pl.pallas_call(
        _mem_attn_kernel,
        grid=(b, t // tq),
        in_specs=[
            pl.BlockSpec((None, tq, MEM_WIDTH), lambda b, i: (b, i, qb)),
            pl.BlockSpec((None, n_mem, MEM_WIDTH), lambda b, i: (b, 0, 0)),
            pl.BlockSpec((None, n_mem, MEM_WIDTH), lambda b, i: (b, 0, 0)),
        ],
        out_specs=pl.BlockSpec((None, tq, MEM_WIDTH), lambda b, i: (b, i, 0)),
        out_shape=jax.ShapeDtypeStruct((b, t, MEM_WIDTH), BF16),
        compiler_params=_params(2),
        name="mem_attention",
    )(y3, mk, mv)


def _decode_attn_kernel(qs_ref, kns_ref, vns_ref, kcs_hbm, vcs_hbm, negu2_ref,
                        qb_ref, knb_ref, vnb_ref, kcb_ref, vcb_ref, bias_ref, qm_ref, mk_ref, mv_ref,
                        osb_ref, obd_ref, omm_ref,
                        acc_ref, carry_ref, kpad_ref, vpad_ref, kbuf_ref, vbuf_ref, sem, kpadb_ref, vpadb_ref,
                        *, n_new, n_blocks, r_band):
    def band_and_memory():
        _band_decode_kernel(qb_ref, knb_ref, vnb_ref, kcb_ref, vcb_ref, bias_ref, obd_ref,
                            kpadb_ref, vpadb_ref, n_new=n_new, r_band=r_band)
        _mem_attn_kernel(qm_ref, mk_ref, mv_ref, omm_ref)

    _sb_decode_body(qs_ref, kns_ref, vns_ref, kcs_hbm, vcs_hbm, negu2_ref, osb_ref,
                    acc_ref, carry_ref, kpad_ref, vpad_ref, kbuf_ref, vbuf_ref, sem,
                    n_new=n_new, n_blocks=n_blocks, overlap=band_and_memory)


def _decode_attention(yqg3, ykv3, ymg3, cache_sb_k, cache_sb_v, cache_bd_k, cache_bd_v, mk, mv, bias_tbl, negu2):
    bd, n_new, _ = yqg3.shape
    past = cache_sb_k.shape[2]
    r_band = cache_bd_k.shape[2]
    n_mem = mk.shape[1]
    assert past % SB_TK == 0 and n_new <= NEW_PAD and n_new % 16 == 0
    kern = functools.partial(_decode_attn_kernel, n_new=n_new, n_blocks=past // SB_TK, r_band=r_band)
    slab = lambda width, col0: pl.BlockSpec((None, n_new, width), functools.partial(lambda b, c: (b, 0, c), c=col0 // width))
    band_cache = pl.BlockSpec((None, BAND_HEADS, r_band, HEAD_DIM), lambda b: (b, 0, 0, 0))
    mem_cache = pl.BlockSpec((None, n_mem, MEM_HEADS, HEAD_DIM), lambda b: (b, 0, 0, 0))
    return pl.pallas_call(
        kern,
        grid=(bd,),
        in_specs=[
            slab(SB_WIDTH, QG_SB_Q), slab(SB_WIDTH, KV_SB_K), slab(SB_WIDTH, KV_SB_V),
            pl.BlockSpec(memory_space=pl.ANY), pl.BlockSpec(memory_space=pl.ANY),
            pl.BlockSpec((2 * SB_TK, SB_TK), lambda b: (0, 0)),
            slab(BAND_WIDTH, QG_BD_Q), slab(BAND_WIDTH, KV_BD_K), slab(BAND_WIDTH, KV_BD_V),
            band_cache, band_cache,
            pl.BlockSpec((BAND_HEADS, n_new, r_band + NEW_PAD), lambda b: (0, 0, 0)),
            slab(MEM_WIDTH, MG_MM_Q), mem_cache, mem_cache,
        ],
        out_specs=[pl.BlockSpec((None, n_new, SB_WIDTH), lambda b: (b, 0, 0)),
                   pl.BlockSpec((None, n_new, BAND_WIDTH), lambda b: (b, 0, 0)),
                   pl.BlockSpec((None, n_new, MEM_WIDTH), lambda b: (b, 0, 0))],
        out_shape=[jax.ShapeDtypeStruct((bd, n_new, SB_WIDTH), BF16),
                   jax.ShapeDtypeStruct((bd, n_new, BAND_WIDTH), BF16),
                   jax.ShapeDtypeStruct((bd, n_new, MEM_WIDTH), BF16)],
        scratch_shapes=[pltpu.VMEM((SB_HEADS * n_new, HEAD_DIM), F32),
                        pltpu.VMEM((SB_HEADS * n_new, 1), F32),
                        pltpu.VMEM((SB_HEADS, NEW_PAD, HEAD_DIM), BF16),
                        pltpu.VMEM((SB_HEADS, NEW_PAD, HEAD_DIM), BF16),
                        pltpu.VMEM((2, SB_HEADS, SB_TK, HEAD_DIM), F32),
                        pltpu.VMEM((2, SB_HEADS, SB_TK, HEAD_DIM), F32),
                        pltpu.SemaphoreType.DMA((2, 2)),
                        pltpu.VMEM((BAND_HEADS, NEW_PAD, HEAD_DIM), BF16),
                        pltpu.VMEM((BAND_HEADS, NEW_PAD, HEAD_DIM), BF16)],
        compiler_params=_params(1),
        name="decode_attention",
    )(yqg3, ykv3, ykv3, cache_sb_k, cache_sb_v, negu2, yqg3, ykv3, ykv3, cache_bd_k, cache_bd_v, bias_tbl,
      ymg3, mk, mv)


def _silu_of_half(h):
    return h + h * jnp.tanh(h)


def _merge_kernel(osb_ref, obd_ref, omm_ref, gsb_ref, gbd_ref, gmm_ref,
                  mg0_ref, mg1_ref, mg2_ref, mg3_ref, mg4_ref, mg5_ref,
                  wsb_ref, wbd_ref, wmm_ref, merged_ref, *, half):
    u_sb = (osb_ref[...].astype(F32) * _silu_of_half(gsb_ref[...].astype(F32))).astype(BF16)
    u_bd = (obd_ref[...].astype(F32) * _silu_of_half(gbd_ref[...].astype(F32))).astype(BF16)
    u_mm = (omm_ref[...].astype(F32) * _silu_of_half(gmm_ref[...].astype(F32))).astype(BF16)
    mg = ((mg0_ref, mg2_ref, mg4_ref), (mg1_ref, mg3_ref, mg5_ref))
    for n in range(2):
        cols = slice(n * half, (n + 1) * half)
        merged = None
        for m_ref, u, w_ref in zip(mg[n], (u_sb, u_bd, u_mm), (wsb_ref, wbd_ref, wmm_ref)):
            a = _dot(u, w_ref[:, cols])
            term = a + a * jnp.tanh(m_ref[...].astype(F32))
            merged = term if merged is None else merged + term
        merged_ref[:, cols] = merged.astype(BF16)


def _merge_branches(yqg, ymg, o_sb, o_bd, o_mm, w_sb, w_bd, w_mm, *, tm):
    m = yqg.shape[0]
    d = w_sb.shape[1]
    half = d // 2
    assert MG_MG % half == 0
    mgb = MG_MG // half
    const = dict(pipeline_mode=pl.Buffered(1))
    kern = functools.partial(_merge_kernel, half=half)
    return pl.pallas_call(
        kern,
        grid=(m // tm,),
        in_specs=[
            pl.BlockSpec((tm, SB_WIDTH), lambda i: (i, 0)),
            pl.BlockSpec((tm, BAND_WIDTH), lambda i: (i, 0)),
            pl.BlockSpec((tm, MEM_WIDTH), lambda i: (i, 0)),
            pl.BlockSpec((tm, SB_WIDTH), lambda i: (i, QG_SB_G // SB_WIDTH)),
            pl.BlockSpec((tm, BAND_WIDTH), lambda i: (i, QG_BD_G // BAND_WIDTH)),
            pl.BlockSpec((tm, MEM_WIDTH), lambda i: (i, MG_MM_G // MEM_WIDTH)),
        ] + [pl.BlockSpec((tm, half), functools.partial(lambda i, c: (i, c), c=mgb + c)) for c in range(6)] + [
            pl.BlockSpec((SB_WIDTH, d), lambda i: (0, 0), **const),
            pl.BlockSpec((BAND_WIDTH, d), lambda i: (0, 0), **const),
            pl.BlockSpec((MEM_WIDTH, d), lambda i: (0, 0), **const),
        ],
        out_specs=pl.BlockSpec((tm, d), lambda i: (i, 0)),
        out_shape=jax.ShapeDtypeStruct((m, d), BF16),
        compiler_params=_params(1),
        name="merge_branches",
    )(o_sb, o_bd, o_mm, yqg, yqg, ymg, *([ymg] * 6), w_sb, w_bd, w_mm)


def _out_proj_kernel(x_ref, merged_ref, wout_ref, gpost_ref, y_ref):
    y = _dot(merged_ref[...], wout_ref[...])
    ms = jnp.mean(y * y, axis=-1, keepdims=True)
    y_ref[...] = x_ref[...] + (y * lax.rsqrt(ms + RMS_EPS)) * gpost_ref[...]


def _merge_out_kernel(x_ref, osb_ref, obd_ref, omm_ref, gsb_ref, gbd_ref, gmm_ref,
                      mg0_ref, mg1_ref, mg2_ref, mg3_ref, mg4_ref, mg5_ref,
                      wsb_ref, wbd_ref, wmm_ref, wout_ref, gpost_ref, y_ref, merged_ref, *, half):
    _merge_kernel(osb_ref, obd_ref, omm_ref, gsb_ref, gbd_ref, gmm_ref,
                  mg0_ref, mg1_ref, mg2_ref, mg3_ref, mg4_ref, mg5_ref,
                  wsb_ref, wbd_ref, wmm_ref, merged_ref, half=half)
    _out_proj_kernel(x_ref, merged_ref, wout_ref, gpost_ref, y_ref)


def _merge_out(x2d, yqg, ymg, o_sb, o_bd, o_mm, w_sb, w_bd, w_mm, w_out, g_post):
    m, d = x2d.shape
    half = d // 2
    mgb = MG_MG // half
    whole = lambda shape: pl.BlockSpec(shape, lambda i: (0,) * len(shape))
    col = lambda width, c: pl.BlockSpec((m, width), functools.partial(lambda i, c: (0, c), c=c))
    kern = functools.partial(_merge_out_kernel, half=half)
    return pl.pallas_call(
        kern,
        grid=(1,),
        in_specs=[
            whole((m, d)), whole((m, SB_WIDTH)), whole((m, BAND_WIDTH)), whole((m, MEM_WIDTH)),
            col(SB_WIDTH, QG_SB_G // SB_WIDTH), col(BAND_WIDTH, QG_BD_G // BAND_WIDTH),
            col(MEM_WIDTH, MG_MM_G // MEM_WIDTH),
        ] + [col(half, mgb + c) for c in range(6)] + [
            whole((SB_WIDTH, d)), whole((BAND_WIDTH, d)), whole((MEM_WIDTH, d)), whole((d, d)), whole((1, d)),
        ],
        out_specs=whole((m, d)),
        out_shape=jax.ShapeDtypeStruct((m, d), F32),
        scratch_shapes=[pltpu.VMEM((m, d), BF16)],
        compiler_params=_params(1),
        name="merge_out",
    )(x2d, o_sb, o_bd, o_mm, yqg, yqg, ymg, *([ymg] * 6), w_sb, w_bd, w_mm, w_out, g_post.reshape(1, d))


def _out_projection(x2d, merged, w_out, g_post, *, tm):
    m, d = x2d.shape
    return pl.pallas_call(
        _out_proj_kernel,
        grid=(m // tm,),
        in_specs=[
            pl.BlockSpec((tm, d), lambda i: (i, 0)),
            pl.BlockSpec((tm, d), lambda i: (i, 0)),
            pl.BlockSpec((d, d), lambda i: (0, 0), pipeline_mode=pl.Buffered(1)),
            pl.BlockSpec((1, d), lambda i: (0, 0)),
        ],
        out_specs=pl.BlockSpec((tm, d), lambda i: (i, 0)),
        out_shape=jax.ShapeDtypeStruct((m, d), F32),
        compiler_params=_params(1),
        name="out_projection",
    )(x2d, merged, w_out, g_post.reshape(1, d))


def _head_major(a):
    return jnp.transpose(a, (0, 2, 1, 3))


def kernel(x_prompt, x_sample, cache_sb_k, cache_sb_v, cache_band_k, cache_band_v, cache_mem_k, cache_mem_v, mem_prompt, g_pre, w_in, rel_bias, g_mem, w_mem_kv, w_up_sb, w_up_band, w_up_mem, w_out, g_post):
    depth = w_in.shape[0]
    b, t, d = x_prompt.shape
    bd, n_new, _ = x_sample.shape
    n_mem = mem_prompt.shape[1]
    r_band = cache_band_k.shape[2]
    in_width = w_in.shape[2]
    band_keep = min(BAND_ROWS, t)
    assert COL_MG + 3 * d == in_width
    assert t % SB_TQ == 0 and t % (BAND_TQ * BAND_STEP_GROUPS) == 0 and t % band_keep == 0
    assert r_band == BAND_ROWS and n_new <= CHUNK

    negu2 = jnp.where(jnp.arange(2 * SB_TK)[:, None] % SB_TK >= jnp.arange(SB_TK)[None, :], -1.0, 0.0).astype(BF16)
    mg_width = in_width - COL_MM_Q
    assert COL_MM_Q % MG_TN == 0 and mg_width % MG_TN == 0 and COL_SB_G == 3 * KV_TN and COL_BD_Q == 4 * KV_TN
    cols = jnp.arange(QG_WIDTH)
    qg_scale = jnp.where((cols // SB_WIDTH) % 2 == 0, Q_SCALE, 0.5).astype(F32).reshape(1, QG_WIDTH)
    cols = jnp.arange(mg_width)
    mg_scale = jnp.where(cols < MG_MM_G, Q_SCALE, 0.5).astype(F32).reshape(1, mg_width)
    qg_block = lambda j: j + 2 * ((j + 1) // 2)
    mg_block = lambda j: j + COL_MM_Q // MG_TN

    xp = x_prompt.reshape(b * t, d)
    xs = x_sample.reshape(bd * n_new, d)
    ms = bd * n_new
    outs = [[] for _ in range(10)]
    for l in range(depth):
        w_kvp_b = _kv_weight_bf16(w_in[l])
        w_sb_b = (0.5 * w_up_sb[l]).astype(BF16)
        w_bd_b = (0.5 * w_up_band[l]).astype(BF16)
        w_mm_b = (0.5 * w_up_mem[l]).astype(BF16)
        w_out_b = w_out[l].astype(BF16)
        bias_p, bias_d = _band_bias_tables(rel_bias[l], n_new, r_band)

        ykv, hp, sbk, sbv, bdk, bdv = _kv_projection(xp, g_pre[l], w_kvp_b, tm=band_keep, seqs=1, n_seq=b,
                                                     band_keep=band_keep)
        ykv_s, hs, sbk2, sbv2, bdk2, bdv2 = _kv_projection(xs, g_pre[l], w_kvp_b, tm=ms, seqs=bd, n_seq=bd,
                                                           band_keep=n_new)
        yqg, yqg_s = _col_projection(hp, hs, w_in[l], qg_scale, qg_block, tm=PROJ_TM, tn=KV_TN,
                                     name="qg_projection")
        ymg, ymg_s = _col_projection(hp, hs, w_in[l], mg_scale, mg_block, tm=PROJ_TM, tn=MG_TN,
                                     name="mg_projection")

        mk, mv = _memory_kv(mem_prompt.reshape(b * n_mem, d), g_mem[l], w_mem_kv[l], tm=n_mem)
        yqg3 = yqg.reshape(b, t, QG_WIDTH)
        ykv3 = ykv.reshape(b, t, KV_WIDTH)
        o_sb, o_bd = _prompt_attention(yqg3, ykv3, bias_p, negu2)
        o_mm = _mem_attention(ymg.reshape(b, t, mg_width), mk.reshape(b, n_mem, MEM_WIDTH),
                              mv.reshape(b, n_mem, MEM_WIDTH), tq=MEM_TQ)
        merged = _merge_branches(yqg, ymg, o_sb.reshape(b * t, -1), o_bd.reshape(b * t, -1),
                                 o_mm.reshape(b * t, -1), w_sb_b, w_bd_b, w_mm_b, tm=OUT_TM)
        xp = _out_projection(xp, merged, w_out_b, g_post[l], tm=OUT_TM)
        outs[0].append(_head_major(sbk))
        outs[1].append(_head_major(sbv))
        outs[2].append(_head_major(bdk))
        outs[3].append(_head_major(bdv))
        outs[4].append(mk.reshape(b, n_mem, MEM_HEADS, HEAD_DIM))
        outs[5].append(mv.reshape(b, n_mem, MEM_HEADS, HEAD_DIM))

        o_sb2, o_bd2, o_mm2 = _decode_attention(
            yqg_s.reshape(bd, n_new, QG_WIDTH), ykv_s.reshape(bd, n_new, KV_WIDTH),
            ymg_s.reshape(bd, n_new, mg_width),
            _head_major(cache_sb_k[l]), _head_major(cache_sb_v[l]),
            _head_major(cache_band_k[l]), _head_major(cache_band_v[l]),
            cache_mem_k[l], cache_mem_v[l],
            bias_d, negu2)
        xs = _merge_out(xs, yqg_s, ymg_s, o_sb2.reshape(ms, -1), o_bd2.reshape(ms, -1), o_mm2.reshape(ms, -1),
                        w_sb_b, w_bd_b, w_mm_b, w_out_b, g_post[l])
        outs[6].append(_head_major(sbk2))
        outs[7].append(_head_major(sbv2))
        outs[8].append(_head_major(bdk2))
        outs[9].append(_head_major(bdv2))

    return (xp.reshape(b, t, d), xs.reshape(bd, n_new, d)) + tuple(jnp.stack(o) for o in outs)
```

```python
import functools
import math

import jax
import jax.numpy as jnp
from jax import lax
from jax.experimental import pallas as pl
from jax.experimental.pallas import tpu as pltpu

F32 = jnp.float32
BF16 = jnp.bfloat16

HEAD_DIM = 128
SB_HEADS = 6
BAND_HEADS = 6
MEM_HEADS = 4
SB_WIDTH = SB_HEADS * HEAD_DIM
BAND_WIDTH = BAND_HEADS * HEAD_DIM
MEM_WIDTH = MEM_HEADS * HEAD_DIM
CHUNK = 64
CHUNK_SHIFT = 6
BAND_LEFT_CHUNKS = 8
BAND_ROWS = BAND_LEFT_CHUNKS * CHUNK
MAX_REL = 256
RMS_EPS = 1e-6
NEG_INF = -1e30
LOG2E = math.log2(math.e)
Q_SCALE = HEAD_DIM ** -0.5 * LOG2E

COL_SB_Q = 0
COL_SB_K = COL_SB_Q + SB_WIDTH
COL_SB_V = COL_SB_K + SB_WIDTH
COL_SB_G = COL_SB_V + SB_WIDTH
COL_BD_Q = COL_SB_G + SB_WIDTH
COL_BD_K = COL_BD_Q + BAND_WIDTH
COL_BD_V = COL_BD_K + BAND_WIDTH
COL_BD_G = COL_BD_V + BAND_WIDTH
COL_MM_Q = COL_BD_G + BAND_WIDTH
COL_MM_G = COL_MM_Q + MEM_WIDTH
COL_MG = COL_MM_G + MEM_WIDTH

KV_SB_K = 0
KV_SB_V = KV_SB_K + SB_WIDTH
KV_BD_K = KV_SB_V + SB_WIDTH
KV_BD_V = KV_BD_K + BAND_WIDTH
KV_WIDTH = KV_BD_V + BAND_WIDTH
QG_SB_Q = 0
QG_SB_G = QG_SB_Q + SB_WIDTH
QG_BD_Q = QG_SB_G + SB_WIDTH
QG_BD_G = QG_BD_Q + BAND_WIDTH
QG_WIDTH = QG_BD_G + BAND_WIDTH
MG_MM_Q = 0
MG_MM_G = MG_MM_Q + MEM_WIDTH
MG_MG = MG_MM_G + MEM_WIDTH

VMEM_LIMIT_BYTES = 56 * 1024 * 1024
MAX_VMEM_LIMIT_BYTES = 58 * 1024 * 1024
COMPILER_TEMP_BYTES = 2 * 1024 * 1024
PROJ_TM = 2048
OUT_TM = 512
MEM_TQ = 1024
CAST_ROWS = 256
KV_TN = SB_WIDTH
MG_TN = 1024
SB_TK = 256
SB_TQ = 16 * SB_TK
SB_DEAD = -160.0
BAND_TQ = 4 * CHUNK
BAND_WIN = BAND_TQ + BAND_ROWS
BAND_STEP_GROUPS = 16
BIAS_LANES = 1024
NEW_PAD = 128


def _params(n_axes, vmem=VMEM_LIMIT_BYTES):
    return pltpu.CompilerParams(dimension_semantics=("arbitrary",) * n_axes,
                                vmem_limit_bytes=vmem)


def _nt_dot(a, b):
    return lax.dot_general(a, b, (((1,), (1,)), ((), ())), preferred_element_type=F32)


def _dot(a, b):
    return jnp.dot(a, b, preferred_element_type=F32)


def _pre_norm_to(h_ref, x_ref, g_ref):
    x = x_ref[...]
    ms = jnp.mean(x * x, axis=-1, keepdims=True)
    h_ref[...] = ((x * lax.rsqrt(ms + RMS_EPS)) * g_ref[...]).astype(BF16)


def _kv_proj_kernel(x_ref, g_ref, wsk_ref, wsv_ref, wbk_ref, wbv_ref,
                    y_ref, h_ref, sbk_ref, sbv_ref, bdk_ref, bdv_ref, *, seqs, rows):
    _pre_norm_to(h_ref, x_ref, g_ref)
    groups = ((wsk_ref, sbk_ref), (wsv_ref, sbv_ref), (wbk_ref, bdk_ref), (wbv_ref, bdv_ref))
    for group, (w_ref, dst_ref) in enumerate(groups):
        acc = _dot(h_ref[...], w_ref[...])
        y_ref[:, group * KV_TN:(group + 1) * KV_TN] = acc.astype(BF16)
        for h in range(SB_HEADS):
            for s in range(seqs):
                dst_ref[s, h] = acc[s * rows:(s + 1) * rows, h * HEAD_DIM:(h + 1) * HEAD_DIM]


def _kv_projection(x2d, g_pre, w_kv_b, *, tm, seqs, n_seq, band_keep):
    m, d = x2d.shape
    assert KV_WIDTH == 4 * KV_TN and SB_HEADS == BAND_HEADS
    rows = tm // seqs
    seq_rows = m // n_seq
    blocks_per_seq = seq_rows // rows
    assert rows == band_keep
    kern = functools.partial(_kv_proj_kernel, seqs=seqs, rows=rows)
    sb_spec = pl.BlockSpec((seqs, SB_HEADS, rows, HEAD_DIM),
                           lambda i: (i // blocks_per_seq, 0, i % blocks_per_seq, 0))
    bd_spec = pl.BlockSpec((seqs, BAND_HEADS, rows, HEAD_DIM), lambda i: (i // blocks_per_seq, 0, 0, 0))
    sb_shape = jax.ShapeDtypeStruct((n_seq, SB_HEADS, seq_rows, HEAD_DIM), F32)
    bd_shape = jax.ShapeDtypeStruct((n_seq, BAND_HEADS, band_keep, HEAD_DIM), F32)

    def w_spec(group):
        return pl.BlockSpec((d, KV_TN), functools.partial(lambda i, c: (0, c), c=group),
                            pipeline_mode=pl.Buffered(1))

    return pl.pallas_call(
        kern,
        grid=(m // tm,),
        in_specs=[
            pl.BlockSpec((tm, d), lambda i: (i, 0)),
            pl.BlockSpec((1, d), lambda i: (0, 0)),
            w_spec(0), w_spec(1), w_spec(2), w_spec(3),
        ],
        out_specs=[pl.BlockSpec((tm, KV_WIDTH), lambda i: (i, 0)), pl.BlockSpec((tm, d), lambda i: (i, 0)),
                   sb_spec, sb_spec, bd_spec, bd_spec],
        out_shape=[jax.ShapeDtypeStruct((m, KV_WIDTH), BF16), jax.ShapeDtypeStruct((m, d), BF16),
                   sb_shape, sb_shape, bd_shape, bd_shape],
        compiler_params=_params(1),
        name="kv_projection",
    )(x2d, g_pre.reshape(1, d), w_kv_b, w_kv_b, w_kv_b, w_kv_b)


def _col_proj_kernel(h_ref, hs_ref, w_ref, cs_ref, y_ref, ys_ref, wb_ref):
    @pl.when(pl.program_id(1) == 0)
    def _():
        wb_ref[...] = w_ref[...].astype(BF16)
        ys_ref[...] = (_dot(hs_ref[...], wb_ref[...]) * cs_ref[...]).astype(BF16)

    y_ref[...] = (_dot(h_ref[...], wb_ref[...]) * cs_ref[...]).astype(BF16)


def _col_projection(h2d, hs2d, w_in, col_scale, src_block, *, tm, tn, name):
    m, d = h2d.shape
    ms = hs2d.shape[0]
    n = col_scale.shape[1]
    assert n % tn == 0 and m % tm == 0
    vmem = (2 * tm * d * 2 + 2 * d * tn * 4 + 2 * tm * tn * 2 + d * tn * 2 + 2 * tm * tn * 4
            + 2 * ms * (d + tn) * 2 + COMPILER_TEMP_BYTES)
    vmem = min(vmem, MAX_VMEM_LIMIT_BYTES)
    return pl.pallas_call(
        _col_proj_kernel,
        grid=(n // tn, m // tm),
        in_specs=[
            pl.BlockSpec((tm, d), lambda j, i: (i, 0)),
            pl.BlockSpec((ms, d), lambda j, i: (0, 0)),
            pl.BlockSpec((d, tn), lambda j, i: (0, src_block(j))),
            pl.BlockSpec((1, tn), lambda j, i: (0, j)),
        ],
        out_specs=[pl.BlockSpec((tm, tn), lambda j, i: (i, j)), pl.BlockSpec((ms, tn), lambda j, i: (0, j))],
        out_shape=[jax.ShapeDtypeStruct((m, n), BF16), jax.ShapeDtypeStruct((ms, n), BF16)],
        scratch_shapes=[pltpu.VMEM((d, tn), BF16)],
        compiler_params=_params(2, vmem=vmem),
        name=name,
    )(h2d, hs2d, w_in, col_scale)


def _kv_weight_kernel(a_ref, b_ref, c_ref, d_ref, o_ref):
    for group, w_ref in enumerate((a_ref, b_ref, c_ref, d_ref)):
        o_ref[:, group * KV_TN:(group + 1) * KV_TN] = w_ref[...].astype(BF16)


def _kv_weight_bf16(w):
    d = w.shape[0]
    tr = CAST_ROWS
    assert d % tr == 0

    def spec(col0):
        assert col0 % KV_TN == 0
        return pl.BlockSpec((tr, KV_TN), functools.partial(lambda i, c: (i, c), c=col0 // KV_TN))

    return pl.pallas_call(
        _kv_weight_kernel,
        grid=(d // tr,),
        in_specs=[spec(COL_SB_K), spec(COL_SB_V), spec(COL_BD_K), spec(COL_BD_V)],
        out_specs=pl.BlockSpec((tr, KV_WIDTH), lambda i: (i, 0)),
        out_shape=jax.ShapeDtypeStruct((d, KV_WIDTH), BF16),
        compiler_params=_params(1),
        name="kv_weight_cast",
    )(w, w, w, w)


def _memkv_kernel(x_ref, g_ref, w_ref, mk_ref, mv_ref):
    x = x_ref[...]
    ms = jnp.mean(x * x, axis=-1, keepdims=True)
    h = ((x * lax.rsqrt(ms + RMS_EPS)) * g_ref[...]).astype(BF16)
    acc = _dot(h, w_ref[...].astype(BF16))
    mk_ref[...] = acc[:, :MEM_WIDTH]
    mv_ref[...] = acc[:, MEM_WIDTH:]


def _memory_kv(mem2d, g_mem, w_bf16, *, tm):
    m, d = mem2d.shape
    return pl.pallas_call(
        _memkv_kernel,
        grid=(m // tm,),
        in_specs=[
            pl.BlockSpec((tm, d), lambda i: (i, 0)),
            pl.BlockSpec((1, d), lambda i: (0, 0)),
            pl.BlockSpec((d, 2 * MEM_WIDTH), lambda i: (0, 0)),
        ],
        out_specs=[pl.BlockSpec((tm, MEM_WIDTH), lambda i: (i, 0)),
                   pl.BlockSpec((tm, MEM_WIDTH), lambda i: (i, 0))],
        out_shape=[jax.ShapeDtypeStruct((m, MEM_WIDTH), F32),
                   jax.ShapeDtypeStruct((m, MEM_WIDTH), F32)],
        compiler_params=_params(1),
        name="memory_kv",
    )(mem2d, g_mem.reshape(1, d), w_bf16)


def _neg_suffix_matrix(n):
    row = lax.broadcasted_iota(jnp.int32, (2 * n, n), 0)
    col = lax.broadcasted_iota(jnp.int32, (2 * n, n), 1)
    row = jnp.where(row >= n, row - n, row)
    return jnp.where(row >= col, -1.0, 0.0).astype(BF16)


def _sb_weights(z2, carry2, negu2, mask):
    p = jnp.maximum(z2, 0.0) + jnp.log(1.0 + jnp.exp2(-jnp.abs(z2))) * LOG2E
    if mask is not None:
        p = jnp.where(mask, p, 0.0)
    p_hi = p.astype(BF16)
    p_lo = (p - p_hi.astype(F32)).astype(BF16)
    suffix = _dot(jnp.concatenate([p_hi, p_lo], axis=1), negu2)
    w = jnp.exp2(z2 + suffix + carry2)
    if mask is not None:
        w = jnp.where(mask, w, 0.0)
    return w, carry2 - jnp.sum(p, axis=-1, keepdims=True)


def _sb_prompt_body(q_ref, k_ref, v_ref, negu2_ref, o_ref, acc_ref, carry_ref, kpad_ref, vpad_ref,
                    *, t, at_first_step, beside_diagonal):
    i = pl.program_id(2)
    n_sub = SB_TQ // SB_TK
    negu2 = negu2_ref[...]

    @pl.when(i == 0)
    def _():
        kpad_ref[0:SB_TK, :] = jnp.zeros((SB_TK, HEAD_DIM), BF16)
        vpad_ref[0:SB_TK, :] = jnp.zeros((SB_TK, HEAD_DIM), BF16)
        kpad_ref[SB_TK:SB_TK + t, :] = k_ref[...]
        vpad_ref[SB_TK:SB_TK + t, :] = v_ref[...]
        at_first_step()

    def kv_block(j):
        start = pl.multiple_of((j + 1) * SB_TK, SB_TK)
        return kpad_ref[pl.ds(start, SB_TK), :], vpad_ref[pl.ds(start, SB_TK), :]

    row = lax.broadcasted_iota(jnp.int32, (SB_TK, SB_TK), 0)
    col = lax.broadcasted_iota(jnp.int32, (SB_TK, SB_TK), 1)
    for r in range(n_sub):
        rows = slice(r * SB_TK, (r + 1) * SB_TK)
        s = i * n_sub + r
        q = q_ref[rows, :]
        kb, vb = kv_block(s)
        w, carry = _sb_weights(_nt_dot(q, kb), jnp.zeros((SB_TK, 1), F32), negu2, col < row)
        acc = _dot(w.astype(BF16), vb)
        kb, vb = kv_block(s - 1)
        prev_exists = None if r > 0 else (jnp.zeros((SB_TK, SB_TK), jnp.int32) + i) > 0
        w, carry = _sb_weights(_nt_dot(q, kb), carry, negu2, prev_exists)
        acc_ref[rows, :] = acc + _dot(w.astype(BF16), vb)
        carry_ref[rows, :] = carry
    beside_diagonal()

    row_q = lax.broadcasted_iota(jnp.int32, (SB_TQ, 1), 0)
    row_t = lax.broadcasted_iota(jnp.int32, (SB_TQ, SB_TK), 0)
    has_more = row_q >= (2 - n_sub * i) * SB_TK

    def any_alive(carry):
        return (jnp.max(jnp.where(has_more, carry, NEG_INF)) > SB_DEAD).astype(jnp.int32)

    def cond(state):
        j, alive = state
        return jnp.logical_and(j >= 0, alive > 0)

    def body(state):
        j, _ = state
        kb, vb = kv_block(j)
        visits = row_t >= (j - n_sub * i + 2) * SB_TK
        w, carry = _sb_weights(_nt_dot(q_ref[...], kb), carry_ref[...], negu2, visits)
        acc_ref[...] += _dot(w.astype(BF16), vb)
        carry_ref[...] = carry
        return j - 1, any_alive(carry)

    lax.while_loop(cond, body, (n_sub * i + n_sub - 3, any_alive(carry_ref[...])))
    o_ref[...] = acc_ref[...].astype(BF16)


def _sb_decode_body(q_ref, kn_ref, vn_ref, kc_hbm, vc_hbm, negu2_ref, o_ref,
                    acc_ref, carry_ref, kpad_ref, vpad_ref, kbuf_ref, vbuf_ref, sem,
                    *, n_new, n_blocks, overlap):
    b = pl.program_id(0)
    heads = SB_HEADS

    def cache_copies(j, slot):
        rows = pl.ds(pl.multiple_of((n_blocks - 1 - j) * SB_TK, SB_TK), SB_TK)
        return (pltpu.make_async_copy(kc_hbm.at[b, :, rows, :], kbuf_ref.at[slot], sem.at[0, slot]),
                pltpu.make_async_copy(vc_hbm.at[b, :, rows, :], vbuf_ref.at[slot], sem.at[1, slot]))

    def start_fetch(j, slot):
        for cp in cache_copies(j, slot):
            cp.start()

    def wait_fetch(j, slot):
        for cp in cache_copies(j, slot):
            cp.wait()

    start_fetch(0, 0)
    overlap()

    def head_cols(h):
        return slice(h * HEAD_DIM, (h + 1) * HEAD_DIM)

    def head_rows(h):
        return slice(h * n_new, (h + 1) * n_new)

    def block(k_of, v_of, negu2, mask):
        z2 = jnp.concatenate([_nt_dot(q_ref[:, head_cols(h)], k_of(h)) for h in range(heads)], axis=0)
        w, carry = _sb_weights(z2, carry_ref[...], negu2, mask)
        wb = w.astype(BF16)
        for h in range(heads):
            acc_ref[head_rows(h), :] += _dot(wb[head_rows(h), :], v_of(h))
        carry_ref[...] = carry

    def any_alive():
        return (jnp.max(carry_ref[...]) > SB_DEAD).astype(jnp.int32)

    acc_ref[...] = jnp.zeros_like(acc_ref)
    carry_ref[...] = jnp.zeros_like(carry_ref)
    kpad_ref[...] = jnp.zeros_like(kpad_ref)
    vpad_ref[...] = jnp.zeros_like(vpad_ref)
    for h in range(heads):
        kpad_ref[h, 0:n_new, :] = kn_ref[:, head_cols(h)]
        vpad_ref[h, 0:n_new, :] = vn_ref[:, head_cols(h)]
    row = lax.broadcasted_iota(jnp.int32, (n_new, NEW_PAD), 0)
    col = lax.broadcasted_iota(jnp.int32, (n_new, NEW_PAD), 1)
    mask = jnp.concatenate([(col < row).astype(jnp.int32)] * heads, axis=0) == 1
    block(lambda h: kpad_ref[h], lambda h: vpad_ref[h], _neg_suffix_matrix(NEW_PAD), mask)

    negu2 = negu2_ref[...]

    def cond(state):
        j, alive = state
        return jnp.logical_and(j < n_blocks, alive > 0)

    def body(state):
        j, _ = state
        slot = j % 2
        wait_fetch(j, slot)

        @pl.when(j + 1 < n_blocks)
        def _():
            start_fetch(j + 1, 1 - slot)

        block(lambda h: kbuf_ref[slot, h].astype(BF16), lambda h: vbuf_ref[slot, h].astype(BF16),
              negu2, None)
        return j + 1, any_alive()

    j_end, _ = lax.while_loop(cond, body, (0, any_alive()))

    @pl.when(j_end < n_blocks)
    def _():
        wait_fetch(j_end, j_end % 2)

    for h in range(heads):
        o_ref[:, head_cols(h)] = acc_ref[head_rows(h), :].astype(BF16)


def _softmax2_pv(parts):
    mx = functools.reduce(jnp.maximum, [jnp.max(s, axis=-1, keepdims=True) for s, _ in parts])
    num = None
    den = None
    for s, v in parts:
        p = jnp.exp2(s - mx)
        d = jnp.sum(p, axis=-1, keepdims=True)
        o = _dot(p.astype(BF16), v)
        num = o if num is None else num + o
        den = d if den is None else den + d
    return num / den


def _band_bias_kernel(g_ref, tp_ref, td_ref, *, n_new, r_band):
    x = jnp.broadcast_to(g_ref[...], (BAND_TQ, BIAS_LANES))
    x = pltpu.roll(x, BAND_TQ, 1, stride=1, stride_axis=0)
    tbl = x[:, :BAND_WIN] * LOG2E
    r = lax.broadcasted_iota(jnp.int32, (BAND_TQ, BAND_WIN), 0)
    j = lax.broadcasted_iota(jnp.int32, (BAND_TQ, BAND_WIN), 1)
    dc = (j >> CHUNK_SHIFT) - (r >> CHUNK_SHIFT)
    tp_ref[...] = jnp.where(jnp.logical_and(dc >= 0, dc <= BAND_LEFT_CHUNKS), tbl, NEG_INF)
    jd = lax.broadcasted_iota(jnp.int32, (n_new, r_band + NEW_PAD), 1)
    td_ref[...] = jnp.where(jd < r_band + n_new, tbl[:n_new, :r_band + NEW_PAD], NEG_INF)


def _band_bias_tables(rel_bias, n_new, r_band):
    h = rel_bias.shape[0]
    assert BAND_ROWS == 2 * MAX_REL and r_band == BAND_ROWS and BIAS_LANES == 2 * BAND_ROWS
    rb = rel_bias.astype(F32)
    g = jnp.concatenate([rb[:, :0:-1], jnp.broadcast_to(rb[:, -1:], (h, BIAS_LANES - 2 * MAX_REL))], axis=1)
    kern = functools.partial(_band_bias_kernel, n_new=n_new, r_band=r_band)
    return pl.pallas_call(
        kern,
        grid=(h,),
        in_specs=[pl.BlockSpec((None, 1, BIAS_LANES), lambda i: (i, 0, 0))],
        out_specs=[pl.BlockSpec((None, BAND_TQ, BAND_WIN), lambda i: (i, 0, 0)),
                   pl.BlockSpec((None, n_new, r_band + NEW_PAD), lambda i: (i, 0, 0))],
        out_shape=[jax.ShapeDtypeStruct((h, BAND_TQ, BAND_WIN), F32),
                   jax.ShapeDtypeStruct((h, n_new, r_band + NEW_PAD), F32)],
        compiler_params=_params(1),
        name="band_bias",
    )(g.reshape(h, 1, BIAS_LANES))


def _band_pad(k_ref, v_ref, kpad_ref, vpad_ref, t):
    kpad_ref[0:BAND_ROWS, :] = jnp.zeros((BAND_ROWS, HEAD_DIM), BF16)
    vpad_ref[0:BAND_ROWS, :] = jnp.zeros((BAND_ROWS, HEAD_DIM), BF16)
    kpad_ref[BAND_ROWS:BAND_ROWS + t, :] = k_ref[...]
    vpad_ref[BAND_ROWS:BAND_ROWS + t, :] = v_ref[...]


def _band_groups(q_ref, bias_ref, o_ref, kpad_ref, vpad_ref):
    s_idx = pl.program_id(2)
    col = lax.broadcasted_iota(jnp.int32, (BAND_TQ, BAND_WIN), 1)
    for gg in range(BAND_STEP_GROUPS):
        g = s_idx * BAND_STEP_GROUPS + gg
        start = pl.multiple_of(g * BAND_TQ, BAND_TQ)
        rows = slice(gg * BAND_TQ, (gg + 1) * BAND_TQ)
        s = _nt_dot(q_ref[rows, :], kpad_ref[pl.ds(start, BAND_WIN), :]) + bias_ref[...]
        s = jnp.where(col + g * BAND_TQ >= BAND_ROWS, s, NEG_INF)
        o_ref[rows, :] = _softmax2_pv([(s, vpad_ref[pl.ds(start, BAND_WIN), :])]).astype(BF16)


def _prompt_attn_kernel(qs_ref, ks_ref, vs_ref, negu2_ref, qb_ref, kb_ref, vb_ref, bias_ref,
                        osb_ref, obd_ref, acc_ref, carry_ref, kpads_ref, vpads_ref, kpadb_ref, vpadb_ref, *, t):
    _sb_prompt_body(qs_ref, ks_ref, vs_ref, negu2_ref, osb_ref, acc_ref, carry_ref, kpads_ref, vpads_ref, t=t,
                    at_first_step=lambda: _band_pad(kb_ref, vb_ref, kpadb_ref, vpadb_ref, t),
                    beside_diagonal=lambda: _band_groups(qb_ref, bias_ref, obd_ref, kpadb_ref, vpadb_ref))


def _prompt_attention(yqg3, ykv3, bias_tbl, negu2):
    b, t, _ = yqg3.shape
    tq = SB_TQ
    assert SB_TQ == BAND_TQ * BAND_STEP_GROUPS and SB_TQ // SB_TK >= 3 and SB_HEADS == BAND_HEADS
    q_spec = lambda c0: pl.BlockSpec((None, tq, HEAD_DIM),
                                     functools.partial(lambda b, h, i, c: (b, i, c + h), c=c0 // HEAD_DIM))
    kv_spec = lambda c0: pl.BlockSpec((None, t, HEAD_DIM),
                                      functools.partial(lambda b, h, i, c: (b, 0, c + h), c=c0 // HEAD_DIM))
    out_spec = pl.BlockSpec((None, tq, HEAD_DIM), lambda b, h, i: (b, i, h))
    kern = functools.partial(_prompt_attn_kernel, t=t)
    return pl.pallas_call(
        kern,
        grid=(b, SB_HEADS, t // tq),
        in_specs=[
            q_spec(QG_SB_Q), kv_spec(KV_SB_K), kv_spec(KV_SB_V),
            pl.BlockSpec((2 * SB_TK, SB_TK), lambda b, h, i: (0, 0)),
            q_spec(QG_BD_Q), kv_spec(KV_BD_K), kv_spec(KV_BD_V),
            pl.BlockSpec((None, BAND_TQ, BAND_WIN), lambda b, h, i: (h, 0, 0)),
        ],
        out_specs=[out_spec, out_spec],
        out_shape=[jax.ShapeDtypeStruct((b, t, SB_WIDTH), BF16), jax.ShapeDtypeStruct((b, t, BAND_WIDTH), BF16)],
        scratch_shapes=[pltpu.VMEM((SB_TQ, HEAD_DIM), F32), pltpu.VMEM((SB_TQ, 1), F32),
                        pltpu.VMEM((SB_TK + t, HEAD_DIM), BF16), pltpu.VMEM((SB_TK + t, HEAD_DIM), BF16),
                        pltpu.VMEM((BAND_ROWS + t, HEAD_DIM), BF16), pltpu.VMEM((BAND_ROWS + t, HEAD_DIM), BF16)],
        compiler_params=_params(3, vmem=MAX_VMEM_LIMIT_BYTES),
        name="prompt_attention",
    )(yqg3, ykv3, ykv3, negu2, yqg3, ykv3, ykv3, bias_tbl)


def _band_decode_kernel(q_ref, kn_ref, vn_ref, kc_ref, vc_ref, bias_ref, o_ref, kpad_ref, vpad_ref,
                        *, n_new, r_band):
    kpad_ref[...] = jnp.zeros_like(kpad_ref)
    vpad_ref[...] = jnp.zeros_like(vpad_ref)
    for h in range(BAND_HEADS):
        cols = slice(h * HEAD_DIM, (h + 1) * HEAD_DIM)
        kpad_ref[h, 0:n_new, :] = kn_ref[:, cols]
        vpad_ref[h, 0:n_new, :] = vn_ref[:, cols]
    for h in range(BAND_HEADS):
        cols = slice(h * HEAD_DIM, (h + 1) * HEAD_DIM)
        q = q_ref[:, cols]
        s_cache = _nt_dot(q, kc_ref[h].astype(BF16)) + bias_ref[h, :, 0:r_band]
        s_new = _nt_dot(q, kpad_ref[h]) + bias_ref[h, :, r_band:r_band + NEW_PAD]
        o_ref[:, cols] = _softmax2_pv([(s_cache, vc_ref[h].astype(BF16)),
                                       (s_new, vpad_ref[h])]).astype(BF16)


def _mem_attn_kernel(q_ref, mk_ref, mv_ref, o_ref):
    per_head = len(mk_ref.shape) == 3
    for h in range(MEM_HEADS):
        sl = slice(h * HEAD_DIM, (h + 1) * HEAD_DIM)
        mk = mk_ref[:, h, :] if per_head else mk_ref[:, sl]
        mv = mv_ref[:, h, :] if per_head else mv_ref[:, sl]
        s = _nt_dot(q_ref[:, sl], mk.astype(BF16))
        o_ref[:, sl] = _softmax2_pv([(s, mv.astype(BF16))]).astype(BF16)


def _mem_attention(y3, mk, mv, *, tq):
    b, t, _ = y3.shape
    n_mem = mk.shape[1]
    qb = MG_MM_Q // MEM_WIDTH
    return pl.pallas_call(
        _mem_attn_kernel,
        grid=(b, t // tq),
        in_specs=[
            pl.BlockSpec((None, tq, MEM_WIDTH), lambda b, i: (b, i, qb)),
            pl.BlockSpec((None, n_mem, MEM_WIDTH), lambda b, i: (b, 0, 0)),
            pl.BlockSpec((None, n_mem, MEM_WIDTH), lambda b, i: (b, 0, 0)),
        ],
        out_specs=pl.BlockSpec((None, tq, MEM_WIDTH), lambda b, i: (b, i, 0)),
        out_shape=jax.ShapeDtypeStruct((b, t, MEM_WIDTH), BF16),
        compiler_params=_params(2),
        name="mem_attention",
    )(y3, mk, mv)


def _decode_attn_kernel(qs_ref, kns_ref, vns_ref, kcs_hbm, vcs_hbm, negu2_ref,
                        qb_ref, knb_ref, vnb_ref, kcb_ref, vcb_ref, bias_ref, qm_ref, mk_ref, mv_ref,
                        osb_ref, obd_ref, omm_ref,
                        acc_ref, carry_ref, kpad_ref, vpad_ref, kbuf_ref, vbuf_ref, sem, kpadb_ref, vpadb_ref,
                        *, n_new, n_blocks, r_band):
    def band_and_memory():
        _band_decode_kernel(qb_ref, knb_ref, vnb_ref, kcb_ref, vcb_ref, bias_ref, obd_ref,
                            kpadb_ref, vpadb_ref, n_new=n_new, r_band=r_band)
        _mem_attn_kernel(qm_ref, mk_ref, mv_ref, omm_ref)

    _sb_decode_body(qs_ref, kns_ref, vns_ref, kcs_hbm, vcs_hbm, negu2_ref, osb_ref,
                    acc_ref, carry_ref, kpad_ref, vpad_ref, kbuf_ref, vbuf_ref, sem,
                    n_new=n_new, n_blocks=n_blocks, overlap=band_and_memory)


def _decode_attention(yqg3, ykv3, ymg3, cache_sb_k, cache_sb_v, cache_bd_k, cache_bd_v, mk, mv, bias_tbl, negu2):
    bd, n_new, _ = yqg3.shape
    past = cache_sb_k.shape[2]
    r_band = cache_bd_k.shape[2]
    n_mem = mk.shape[1]
    assert past % SB_TK == 0 and n_new <= NEW_PAD and n_new % 16 == 0
    kern = functools.partial(_decode_attn_kernel, n_new=n_new, n_blocks=past // SB_TK, r_band=r_band)
    slab = lambda width, col0: pl.BlockSpec((None, n_new, width), functools.partial(lambda b, c: (b, 0, c), c=col0 // width))
    band_cache = pl.BlockSpec((None, BAND_HEADS, r_band, HEAD_DIM), lambda b: (b, 0, 0, 0))
    mem_cache = pl.BlockSpec((None, n_mem, MEM_HEADS, HEAD_DIM), lambda b: (b, 0, 0, 0))
    return pl.pallas_call(
        kern,
        grid=(bd,),
        in_specs=[
            slab(SB_WIDTH, QG_SB_Q), slab(SB_WIDTH, KV_SB_K), slab(SB_WIDTH, KV_SB_V),
            pl.BlockSpec(memory_space=pl.ANY), pl.BlockSpec(memory_space=pl.ANY),
            pl.BlockSpec((2 * SB_TK, SB_TK), lambda b: (0, 0)),
            slab(BAND_WIDTH, QG_BD_Q), slab(BAND_WIDTH, KV_BD_K), slab(BAND_WIDTH, KV_BD_V),
            band_cache, band_cache,
            pl.BlockSpec((BAND_HEADS, n_new, r_band + NEW_PAD), lambda b: (0, 0, 0)),
            slab(MEM_WIDTH, MG_MM_Q), mem_cache, mem_cache,
        ],
        out_specs=[pl.BlockSpec((None, n_new, SB_WIDTH), lambda b: (b, 0, 0)),
                   pl.BlockSpec((None, n_new, BAND_WIDTH), lambda b: (b, 0, 0)),
                   pl.BlockSpec((None, n_new, MEM_WIDTH), lambda b: (b, 0, 0))],
        out_shape=[jax.ShapeDtypeStruct((bd, n_new, SB_WIDTH), BF16),
                   jax.ShapeDtypeStruct((bd, n_new, BAND_WIDTH), BF16),
                   jax.ShapeDtypeStruct((bd, n_new, MEM_WIDTH), BF16)],
        scratch_shapes=[pltpu.VMEM((SB_HEADS * n_new, HEAD_DIM), F32),
                        pltpu.VMEM((SB_HEADS * n_new, 1), F32),
                        pltpu.VMEM((SB_HEADS, NEW_PAD, HEAD_DIM), BF16),
                        pltpu.VMEM((SB_HEADS, NEW_PAD, HEAD_DIM), BF16),
                        pltpu.VMEM((2, SB_HEADS, SB_TK, HEAD_DIM), F32),
                        pltpu.VMEM((2, SB_HEADS, SB_TK, HEAD_DIM), F32),
                        pltpu.SemaphoreType.DMA((2, 2)),
                        pltpu.VMEM((BAND_HEADS, NEW_PAD, HEAD_DIM), BF16),
                        pltpu.VMEM((BAND_HEADS, NEW_PAD, HEAD_DIM), BF16)],
        compiler_params=_params(1),
        name="decode_attention",
    )(yqg3, ykv3, ykv3, cache_sb_k, cache_sb_v, negu2, yqg3, ykv3, ykv3, cache_bd_k, cache_bd_v, bias_tbl,
      ymg3, mk, mv)


def _silu_of_half(h):
    return h + h * jnp.tanh(h)


def _merge_kernel(osb_ref, obd_ref, omm_ref, gsb_ref, gbd_ref, gmm_ref,
                  mg0_ref, mg1_ref, mg2_ref, mg3_ref, mg4_ref, mg5_ref,
                  wsb_ref, wbd_ref, wmm_ref, merged_ref, *, half):
    u_sb = (osb_ref[...].astype(F32) * _silu_of_half(gsb_ref[...].astype(F32))).astype(BF16)
    u_bd = (obd_ref[...].astype(F32) * _silu_of_half(gbd_ref[...].astype(F32))).astype(BF16)
    u_mm = (omm_ref[...].astype(F32) * _silu_of_half(gmm_ref[...].astype(F32))).astype(BF16)
    mg = ((mg0_ref, mg2_ref, mg4_ref), (mg1_ref, mg3_ref, mg5_ref))
    for n in range(2):
        cols = slice(n * half, (n + 1) * half)
        merged = None
        for m_ref, u, w_ref in zip(mg[n], (u_sb, u_bd, u_mm), (wsb_ref, wbd_ref, wmm_ref)):
            a = _dot(u, w_ref[:, cols])
            term = a + a * jnp.tanh(m_ref[...].astype(F32))
            merged = term if merged is None else merged + term
        merged_ref[:, cols] = merged.astype(BF16)


def _merge_branches(yqg, ymg, o_sb, o_bd, o_mm, w_sb, w_bd, w_mm, *, tm):
    m = yqg.shape[0]
    d = w_sb.shape[1]
    half = d // 2
    assert MG_MG % half == 0
    mgb = MG_MG // half
    const = dict(pipeline_mode=pl.Buffered(1))
    kern = functools.partial(_merge_kernel, half=half)
    return pl.pallas_call(
        kern,
        grid=(m // tm,),
        in_specs=[
            pl.BlockSpec((tm, SB_WIDTH), lambda i: (i, 0)),
            pl.BlockSpec((tm, BAND_WIDTH), lambda i: (i, 0)),
            pl.BlockSpec((tm, MEM_WIDTH), lambda i: (i, 0)),
            pl.BlockSpec((tm, SB_WIDTH), lambda i: (i, QG_SB_G // SB_WIDTH)),
            pl.BlockSpec((tm, BAND_WIDTH), lambda i: (i, QG_BD_G // BAND_WIDTH)),
            pl.BlockSpec((tm, MEM_WIDTH), lambda i: (i, MG_MM_G // MEM_WIDTH)),
        ] + [pl.BlockSpec((tm, half), functools.partial(lambda i, c: (i, c), c=mgb + c)) for c in range(6)] + [
            pl.BlockSpec((SB_WIDTH, d), lambda i: (0, 0), **const),
            pl.BlockSpec((BAND_WIDTH, d), lambda i: (0, 0), **const),
            pl.BlockSpec((MEM_WIDTH, d), lambda i: (0, 0), **const),
        ],
        out_specs=pl.BlockSpec((tm, d), lambda i: (i, 0)),
        out_shape=jax.ShapeDtypeStruct((m, d), BF16),
        compiler_params=_params(1),
        name="merge_branches",
    )(o_sb, o_bd, o_mm, yqg, yqg, ymg, *([ymg] * 6), w_sb, w_bd, w_mm)


def _out_proj_kernel(x_ref, merged_ref, wout_ref, gpost_ref, y_ref):
    y = _dot(merged_ref[...], wout_ref[...])
    ms = jnp.mean(y * y, axis=-1, keepdims=True)
    y_ref[...] = x_ref[...] + (y * lax.rsqrt(ms + RMS_EPS)) * gpost_ref[...]


def _merge_out_kernel(x_ref, osb_ref, obd_ref, omm_ref, gsb_ref, gbd_ref, gmm_ref,
                      mg0_ref, mg1_ref, mg2_ref, mg3_ref, mg4_ref, mg5_ref,
                      wsb_ref, wbd_ref, wmm_ref, wout_ref, gpost_ref, y_ref, merged_ref, *, half):
    _merge_kernel(osb_ref, obd_ref, omm_ref, gsb_ref, gbd_ref, gmm_ref,
                  mg0_ref, mg1_ref, mg2_ref, mg3_ref, mg4_ref, mg5_ref,
                  wsb_ref, wbd_ref, wmm_ref, merged_ref, half=half)
    _out_proj_kernel(x_ref, merged_ref, wout_ref, gpost_ref, y_ref)


def _merge_out(x2d, yqg, ymg, o_sb, o_bd, o_mm, w_sb, w_bd, w_mm, w_out, g_post):
    m, d = x2d.shape
    half = d // 2
    mgb = MG_MG // half
    whole = lambda shape: pl.BlockSpec(shape, lambda i: (0,) * len(shape))
    col = lambda width, c: pl.BlockSpec((m, width), functools.partial(lambda i, c: (0, c), c=c))
    kern = functools.partial(_merge_out_kernel, half=half)
    return pl.pallas_call(
        kern,
        grid=(1,),
        in_specs=[
            whole((m, d)), whole((m, SB_WIDTH)), whole((m, BAND_WIDTH)), whole((m, MEM_WIDTH)),
            col(SB_WIDTH, QG_SB_G // SB_WIDTH), col(BAND_WIDTH, QG_BD_G // BAND_WIDTH),
            col(MEM_WIDTH, MG_MM_G // MEM_WIDTH),
        ] + [col(half, mgb + c) for c in range(6)] + [
            whole((SB_WIDTH, d)), whole((BAND_WIDTH, d)), whole((MEM_WIDTH, d)), whole((d, d)), whole((1, d)),
        ],
        out_specs=whole((m, d)),
        out_shape=jax.ShapeDtypeStruct((m, d), F32),
        scratch_shapes=[pltpu.VMEM((m, d), BF16)],
        compiler_params=_params(1),
        name="merge_out",
    )(x2d, o_sb, o_bd, o_mm, yqg, yqg, ymg, *([ymg] * 6), w_sb, w_bd, w_mm, w_out, g_post.reshape(1, d))


def _out_projection(x2d, merged, w_out, g_post, *, tm):
    m, d = x2d.shape
    return pl.pallas_call(
        _out_proj_kernel,
        grid=(m // tm,),
        in_specs=[
            pl.BlockSpec((tm, d), lambda i: (i, 0)),
            pl.BlockSpec((tm, d), lambda i: (i, 0)),
            pl.BlockSpec((d, d), lambda i: (0, 0), pipeline_mode=pl.Buffered(1)),
            pl.BlockSpec((1, d), lambda i: (0, 0)),
        ],
        out_specs=pl.BlockSpec((tm, d), lambda i: (i, 0)),
        out_shape=jax.ShapeDtypeStruct((m, d), F32),
        compiler_params=_params(1),
        name="out_projection",
    )(x2d, merged, w_out, g_post.reshape(1, d))


def _head_major(a):
    return jnp.transpose(a, (0, 2, 1, 3))


def kernel(x_prompt, x_sample, cache_sb_k, cache_sb_v, cache_band_k, cache_band_v, cache_mem_k, cache_mem_v, mem_prompt, g_pre, w_in, rel_bias, g_mem, w_mem_kv, w_up_sb, w_up_band, w_up_mem, w_out, g_post):
    depth = w_in.shape[0]
    b, t, d = x_prompt.shape
    bd, n_new, _ = x_sample.shape
    n_mem = mem_prompt.shape[1]
    r_band = cache_band_k.shape[2]
    in_width = w_in.shape[2]
    band_keep = min(BAND_ROWS, t)
    assert COL_MG + 3 * d == in_width
    assert t % SB_TQ == 0 and t % (BAND_TQ * BAND_STEP_GROUPS) == 0 and t % band_keep == 0
    assert r_band == BAND_ROWS and n_new <= CHUNK

    negu2 = jnp.where(jnp.arange(2 * SB_TK)[:, None] % SB_TK >= jnp.arange(SB_TK)[None, :], -1.0, 0.0).astype(BF16)
    mg_width = in_width - COL_MM_Q
    assert COL_MM_Q % MG_TN == 0 and mg_width % MG_TN == 0 and COL_SB_G == 3 * KV_TN and COL_BD_Q == 4 * KV_TN
    cols = jnp.arange(QG_WIDTH)
    qg_scale = jnp.where((cols // SB_WIDTH) % 2 == 0, Q_SCALE, 0.5).astype(F32).reshape(1, QG_WIDTH)
    cols = jnp.arange(mg_width)
    mg_scale = jnp.where(cols < MG_MM_G, Q_SCALE, 0.5).astype(F32).reshape(1, mg_width)
    qg_block = lambda j: j + 2 * ((j + 1) // 2)
    mg_block = lambda j: j + COL_MM_Q // MG_TN

    xp = x_prompt.reshape(b * t, d)
    xs = x_sample.reshape(bd * n_new, d)
    ms = bd * n_new
    outs = [[] for _ in range(10)]
    for l in range(depth):
        w_kvp_b = _kv_weight_bf16(w_in[l])
        w_sb_b = (0.5 * w_up_sb[l]).astype(BF16)
        w_bd_b = (0.5 * w_up_band[l]).astype(BF16)
        w_mm_b = (0.5 * w_up_mem[l]).astype(BF16)
        w_out_b = w_out[l].astype(BF16)
        bias_p, bias_d = _band_bias_tables(rel_bias[l], n_new, r_band)

        ykv, hp, sbk, sbv, bdk, bdv = _kv_projection(xp, g_pre[l], w_kvp_b, tm=band_keep, seqs=1, n_seq=b,
                                                     band_keep=band_keep)
        ykv_s, hs, sbk2, sbv2, bdk2, bdv2 = _kv_projection(xs, g_pre[l], w_kvp_b, tm=ms, seqs=bd, n_seq=bd,
                                                           band_keep=n_new)
        yqg, yqg_s = _col_projection(hp, hs, w_in[l], qg_scale, qg_block, tm=PROJ_TM, tn=KV_TN,
                                     name="qg_projection")
        ymg, ymg_s = _col_projection(hp, hs, w_in[l], mg_scale, mg_block, tm=PROJ_TM, tn=MG_TN,
                                     name="mg_projection")

        mk, mv = _memory_kv(mem_prompt.reshape(b * n_mem, d), g_mem[l], w_mem_kv[l], tm=n_mem)
        yqg3 = yqg.reshape(b, t, QG_WIDTH)
        ykv3 = ykv.reshape(b, t, KV_WIDTH)
        o_sb, o_bd = _prompt_attention(yqg3, ykv3, bias_p, negu2)
        o_mm = _mem_attention(ymg.reshape(b, t, mg_width), mk.reshape(b, n_mem, MEM_WIDTH),
                              mv.reshape(b, n_mem, MEM_WIDTH), tq=MEM_TQ)
        merged = _merge_branches(yqg, ymg, o_sb.reshape(b * t, -1), o_bd.reshape(b * t, -1),
                                 o_mm.reshape(b * t, -1), w_sb_b, w_bd_b, w_mm_b, tm=OUT_TM)
        xp = _out_projection(xp, merged, w_out_b, g_post[l], tm=OUT_TM)
        outs[0].append(_head_major(sbk))
        outs[1].append(_head_major(sbv))
        outs[2].append(_head_major(bdk))
        outs[3].append(_head_major(bdv))
        outs[4].append(mk.reshape(b, n_mem, MEM_HEADS, HEAD_DIM))
        outs[5].append(mv.reshape(b, n_mem, MEM_HEADS, HEAD_DIM))

        o_sb2, o_bd2, o_mm2 = _decode_attention(
            yqg_s.reshape(bd, n_new, QG_WIDTH), ykv_s.reshape(bd, n_new, KV_WIDTH),
            ymg_s.reshape(bd, n_new, mg_width),
            _head_major(cache_sb_k[l]), _head_major(cache_sb_v[l]),
            _head_major(cache_band_k[l]), _head_major(cache_band_v[l]),
            cache_mem_k[l], cache_mem_v[l],
            bias_d, negu2)
        xs = _merge_out(xs, yqg_s, ymg_s, o_sb2.reshape(ms, -1), o_bd2.reshape(ms, -1), o_mm2.reshape(ms, -1),
                        w_sb_b, w_bd_b, w_mm_b, w_out_b, g_post[l])
        outs[6].append(_head_major(sbk2))
        outs[7].append(_head_major(sbv2))
        outs[8].append(_head_major(bdk2))
        outs[9].append(_head_major(bdv2))

    return (xp.reshape(b, t, d), xs.reshape(bd, n_new, d)) + tuple(jnp.stack(o) for o in outs)
```

```python
import functools
import math

import jax
import jax.numpy as jnp
from jax import lax
from jax.experimental import pallas as pl
from jax.experimental.pallas import tpu as pltpu

F32 = jnp.float32
BF16 = jnp.bfloat16

HEAD_DIM = 128
SB_HEADS = 6
BAND_HEADS = 6
MEM_HEADS = 4
SB_WIDTH = SB_HEADS * HEAD_DIM
BAND_WIDTH = BAND_HEADS * HEAD_DIM
MEM_WIDTH = MEM_HEADS * HEAD_DIM
CHUNK = 64
CHUNK_SHIFT = 6
BAND_LEFT_CHUNKS = 8
BAND_ROWS = BAND_LEFT_CHUNKS * CHUNK
MAX_REL = 256
RMS_EPS = 1e-6
NEG_INF = -1e30
LOG2E = math.log2(math.e)
Q_SCALE = HEAD_DIM ** -0.5 * LOG2E

COL_SB_Q = 0
COL_SB_K = COL_SB_Q + SB_WIDTH
COL_SB_V = COL_SB_K + SB_WIDTH
COL_SB_G = COL_SB_V + SB_WIDTH
COL_BD_Q = COL_SB_G + SB_WIDTH
COL_BD_K = COL_BD_Q + BAND_WIDTH
COL_BD_V = COL_BD_K + BAND_WIDTH
COL_BD_G = COL_BD_V + BAND_WIDTH
COL_MM_Q = COL_BD_G + BAND_WIDTH
COL_MM_G = COL_MM_Q + MEM_WIDTH
COL_MG = COL_MM_G + MEM_WIDTH

KV_SB_K = 0
KV_SB_V = KV_SB_K + SB_WIDTH
KV_BD_K = KV_SB_V + SB_WIDTH
KV_BD_V = KV_BD_K + BAND_WIDTH
KV_WIDTH = KV_BD_V + BAND_WIDTH
QG_SB_Q = 0
QG_SB_G = QG_SB_Q + SB_WIDTH
QG_BD_Q = QG_SB_G + SB_WIDTH
QG_BD_G = QG_BD_Q + BAND_WIDTH
QG_WIDTH = QG_BD_G + BAND_WIDTH
MG_MM_Q = 0
MG_MM_G = MG_MM_Q + MEM_WIDTH
MG_MG = MG_MM_G + MEM_WIDTH

VMEM_LIMIT_BYTES = 56 * 1024 * 1024
MAX_VMEM_LIMIT_BYTES = 58 * 1024 * 1024
COMPILER_TEMP_BYTES = 2 * 1024 * 1024
PROJ_TM = 2048
OUT_TM = 512
MEM_TQ = 1024
CAST_ROWS = 256
KV_TN = SB_WIDTH
MG_TN = 1024
SB_TK = 256
SB_TQ = 16 * SB_TK
SB_DEAD = -160.0
BAND_TQ = 4 * CHUNK
BAND_WIN = BAND_TQ + BAND_ROWS
BAND_STEP_GROUPS = 16
BIAS_LANES = 1024
NEW_PAD = 128


def _params(n_axes, vmem=VMEM_LIMIT_BYTES):
    return pltpu.CompilerParams(dimension_semantics=("arbitrary",) * n_axes,
                                vmem_limit_bytes=vmem)


def _nt_dot(a, b):
    return lax.dot_general(a, b, (((1,), (1,)), ((), ())), preferred_element_type=F32)


def _dot(a, b):
    return jnp.dot(a, b, preferred_element_type=F32)


def _pre_norm_to(h_ref, x_ref, g_ref):
    x = x_ref[...]
    ms = jnp.mean(x * x, axis=-1, keepdims=True)
    h_ref[...] = ((x * lax.rsqrt(ms + RMS_EPS)) * g_ref[...]).astype(BF16)


def _kv_project_rows(x_ref, g_ref, w_refs, y_ref, h_ref, f32_refs, seqs, rows):
    _pre_norm_to(h_ref, x_ref, g_ref)
    for group, (w_ref, dst_ref) in enumerate(zip(w_refs, f32_refs)):
        acc = _dot(h_ref[...], w_ref[...])
        y_ref[:, group * KV_TN:(group + 1) * KV_TN] = acc.astype(BF16)
        for h in range(SB_HEADS):
            for s in range(seqs):
                dst_ref[s, h] = acc[s * rows:(s + 1) * rows, h * HEAD_DIM:(h + 1) * HEAD_DIM]


def _kv_proj_kernel(x_ref, xs_ref, g_ref, wsk_ref, wsv_ref, wbk_ref, wbv_ref,
                    y_ref, h_ref, sbk_ref, sbv_ref, bdk_ref, bdv_ref,
                    ys_ref, hs_ref, sbks_ref, sbvs_ref, bdks_ref, bdvs_ref, *, rows, seqs_s, rows_s):
    w_refs = (wsk_ref, wsv_ref, wbk_ref, wbv_ref)

    @pl.when(pl.program_id(0) == 0)
    def _():
        _kv_project_rows(xs_ref, g_ref, w_refs, ys_ref, hs_ref, (sbks_ref, sbvs_ref, bdks_ref, bdvs_ref),
                         seqs_s, rows_s)

    _kv_project_rows(x_ref, g_ref, w_refs, y_ref, h_ref, (sbk_ref, sbv_ref, bdk_ref, bdv_ref), 1, rows)


def _kv_projection(x2d, xs2d, g_pre, w_kv_b, *, n_seq, n_seq_s, band_keep):
    m, d = x2d.shape
    ms = xs2d.shape[0]
    assert KV_WIDTH == 4 * KV_TN and SB_HEADS == BAND_HEADS
    tm = band_keep
    seq_rows = m // n_seq
    blocks_per_seq = seq_rows // tm
    rows_s = ms // n_seq_s
    kern = functools.partial(_kv_proj_kernel, rows=tm, seqs_s=n_seq_s, rows_s=rows_s)
    sb_spec = pl.BlockSpec((1, SB_HEADS, tm, HEAD_DIM), lambda i: (i // blocks_per_seq, 0, i % blocks_per_seq, 0))
    bd_spec = pl.BlockSpec((1, BAND_HEADS, tm, HEAD_DIM), lambda i: (i // blocks_per_seq, 0, 0, 0))
    sb_shape = jax.ShapeDtypeStruct((n_seq, SB_HEADS, seq_rows, HEAD_DIM), F32)
    bd_shape = jax.ShapeDtypeStruct((n_seq, BAND_HEADS, band_keep, HEAD_DIM), F32)
    whole = lambda shape: pl.BlockSpec(shape, lambda i: (0,) * len(shape))
    s_shape = (n_seq_s, SB_HEADS, rows_s, HEAD_DIM)

    def w_spec(group):
        return pl.BlockSpec((d, KV_TN), functools.partial(lambda i, c: (0, c), c=group),
                            pipeline_mode=pl.Buffered(1))

    return pl.pallas_call(
        kern,
        grid=(m // tm,),
        in_specs=[
            pl.BlockSpec((tm, d), lambda i: (i, 0)),
            whole((ms, d)),
            pl.BlockSpec((1, d), lambda i: (0, 0)),
            w_spec(0), w_spec(1), w_spec(2), w_spec(3),
        ],
        out_specs=[pl.BlockSpec((tm, KV_WIDTH), lambda i: (i, 0)), pl.BlockSpec((tm, d), lambda i: (i, 0)),
                   sb_spec, sb_spec, bd_spec, bd_spec,
                   whole((ms, KV_WIDTH)), whole((ms, d)),
                   whole(s_shape), whole(s_shape), whole(s_shape), whole(s_shape)],
        out_shape=[jax.ShapeDtypeStruct((m, KV_WIDTH), BF16), jax.ShapeDtypeStruct((m, d), BF16),
                   sb_shape, sb_shape, bd_shape, bd_shape,
                   jax.ShapeDtypeStruct((ms, KV_WIDTH), BF16), jax.ShapeDtypeStruct((ms, d), BF16)]
                  + [jax.ShapeDtypeStruct(s_shape, F32)] * 4,
        compiler_params=_params(1),
        name="kv_projection",
    )(x2d, xs2d, g_pre.reshape(1, d), w_kv_b, w_kv_b, w_kv_b, w_kv_b)


def _col_proj_kernel(h_ref, hs_ref, w_ref, cs_ref, y_ref, ys_ref, wb_ref):
    @pl.when(pl.program_id(1) == 0)
    def _():
        wb_ref[...] = w_ref[...].astype(BF16)
        ys_ref[...] = (_dot(hs_ref[...], wb_ref[...]) * cs_ref[...]).astype(BF16)

    y_ref[...] = (_dot(h_ref[...], wb_ref[...]) * cs_ref[...]).astype(BF16)


def _col_projection(h2d, hs2d, w_in, col_scale, src_block, *, tm, tn, name):
    m, d = h2d.shape
    ms = hs2d.shape[0]
    n = col_scale.shape[1]
    assert n % tn == 0 and m % tm == 0
    vmem = (2 * tm * d * 2 + 2 * d * tn * 4 + 2 * tm * tn * 2 + d * tn * 2 + 2 * tm * tn * 4
            + 2 * ms * (d + tn) * 2 + COMPILER_TEMP_BYTES)
    vmem = min(vmem, MAX_VMEM_LIMIT_BYTES)
    return pl.pallas_call(
        _col_proj_kernel,
        grid=(n // tn, m // tm),
        in_specs=[
            pl.BlockSpec((tm, d), lambda j, i: (i, 0)),
            pl.BlockSpec((ms, d), lambda j, i: (0, 0)),
            pl.BlockSpec((d, tn), lambda j, i: (0, src_block(j))),
            pl.BlockSpec((1, tn), lambda j, i: (0, j)),
        ],
        out_specs=[pl.BlockSpec((tm, tn), lambda j, i: (i, j)), pl.BlockSpec((ms, tn), lambda j, i: (0, j))],
        out_shape=[jax.ShapeDtypeStruct((m, n), BF16), jax.ShapeDtypeStruct((ms, n), BF16)],
        scratch_shapes=[pltpu.VMEM((d, tn), BF16)],
        compiler_params=_params(2, vmem=vmem),
        name=name,
    )(h2d, hs2d, w_in, col_scale)


def _kv_weight_kernel(a_ref, b_ref, c_ref, d_ref, o_ref):
    for group, w_ref in enumerate((a_ref, b_ref, c_ref, d_ref)):
        o_ref[:, group * KV_TN:(group + 1) * KV_TN] = w_ref[...].astype(BF16)


def _kv_weight_bf16(w):
    d = w.shape[0]
    tr = CAST_ROWS
    assert d % tr == 0

    def spec(col0):
        assert col0 % KV_TN == 0
        return pl.BlockSpec((tr, KV_TN), functools.partial(lambda i, c: (i, c), c=col0 // KV_TN))

    return pl.pallas_call(
        _kv_weight_kernel,
        grid=(d // tr,),
        in_specs=[spec(COL_SB_K), spec(COL_SB_V), spec(COL_BD_K), spec(COL_BD_V)],
        out_specs=pl.BlockSpec((tr, KV_WIDTH), lambda i: (i, 0)),
        out_shape=jax.ShapeDtypeStruct((d, KV_WIDTH), BF16),
        compiler_params=_params(1),
        name="kv_weight_cast",
    )(w, w, w, w)


def _memkv_kernel(x_ref, g_ref, w_ref, mk_ref, mv_ref):
    x = x_ref[...]
    ms = jnp.mean(x * x, axis=-1, keepdims=True)
    h = ((x * lax.rsqrt(ms + RMS_EPS)) * g_ref[...]).astype(BF16)
    acc = _dot(h, w_ref[...].astype(BF16))
    mk_ref[...] = acc[:, :MEM_WIDTH]
    mv_ref[...] = acc[:, MEM_WIDTH:]


def _memory_kv(mem2d, g_mem, w_bf16, *, tm):
    m, d = mem2d.shape
    return pl.pallas_call(
        _memkv_kernel,
        grid=(m // tm,),
        in_specs=[
            pl.BlockSpec((tm, d), lambda i: (i, 0)),
            pl.BlockSpec((1, d), lambda i: (0, 0)),
            pl.BlockSpec((d, 2 * MEM_WIDTH), lambda i: (0, 0)),
        ],
        out_specs=[pl.BlockSpec((tm, MEM_WIDTH), lambda i: (i, 0)),
                   pl.BlockSpec((tm, MEM_WIDTH), lambda i: (i, 0))],
        out_shape=[jax.ShapeDtypeStruct((m, MEM_WIDTH), F32),
                   jax.ShapeDtypeStruct((m, MEM_WIDTH), F32)],
        compiler_params=_params(1),
        name="memory_kv",
    )(mem2d, g_mem.reshape(1, d), w_bf16)


def _neg_suffix_matrix(n):
    row = lax.broadcasted_iota(jnp.int32, (2 * n, n), 0)
    col = lax.broadcasted_iota(jnp.int32, (2 * n, n), 1)
    row = jnp.where(row >= n, row - n, row)
    return jnp.where(row >= col, -1.0, 0.0).astype(BF16)


def _sb_weights(z2, carry2, negu2, mask):
    p = jnp.maximum(z2, 0.0) + jnp.log(1.0 + jnp.exp2(-jnp.abs(z2))) * LOG2E
    if mask is not None:
        p = jnp.where(mask, p, 0.0)
    p_hi = p.astype(BF16)
    p_lo = (p - p_hi.astype(F32)).astype(BF16)
    suffix = _dot(jnp.concatenate([p_hi, p_lo], axis=1), negu2)
    w = jnp.exp2(z2 + suffix + carry2)
    if mask is not None:
        w = jnp.where(mask, w, 0.0)
    return w, carry2 - jnp.sum(p, axis=-1, keepdims=True)


def _sb_prompt_body(q_ref, k_ref, v_ref, negu2_ref, o_ref, acc_ref, carry_ref, kpad_ref, vpad_ref,
                    *, t, at_first_step, beside_diagonal):
    i = pl.program_id(2)
    n_sub = SB_TQ // SB_TK
    negu2 = negu2_ref[...]

    @pl.when(i == 0)
    def _():
        kpad_ref[0:SB_TK, :] = jnp.zeros((SB_TK, HEAD_DIM), BF16)
        vpad_ref[0:SB_TK, :] = jnp.zeros((SB_TK, HEAD_DIM), BF16)
        kpad_ref[SB_TK:SB_TK + t, :] = k_ref[...]
        vpad_ref[SB_TK:SB_TK + t, :] = v_ref[...]
        at_first_step()

    def kv_block(j):
        start = pl.multiple_of((j + 1) * SB_TK, SB_TK)
        return kpad_ref[pl.ds(start, SB_TK), :], vpad_ref[pl.ds(start, SB_TK), :]

    row = lax.broadcasted_iota(jnp.int32, (SB_TK, SB_TK), 0)
    col = lax.broadcasted_iota(jnp.int32, (SB_TK, SB_TK), 1)
    for r in range(n_sub):
        rows = slice(r * SB_TK, (r + 1) * SB_TK)
        s = i * n_sub + r
        q = q_ref[rows, :]
        kb, vb = kv_block(s)
        w, carry = _sb_weights(_nt_dot(q, kb), jnp.zeros((SB_TK, 1), F32), negu2, col < row)
        acc = _dot(w.astype(BF16), vb)
        kb, vb = kv_block(s - 1)
        prev_exists = None if r > 0 else (jnp.zeros((SB_TK, SB_TK), jnp.int32) + i) > 0
        w, carry = _sb_weights(_nt_dot(q, kb), carry, negu2, prev_exists)
        acc_ref[rows, :] = acc + _dot(w.astype(BF16), vb)
        carry_ref[rows, :] = carry
    beside_diagonal()

    row_q = lax.broadcasted_iota(jnp.int32, (SB_TQ, 1), 0)
    row_t = lax.broadcasted_iota(jnp.int32, (SB_TQ, SB_TK), 0)
    has_more = row_q >= (2 - n_sub * i) * SB_TK

    def any_alive(carry):
        return (jnp.max(jnp.where(has_more, carry, NEG_INF)) > SB_DEAD).astype(jnp.int32)

    def cond(state):
        j, alive = state
        return jnp.logical_and(j >= 0, alive > 0)

    def body(state):
        j, _ = state
        kb, vb = kv_block(j)
        visits = row_t >= (j - n_sub * i + 2) * SB_TK
        w, carry = _sb_weights(_nt_dot(q_ref[...], kb), carry_ref[...], negu2, visits)
        acc_ref[...] += _dot(w.astype(BF16), vb)
        carry_ref[...] = carry
        return j - 1, any_alive(carry)

    lax.while_loop(cond, body, (n_sub * i + n_sub - 3, any_alive(carry_ref[...])))
    o_ref[...] = acc_ref[...].astype(BF16)


def _sb_decode_body(q_ref, kn_ref, vn_ref, kc_hbm, vc_hbm, negu2_ref, o_ref,
                    acc_ref, carry_ref, kpad_ref, vpad_ref, kbuf_ref, vbuf_ref, sem,
                    *, n_new, n_blocks, overlap):
    b = pl.program_id(0)
    heads = SB_HEADS

    def cache_copies(j, slot):
        rows = pl.ds(pl.multiple_of((n_blocks - 1 - j) * SB_TK, SB_TK), SB_TK)
        return (pltpu.make_async_copy(kc_hbm.at[b, :, rows, :], kbuf_ref.at[slot], sem.at[0, slot]),
                pltpu.make_async_copy(vc_hbm.at[b, :, rows, :], vbuf_ref.at[slot], sem.at[1, slot]))

    def start_fetch(j, slot):
        for cp in cache_copies(j, slot):
            cp.start()

    def wait_fetch(j, slot):
        for cp in cache_copies(j, slot):
            cp.wait()

    start_fetch(0, 0)
    overlap()

    def head_cols(h):
        return slice(h * HEAD_DIM, (h + 1) * HEAD_DIM)

    def head_rows(h):
        return slice(h * n_new, (h + 1) * n_new)

    def block(k_of, v_of, negu2, mask):
        z2 = jnp.concatenate([_nt_dot(q_ref[:, head_cols(h)], k_of(h)) for h in range(heads)], axis=0)
        w, carry = _sb_weights(z2, carry_ref[...], negu2, mask)
        wb = w.astype(BF16)
        for h in range(heads):
            acc_ref[head_rows(h), :] += _dot(wb[head_rows(h), :], v_of(h))
        carry_ref[...] = carry

    def any_alive():
        return (jnp.max(carry_ref[...]) > SB_DEAD).astype(jnp.int32)

    acc_ref[...] = jnp.zeros_like(acc_ref)
    carry_ref[...] = jnp.zeros_like(carry_ref)
    kpad_ref[...] = jnp.zeros_like(kpad_ref)
    vpad_ref[...] = jnp.zeros_like(vpad_ref)
    for h in range(heads):
        kpad_ref[h, 0:n_new, :] = kn_ref[:, head_cols(h)]
        vpad_ref[h, 0:n_new, :] = vn_ref[:, head_cols(h)]
    row = lax.broadcasted_iota(jnp.int32, (n_new, NEW_PAD), 0)
    col = lax.broadcasted_iota(jnp.int32, (n_new, NEW_PAD), 1)
    mask = jnp.concatenate([(col < row).astype(jnp.int32)] * heads, axis=0) == 1
    block(lambda h: kpad_ref[h], lambda h: vpad_ref[h], _neg_suffix_matrix(NEW_PAD), mask)

    negu2 = negu2_ref[...]

    def cond(state):
        j, alive = state
        return jnp.logical_and(j < n_blocks, alive > 0)

    def body(state):
        j, _ = state
        slot = j % 2
        wait_fetch(j, slot)

        @pl.when(j + 1 < n_blocks)
        def _():
            start_fetch(j + 1, 1 - slot)

        block(lambda h: kbuf_ref[slot, h].astype(BF16), lambda h: vbuf_ref[slot, h].astype(BF16),
              negu2, None)
        return j + 1, any_alive()

    j_end, _ = lax.while_loop(cond, body, (0, any_alive()))

    @pl.when(j_end < n_blocks)
    def _():
        wait_fetch(j_end, j_end % 2)

    for h in range(heads):
        o_ref[:, head_cols(h)] = acc_ref[head_rows(h), :].astype(BF16)


def _softmax2_pv(parts):
    mx = functools.reduce(jnp.maximum, [jnp.max(s, axis=-1, keepdims=True) for s, _ in parts])
    num = None
    den = None
    for s, v in parts:
        p = jnp.exp2(s - mx)
        d = jnp.sum(p, axis=-1, keepdims=True)
        o = _dot(p.astype(BF16), v)
        num = o if num is None else num + o
        den = d if den is None else den + d
    return num / den


def _band_bias_kernel(g_ref, tp_ref, td_ref, *, n_new, r_band):
    x = jnp.broadcast_to(g_ref[...], (BAND_TQ, BIAS_LANES))
    x = pltpu.roll(x, BAND_TQ, 1, stride=1, stride_axis=0)
    tbl = x[:, :BAND_WIN] * LOG2E
    r = lax.broadcasted_iota(jnp.int32, (BAND_TQ, BAND_WIN), 0)
    j = lax.broadcasted_iota(jnp.int32, (BAND_TQ, BAND_WIN), 1)
    dc = (j >> CHUNK_SHIFT) - (r >> CHUNK_SHIFT)
    tp_ref[...] = jnp.where(jnp.logical_and(dc >= 0, dc <= BAND_LEFT_CHUNKS), tbl, NEG_INF)
    jd = lax.broadcasted_iota(jnp.int32, (n_new, r_band + NEW_PAD), 1)
    td_ref[...] = jnp.where(jd < r_band + n_new, tbl[:n_new, :r_band + NEW_PAD], NEG_INF)


def _band_bias_tables(rel_bias, n_new, r_band):
    h = rel_bias.shape[0]
    assert BAND_ROWS == 2 * MAX_REL and r_band == BAND_ROWS and BIAS_LANES == 2 * BAND_ROWS
    rb = rel_bias.astype(F32)
    g = jnp.concatenate([rb[:, :0:-1], jnp.broadcast_to(rb[:, -1:], (h, BIAS_LANES - 2 * MAX_REL))], axis=1)
    kern = functools.partial(_band_bias_kernel, n_new=n_new, r_band=r_band)
    return pl.pallas_call(
        kern,
        grid=(h,),
        in_specs=[pl.BlockSpec((None, 1, BIAS_LANES), lambda i: (i, 0, 0))],
        out_specs=[pl.BlockSpec((None, BAND_TQ, BAND_WIN), lambda i: (i, 0, 0)),
                   pl.BlockSpec((None, n_new, r_band + NEW_PAD), lambda i: (i, 0, 0))],
        out_shape=[jax.ShapeDtypeStruct((h, BAND_TQ, BAND_WIN), F32),
                   jax.ShapeDtypeStruct((h, n_new, r_band + NEW_PAD), F32)],
        compiler_params=_params(1),
        name="band_bias",
    )(g.reshape(h, 1, BIAS_LANES))


def _band_pad(k_ref, v_ref, kpad_ref, vpad_ref, t):
    kpad_ref[0:BAND_ROWS, :] = jnp.zeros((BAND_ROWS, HEAD_DIM), BF16)
    vpad_ref[0:BAND_ROWS, :] = jnp.zeros((BAND_ROWS, HEAD_DIM), BF16)
    kpad_ref[BAND_ROWS:BAND_ROWS + t, :] = k_ref[...]
    vpad_ref[BAND_ROWS:BAND_ROWS + t, :] = v_ref[...]


def _band_groups(q_ref, bias_ref, o_ref, kpad_ref, vpad_ref):
    s_idx = pl.program_id(2)
    col = lax.broadcasted_iota(jnp.int32, (BAND_TQ, BAND_WIN), 1)
    for gg in range(BAND_STEP_GROUPS):
        g = s_idx * BAND_STEP_GROUPS + gg
        start = pl.multiple_of(g * BAND_TQ, BAND_TQ)
        rows = slice(gg * BAND_TQ, (gg + 1) * BAND_TQ)
        s = _nt_dot(q_ref[rows, :], kpad_ref[pl.ds(start, BAND_WIN), :]) + bias_ref[...]
        s = jnp.where(col + g * BAND_TQ >= BAND_ROWS, s, NEG_INF)
        o_ref[rows, :] = _softmax2_pv([(s, vpad_ref[pl.ds(start, BAND_WIN), :])]).astype(BF16)


def _prompt_attn_kernel(qs_ref, ks_ref, vs_ref, negu2_ref, qb_ref, kb_ref, vb_ref, bias_ref,
                        osb_ref, obd_ref, acc_ref, carry_ref, kpads_ref, vpads_ref, kpadb_ref, vpadb_ref, *, t):
    _sb_prompt_body(qs_ref, ks_ref, vs_ref, negu2_ref, osb_ref, acc_ref, carry_ref, kpads_ref, vpads_ref, t=t,
                    at_first_step=lambda: _band_pad(kb_ref, vb_ref, kpadb_ref, vpadb_ref, t),
                    beside_diagonal=lambda: _band_groups(qb_ref, bias_ref, obd_ref, kpadb_ref, vpadb_ref))


def _prompt_attention(yqg3, ykv3, bias_tbl, negu2):
    b, t, _ = yqg3.shape
    tq = SB_TQ
    assert SB_TQ == BAND_TQ * BAND_STEP_GROUPS and SB_TQ // SB_TK >= 3 and SB_HEADS == BAND_HEADS
    q_spec = lambda c0: pl.BlockSpec((None, tq, HEAD_DIM),
                                     functools.partial(lambda b, h, i, c: (b, i, c + h), c=c0 // HEAD_DIM))
    kv_spec = lambda c0: pl.BlockSpec((None, t, HEAD_DIM),
                                      functools.partial(lambda b, h, i, c: (b, 0, c + h), c=c0 // HEAD_DIM))
    out_spec = pl.BlockSpec((None, tq, HEAD_DIM), lambda b, h, i: (b, i, h))
    kern = functools.partial(_prompt_attn_kernel, t=t)
    return pl.pallas_call(
        kern,
        grid=(b, SB_HEADS, t // tq),
        in_specs=[
            q_spec(QG_SB_Q), kv_spec(KV_SB_K), kv_spec(KV_SB_V),
            pl.BlockSpec((2 * SB_TK, SB_TK), lambda b, h, i: (0, 0)),
            q_spec(QG_BD_Q), kv_spec(KV_BD_K), kv_spec(KV_BD_V),
            pl.BlockSpec((None, BAND_TQ, BAND_WIN), lambda b, h, i: (h, 0, 0)),
        ],
        out_specs=[out_spec, out_spec],
        out_shape=[jax.ShapeDtypeStruct((b, t, SB_WIDTH), BF16), jax.ShapeDtypeStruct((b, t, BAND_WIDTH), BF16)],
        scratch_shapes=[pltpu.VMEM((SB_TQ, HEAD_DIM), F32), pltpu.VMEM((SB_TQ, 1), F32),
                        pltpu.VMEM((SB_TK + t, HEAD_DIM), BF16), pltpu.VMEM((SB_TK + t, HEAD_DIM), BF16),
                        pltpu.VMEM((BAND_ROWS + t, HEAD_DIM), BF16), pltpu.VMEM((BAND_ROWS + t, HEAD_DIM), BF16)],
        compiler_params=_params(3, vmem=MAX_VMEM_LIMIT_BYTES),
        name="prompt_attention",
    )(yqg3, ykv3, ykv3, negu2, yqg3, ykv3, ykv3, bias_tbl)


def _band_decode_kernel(q_ref, kn_ref, vn_ref, kc_ref, vc_ref, bias_ref, o_ref, kpad_ref, vpad_ref,
                        *, n_new, r_band):
    kpad_ref[...] = jnp.zeros_like(kpad_ref)
    vpad_ref[...] = jnp.zeros_like(vpad_ref)
    for h in range(BAND_HEADS):
        cols = slice(h * HEAD_DIM, (h + 1) * HEAD_DIM)
        kpad_ref[h, 0:n_new, :] = kn_ref[:, cols]
        vpad_ref[h, 0:n_new, :] = vn_ref[:, cols]
    for h in range(BAND_HEADS):
        cols = slice(h * HEAD_DIM, (h + 1) * HEAD_DIM)
        q = q_ref[:, cols]
        s_cache = _nt_dot(q, kc_ref[h].astype(BF16)) + bias_ref[h, :, 0:r_band]
        s_new = _nt_dot(q, kpad_ref[h]) + bias_ref[h, :, r_band:r_band + NEW_PAD]
        o_ref[:, cols] = _softmax2_pv([(s_cache, vc_ref[h].astype(BF16)),
                                       (s_new, vpad_ref[h])]).astype(BF16)


def _mem_attn_kernel(q_ref, mk_ref, mv_ref, o_ref):
    per_head = len(mk_ref.shape) == 3
    for h in range(MEM_HEADS):
        sl = slice(h * HEAD_DIM, (h + 1) * HEAD_DIM)
        mk = mk_ref[:, h, :] if per_head else mk_ref[:, sl]
        mv = mv_ref[:, h, :] if per_head else mv_ref[:, sl]
        s = _nt_dot(q_ref[:, sl], mk.astype(BF16))
        o_ref[:, sl] = _softmax2_pv([(s, mv.astype(BF16))]).astype(BF16)


def _mem_attention(y3, mk, mv, *, tq):
    b, t, _ = y3.shape
    n_mem = mk.shape[1]
    qb = MG_MM_Q // MEM_WIDTH
    return pl.pallas_call(
        _mem_attn_kernel,
        grid=(b, t // tq),
        in_specs=[
            pl.BlockSpec((None, tq, MEM_WIDTH), lambda b, i: (b, i, qb)),
            pl.BlockSpec((None, n_mem, MEM_WIDTH), lambda b, i: (b, 0, 0)),
            pl.BlockSpec((None, n_mem, MEM_WIDTH), lambda b, i: (b, 0, 0)),
        ],
        out_specs=pl.BlockSpec((None, tq, MEM_WIDTH), lambda b, i: (b, i, 0)),
        out_shape=jax.ShapeDtypeStruct((b, t, MEM_WIDTH), BF16),
        compiler_params=_params(2),
        name="mem_attention",
    )(y3, mk, mv)


def _decode_attn_kernel(qs_ref, kns_ref, vns_ref, kcs_hbm, vcs_hbm, negu2_ref,
                        qb_ref, knb_ref, vnb_ref, kcb_ref, vcb_ref, bias_ref, qm_ref, mk_ref, mv_ref,
                        osb_ref, obd_ref, omm_ref,
                        acc_ref, carry_ref, kpad_ref, vpad_ref, kbuf_ref, vbuf_ref, sem, kpadb_ref, vpadb_ref,
                        *, n_new, n_blocks, r_band):
    def band_and_memory():
        _band_decode_kernel(qb_ref, knb_ref, vnb_ref, kcb_ref, vcb_ref, bias_ref, obd_ref,
                            kpadb_ref, vpadb_ref, n_new=n_new, r_band=r_band)
        _mem_attn_kernel(qm_ref, mk_ref, mv_ref, omm_ref)

    _sb_decode_body(qs_ref, kns_ref, vns_ref, kcs_hbm, vcs_hbm, negu2_ref, osb_ref,
                    acc_ref, carry_ref, kpad_ref, vpad_ref, kbuf_ref, vbuf_ref, sem,
                    n_new=n_new, n_blocks=n_blocks, overlap=band_and_memory)


def _decode_attention(yqg3, ykv3, ymg3, cache_sb_k, cache_sb_v, cache_bd_k, cache_bd_v, mk, mv, bias_tbl, negu2):
    bd, n_new, _ = yqg3.shape
    past = cache_sb_k.shape[2]
    r_band = cache_bd_k.shape[2]
    n_mem = mk.shape[1]
    assert past % SB_TK == 0 and n_new <= NEW_PAD and n_new % 16 == 0
    kern = functools.partial(_decode_attn_kernel, n_new=n_new, n_blocks=past // SB_TK, r_band=r_band)
    slab = lambda width, col0: pl.BlockSpec((None, n_new, width), functools.partial(lambda b, c: (b, 0, c), c=col0 // width))
    band_cache = pl.BlockSpec((None, BAND_HEADS, r_band, HEAD_DIM), lambda b: (b, 0, 0, 0))
    mem_cache = pl.BlockSpec((None, n_mem, MEM_HEADS, HEAD_DIM), lambda b: (b, 0, 0, 0))
    return pl.pallas_call(
        kern,
        grid=(bd,),
        in_specs=[
            slab(SB_WIDTH, QG_SB_Q), slab(SB_WIDTH, KV_SB_K), slab(SB_WIDTH, KV_SB_V),
            pl.BlockSpec(memory_space=pl.ANY), pl.BlockSpec(memory_space=pl.ANY),
            pl.BlockSpec((2 * SB_TK, SB_TK), lambda b: (0, 0)),
            slab(BAND_WIDTH, QG_BD_Q), slab(BAND_WIDTH, KV_BD_K), slab(BAND_WIDTH, KV_BD_V),
            band_cache, band_cache,
            pl.BlockSpec((BAND_HEADS, n_new, r_band + NEW_PAD), lambda b: (0, 0, 0)),
            slab(MEM_WIDTH, MG_MM_Q), mem_cache, mem_cache,
        ],
        out_specs=[pl.BlockSpec((None, n_new, SB_WIDTH), lambda b: (b, 0, 0)),
                   pl.BlockSpec((None, n_new, BAND_WIDTH), lambda b: (b, 0, 0)),
                   pl.BlockSpec((None, n_new, MEM_WIDTH), lambda b: (b, 0, 0))],
        out_shape=[jax.ShapeDtypeStruct((bd, n_new, SB_WIDTH), BF16),
                   jax.ShapeDtypeStruct((bd, n_new, BAND_WIDTH), BF16),
                   jax.ShapeDtypeStruct((bd, n_new, MEM_WIDTH), BF16)],
        scratch_shapes=[pltpu.VMEM((SB_HEADS * n_new, HEAD_DIM), F32),
                        pltpu.VMEM((SB_HEADS * n_new, 1), F32),
                        pltpu.VMEM((SB_HEADS, NEW_PAD, HEAD_DIM), BF16),
                        pltpu.VMEM((SB_HEADS, NEW_PAD, HEAD_DIM), BF16),
                        pltpu.VMEM((2, SB_HEADS, SB_TK, HEAD_DIM), F32),
                        pltpu.VMEM((2, SB_HEADS, SB_TK, HEAD_DIM), F32),
                        pltpu.SemaphoreType.DMA((2, 2)),
                        pltpu.VMEM((BAND_HEADS, NEW_PAD, HEAD_DIM), BF16),
                        pltpu.VMEM((BAND_HEADS, NEW_PAD, HEAD_DIM), BF16)],
        compiler_params=_params(1),
        name="decode_attention",
    )(yqg3, ykv3, ykv3, cache_sb_k, cache_sb_v, negu2, yqg3, ykv3, ykv3, cache_bd_k, cache_bd_v, bias_tbl,
      ymg3, mk, mv)


def _silu_of_half(h):
    return h + h * jnp.tanh(h)


def _merge_kernel(osb_ref, obd_ref, omm_ref, gsb_ref, gbd_ref, gmm_ref,
                  mg0_ref, mg1_ref, mg2_ref, mg3_ref, mg4_ref, mg5_ref,
                  wsb_ref, wbd_ref, wmm_ref, merged_ref, *, half):
    u_sb = (osb_ref[...].astype(F32) * _silu_of_half(gsb_ref[...].astype(F32))).astype(BF16)
    u_bd = (obd_ref[...].astype(F32) * _silu_of_half(gbd_ref[...].astype(F32))).astype(BF16)
    u_mm = (omm_ref[...].astype(F32) * _silu_of_half(gmm_ref[...].astype(F32))).astype(BF16)
    mg = ((mg0_ref, mg2_ref, mg4_ref), (mg1_ref, mg3_ref, mg5_ref))
    for n in range(2):
        cols = slice(n * half, (n + 1) * half)
        merged = None
        for m_ref, u, w_ref in zip(mg[n], (u_sb, u_bd, u_mm), (wsb_ref, wbd_ref, wmm_ref)):
            a = _dot(u, w_ref[:, cols])
            term = a + a * jnp.tanh(m_ref[...].astype(F32))
            merged = term if merged is None else merged + term
        merged_ref[:, cols] = merged.astype(BF16)


def _merge_branches(yqg, ymg, o_sb, o_bd, o_mm, w_sb, w_bd, w_mm, *, tm):
    m = yqg.shape[0]
    d = w_sb.shape[1]
    half = d // 2
    assert MG_MG % half == 0
    mgb = MG_MG // half
    const = dict(pipeline_mode=pl.Buffered(1))
    kern = functools.partial(_merge_kernel, half=half)
    return pl.pallas_call(
        kern,
        grid=(m // tm,),
        in_specs=[
            pl.BlockSpec((tm, SB_WIDTH), lambda i: (i, 0)),
            pl.BlockSpec((tm, BAND_WIDTH), lambda i: (i, 0)),
            pl.BlockSpec((tm, MEM_WIDTH), lambda i: (i, 0)),
            pl.BlockSpec((tm, SB_WIDTH), lambda i: (i, QG_SB_G // SB_WIDTH)),
            pl.BlockSpec((tm, BAND_WIDTH), lambda i: (i, QG_BD_G // BAND_WIDTH)),
            pl.BlockSpec((tm, MEM_WIDTH), lambda i: (i, MG_MM_G // MEM_WIDTH)),
        ] + [pl.BlockSpec((tm, half), functools.partial(lambda i, c: (i, c), c=mgb + c)) for c in range(6)] + [
            pl.BlockSpec((SB_WIDTH, d), lambda i: (0, 0), **const),
            pl.BlockSpec((BAND_WIDTH, d), lambda i: (0, 0), **const),
            pl.BlockSpec((MEM_WIDTH, d), lambda i: (0, 0), **const),
        ],
        out_specs=pl.BlockSpec((tm, d), lambda i: (i, 0)),
        out_shape=jax.ShapeDtypeStruct((m, d), BF16),
        compiler_params=_params(1),
        name="merge_branches",
    )(o_sb, o_bd, o_mm, yqg, yqg, ymg, *([ymg] * 6), w_sb, w_bd, w_mm)


def _out_proj_kernel(x_ref, merged_ref, wout_ref, gpost_ref, y_ref):
    y = _dot(merged_ref[...], wout_ref[...])
    ms = jnp.mean(y * y, axis=-1, keepdims=True)
    y_ref[...] = x_ref[...] + (y * lax.rsqrt(ms + RMS_EPS)) * gpost_ref[...]


def _merge_out_kernel(x_ref, osb_ref, obd_ref, omm_ref, gsb_ref, gbd_ref, gmm_ref,
                      mg0_ref, mg1_ref, mg2_ref, mg3_ref, mg4_ref, mg5_ref,
                      wsb_ref, wbd_ref, wmm_ref, wout_ref, gpost_ref, y_ref, merged_ref, *, half):
    _merge_kernel(osb_ref, obd_ref, omm_ref, gsb_ref, gbd_ref, gmm_ref,
                  mg0_ref, mg1_ref, mg2_ref, mg3_ref, mg4_ref, mg5_ref,
                  wsb_ref, wbd_ref, wmm_ref, merged_ref, half=half)
    _out_proj_kernel(x_ref, merged_ref, wout_ref, gpost_ref, y_ref)


def _merge_out(x2d, yqg, ymg, o_sb, o_bd, o_mm, w_sb, w_bd, w_mm, w_out, g_post):
    m, d = x2d.shape
    half = d // 2
    mgb = MG_MG // half
    whole = lambda shape: pl.BlockSpec(shape, lambda i: (0,) * len(shape))
    col = lambda width, c: pl.BlockSpec((m, width), functools.partial(lambda i, c: (0, c), c=c))
    kern = functools.partial(_merge_out_kernel, half=half)
    return pl.pallas_call(
        kern,
        grid=(1,),
        in_specs=[
            whole((m, d)), whole((m, SB_WIDTH)), whole((m, BAND_WIDTH)), whole((m, MEM_WIDTH)),
            col(SB_WIDTH, QG_SB_G // SB_WIDTH), col(BAND_WIDTH, QG_BD_G // BAND_WIDTH),
            col(MEM_WIDTH, MG_MM_G // MEM_WIDTH),
        ] + [col(half, mgb + c) for c in range(6)] + [
            whole((SB_WIDTH, d)), whole((BAND_WIDTH, d)), whole((MEM_WIDTH, d)), whole((d, d)), whole((1, d)),
        ],
        out_specs=whole((m, d)),
        out_shape=jax.ShapeDtypeStruct((m, d), F32),
        scratch_shapes=[pltpu.VMEM((m, d), BF16)],
        compiler_params=_params(1),
        name="merge_out",
    )(x2d, o_sb, o_bd, o_mm, yqg, yqg, ymg, *([ymg] * 6), w_sb, w_bd, w_mm, w_out, g_post.reshape(1, d))


def _out_projection(x2d, merged, w_out, g_post, *, tm):
    m, d = x2d.shape
    return pl.pallas_call(
        _out_proj_kernel,
        grid=(m // tm,),
        in_specs=[
            pl.BlockSpec((tm, d), lambda i: (i, 0)),
            pl.BlockSpec((tm, d), lambda i: (i, 0)),
            pl.BlockSpec((d, d), lambda i: (0, 0), pipeline_mode=pl.Buffered(1)),
            pl.BlockSpec((1, d), lambda i: (0, 0)),
        ],
        out_specs=pl.BlockSpec((tm, d), lambda i: (i, 0)),
        out_shape=jax.ShapeDtypeStruct((m, d), F32),
        compiler_params=_params(1),
        name="out_projection",
    )(x2d, merged, w_out, g_post.reshape(1, d))


def _head_major(a):
    return jnp.transpose(a, (0, 2, 1, 3))


def kernel(x_prompt, x_sample, cache_sb_k, cache_sb_v, cache_band_k, cache_band_v, cache_mem_k, cache_mem_v, mem_prompt, g_pre, w_in, rel_bias, g_mem, w_mem_kv, w_up_sb, w_up_band, w_up_mem, w_out, g_post):
    depth = w_in.shape[0]
    b, t, d = x_prompt.shape
    bd, n_new, _ = x_sample.shape
    n_mem = mem_prompt.shape[1]
    r_band = cache_band_k.shape[2]
    in_width = w_in.shape[2]
    band_keep = min(BAND_ROWS, t)
    assert COL_MG + 3 * d == in_width
    assert t % SB_TQ == 0 and t % (BAND_TQ * BAND_STEP_GROUPS) == 0 and t % band_keep == 0
    assert r_band == BAND_ROWS and n_new <= CHUNK

    negu2 = jnp.where(jnp.arange(2 * SB_TK)[:, None] % SB_TK >= jnp.arange(SB_TK)[None, :], -1.0, 0.0).astype(BF16)
    mg_width = in_width - COL_MM_Q
    assert COL_MM_Q % MG_TN == 0 and mg_width % MG_TN == 0 and COL_SB_G == 3 * KV_TN and COL_BD_Q == 4 * KV_TN
    cols = jnp.arange(QG_WIDTH)
    qg_scale = jnp.where((cols // SB_WIDTH) % 2 == 0, Q_SCALE, 0.5).astype(F32).reshape(1, QG_WIDTH)
    cols = jnp.arange(mg_width)
    mg_scale = jnp.where(cols < MG_MM_G, Q_SCALE, 0.5).astype(F32).reshape(1, mg_width)
    qg_block = lambda j: j + 2 * ((j + 1) // 2)
    mg_block = lambda j: j + COL_MM_Q // MG_TN

    xp = x_prompt.reshape(b * t, d)
    xs = x_sample.reshape(bd * n_new, d)
    ms = bd * n_new
    outs = [[] for _ in range(10)]
    for l in range(depth):
        w_kvp_b = _kv_weight_bf16(w_in[l])
        w_sb_b = (0.5 * w_up_sb[l]).astype(BF16)
        w_bd_b = (0.5 * w_up_band[l]).astype(BF16)
        w_mm_b = (0.5 * w_up_mem[l]).astype(BF16)
        w_out_b = w_out[l].astype(BF16)
        bias_p, bias_d = _band_bias_tables(rel_bias[l], n_new, r_band)

        (ykv, hp, sbk, sbv, bdk, bdv, ykv_s, hs, sbk2, sbv2, bdk2, bdv2) = _kv_projection(
            xp, xs, g_pre[l], w_kvp_b, n_seq=b, n_seq_s=bd, band_keep=band_keep)
        yqg, yqg_s = _col_projection(hp, hs, w_in[l], qg_scale, qg_block, tm=PROJ_TM, tn=KV_TN,
                                     name="qg_projection")
        ymg, ymg_s = _col_projection(hp, hs, w_in[l], mg_scale, mg_block, tm=PROJ_TM, tn=MG_TN,
                                     name="mg_projection")

        mk, mv = _memory_kv(mem_prompt.reshape(b * n_mem, d), g_mem[l], w_mem_kv[l], tm=n_mem)
        yqg3 = yqg.reshape(b, t, QG_WIDTH)
        ykv3 = ykv.reshape(b, t, KV_WIDTH)
        o_sb, o_bd = _prompt_attention(yqg3, ykv3, bias_p, negu2)
        o_mm = _mem_attention(ymg.reshape(b, t, mg_width), mk.reshape(b, n_mem, MEM_WIDTH),
                              mv.reshape(b, n_mem, MEM_WIDTH), tq=MEM_TQ)
        merged = _merge_branches(yqg, ymg, o_sb.reshape(b * t, -1), o_bd.reshape(b * t, -1),
                                 o_mm.reshape(b * t, -1), w_sb_b, w_bd_b, w_mm_b, tm=OUT_TM)
        xp = _out_projection(xp, merged, w_out_b, g_post[l], tm=OUT_TM)
        outs[0].append(_head_major(sbk))
        outs[1].append(_head_major(sbv))
        outs[2].append(_head_major(bdk))
        outs[3].append(_head_major(bdv))
        outs[4].append(mk.reshape(b, n_mem, MEM_HEADS, HEAD_DIM))
        outs[5].append(mv.reshape(b, n_mem, MEM_HEADS, HEAD_DIM))

        o_sb2, o_bd2, o_mm2 = _decode_attention(
            yqg_s.reshape(bd, n_new, QG_WIDTH), ykv_s.reshape(bd, n_new, KV_WIDTH),
            ymg_s.reshape(bd, n_new, mg_width),
            _head_major(cache_sb_k[l]), _head_major(cache_sb_v[l]),
            _head_major(cache_band_k[l]), _head_major(cache_band_v[l]),
            cache_mem_k[l], cache_mem_v[l],
            bias_d, negu2)
        xs = _merge_out(xs, yqg_s, ymg_s, o_sb2.reshape(ms, -1), o_bd2.reshape(ms, -1), o_mm2.reshape(ms, -1),
                        w_sb_b, w_bd_b, w_mm_b, w_out_b, g_post[l])
        outs[6].append(_head_major(sbk2))
        outs[7].append(_head_major(sbv2))
        outs[8].append(_head_major(bdk2))
        outs[9].append(_head_major(bdv2))

    return (xp.reshape(b, t, d), xs.reshape(bd, n_new, d)) + tuple(jnp.stack(o) for o in outs)
```

```python
import functools
import math

import jax
import jax.numpy as jnp
from jax import lax
from jax.experimental import pallas as pl
from jax.experimental.pallas import tpu as pltpu

F32 = jnp.float32
BF16 = jnp.bfloat16

HEAD_DIM = 128
SB_HEADS = 6
BAND_HEADS = 6
MEM_HEADS = 4
SB_WIDTH = SB_HEADS * HEAD_DIM
BAND_WIDTH = BAND_HEADS * HEAD_DIM
MEM_WIDTH = MEM_HEADS * HEAD_DIM
CHUNK = 64
CHUNK_SHIFT = 6
BAND_LEFT_CHUNKS = 8
BAND_ROWS = BAND_LEFT_CHUNKS * CHUNK
MAX_REL = 256
RMS_EPS = 1e-6
NEG_INF = -1e30
LOG2E = math.log2(math.e)
Q_SCALE = HEAD_DIM ** -0.5 * LOG2E

COL_SB_Q = 0
COL_SB_K = COL_SB_Q + SB_WIDTH
COL_SB_V = COL_SB_K + SB_WIDTH
COL_SB_G = COL_SB_V + SB_WIDTH
COL_BD_Q = COL_SB_G + SB_WIDTH
COL_BD_K = COL_BD_Q + BAND_WIDTH
COL_BD_V = COL_BD_K + BAND_WIDTH
COL_BD_G = COL_BD_V + BAND_WIDTH
COL_MM_Q = COL_BD_G + BAND_WIDTH
COL_MM_G = COL_MM_Q + MEM_WIDTH
COL_MG = COL_MM_G + MEM_WIDTH

KV_SB_K = 0
KV_SB_V = KV_SB_K + SB_WIDTH
KV_BD_K = KV_SB_V + SB_WIDTH
KV_BD_V = KV_BD_K + BAND_WIDTH
KV_WIDTH = KV_BD_V + BAND_WIDTH
QG_SB_Q = 0
QG_SB_G = QG_SB_Q + SB_WIDTH
QG_BD_Q = QG_SB_G + SB_WIDTH
QG_BD_G = QG_BD_Q + BAND_WIDTH
QG_WIDTH = QG_BD_G + BAND_WIDTH
MG_MM_Q = 0
MG_MM_G = MG_MM_Q + MEM_WIDTH
MG_MG = MG_MM_G + MEM_WIDTH

VMEM_LIMIT_BYTES = 56 * 1024 * 1024
MAX_VMEM_LIMIT_BYTES = 58 * 1024 * 1024
COMPILER_TEMP_BYTES = 2 * 1024 * 1024
PROJ_TM = 2048
OUT_TM = 512
MEM_TQ = 2048
CAST_ROWS = 512
KV_TN = SB_WIDTH
MG_TN = 1024
SB_TK = 256
SB_TQ = 16 * SB_TK
SB_DEAD = -160.0
BAND_TQ = 4 * CHUNK
BAND_WIN = BAND_TQ + BAND_ROWS
BAND_STEP_GROUPS = 16
BIAS_LANES = 1024
NEW_PAD = 128


def _params(n_axes, vmem=VMEM_LIMIT_BYTES):
    return pltpu.CompilerParams(dimension_semantics=("arbitrary",) * n_axes,
                                vmem_limit_bytes=vmem)


def _nt_dot(a, b):
    return lax.dot_general(a, b, (((1,), (1,)), ((), ())), preferred_element_type=F32)


def _dot(a, b):
    return jnp.dot(a, b, preferred_element_type=F32)


def _pre_norm_to(h_ref, x_ref, g_ref):
    x = x_ref[...]
    ms = jnp.mean(x * x, axis=-1, keepdims=True)
    h_ref[...] = ((x * lax.rsqrt(ms + RMS_EPS)) * g_ref[...]).astype(BF16)


def _kv_project_rows(x_ref, g_ref, w_refs, y_ref, h_ref, f32_refs, seqs, rows):
    _pre_norm_to(h_ref, x_ref, g_ref)
    for group, (w_ref, dst_ref) in enumerate(zip(w_refs, f32_refs)):
        acc = _dot(h_ref[...], w_ref[...])
        y_ref[:, group * KV_TN:(group + 1) * KV_TN] = acc.astype(BF16)
        for h in range(SB_HEADS):
            for s in range(seqs):
                dst_ref[s, h] = acc[s * rows:(s + 1) * rows, h * HEAD_DIM:(h + 1) * HEAD_DIM]


def _kv_proj_kernel(x_ref, xs_ref, g_ref, wsk_ref, wsv_ref, wbk_ref, wbv_ref,
                    y_ref, h_ref, sbk_ref, sbv_ref, bdk_ref, bdv_ref,
                    ys_ref, hs_ref, sbks_ref, sbvs_ref, bdks_ref, bdvs_ref, *, rows, seqs_s, rows_s):
    w_refs = (wsk_ref, wsv_ref, wbk_ref, wbv_ref)

    @pl.when(pl.program_id(0) == 0)
    def _():
        _kv_project_rows(xs_ref, g_ref, w_refs, ys_ref, hs_ref, (sbks_ref, sbvs_ref, bdks_ref, bdvs_ref),
                         seqs_s, rows_s)

    _kv_project_rows(x_ref, g_ref, w_refs, y_ref, h_ref, (sbk_ref, sbv_ref, bdk_ref, bdv_ref), 1, rows)


def _kv_projection(x2d, xs2d, g_pre, w_kv_b, *, n_seq, n_seq_s, band_keep):
    m, d = x2d.shape
    ms = xs2d.shape[0]
    assert KV_WIDTH == 4 * KV_TN and SB_HEADS == BAND_HEADS
    tm = band_keep
    seq_rows = m // n_seq
    blocks_per_seq = seq_rows // tm
    rows_s = ms // n_seq_s
    kern = functools.partial(_kv_proj_kernel, rows=tm, seqs_s=n_seq_s, rows_s=rows_s)
    sb_spec = pl.BlockSpec((1, SB_HEADS, tm, HEAD_DIM), lambda i: (i // blocks_per_seq, 0, i % blocks_per_seq, 0))
    bd_spec = pl.BlockSpec((1, BAND_HEADS, tm, HEAD_DIM), lambda i: (i // blocks_per_seq, 0, 0, 0))
    sb_shape = jax.ShapeDtypeStruct((n_seq, SB_HEADS, seq_rows, HEAD_DIM), F32)
    bd_shape = jax.ShapeDtypeStruct((n_seq, BAND_HEADS, band_keep, HEAD_DIM), F32)
    whole = lambda shape: pl.BlockSpec(shape, lambda i: (0,) * len(shape))
    s_shape = (n_seq_s, SB_HEADS, rows_s, HEAD_DIM)

    def w_spec(group):
        return pl.BlockSpec((d, KV_TN), functools.partial(lambda i, c: (0, c), c=group),
                            pipeline_mode=pl.Buffered(1))

    return pl.pallas_call(
        kern,
        grid=(m // tm,),
        in_specs=[
            pl.BlockSpec((tm, d), lambda i: (i, 0)),
            whole((ms, d)),
            pl.BlockSpec((1, d), lambda i: (0, 0)),
            w_spec(0), w_spec(1), w_spec(2), w_spec(3),
        ],
        out_specs=[pl.BlockSpec((tm, KV_WIDTH), lambda i: (i, 0)), pl.BlockSpec((tm, d), lambda i: (i, 0)),
                   sb_spec, sb_spec, bd_spec, bd_spec,
                   whole((ms, KV_WIDTH)), whole((ms, d)),
                   whole(s_shape), whole(s_shape), whole(s_shape), whole(s_shape)],
        out_shape=[jax.ShapeDtypeStruct((m, KV_WIDTH), BF16), jax.ShapeDtypeStruct((m, d), BF16),
                   sb_shape, sb_shape, bd_shape, bd_shape,
                   jax.ShapeDtypeStruct((ms, KV_WIDTH), BF16), jax.ShapeDtypeStruct((ms, d), BF16)]
                  + [jax.ShapeDtypeStruct(s_shape, F32)] * 4,
        compiler_params=_params(1),
        name="kv_projection",
    )(x2d, xs2d, g_pre.reshape(1, d), w_kv_b, w_kv_b, w_kv_b, w_kv_b)


def _col_proj_kernel(h_ref, hs_ref, w_ref, cs_ref, y_ref, ys_ref, wb_ref):
    @pl.when(pl.program_id(1) == 0)
    def _():
        wb_ref[...] = w_ref[...].astype(BF16)
        ys_ref[...] = (_dot(hs_ref[...], wb_ref[...]) * cs_ref[...]).astype(BF16)

    y_ref[...] = (_dot(h_ref[...], wb_ref[...]) * cs_ref[...]).astype(BF16)


def _col_projection(h2d, hs2d, w_in, col_scale, src_block, *, tm, tn, name):
    m, d = h2d.shape
    ms = hs2d.shape[0]
    n = col_scale.shape[1]
    assert n % tn == 0 and m % tm == 0
    vmem = (2 * tm * d * 2 + 2 * d * tn * 4 + 2 * tm * tn * 2 + d * tn * 2 + 2 * tm * tn * 4
            + 2 * ms * (d + tn) * 2 + COMPILER_TEMP_BYTES)
    vmem = min(vmem, MAX_VMEM_LIMIT_BYTES)
    return pl.pallas_call(
        _col_proj_kernel,
        grid=(n // tn, m // tm),
        in_specs=[
            pl.BlockSpec((tm, d), lambda j, i: (i, 0)),
            pl.BlockSpec((ms, d), lambda j, i: (0, 0)),
            pl.BlockSpec((d, tn), lambda j, i: (0, src_block(j))),
            pl.BlockSpec((1, tn), lambda j, i: (0, j)),
        ],
        out_specs=[pl.BlockSpec((tm, tn), lambda j, i: (i, j)), pl.BlockSpec((ms, tn), lambda j, i: (0, j))],
        out_shape=[jax.ShapeDtypeStruct((m, n), BF16), jax.ShapeDtypeStruct((ms, n), BF16)],
        scratch_shapes=[pltpu.VMEM((d, tn), BF16)],
        compiler_params=_params(2, vmem=vmem),
        name=name,
    )(h2d, hs2d, w_in, col_scale)


def _kv_weight_kernel(a_ref, b_ref, c_ref, d_ref, o_ref):
    for group, w_ref in enumerate((a_ref, b_ref, c_ref, d_ref)):
        o_ref[:, group * KV_TN:(group + 1) * KV_TN] = w_ref[...].astype(BF16)


def _kv_weight_bf16(w):
    d = w.shape[0]
    tr = CAST_ROWS
    assert d % tr == 0

    def spec(col0):
        assert col0 % KV_TN == 0
        return pl.BlockSpec((tr, KV_TN), functools.partial(lambda i, c: (i, c), c=col0 // KV_TN))

    return pl.pallas_call(
        _kv_weight_kernel,
        grid=(d // tr,),
        in_specs=[spec(COL_SB_K), spec(COL_SB_V), spec(COL_BD_K), spec(COL_BD_V)],
        out_specs=pl.BlockSpec((tr, KV_WIDTH), lambda i: (i, 0)),
        out_shape=jax.ShapeDtypeStruct((d, KV_WIDTH), BF16),
        compiler_params=_params(1),
        name="kv_weight_cast",
    )(w, w, w, w)


def _memkv_kernel(x_ref, g_ref, w_ref, mk_ref, mv_ref):
    x = x_ref[...]
    ms = jnp.mean(x * x, axis=-1, keepdims=True)
    h = ((x * lax.rsqrt(ms + RMS_EPS)) * g_ref[...]).astype(BF16)
    acc = _dot(h, w_ref[...].astype(BF16))
    mk_ref[...] = acc[:, :MEM_WIDTH]
    mv_ref[...] = acc[:, MEM_WIDTH:]


def _memory_kv(mem2d, g_mem, w_bf16, *, tm):
    m, d = mem2d.shape
    return pl.pallas_call(
        _memkv_kernel,
        grid=(m // tm,),
        in_specs=[
            pl.BlockSpec((tm, d), lambda i: (i, 0)),
            pl.BlockSpec((1, d), lambda i: (0, 0)),
            pl.BlockSpec((d, 2 * MEM_WIDTH), lambda i: (0, 0)),
        ],
        out_specs=[pl.BlockSpec((tm, MEM_WIDTH), lambda i: (i, 0)),
                   pl.BlockSpec((tm, MEM_WIDTH), lambda i: (i, 0))],
        out_shape=[jax.ShapeDtypeStruct((m, MEM_WIDTH), F32),
                   jax.ShapeDtypeStruct((m, MEM_WIDTH), F32)],
        compiler_params=_params(1),
        name="memory_kv",
    )(mem2d, g_mem.reshape(1, d), w_bf16)


def _neg_suffix_matrix(n):
    row = lax.broadcasted_iota(jnp.int32, (2 * n, n), 0)
    col = lax.broadcasted_iota(jnp.int32, (2 * n, n), 1)
    row = jnp.where(row >= n, row - n, row)
    return jnp.where(row >= col, -1.0, 0.0).astype(BF16)


def _sb_weights(z2, carry2, negu2, mask):
    p = jnp.maximum(z2, 0.0) + jnp.log(1.0 + jnp.exp2(-jnp.abs(z2))) * LOG2E
    if mask is not None:
        p = jnp.where(mask, p, 0.0)
    p_hi = p.astype(BF16)
    p_lo = (p - p_hi.astype(F32)).astype(BF16)
    suffix = _dot(jnp.concatenate([p_hi, p_lo], axis=1), negu2)
    w = jnp.exp2(z2 + suffix + carry2)
    if mask is not None:
        w = jnp.where(mask, w, 0.0)
    return w, carry2 - jnp.sum(p, axis=-1, keepdims=True)


def _sb_prompt_body(q_ref, k_ref, v_ref, negu2_ref, o_ref, acc_ref, carry_ref, kpad_ref, vpad_ref,
                    *, t, at_first_step, beside_diagonal):
    i = pl.program_id(2)
    n_sub = SB_TQ // SB_TK
    negu2 = negu2_ref[...]

    @pl.when(i == 0)
    def _():
        kpad_ref[0:SB_TK, :] = jnp.zeros((SB_TK, HEAD_DIM), BF16)
        vpad_ref[0:SB_TK, :] = jnp.zeros((SB_TK, HEAD_DIM), BF16)
        kpad_ref[SB_TK:SB_TK + t, :] = k_ref[...]
        vpad_ref[SB_TK:SB_TK + t, :] = v_ref[...]
        at_first_step()

    def kv_block(j):
        start = pl.multiple_of((j + 1) * SB_TK, SB_TK)
        return kpad_ref[pl.ds(start, SB_TK), :], vpad_ref[pl.ds(start, SB_TK), :]

    row = lax.broadcasted_iota(jnp.int32, (SB_TK, SB_TK), 0)
    col = lax.broadcasted_iota(jnp.int32, (SB_TK, SB_TK), 1)
    for r in range(n_sub):
        rows = slice(r * SB_TK, (r + 1) * SB_TK)
        s = i * n_sub + r
        q = q_ref[rows, :]
        kb, vb = kv_block(s)
        w, carry = _sb_weights(_nt_dot(q, kb), jnp.zeros((SB_TK, 1), F32), negu2, col < row)
        acc = _dot(w.astype(BF16), vb)
        kb, vb = kv_block(s - 1)
        prev_exists = None if r > 0 else (jnp.zeros((SB_TK, SB_TK), jnp.int32) + i) > 0
        w, carry = _sb_weights(_nt_dot(q, kb), carry, negu2, prev_exists)
        acc_ref[rows, :] = acc + _dot(w.astype(BF16), vb)
        carry_ref[rows, :] = carry
    beside_diagonal()

    row_q = lax.broadcasted_iota(jnp.int32, (SB_TQ, 1), 0)
    row_t = lax.broadcasted_iota(jnp.int32, (SB_TQ, SB_TK), 0)
    has_more = row_q >= (2 - n_sub * i) * SB_TK

    def any_alive(carry):
        return (jnp.max(jnp.where(has_more, carry, NEG_INF)) > SB_DEAD).astype(jnp.int32)

    def cond(state):
        j, alive = state
        return jnp.logical_and(j >= 0, alive > 0)

    def body(state):
        j, _ = state
        kb, vb = kv_block(j)
        visits = row_t >= (j - n_sub * i + 2) * SB_TK
        w, carry = _sb_weights(_nt_dot(q_ref[...], kb), carry_ref[...], negu2, visits)
        acc_ref[...] += _dot(w.astype(BF16), vb)
        carry_ref[...] = carry
        return j - 1, any_alive(carry)

    lax.while_loop(cond, body, (n_sub * i + n_sub - 3, any_alive(carry_ref[...])))
    o_ref[...] = acc_ref[...].astype(BF16)


def _sb_decode_body(q_ref, kn_ref, vn_ref, kc_hbm, vc_hbm, negu2_ref, o_ref,
                    acc_ref, carry_ref, kpad_ref, vpad_ref, kbuf_ref, vbuf_ref, sem,
                    *, n_new, n_blocks, overlap):
    b = pl.program_id(0)
    heads = SB_HEADS

    def cache_copies(j, slot):
        rows = pl.ds(pl.multiple_of((n_blocks - 1 - j) * SB_TK, SB_TK), SB_TK)
        return (pltpu.make_async_copy(kc_hbm.at[b, :, rows, :], kbuf_ref.at[slot], sem.at[0, slot]),
                pltpu.make_async_copy(vc_hbm.at[b, :, rows, :], vbuf_ref.at[slot], sem.at[1, slot]))

    def start_fetch(j, slot):
        for cp in cache_copies(j, slot):
            cp.start()

    def wait_fetch(j, slot):
        for cp in cache_copies(j, slot):
            cp.wait()

    start_fetch(0, 0)
    overlap()

    def head_cols(h):
        return slice(h * HEAD_DIM, (h + 1) * HEAD_DIM)

    def head_rows(h):
        return slice(h * n_new, (h + 1) * n_new)

    def block(k_of, v_of, negu2, mask):
        z2 = jnp.concatenate([_nt_dot(q_ref[:, head_cols(h)], k_of(h)) for h in range(heads)], axis=0)
        w, carry = _sb_weights(z2, carry_ref[...], negu2, mask)
        wb = w.astype(BF16)
        for h in range(heads):
            acc_ref[head_rows(h), :] += _dot(wb[head_rows(h), :], v_of(h))
        carry_ref[...] = carry

    def any_alive():
        return (jnp.max(carry_ref[...]) > SB_DEAD).astype(jnp.int32)

    acc_ref[...] = jnp.zeros_like(acc_ref)
    carry_ref[...] = jnp.zeros_like(carry_ref)
    kpad_ref[...] = jnp.zeros_like(kpad_ref)
    vpad_ref[...] = jnp.zeros_like(vpad_ref)
    for h in range(heads):
        kpad_ref[h, 0:n_new, :] = kn_ref[:, head_cols(h)]
        vpad_ref[h, 0:n_new, :] = vn_ref[:, head_cols(h)]
    row = lax.broadcasted_iota(jnp.int32, (n_new, NEW_PAD), 0)
    col = lax.broadcasted_iota(jnp.int32, (n_new, NEW_PAD), 1)
    mask = jnp.concatenate([(col < row).astype(jnp.int32)] * heads, axis=0) == 1
    block(lambda h: kpad_ref[h], lambda h: vpad_ref[h], _neg_suffix_matrix(NEW_PAD), mask)

    negu2 = negu2_ref[...]

    def cond(state):
        j, alive = state
        return jnp.logical_and(j < n_blocks, alive > 0)

    def body(state):
        j, _ = state
        slot = j % 2
        wait_fetch(j, slot)

        @pl.when(j + 1 < n_blocks)
        def _():
            start_fetch(j + 1, 1 - slot)

        block(lambda h: kbuf_ref[slot, h].astype(BF16), lambda h: vbuf_ref[slot, h].astype(BF16),
              negu2, None)
        return j + 1, any_alive()

    j_end, _ = lax.while_loop(cond, body, (0, any_alive()))

    for h in range(heads):
        o_ref[:, head_cols(h)] = acc_ref[head_rows(h), :].astype(BF16)

    @pl.when(j_end < n_blocks)
    def _():
        wait_fetch(j_end, j_end % 2)


def _softmax2_pv(parts):
    mx = functools.reduce(jnp.maximum, [jnp.max(s, axis=-1, keepdims=True) for s, _ in parts])
    num = None
    den = None
    for s, v in parts:
        p = jnp.exp2(s - mx)
        d = jnp.sum(p, axis=-1, keepdims=True)
        o = _dot(p.astype(BF16), v)
        num = o if num is None else num + o
        den = d if den is None else den + d
    return num / den


def _band_bias_kernel(g_ref, tp_ref, td_ref, *, n_new, r_band):
    x = jnp.broadcast_to(g_ref[...], (BAND_TQ, BIAS_LANES))
    x = pltpu.roll(x, BAND_TQ, 1, stride=1, stride_axis=0)
    tbl = x[:, :BAND_WIN] * LOG2E
    r = lax.broadcasted_iota(jnp.int32, (BAND_TQ, BAND_WIN), 0)
    j = lax.broadcasted_iota(jnp.int32, (BAND_TQ, BAND_WIN), 1)
    dc = (j >> CHUNK_SHIFT) - (r >> CHUNK_SHIFT)
    tp_ref[...] = jnp.where(jnp.logical_and(dc >= 0, dc <= BAND_LEFT_CHUNKS), tbl, NEG_INF)
    jd = lax.broadcasted_iota(jnp.int32, (n_new, r_band + NEW_PAD), 1)
    td_ref[...] = jnp.where(jd < r_band + n_new, tbl[:n_new, :r_band + NEW_PAD], NEG_INF)


def _band_bias_tables(rel_bias, n_new, r_band):
    h = rel_bias.shape[0]
    assert BAND_ROWS == 2 * MAX_REL and r_band == BAND_ROWS and BIAS_LANES == 2 * BAND_ROWS
    rb = rel_bias.astype(F32)
    g = jnp.concatenate([rb[:, :0:-1], jnp.broadcast_to(rb[:, -1:], (h, BIAS_LANES - 2 * MAX_REL))], axis=1)
    kern = functools.partial(_band_bias_kernel, n_new=n_new, r_band=r_band)
    return pl.pallas_call(
        kern,
        grid=(h,),
        in_specs=[pl.BlockSpec((None, 1, BIAS_LANES), lambda i: (i, 0, 0))],
        out_specs=[pl.BlockSpec((None, BAND_TQ, BAND_WIN), lambda i: (i, 0, 0)),
                   pl.BlockSpec((None, n_new, r_band + NEW_PAD), lambda i: (i, 0, 0))],
        out_shape=[jax.ShapeDtypeStruct((h, BAND_TQ, BAND_WIN), F32),
                   jax.ShapeDtypeStruct((h, n_new, r_band + NEW_PAD), F32)],
        compiler_params=_params(1),
        name="band_bias",
    )(g.reshape(h, 1, BIAS_LANES))


def _band_pad(k_ref, v_ref, kpad_ref, vpad_ref, t):
    kpad_ref[0:BAND_ROWS, :] = jnp.zeros((BAND_ROWS, HEAD_DIM), BF16)
    vpad_ref[0:BAND_ROWS, :] = jnp.zeros((BAND_ROWS, HEAD_DIM), BF16)
    kpad_ref[BAND_ROWS:BAND_ROWS + t, :] = k_ref[...]
    vpad_ref[BAND_ROWS:BAND_ROWS + t, :] = v_ref[...]


def _band_groups(q_ref, bias_ref, o_ref, kpad_ref, vpad_ref):
    s_idx = pl.program_id(2)
    col = lax.broadcasted_iota(jnp.int32, (BAND_TQ, BAND_WIN), 1)
    for gg in range(BAND_STEP_GROUPS):
        g = s_idx * BAND_STEP_GROUPS + gg
        start = pl.multiple_of(g * BAND_TQ, BAND_TQ)
        rows = slice(gg * BAND_TQ, (gg + 1) * BAND_TQ)
        s = _nt_dot(q_ref[rows, :], kpad_ref[pl.ds(start, BAND_WIN), :]) + bias_ref[...]
        s = jnp.where(col + g * BAND_TQ >= BAND_ROWS, s, NEG_INF)
        o_ref[rows, :] = _softmax2_pv([(s, vpad_ref[pl.ds(start, BAND_WIN), :])]).astype(BF16)


def _prompt_attn_kernel(qs_ref, ks_ref, vs_ref, negu2_ref, qb_ref, kb_ref, vb_ref, bias_ref,
                        osb_ref, obd_ref, acc_ref, carry_ref, kpads_ref, vpads_ref, kpadb_ref, vpadb_ref, *, t):
    _sb_prompt_body(qs_ref, ks_ref, vs_ref, negu2_ref, osb_ref, acc_ref, carry_ref, kpads_ref, vpads_ref, t=t,
                    at_first_step=lambda: _band_pad(kb_ref, vb_ref, kpadb_ref, vpadb_ref, t),
                    beside_diagonal=lambda: _band_groups(qb_ref, bias_ref, obd_ref, kpadb_ref, vpadb_ref))


def _prompt_attention(yqg3, ykv3, bias_tbl, negu2):
    b, t, _ = yqg3.shape
    tq = SB_TQ
    assert SB_TQ == BAND_TQ * BAND_STEP_GROUPS and SB_TQ // SB_TK >= 3 and SB_HEADS == BAND_HEADS
    q_spec = lambda c0: pl.BlockSpec((None, tq, HEAD_DIM),
                                     functools.partial(lambda b, h, i, c: (b, i, c + h), c=c0 // HEAD_DIM))
    kv_spec = lambda c0: pl.BlockSpec((None, t, HEAD_DIM),
                                      functools.partial(lambda b, h, i, c: (b, 0, c + h), c=c0 // HEAD_DIM))
    out_spec = pl.BlockSpec((None, tq, HEAD_DIM), lambda b, h, i: (b, i, h))
    kern = functools.partial(_prompt_attn_kernel, t=t)
    return pl.pallas_call(
        kern,
        grid=(b, SB_HEADS, t // tq),
        in_specs=[
            q_spec(QG_SB_Q), kv_spec(KV_SB_K), kv_spec(KV_SB_V),
            pl.BlockSpec((2 * SB_TK, SB_TK), lambda b, h, i: (0, 0)),
            q_spec(QG_BD_Q), kv_spec(KV_BD_K), kv_spec(KV_BD_V),
            pl.BlockSpec((None, BAND_TQ, BAND_WIN), lambda b, h, i: (h, 0, 0)),
        ],
        out_specs=[out_spec, out_spec],
        out_shape=[jax.ShapeDtypeStruct((b, t, SB_WIDTH), BF16), jax.ShapeDtypeStruct((b, t, BAND_WIDTH), BF16)],
        scratch_shapes=[pltpu.VMEM((SB_TQ, HEAD_DIM), F32), pltpu.VMEM((SB_TQ, 1), F32),
                        pltpu.VMEM((SB_TK + t, HEAD_DIM), BF16), pltpu.VMEM((SB_TK + t, HEAD_DIM), BF16),
                        pltpu.VMEM((BAND_ROWS + t, HEAD_DIM), BF16), pltpu.VMEM((BAND_ROWS + t, HEAD_DIM), BF16)],
        compiler_params=_params(3, vmem=MAX_VMEM_LIMIT_BYTES),
        name="prompt_attention",
    )(yqg3, ykv3, ykv3, negu2, yqg3, ykv3, ykv3, bias_tbl)


def _band_decode_kernel(q_ref, kn_ref, vn_ref, kc_ref, vc_ref, bias_ref, o_ref, kpad_ref, vpad_ref,
                        *, n_new, r_band):
    kpad_ref[...] = jnp.zeros_like(kpad_ref)
    vpad_ref[...] = jnp.zeros_like(vpad_ref)
    for h in range(BAND_HEADS):
        cols = slice(h * HEAD_DIM, (h + 1) * HEAD_DIM)
        kpad_ref[h, 0:n_new, :] = kn_ref[:, cols]
        vpad_ref[h, 0:n_new, :] = vn_ref[:, cols]
    for h in range(BAND_HEADS):
        cols = slice(h * HEAD_DIM, (h + 1) * HEAD_DIM)
        q = q_ref[:, cols]
        s_cache = _nt_dot(q, kc_ref[h].astype(BF16)) + bias_ref[h, :, 0:r_band]
        s_new = _nt_dot(q, kpad_ref[h]) + bias_ref[h, :, r_band:r_band + NEW_PAD]
        o_ref[:, cols] = _softmax2_pv([(s_cache, vc_ref[h].astype(BF16)),
                                       (s_new, vpad_ref[h])]).astype(BF16)


def _mem_attn_kernel(q_ref, mk_ref, mv_ref, o_ref):
    per_head = len(mk_ref.shape) == 3
    for h in range(MEM_HEADS):
        sl = slice(h * HEAD_DIM, (h + 1) * HEAD_DIM)
        mk = mk_ref[:, h, :] if per_head else mk_ref[:, sl]
        mv = mv_ref[:, h, :] if per_head else mv_ref[:, sl]
        s = _nt_dot(q_ref[:, sl], mk.astype(BF16))
        o_ref[:, sl] = _softmax2_pv([(s, mv.astype(BF16))]).astype(BF16)


def _mem_attention(y3, mk, mv, *, tq):
    b, t, _ = y3.shape
    n_mem = mk.shape[1]
    qb = MG_MM_Q // MEM_WIDTH
    return pl.pallas_call(
        _mem_attn_kernel,
        grid=(b, t // tq),
        in_specs=[
            pl.BlockSpec((None, tq, MEM_WIDTH), lambda b, i: (b, i, qb)),
            pl.BlockSpec((None, n_mem, MEM_WIDTH), lambda b, i: (b, 0, 0)),
            pl.BlockSpec((None, n_mem, MEM_WIDTH), lambda b, i: (b, 0, 0)),
        ],
        out_specs=pl.BlockSpec((None, tq, MEM_WIDTH), lambda b, i: (b, i, 0)),
        out_shape=jax.ShapeDtypeStruct((b, t, MEM_WIDTH), BF16),
        compiler_params=_params(2),
        name="mem_attention",
    )(y3, mk, mv)


def _decode_attn_kernel(qs_ref, kns_ref, vns_ref, kcs_hbm, vcs_hbm, negu2_ref,
                        qb_ref, knb_ref, vnb_ref, kcb_ref, vcb_ref, bias_ref, qm_ref, mk_ref, mv_ref,
                        osb_ref, obd_ref, omm_ref,
                        acc_ref, carry_ref, kpad_ref, vpad_ref, kbuf_ref, vbuf_ref, sem, kpadb_ref, vpadb_ref,
                        *, n_new, n_blocks, r_band):
    def band_and_memory():
        _band_decode_kernel(qb_ref, knb_ref, vnb_ref, kcb_ref, vcb_ref, bias_ref, obd_ref,
                            kpadb_ref, vpadb_ref, n_new=n_new, r_band=r_band)
        _mem_attn_kernel(qm_ref, mk_ref, mv_ref, omm_ref)

    _sb_decode_body(qs_ref, kns_ref, vns_ref, kcs_hbm, vcs_hbm, negu2_ref, osb_ref,
                    acc_ref, carry_ref, kpad_ref, vpad_ref, kbuf_ref, vbuf_ref, sem,
                    n_new=n_new, n_blocks=n_blocks, overlap=band_and_memory)


def _decode_attention(yqg3, ykv3, ymg3, cache_sb_k, cache_sb_v, cache_bd_k, cache_bd_v, mk, mv, bias_tbl, negu2):
    bd, n_new, _ = yqg3.shape
    past = cache_sb_k.shape[2]
    r_band = cache_bd_k.shape[2]
    n_mem = mk.shape[1]
    assert past % SB_TK == 0 and n_new <= NEW_PAD and n_new % 16 == 0
    kern = functools.partial(_decode_attn_kernel, n_new=n_new, n_blocks=past // SB_TK, r_band=r_band)
    slab = lambda width, col0: pl.BlockSpec((None, n_new, width), functools.partial(lambda b, c: (b, 0, c), c=col0 // width))
    band_cache = pl.BlockSpec((None, BAND_HEADS, r_band, HEAD_DIM), lambda b: (b, 0, 0, 0))
    mem_cache = pl.BlockSpec((None, n_mem, MEM_HEADS, HEAD_DIM), lambda b: (b, 0, 0, 0))
    return pl.pallas_call(
        kern,
        grid=(bd,),
        in_specs=[
            slab(SB_WIDTH, QG_SB_Q), slab(SB_WIDTH, KV_SB_K), slab(SB_WIDTH, KV_SB_V),
            pl.BlockSpec(memory_space=pl.ANY), pl.BlockSpec(memory_space=pl.ANY),
            pl.BlockSpec((2 * SB_TK, SB_TK), lambda b: (0, 0)),
            slab(BAND_WIDTH, QG_BD_Q), slab(BAND_WIDTH, KV_BD_K), slab(BAND_WIDTH, KV_BD_V),
            band_cache, band_cache,
            pl.BlockSpec((BAND_HEADS, n_new, r_band + NEW_PAD), lambda b: (0, 0, 0)),
            slab(MEM_WIDTH, MG_MM_Q), mem_cache, mem_cache,
        ],
        out_specs=[pl.BlockSpec((None, n_new, SB_WIDTH), lambda b: (b, 0, 0)),
                   pl.BlockSpec((None, n_new, BAND_WIDTH), lambda b: (b, 0, 0)),
                   pl.BlockSpec((None, n_new, MEM_WIDTH), lambda b: (b, 0, 0))],
        out_shape=[jax.ShapeDtypeStruct((bd, n_new, SB_WIDTH), BF16),
                   jax.ShapeDtypeStruct((bd, n_new, BAND_WIDTH), BF16),
                   jax.ShapeDtypeStruct((bd, n_new, MEM_WIDTH), BF16)],
        scratch_shapes=[pltpu.VMEM((SB_HEADS * n_new, HEAD_DIM), F32),
                        pltpu.VMEM((SB_HEADS * n_new, 1), F32),
                        pltpu.VMEM((SB_HEADS, NEW_PAD, HEAD_DIM), BF16),
                        pltpu.VMEM((SB_HEADS, NEW_PAD, HEAD_DIM), BF16),
                        pltpu.VMEM((2, SB_HEADS, SB_TK, HEAD_DIM), F32),
                        pltpu.VMEM((2, SB_HEADS, SB_TK, HEAD_DIM), F32),
                        pltpu.SemaphoreType.DMA((2, 2)),
                        pltpu.VMEM((BAND_HEADS, NEW_PAD, HEAD_DIM), BF16),
                        pltpu.VMEM((BAND_HEADS, NEW_PAD, HEAD_DIM), BF16)],
        compiler_params=_params(1),
        name="decode_attention",
    )(yqg3, ykv3, ykv3, cache_sb_k, cache_sb_v, negu2, yqg3, ykv3, ykv3, cache_bd_k, cache_bd_v, bias_tbl,
      ymg3, mk, mv)


def _silu_of_half(h):
    return h + h * jnp.tanh(h)


def _merge_kernel(osb_ref, obd_ref, omm_ref, gsb_ref, gbd_ref, gmm_ref,
                  mg0_ref, mg1_ref, mg2_ref, mg3_ref, mg4_ref, mg5_ref,
                  wsb_ref, wbd_ref, wmm_ref, merged_ref, *, half):
    u_sb = (osb_ref[...].astype(F32) * _silu_of_half(gsb_ref[...].astype(F32))).astype(BF16)
    u_bd = (obd_ref[...].astype(F32) * _silu_of_half(gbd_ref[...].astype(F32))).astype(BF16)
    u_mm = (omm_ref[...].astype(F32) * _silu_of_half(gmm_ref[...].astype(F32))).astype(BF16)
    mg = ((mg0_ref, mg2_ref, mg4_ref), (mg1_ref, mg3_ref, mg5_ref))
    for n in range(2):
        cols = slice(n * half, (n + 1) * half)
        merged = None
        for m_ref, u, w_ref in zip(mg[n], (u_sb, u_bd, u_mm), (wsb_ref, wbd_ref, wmm_ref)):
            a = _dot(u, w_ref[:, cols])
            term = a + a * jnp.tanh(m_ref[...].astype(F32))
            merged = term if merged is None else merged + term
        merged_ref[:, cols] = merged.astype(BF16)


def _merge_branches(yqg, ymg, o_sb, o_bd, o_mm, w_sb, w_bd, w_mm, *, tm):
    m = yqg.shape[0]
    d = w_sb.shape[1]
    half = d // 2
    assert MG_MG % half == 0
    mgb = MG_MG // half
    const = dict(pipeline_mode=pl.Buffered(1))
    kern = functools.partial(_merge_kernel, half=half)
    return pl.pallas_call(
        kern,
        grid=(m // tm,),
        in_specs=[
            pl.BlockSpec((tm, SB_WIDTH), lambda i: (i, 0)),
            pl.BlockSpec((tm, BAND_WIDTH), lambda i: (i, 0)),
            pl.BlockSpec((tm, MEM_WIDTH), lambda i: (i, 0)),
            pl.BlockSpec((tm, SB_WIDTH), lambda i: (i, QG_SB_G // SB_WIDTH)),
            pl.BlockSpec((tm, BAND_WIDTH), lambda i: (i, QG_BD_G // BAND_WIDTH)),
            pl.BlockSpec((tm, MEM_WIDTH), lambda i: (i, MG_MM_G // MEM_WIDTH)),
        ] + [pl.BlockSpec((tm, half), functools.partial(lambda i, c: (i, c), c=mgb + c)) for c in range(6)] + [
            pl.BlockSpec((SB_WIDTH, d), lambda i: (0, 0), **const),
            pl.BlockSpec((BAND_WIDTH, d), lambda i: (0, 0), **const),
            pl.BlockSpec((MEM_WIDTH, d), lambda i: (0, 0), **const),
        ],
        out_specs=pl.BlockSpec((tm, d), lambda i: (i, 0)),
        out_shape=jax.ShapeDtypeStruct((m, d), BF16),
        compiler_params=_params(1),
        name="merge_branches",
    )(o_sb, o_bd, o_mm, yqg, yqg, ymg, *([ymg] * 6), w_sb, w_bd, w_mm)


def _out_proj_kernel(x_ref, merged_ref, wout_ref, gpost_ref, y_ref):
    y = _dot(merged_ref[...], wout_ref[...])
    ms = jnp.mean(y * y, axis=-1, keepdims=True)
    y_ref[...] = x_ref[...] + (y * lax.rsqrt(ms + RMS_EPS)) * gpost_ref[...]


def _merge_out_kernel(x_ref, osb_ref, obd_ref, omm_ref, gsb_ref, gbd_ref, gmm_ref,
                      mg0_ref, mg1_ref, mg2_ref, mg3_ref, mg4_ref, mg5_ref,
                      wsb_ref, wbd_ref, wmm_ref, wout_ref, gpost_ref, y_ref, merged_ref, *, half):
    _merge_kernel(osb_ref, obd_ref, omm_ref, gsb_ref, gbd_ref, gmm_ref,
                  mg0_ref, mg1_ref, mg2_ref, mg3_ref, mg4_ref, mg5_ref,
                  wsb_ref, wbd_ref, wmm_ref, merged_ref, half=half)
    _out_proj_kernel(x_ref, merged_ref, wout_ref, gpost_ref, y_ref)


def _merge_out(x2d, yqg, ymg, o_sb, o_bd, o_mm, w_sb, w_bd, w_mm, w_out, g_post):
    m, d = x2d.shape
    half = d // 2
    mgb = MG_MG // half
    whole = lambda shape: pl.BlockSpec(shape, lambda i: (0,) * len(shape))
    col = lambda width, c: pl.BlockSpec((m, width), functools.partial(lambda i, c: (0, c), c=c))
    kern = functools.partial(_merge_out_kernel, half=half)
    return pl.pallas_call(
        kern,
        grid=(1,),
        in_specs=[
            whole((m, d)), whole((m, SB_WIDTH)), whole((m, BAND_WIDTH)), whole((m, MEM_WIDTH)),
            col(SB_WIDTH, QG_SB_G // SB_WIDTH), col(BAND_WIDTH, QG_BD_G // BAND_WIDTH),
            col(MEM_WIDTH, MG_MM_G // MEM_WIDTH),
        ] + [col(half, mgb + c) for c in range(6)] + [
            whole((SB_WIDTH, d)), whole((BAND_WIDTH, d)), whole((MEM_WIDTH, d)), whole((d, d)), whole((1, d)),
        ],
        out_specs=whole((m, d)),
        out_shape=jax.ShapeDtypeStruct((m, d), F32),
        scratch_shapes=[pltpu.VMEM((m, d), BF16)],
        compiler_params=_params(1),
        name="merge_out",
    )(x2d, o_sb, o_bd, o_mm, yqg, yqg, ymg, *([ymg] * 6), w_sb, w_bd, w_mm, w_out, g_post.reshape(1, d))


def _out_projection(x2d, merged, w_out, g_post, *, tm):
    m, d = x2d.shape
    return pl.pallas_call(
        _out_proj_kernel,
        grid=(m // tm,),
        in_specs=[
            pl.BlockSpec((tm, d), lambda i: (i, 0)),
            pl.BlockSpec((tm, d), lambda i: (i, 0)),
            pl.BlockSpec((d, d), lambda i: (0, 0), pipeline_mode=pl.Buffered(1)),
            pl.BlockSpec((1, d), lambda i: (0, 0)),
        ],
        out_specs=pl.BlockSpec((tm, d), lambda i: (i, 0)),
        out_shape=jax.ShapeDtypeStruct((m, d), F32),
        compiler_params=_params(1),
        name="out_projection",
    )(x2d, merged, w_out, g_post.reshape(1, d))


def _head_major(a):
    return jnp.transpose(a, (0, 2, 1, 3))


def kernel(x_prompt, x_sample, cache_sb_k, cache_sb_v, cache_band_k, cache_band_v, cache_mem_k, cache_mem_v, mem_prompt, g_pre, w_in, rel_bias, g_mem, w_mem_kv, w_up_sb, w_up_band, w_up_mem, w_out, g_post):
    depth = w_in.shape[0]
    b, t, d = x_prompt.shape
    bd, n_new, _ = x_sample.shape
    n_mem = mem_prompt.shape[1]
    r_band = cache_band_k.shape[2]
    in_width = w_in.shape[2]
    band_keep = min(BAND_ROWS, t)
    assert COL_MG + 3 * d == in_width
    assert t % SB_TQ == 0 and t % (BAND_TQ * BAND_STEP_GROUPS) == 0 and t % band_keep == 0
    assert r_band == BAND_ROWS and n_new <= CHUNK

    negu2 = jnp.where(jnp.arange(2 * SB_TK)[:, None] % SB_TK >= jnp.arange(SB_TK)[None, :], -1.0, 0.0).astype(BF16)
    mg_width = in_width - COL_MM_Q
    assert COL_MM_Q % MG_TN == 0 and mg_width % MG_TN == 0 and COL_SB_G == 3 * KV_TN and COL_BD_Q == 4 * KV_TN
    cols = jnp.arange(QG_WIDTH)
    qg_scale = jnp.where((cols // SB_WIDTH) % 2 == 0, Q_SCALE, 0.5).astype(F32).reshape(1, QG_WIDTH)
    cols = jnp.arange(mg_width)
    mg_scale = jnp.where(cols < MG_MM_G, Q_SCALE, 0.5).astype(F32).reshape(1, mg_width)
    qg_block = lambda j: j + 2 * ((j + 1) // 2)
    mg_block = lambda j: j + COL_MM_Q // MG_TN

    xp = x_prompt.reshape(b * t, d)
    xs = x_sample.reshape(bd * n_new, d)
    ms = bd * n_new
    outs = [[] for _ in range(10)]
    for l in range(depth):
        w_kvp_b = _kv_weight_bf16(w_in[l])
        w_sb_b = (0.5 * w_up_sb[l]).astype(BF16)
        w_bd_b = (0.5 * w_up_band[l]).astype(BF16)
        w_mm_b = (0.5 * w_up_mem[l]).astype(BF16)
        w_out_b = w_out[l].astype(BF16)
        bias_p, bias_d = _band_bias_tables(rel_bias[l], n_new, r_band)

        (ykv, hp, sbk, sbv, bdk, bdv, ykv_s, hs, sbk2, sbv2, bdk2, bdv2) = _kv_projection(
            xp, xs, g_pre[l], w_kvp_b, n_seq=b, n_seq_s=bd, band_keep=band_keep)
        yqg, yqg_s = _col_projection(hp, hs, w_in[l], qg_scale, qg_block, tm=PROJ_TM, tn=KV_TN,
                                     name="qg_projection")
        ymg, ymg_s = _col_projection(hp, hs, w_in[l], mg_scale, mg_block, tm=PROJ_TM, tn=MG_TN,
                                     name="mg_projection")

        mk, mv = _memory_kv(mem_prompt.reshape(b * n_mem, d), g_mem[l], w_mem_kv[l], tm=n_mem)
        yqg3 = yqg.reshape(b, t, QG_WIDTH)
        ykv3 = ykv.reshape(b, t, KV_WIDTH)
        o_sb, o_bd = _prompt_attention(yqg3, ykv3, bias_p, negu2)
        o_mm = _mem_attention(ymg.reshape(b, t, mg_width), mk.reshape(b, n_mem, MEM_WIDTH),
                              mv.reshape(b, n_mem, MEM_WIDTH), tq=MEM_TQ)
        merged = _merge_branches(yqg, ymg, o_sb.reshape(b * t, -1), o_bd.reshape(b * t, -1),
                                 o_mm.reshape(b * t, -1), w_sb_b, w_bd_b, w_mm_b, tm=OUT_TM)
        xp = _out_projection(xp, merged, w_out_b, g_post[l], tm=OUT_TM)
        outs[0].append(_head_major(sbk))
        outs[1].append(_head_major(sbv))
        outs[2].append(_head_major(bdk))
        outs[3].append(_head_major(bdv))
        outs[4].append(mk.reshape(b, n_mem, MEM_HEADS, HEAD_DIM))
        outs[5].append(mv.reshape(b, n_mem, MEM_HEADS, HEAD_DIM))

        o_sb2, o_bd2, o_mm2 = _decode_attention(
            yqg_s.reshape(bd, n_new, QG_WIDTH), ykv_s.reshape(bd, n_new, KV_WIDTH),
            ymg_s.reshape(bd, n_new, mg_width),
            _head_major(cache_sb_k[l]), _head_major(cache_sb_v[l]),
            _head_major(cache_band_k[l]), _head_major(cache_band_v[l]),
            cache_mem_k[l], cache_mem_v[l],
            bias_d, negu2)
        xs = _merge_out(xs, yqg_s, ymg_s, o_sb2.reshape(ms, -1), o_bd2.reshape(ms, -1), o_mm2.reshape(ms, -1),
                        w_sb_b, w_bd_b, w_mm_b, w_out_b, g_post[l])
        outs[6].append(_head_major(sbk2))
        outs[7].append(_head_major(sbv2))
        outs[8].append(_head_major(bdk2))
        outs[9].append(_head_major(bdv2))

    return (xp.reshape(b, t, d), xs.reshape(bd, n_new, d)) + tuple(jnp.stack(o) for o in outs)
```

```python
import functools
import math

import jax
import jax.numpy as jnp
from jax import lax
from jax.experimental import pallas as pl
from jax.experimental.pallas import tpu as pltpu

F32 = jnp.float32
BF16 = jnp.bfloat16

HEAD_DIM = 128
SB_HEADS = 6
BAND_HEADS = 6
MEM_HEADS = 4
SB_WIDTH = SB_HEADS * HEAD_DIM
BAND_WIDTH = BAND_HEADS * HEAD_DIM
MEM_WIDTH = MEM_HEADS * HEAD_DIM
CHUNK = 64
CHUNK_SHIFT = 6
BAND_LEFT_CHUNKS = 8
BAND_ROWS = BAND_LEFT_CHUNKS * CHUNK
MAX_REL = 256
RMS_EPS = 1e-6
NEG_INF = -1e30
LOG2E = math.log2(math.e)
Q_SCALE = HEAD_DIM ** -0.5 * LOG2E

COL_SB_Q = 0
COL_SB_K = COL_SB_Q + SB_WIDTH
COL_SB_V = COL_SB_K + SB_WIDTH
COL_SB_G = COL_SB_V + SB_WIDTH
COL_BD_Q = COL_SB_G + SB_WIDTH
COL_BD_K = COL_BD_Q + BAND_WIDTH
COL_BD_V = COL_BD_K + BAND_WIDTH
COL_BD_G = COL_BD_V + BAND_WIDTH
COL_MM_Q = COL_BD_G + BAND_WIDTH
COL_MM_G = COL_MM_Q + MEM_WIDTH
COL_MG = COL_MM_G + MEM_WIDTH

KV_SB_K = 0
KV_SB_V = KV_SB_K + SB_WIDTH
KV_BD_K = KV_SB_V + SB_WIDTH
KV_BD_V = KV_BD_K + BAND_WIDTH
KV_WIDTH = KV_BD_V + BAND_WIDTH
QG_SB_Q = 0
QG_SB_G = QG_SB_Q + SB_WIDTH
QG_BD_Q = QG_SB_G + SB_WIDTH
QG_BD_G = QG_BD_Q + BAND_WIDTH
QG_WIDTH = QG_BD_G + BAND_WIDTH
MG_MM_Q = 0
MG_MM_G = MG_MM_Q + MEM_WIDTH
MG_MG = MG_MM_G + MEM_WIDTH

VMEM_LIMIT_BYTES = 56 * 1024 * 1024
MAX_VMEM_LIMIT_BYTES = 58 * 1024 * 1024
COMPILER_TEMP_BYTES = 2 * 1024 * 1024
PROJ_TM = 2048
OUT_TM = 512
MEM_TQ = 2048
CAST_ROWS = 512
KV_TN = SB_WIDTH
MG_TN = 1024
SB_TK = 256
SB_TQ = 16 * SB_TK
SB_DEAD = -160.0
BAND_TQ = 4 * CHUNK
BAND_WIN = BAND_TQ + BAND_ROWS
BAND_STEP_GROUPS = 16
BIAS_LANES = 1024
NEW_PAD = 128
HEAD_SLOT = 2


def _params(n_axes, vmem=VMEM_LIMIT_BYTES):
    return pltpu.CompilerParams(dimension_semantics=("arbitrary",) * n_axes,
                                vmem_limit_bytes=vmem)


def _nt_dot(a, b):
    return lax.dot_general(a, b, (((1,), (1,)), ((), ())), preferred_element_type=F32)


def _dot(a, b):
    return jnp.dot(a, b, preferred_element_type=F32)


def _pre_norm_to(h_ref, x_ref, g_ref):
    x = x_ref[...]
    ms = jnp.mean(x * x, axis=-1, keepdims=True)
    h_ref[...] = ((x * lax.rsqrt(ms + RMS_EPS)) * g_ref[...]).astype(BF16)


def _kv_project_rows(x_ref, g_ref, w_refs, y_ref, h_ref, f32_refs, seqs, rows):
    _pre_norm_to(h_ref, x_ref, g_ref)
    for group, (w_ref, dst_ref) in enumerate(zip(w_refs, f32_refs)):
        acc = _dot(h_ref[...], w_ref[...])
        y_ref[:, group * KV_TN:(group + 1) * KV_TN] = acc.astype(BF16)
        for h in range(SB_HEADS):
            for s in range(seqs):
                dst_ref[s, h] = acc[s * rows:(s + 1) * rows, h * HEAD_DIM:(h + 1) * HEAD_DIM]


def _kv_proj_kernel(x_ref, xs_ref, g_ref, wsk_ref, wsv_ref, wbk_ref, wbv_ref,
                    y_ref, h_ref, sbk_ref, sbv_ref, bdk_ref, bdv_ref,
                    ys_ref, hs_ref, sbks_ref, sbvs_ref, bdks_ref, bdvs_ref, *, rows, seqs_s, rows_s):
    w_refs = (wsk_ref, wsv_ref, wbk_ref, wbv_ref)

    @pl.when(pl.program_id(0) == 0)
    def _():
        _kv_project_rows(xs_ref, g_ref, w_refs, ys_ref, hs_ref, (sbks_ref, sbvs_ref, bdks_ref, bdvs_ref),
                         seqs_s, rows_s)

    _kv_project_rows(x_ref, g_ref, w_refs, y_ref, h_ref, (sbk_ref, sbv_ref, bdk_ref, bdv_ref), 1, rows)


def _kv_projection(x2d, xs2d, g_pre, w_kv_b, *, n_seq, n_seq_s, band_keep):
    m, d = x2d.shape
    ms = xs2d.shape[0]
    assert KV_WIDTH == 4 * KV_TN and SB_HEADS == BAND_HEADS
    tm = band_keep
    seq_rows = m // n_seq
    blocks_per_seq = seq_rows // tm
    rows_s = ms // n_seq_s
    kern = functools.partial(_kv_proj_kernel, rows=tm, seqs_s=n_seq_s, rows_s=rows_s)
    sb_spec = pl.BlockSpec((1, SB_HEADS, tm, HEAD_DIM), lambda i: (i // blocks_per_seq, 0, i % blocks_per_seq, 0))
    bd_spec = pl.BlockSpec((1, BAND_HEADS, tm, HEAD_DIM), lambda i: (i // blocks_per_seq, 0, 0, 0))
    sb_shape = jax.ShapeDtypeStruct((n_seq, SB_HEADS, seq_rows, HEAD_DIM), F32)
    bd_shape = jax.ShapeDtypeStruct((n_seq, BAND_HEADS, band_keep, HEAD_DIM), F32)
    whole = lambda shape: pl.BlockSpec(shape, lambda i: (0,) * len(shape))
    s_shape = (n_seq_s, SB_HEADS, rows_s, HEAD_DIM)

    def w_spec(group):
        return pl.BlockSpec((d, KV_TN), functools.partial(lambda i, c: (0, c), c=group),
                            pipeline_mode=pl.Buffered(1))

    return pl.pallas_call(
        kern,
        grid=(m // tm,),
        in_specs=[
            pl.BlockSpec((tm, d), lambda i: (i, 0)),
            whole((ms, d)),
            pl.BlockSpec((1, d), lambda i: (0, 0)),
            w_spec(0), w_spec(1), w_spec(2), w_spec(3),
        ],
        out_specs=[pl.BlockSpec((tm, KV_WIDTH), lambda i: (i, 0)), pl.BlockSpec((tm, d), lambda i: (i, 0)),
                   sb_spec, sb_spec, bd_spec, bd_spec,
                   whole((ms, KV_WIDTH)), whole((ms, d)),
                   whole(s_shape), whole(s_shape), whole(s_shape), whole(s_shape)],
        out_shape=[jax.ShapeDtypeStruct((m, KV_WIDTH), BF16), jax.ShapeDtypeStruct((m, d), BF16),
                   sb_shape, sb_shape, bd_shape, bd_shape,
                   jax.ShapeDtypeStruct((ms, KV_WIDTH), BF16), jax.ShapeDtypeStruct((ms, d), BF16)]
                  + [jax.ShapeDtypeStruct(s_shape, F32)] * 4,
        compiler_params=_params(1),
        name="kv_projection",
    )(x2d, xs2d, g_pre.reshape(1, d), w_kv_b, w_kv_b, w_kv_b, w_kv_b)


def _col_proj_kernel(h_ref, hs_ref, w_ref, cs_ref, y_ref, ys_ref, wb_ref):
    @pl.when(pl.program_id(1) == 0)
    def _():
        wb_ref[...] = w_ref[...].astype(BF16)
        ys_ref[...] = (_dot(hs_ref[...], wb_ref[...]) * cs_ref[...]).astype(BF16)

    y_ref[...] = (_dot(h_ref[...], wb_ref[...]) * cs_ref[...]).astype(BF16)


def _col_projection(h2d, hs2d, w_in, col_scale, src_block, *, tm, tn, name):
    m, d = h2d.shape
    ms = hs2d.shape[0]
    n = col_scale.shape[1]
    assert n % tn == 0 and m % tm == 0
    vmem = (2 * tm * d * 2 + 2 * d * tn * 4 + 2 * tm * tn * 2 + d * tn * 2 + 2 * tm * tn * 4
            + 2 * ms * (d + tn) * 2 + COMPILER_TEMP_BYTES)
    vmem = min(vmem, MAX_VMEM_LIMIT_BYTES)
    return pl.pallas_call(
        _col_proj_kernel,
        grid=(n // tn, m // tm),
        in_specs=[
            pl.BlockSpec((tm, d), lambda j, i: (i, 0)),
            pl.BlockSpec((ms, d), lambda j, i: (0, 0)),
            pl.BlockSpec((d, tn), lambda j, i: (0, src_block(j))),
            pl.BlockSpec((1, tn), lambda j, i: (0, j)),
        ],
        out_specs=[pl.BlockSpec((tm, tn), lambda j, i: (i, j)), pl.BlockSpec((ms, tn), lambda j, i: (0, j))],
        out_shape=[jax.ShapeDtypeStruct((m, n), BF16), jax.ShapeDtypeStruct((ms, n), BF16)],
        scratch_shapes=[pltpu.VMEM((d, tn), BF16)],
        compiler_params=_params(2, vmem=vmem),
        name=name,
    )(h2d, hs2d, w_in, col_scale)


def _kv_weight_kernel(a_ref, b_ref, c_ref, d_ref, o_ref):
    for group, w_ref in enumerate((a_ref, b_ref, c_ref, d_ref)):
        o_ref[:, group * KV_TN:(group + 1) * KV_TN] = w_ref[...].astype(BF16)


def _kv_weight_bf16(w):
    d = w.shape[0]
    tr = CAST_ROWS
    assert d % tr == 0

    def spec(col0):
        assert col0 % KV_TN == 0
        return pl.BlockSpec((tr, KV_TN), functools.partial(lambda i, c: (i, c), c=col0 // KV_TN))

    return pl.pallas_call(
        _kv_weight_kernel,
        grid=(d // tr,),
        in_specs=[spec(COL_SB_K), spec(COL_SB_V), spec(COL_BD_K), spec(COL_BD_V)],
        out_specs=pl.BlockSpec((tr, KV_WIDTH), lambda i: (i, 0)),
        out_shape=jax.ShapeDtypeStruct((d, KV_WIDTH), BF16),
        compiler_params=_params(1),
        name="kv_weight_cast",
    )(w, w, w, w)


def _memkv_kernel(x_ref, g_ref, w_ref, mk_ref, mv_ref):
    x = x_ref[...]
    ms = jnp.mean(x * x, axis=-1, keepdims=True)
    h = ((x * lax.rsqrt(ms + RMS_EPS)) * g_ref[...]).astype(BF16)
    acc = _dot(h, w_ref[...].astype(BF16))
    mk_ref[...] = acc[:, :MEM_WIDTH]
    mv_ref[...] = acc[:, MEM_WIDTH:]


def _memory_kv(mem2d, g_mem, w_bf16, *, tm):
    m, d = mem2d.shape
    return pl.pallas_call(
        _memkv_kernel,
        grid=(m // tm,),
        in_specs=[
            pl.BlockSpec((tm, d), lambda i: (i, 0)),
            pl.BlockSpec((1, d), lambda i: (0, 0)),
            pl.BlockSpec((d, 2 * MEM_WIDTH), lambda i: (0, 0)),
        ],
        out_specs=[pl.BlockSpec((tm, MEM_WIDTH), lambda i: (i, 0)),
                   pl.BlockSpec((tm, MEM_WIDTH), lambda i: (i, 0))],
        out_shape=[jax.ShapeDtypeStruct((m, MEM_WIDTH), F32),
                   jax.ShapeDtypeStruct((m, MEM_WIDTH), F32)],
        compiler_params=_params(1),
        name="memory_kv",
    )(mem2d, g_mem.reshape(1, d), w_bf16)


def _neg_suffix_matrix(n):
    row = lax.broadcasted_iota(jnp.int32, (2 * n, n), 0)
    col = lax.broadcasted_iota(jnp.int32, (2 * n, n), 1)
    row = jnp.where(row >= n, row - n, row)
    return jnp.where(row >= col, -1.0, 0.0).astype(BF16)


def _sb_weights(z2, carry2, negu2, mask):
    p = jnp.maximum(z2, 0.0) + jnp.log(1.0 + jnp.exp2(-jnp.abs(z2))) * LOG2E
    if mask is not None:
        p = jnp.where(mask, p, 0.0)
    p_hi = p.astype(BF16)
    p_lo = (p - p_hi.astype(F32)).astype(BF16)
    suffix = _dot(jnp.concatenate([p_hi, p_lo], axis=1), negu2)
    w = jnp.exp2(z2 + suffix + carry2)
    if mask is not None:
        w = jnp.where(mask, w, 0.0)
    return w, carry2 - jnp.sum(p, axis=-1, keepdims=True)


def _sb_prompt_body(q_ref, k_ref, v_ref, negu2_ref, o_ref, acc_ref, carry_ref, kpad_ref, vpad_ref,
                    *, t, at_first_step, beside_diagonal):
    i = pl.program_id(2)
    n_sub = SB_TQ // SB_TK
    negu2 = negu2_ref[...]

    @pl.when(i == 0)
    def _():
        kpad_ref[0:SB_TK, :] = jnp.zeros((SB_TK, HEAD_DIM), BF16)
        vpad_ref[0:SB_TK, :] = jnp.zeros((SB_TK, HEAD_DIM), BF16)
        kpad_ref[SB_TK:SB_TK + t, :] = k_ref[...]
        vpad_ref[SB_TK:SB_TK + t, :] = v_ref[...]
        at_first_step()

    def kv_block(j):
        start = pl.multiple_of((j + 1) * SB_TK, SB_TK)
        return kpad_ref[pl.ds(start, SB_TK), :], vpad_ref[pl.ds(start, SB_TK), :]

    row = lax.broadcasted_iota(jnp.int32, (SB_TK, SB_TK), 0)
    col = lax.broadcasted_iota(jnp.int32, (SB_TK, SB_TK), 1)
    for r in range(n_sub):
        rows = slice(r * SB_TK, (r + 1) * SB_TK)
        s = i * n_sub + r
        q = q_ref[rows, :]
        kb, vb = kv_block(s)
        w, carry = _sb_weights(_nt_dot(q, kb), jnp.zeros((SB_TK, 1), F32), negu2, col < row)
        acc = _dot(w.astype(BF16), vb)
        kb, vb = kv_block(s - 1)
        prev_exists = None if r > 0 else (jnp.zeros((SB_TK, SB_TK), jnp.int32) + i) > 0
        w, carry = _sb_weights(_nt_dot(q, kb), carry, negu2, prev_exists)
        acc_ref[rows, :] = acc + _dot(w.astype(BF16), vb)
        carry_ref[rows, :] = carry
    beside_diagonal()

    row_q = lax.broadcasted_iota(jnp.int32, (SB_TQ, 1), 0)
    row_t = lax.broadcasted_iota(jnp.int32, (SB_TQ, SB_TK), 0)
    has_more = row_q >= (2 - n_sub * i) * SB_TK

    def any_alive(carry):
        return (jnp.max(jnp.where(has_more, carry, NEG_INF)) > SB_DEAD).astype(jnp.int32)

    def cond(state):
        j, alive = state
        return jnp.logical_and(j >= 0, alive > 0)

    def body(state):
        j, _ = state
        kb, vb = kv_block(j)
        visits = row_t >= (j - n_sub * i + 2) * SB_TK
        w, carry = _sb_weights(_nt_dot(q_ref[...], kb), carry_ref[...], negu2, visits)
        acc_ref[...] += _dot(w.astype(BF16), vb)
        carry_ref[...] = carry
        return j - 1, any_alive(carry)

    lax.while_loop(cond, body, (n_sub * i + n_sub - 3, any_alive(carry_ref[...])))
    o_ref[...] = acc_ref[...].astype(BF16)


def _sb_decode_body(q_ref, kn_ref, vn_ref, kc_hbm, vc_hbm, negu2_ref, o_ref,
                    acc_ref, carry_ref, kpad_ref, vpad_ref, kbuf_ref, vbuf_ref, sem,
                    *, n_new, n_blocks, overlap):
    b = pl.program_id(0)
    heads = SB_HEADS

    def slot_of(j):
        return jnp.where(j == 0, HEAD_SLOT, j % 2)

    def cache_copies(stream, j, slot):
        rows = pl.ds(pl.multiple_of((n_blocks - 1 - j) * SB_TK, SB_TK), SB_TK)
        return (pltpu.make_async_copy(kc_hbm.at[stream, :, rows, :], kbuf_ref.at[slot], sem.at[0, slot]),
                pltpu.make_async_copy(vc_hbm.at[stream, :, rows, :], vbuf_ref.at[slot], sem.at[1, slot]))

    def start_fetch(stream, j, slot):
        for cp in cache_copies(stream, j, slot):
            cp.start()

    def wait_fetch(j, slot):
        for cp in cache_copies(b, j, slot):
            cp.wait()

    @pl.when(b == 0)
    def _():
        start_fetch(b, 0, HEAD_SLOT)

    overlap()

    def head_cols(h):
        return slice(h * HEAD_DIM, (h + 1) * HEAD_DIM)

    def head_rows(h):
        return slice(h * n_new, (h + 1) * n_new)

    def block(k_of, v_of, negu2, mask):
        z2 = jnp.concatenate([_nt_dot(q_ref[:, head_cols(h)], k_of(h)) for h in range(heads)], axis=0)
        w, carry = _sb_weights(z2, carry_ref[...], negu2, mask)
        wb = w.astype(BF16)
        for h in range(heads):
            acc_ref[head_rows(h), :] += _dot(wb[head_rows(h), :], v_of(h))
        carry_ref[...] = carry

    def any_alive():
        return (jnp.max(carry_ref[...]) > SB_DEAD).astype(jnp.int32)

    acc_ref[...] = jnp.zeros_like(acc_ref)
    carry_ref[...] = jnp.zeros_like(carry_ref)
    kpad_ref[...] = jnp.zeros_like(kpad_ref)
    vpad_ref[...] = jnp.zeros_like(vpad_ref)
    for h in range(heads):
        kpad_ref[h, 0:n_new, :] = kn_ref[:, head_cols(h)]
        vpad_ref[h, 0:n_new, :] = vn_ref[:, head_cols(h)]
    row = lax.broadcasted_iota(jnp.int32, (n_new, NEW_PAD), 0)
    col = lax.broadcasted_iota(jnp.int32, (n_new, NEW_PAD), 1)
    mask = jnp.concatenate([(col < row).astype(jnp.int32)] * heads, axis=0) == 1
    block(lambda h: kpad_ref[h], lambda h: vpad_ref[h], _neg_suffix_matrix(NEW_PAD), mask)

    negu2 = negu2_ref[...]

    def cond(state):
        j, alive = state
        return jnp.logical_and(j < n_blocks, alive > 0)

    def body(state):
        j, _ = state
        slot = slot_of(j)
        wait_fetch(j, slot)

        @pl.when(j + 1 < n_blocks)
        def _():
            start_fetch(b, j + 1, (j + 1) % 2)

        block(lambda h: kbuf_ref[slot, h].astype(BF16), lambda h: vbuf_ref[slot, h].astype(BF16),
              negu2, None)
        return j + 1, any_alive()

    j_end, _ = lax.while_loop(cond, body, (0, any_alive()))

    for h in range(heads):
        o_ref[:, head_cols(h)] = acc_ref[head_rows(h), :].astype(BF16)

    @pl.when(j_end < n_blocks)
    def _():
        wait_fetch(j_end, slot_of(j_end))

    @pl.when(b + 1 < pl.num_programs(0))
    def _():
        start_fetch(b + 1, 0, HEAD_SLOT)


def _softmax2_pv(parts):
    mx = functools.reduce(jnp.maximum, [jnp.max(s, axis=-1, keepdims=True) for s, _ in parts])
    num = None
    den = None
    for s, v in parts:
        p = jnp.exp2(s - mx)
        d = jnp.sum(p, axis=-1, keepdims=True)
        o = _dot(p.astype(BF16), v)
        num = o if num is None else num + o
        den = d if den is None else den + d
    return num / den


def _band_bias_kernel(g_ref, tp_ref, td_ref, *, n_new, r_band):
    x = jnp.broadcast_to(g_ref[...], (BAND_TQ, BIAS_LANES))
    x = pltpu.roll(x, BAND_TQ, 1, stride=1, stride_axis=0)
    tbl = x[:, :BAND_WIN] * LOG2E
    r = lax.broadcasted_iota(jnp.int32, (BAND_TQ, BAND_WIN), 0)
    j = lax.broadcasted_iota(jnp.int32, (BAND_TQ, BAND_WIN), 1)
    dc = (j >> CHUNK_SHIFT) - (r >> CHUNK_SHIFT)
    tp_ref[...] = jnp.where(jnp.logical_and(dc >= 0, dc <= BAND_LEFT_CHUNKS), tbl, NEG_INF)
    jd = lax.broadcasted_iota(jnp.int32, (n_new, r_band + NEW_PAD), 1)
    td_ref[...] = jnp.where(jd < r_band + n_new, tbl[:n_new, :r_band + NEW_PAD], NEG_INF)


def _band_bias_tables(rel_bias, n_new, r_band):
    h = rel_bias.shape[0]
    assert BAND_ROWS == 2 * MAX_REL and r_band == BAND_ROWS and BIAS_LANES == 2 * BAND_ROWS
    rb = rel_bias.astype(F32)
    g = jnp.concatenate([rb[:, :0:-1], jnp.broadcast_to(rb[:, -1:], (h, BIAS_LANES - 2 * MAX_REL))], axis=1)
    kern = functools.partial(_band_bias_kernel, n_new=n_new, r_band=r_band)
    return pl.pallas_call(
        kern,
        grid=(h,),
        in_specs=[pl.BlockSpec((None, 1, BIAS_LANES), lambda i: (i, 0, 0))],
        out_specs=[pl.BlockSpec((None, BAND_TQ, BAND_WIN), lambda i: (i, 0, 0)),
                   pl.BlockSpec((None, n_new, r_band + NEW_PAD), lambda i: (i, 0, 0))],
        out_shape=[jax.ShapeDtypeStruct((h, BAND_TQ, BAND_WIN), F32),
                   jax.ShapeDtypeStruct((h, n_new, r_band + NEW_PAD), F32)],
        compiler_params=_params(1),
        name="band_bias",
    )(g.reshape(h, 1, BIAS_LANES))


def _band_pad(k_ref, v_ref, kpad_ref, vpad_ref, t):
    kpad_ref[0:BAND_ROWS, :] = jnp.zeros((BAND_ROWS, HEAD_DIM), BF16)
    vpad_ref[0:BAND_ROWS, :] = jnp.zeros((BAND_ROWS, HEAD_DIM), BF16)
    kpad_ref[BAND_ROWS:BAND_ROWS + t, :] = k_ref[...]
    vpad_ref[BAND_ROWS:BAND_ROWS + t, :] = v_ref[...]


def _band_groups(q_ref, bias_ref, o_ref, kpad_ref, vpad_ref):
    s_idx = pl.program_id(2)
    col = lax.broadcasted_iota(jnp.int32, (BAND_TQ, BAND_WIN), 1)
    for gg in range(BAND_STEP_GROUPS):
        g = s_idx * BAND_STEP_GROUPS + gg
        start = pl.multiple_of(g * BAND_TQ, BAND_TQ)
        rows = slice(gg * BAND_TQ, (gg + 1) * BAND_TQ)
        s = _nt_dot(q_ref[rows, :], kpad_ref[pl.ds(start, BAND_WIN), :]) + bias_ref[...]
        s = jnp.where(col + g * BAND_TQ >= BAND_ROWS, s, NEG_INF)
        o_ref[rows, :] = _softmax2_pv([(s, vpad_ref[pl.ds(start, BAND_WIN), :])]).astype(BF16)


def _prompt_attn_kernel(qs_ref, ks_ref, vs_ref, negu2_ref, qb_ref, kb_ref, vb_ref, bias_ref,
                        osb_ref, obd_ref, acc_ref, carry_ref, kpads_ref, vpads_ref, kpadb_ref, vpadb_ref, *, t):
    _sb_prompt_body(qs_ref, ks_ref, vs_ref, negu2_ref, osb_ref, acc_ref, carry_ref, kpads_ref, vpads_ref, t=t,
                    at_first_step=lambda: _band_pad(kb_ref, vb_ref, kpadb_ref, vpadb_ref, t),
                    beside_diagonal=lambda: _band_groups(qb_ref, bias_ref, obd_ref, kpadb_ref, vpadb_ref))


def _prompt_attention(yqg3, ykv3, bias_tbl, negu2):
    b, t, _ = yqg3.shape
    tq = SB_TQ
    assert SB_TQ == BAND_TQ * BAND_STEP_GROUPS and SB_TQ // SB_TK >= 3 and SB_HEADS == BAND_HEADS
    q_spec = lambda c0: pl.BlockSpec((None, tq, HEAD_DIM),
                                     functools.partial(lambda b, h, i, c: (b, i, c + h), c=c0 // HEAD_DIM))
    kv_spec = lambda c0: pl.BlockSpec((None, t, HEAD_DIM),
                                      functools.partial(lambda b, h, i, c: (b, 0, c + h), c=c0 // HEAD_DIM))
    out_spec = pl.BlockSpec((None, tq, HEAD_DIM), lambda b, h, i: (b, i, h))
    kern = functools.partial(_prompt_attn_kernel, t=t)
    return pl.pallas_call(
        kern,
        grid=(b, SB_HEADS, t // tq),
        in_specs=[
            q_spec(QG_SB_Q), kv_spec(KV_SB_K), kv_spec(KV_SB_V),
            pl.BlockSpec((2 * SB_TK, SB_TK), lambda b, h, i: (0, 0)),
            q_spec(QG_BD_Q), kv_spec(KV_BD_K), kv_spec(KV_BD_V),
            pl.BlockSpec((None, BAND_TQ, BAND_WIN), lambda b, h, i: (h, 0, 0)),
        ],
        out_specs=[out_spec, out_spec],
        out_shape=[jax.ShapeDtypeStruct((b, t, SB_WIDTH), BF16), jax.ShapeDtypeStruct((b, t, BAND_WIDTH), BF16)],
        scratch_shapes=[pltpu.VMEM((SB_TQ, HEAD_DIM), F32), pltpu.VMEM((SB_TQ, 1), F32),
                        pltpu.VMEM((SB_TK + t, HEAD_DIM), BF16), pltpu.VMEM((SB_TK + t, HEAD_DIM), BF16),
                        pltpu.VMEM((BAND_ROWS + t, HEAD_DIM), BF16), pltpu.VMEM((BAND_ROWS + t, HEAD_DIM), BF16)],
        compiler_params=_params(3, vmem=MAX_VMEM_LIMIT_BYTES),
        name="prompt_attention",
    )(yqg3, ykv3, ykv3, negu2, yqg3, ykv3, ykv3, bias_tbl)


def _band_decode_kernel(q_ref, kn_ref, vn_ref, kc_ref, vc_ref, bias_ref, o_ref, kpad_ref, vpad_ref,
                        *, n_new, r_band):
    kpad_ref[...] = jnp.zeros_like(kpad_ref)
    vpad_ref[...] = jnp.zeros_like(vpad_ref)
    for h in range(BAND_HEADS):
        cols = slice(h * HEAD_DIM, (h + 1) * HEAD_DIM)
        kpad_ref[h, 0:n_new, :] = kn_ref[:, cols]
        vpad_ref[h, 0:n_new, :] = vn_ref[:, cols]
    for h in range(BAND_HEADS):
        cols = slice(h * HEAD_DIM, (h + 1) * HEAD_DIM)
        q = q_ref[:, cols]
        s_cache = _nt_dot(q, kc_ref[h].astype(BF16)) + bias_ref[h, :, 0:r_band]
        s_new = _nt_dot(q, kpad_ref[h]) + bias_ref[h, :, r_band:r_band + NEW_PAD]
        o_ref[:, cols] = _softmax2_pv([(s_cache, vc_ref[h].astype(BF16)),
                                       (s_new, vpad_ref[h])]).astype(BF16)


def _mem_attn_kernel(q_ref, mk_ref, mv_ref, o_ref):
    per_head = len(mk_ref.shape) == 3
    for h in range(MEM_HEADS):
        sl = slice(h * HEAD_DIM, (h + 1) * HEAD_DIM)
        mk = mk_ref[:, h, :] if per_head else mk_ref[:, sl]
        mv = mv_ref[:, h, :] if per_head else mv_ref[:, sl]
        s = _nt_dot(q_ref[:, sl], mk.astype(BF16))
        o_ref[:, sl] = _softmax2_pv([(s, mv.astype(BF16))]).astype(BF16)


def _mem_attention(y3, mk, mv, *, tq):
    b, t, _ = y3.shape
    n_mem = mk.shape[1]
    qb = MG_MM_Q // MEM_WIDTH
    return pl.pallas_call(
        _mem_attn_kernel,
        grid=(b, t // tq),
        in_specs=[
            pl.BlockSpec((None, tq, MEM_WIDTH), lambda b, i: (b, i, qb)),
            pl.BlockSpec((None, n_mem, MEM_WIDTH), lambda b, i: (b, 0, 0)),
            pl.BlockSpec((None, n_mem, MEM_WIDTH), lambda b, i: (b, 0, 0)),
        ],
        out_specs=pl.BlockSpec((None, tq, MEM_WIDTH), lambda b, i: (b, i, 0)),
        out_shape=jax.ShapeDtypeStruct((b, t, MEM_WIDTH), BF16),
        compiler_params=_params(2),
        name="mem_attention",
    )(y3, mk, mv)


def _decode_attn_kernel(qs_ref, kns_ref, vns_ref, kcs_hbm, vcs_hbm, negu2_ref,
                        qb_ref, knb_ref, vnb_ref, kcb_ref, vcb_ref, bias_ref, qm_ref, mk_ref, mv_ref,
                        osb_ref, obd_ref, omm_ref,
                        acc_ref, carry_ref, kpad_ref, vpad_ref, kbuf_ref, vbuf_ref, sem, kpadb_ref, vpadb_ref,
                        *, n_new, n_blocks, r_band):
    def band_and_memory():
        _band_decode_kernel(qb_ref, knb_ref, vnb_ref, kcb_ref, vcb_ref, bias_ref, obd_ref,
                            kpadb_ref, vpadb_ref, n_new=n_new, r_band=r_band)
        _mem_attn_kernel(qm_ref, mk_ref, mv_ref, omm_ref)

    _sb_decode_body(qs_ref, kns_ref, vns_ref, kcs_hbm, vcs_hbm, negu2_ref, osb_ref,
                    acc_ref, carry_ref, kpad_ref, vpad_ref, kbuf_ref, vbuf_ref, sem,
                    n_new=n_new, n_blocks=n_blocks, overlap=band_and_memory)


def _decode_attention(yqg3, ykv3, ymg3, cache_sb_k, cache_sb_v, cache_bd_k, cache_bd_v, mk, mv, bias_tbl, negu2):
    bd, n_new, _ = yqg3.shape
    past = cache_sb_k.shape[2]
    r_band = cache_bd_k.shape[2]
    n_mem = mk.shape[1]
    assert past % SB_TK == 0 and n_new <= NEW_PAD and n_new % 16 == 0
    kern = functools.partial(_decode_attn_kernel, n_new=n_new, n_blocks=past // SB_TK, r_band=r_band)
    slab = lambda width, col0: pl.BlockSpec((None, n_new, width), functools.partial(lambda b, c: (b, 0, c), c=col0 // width))
    band_cache = pl.BlockSpec((None, BAND_HEADS, r_band, HEAD_DIM), lambda b: (b, 0, 0, 0))
    mem_cache = pl.BlockSpec((None, n_mem, MEM_HEADS, HEAD_DIM), lambda b: (b, 0, 0, 0))
    return pl.pallas_call(
        kern,
        grid=(bd,),
        in_specs=[
            slab(SB_WIDTH, QG_SB_Q), slab(SB_WIDTH, KV_SB_K), slab(SB_WIDTH, KV_SB_V),
            pl.BlockSpec(memory_space=pl.ANY), pl.BlockSpec(memory_space=pl.ANY),
            pl.BlockSpec((2 * SB_TK, SB_TK), lambda b: (0, 0)),
            slab(BAND_WIDTH, QG_BD_Q), slab(BAND_WIDTH, KV_BD_K), slab(BAND_WIDTH, KV_BD_V),
            band_cache, band_cache,
            pl.BlockSpec((BAND_HEADS, n_new, r_band + NEW_PAD), lambda b: (0, 0, 0)),
            slab(MEM_WIDTH, MG_MM_Q), mem_cache, mem_cache,
        ],
        out_specs=[pl.BlockSpec((None, n_new, SB_WIDTH), lambda b: (b, 0, 0)),
                   pl.BlockSpec((None, n_new, BAND_WIDTH), lambda b: (b, 0, 0)),
                   pl.BlockSpec((None, n_new, MEM_WIDTH), lambda b: (b, 0, 0))],
        out_shape=[jax.ShapeDtypeStruct((bd, n_new, SB_WIDTH), BF16),
                   jax.ShapeDtypeStruct((bd, n_new, BAND_WIDTH), BF16),
                   jax.ShapeDtypeStruct((bd, n_new, MEM_WIDTH), BF16)],
        scratch_shapes=[pltpu.VMEM((SB_HEADS * n_new, HEAD_DIM), F32),
                        pltpu.VMEM((SB_HEADS * n_new, 1), F32),
                        pltpu.VMEM((SB_HEADS, NEW_PAD, HEAD_DIM), BF16),
                        pltpu.VMEM((SB_HEADS, NEW_PAD, HEAD_DIM), BF16),
                        pltpu.VMEM((HEAD_SLOT + 1, SB_HEADS, SB_TK, HEAD_DIM), F32),
                        pltpu.VMEM((HEAD_SLOT + 1, SB_HEADS, SB_TK, HEAD_DIM), F32),
                        pltpu.SemaphoreType.DMA((2, HEAD_SLOT + 1)),
                        pltpu.VMEM((BAND_HEADS, NEW_PAD, HEAD_DIM), BF16),
                        pltpu.VMEM((BAND_HEADS, NEW_PAD, HEAD_DIM), BF16)],
        compiler_params=_params(1),
        name="decode_attention",
    )(yqg3, ykv3, ykv3, cache_sb_k, cache_sb_v, negu2, yqg3, ykv3, ykv3, cache_bd_k, cache_bd_v, bias_tbl,
      ymg3, mk, mv)


def _silu_of_half(h):
    return h + h * jnp.tanh(h)


def _merge_kernel(osb_ref, obd_ref, omm_ref, gsb_ref, gbd_ref, gmm_ref,
                  mg0_ref, mg1_ref, mg2_ref, mg3_ref, mg4_ref, mg5_ref,
                  wsb_ref, wbd_ref, wmm_ref, merged_ref, *, half):
    u_sb = (osb_ref[...].astype(F32) * _silu_of_half(gsb_ref[...].astype(F32))).astype(BF16)
    u_bd = (obd_ref[...].astype(F32) * _silu_of_half(gbd_ref[...].astype(F32))).astype(BF16)
    u_mm = (omm_ref[...].astype(F32) * _silu_of_half(gmm_ref[...].astype(F32))).astype(BF16)
    mg = ((mg0_ref, mg2_ref, mg4_ref), (mg1_ref, mg3_ref, mg5_ref))
    for n in range(2):
        cols = slice(n * half, (n + 1) * half)
        merged = None
        for m_ref, u, w_ref in zip(mg[n], (u_sb, u_bd, u_mm), (wsb_ref, wbd_ref, wmm_ref)):
            a = _dot(u, w_ref[:, cols])
            term = a + a * jnp.tanh(m_ref[...].astype(F32))
            merged = term if merged is None else merged + term
        merged_ref[:, cols] = merged.astype(BF16)


def _merge_branches(yqg, ymg, o_sb, o_bd, o_mm, w_sb, w_bd, w_mm, *, tm):
    m = yqg.shape[0]
    d = w_sb.shape[1]
    half = d // 2
    assert MG_MG % half == 0
    mgb = MG_MG // half
    const = dict(pipeline_mode=pl.Buffered(1))
    kern = functools.partial(_merge_kernel, half=half)
    return pl.pallas_call(
        kern,
        grid=(m // tm,),
        in_specs=[
            pl.BlockSpec((tm, SB_WIDTH), lambda i: (i, 0)),
            pl.BlockSpec((tm, BAND_WIDTH), lambda i: (i, 0)),
            pl.BlockSpec((tm, MEM_WIDTH), lambda i: (i, 0)),
            pl.BlockSpec((tm, SB_WIDTH), lambda i: (i, QG_SB_G // SB_WIDTH)),
            pl.BlockSpec((tm, BAND_WIDTH), lambda i: (i, QG_BD_G // BAND_WIDTH)),
            pl.BlockSpec((tm, MEM_WIDTH), lambda i: (i, MG_MM_G // MEM_WIDTH)),
        ] + [pl.BlockSpec((tm, half), functools.partial(lambda i, c: (i, c), c=mgb + c)) for c in range(6)] + [
            pl.BlockSpec((SB_WIDTH, d), lambda i: (0, 0), **const),
            pl.BlockSpec((BAND_WIDTH, d), lambda i: (0, 0), **const),
            pl.BlockSpec((MEM_WIDTH, d), lambda i: (0, 0), **const),
        ],
        out_specs=pl.BlockSpec((tm, d), lambda i: (i, 0)),
        out_shape=jax.ShapeDtypeStruct((m, d), BF16),
        compiler_params=_params(1),
        name="merge_branches",
    )(o_sb, o_bd, o_mm, yqg, yqg, ymg, *([ymg] * 6), w_sb, w_bd, w_mm)


def _out_proj_kernel(x_ref, merged_ref, wout_ref, gpost_ref, y_ref):
    y = _dot(merged_ref[...], wout_ref[...])
    ms = jnp.mean(y * y, axis=-1, keepdims=True)
    y_ref[...] = x_ref[...] + (y * lax.rsqrt(ms + RMS_EPS)) * gpost_ref[...]


def _merge_out_kernel(x_ref, osb_ref, obd_ref, omm_ref, gsb_ref, gbd_ref, gmm_ref,
                      mg0_ref, mg1_ref, mg2_ref, mg3_ref, mg4_ref, mg5_ref,
                      wsb_ref, wbd_ref, wmm_ref, wout_ref, gpost_ref, y_ref, merged_ref, *, half):
    _merge_kernel(osb_ref, obd_ref, omm_ref, gsb_ref, gbd_ref, gmm_ref,
                  mg0_ref, mg1_ref, mg2_ref, mg3_ref, mg4_ref, mg5_ref,
                  wsb_ref, wbd_ref, wmm_ref, merged_ref, half=half)
    _out_proj_kernel(x_ref, merged_ref, wout_ref, gpost_ref, y_ref)


def _merge_out(x2d, yqg, ymg, o_sb, o_bd, o_mm, w_sb, w_bd, w_mm, w_out, g_post):
    m, d = x2d.shape
    half = d // 2
    mgb = MG_MG // half
    whole = lambda shape: pl.BlockSpec(shape, lambda i: (0,) * len(shape))
    col = lambda width, c: pl.BlockSpec((m, width), functools.partial(lambda i, c: (0, c), c=c))
    kern = functools.partial(_merge_out_kernel, half=half)
    return pl.pallas_call(
        kern,
        grid=(1,),
        in_specs=[
            whole((m, d)), whole((m, SB_WIDTH)), whole((m, BAND_WIDTH)), whole((m, MEM_WIDTH)),
            col(SB_WIDTH, QG_SB_G // SB_WIDTH), col(BAND_WIDTH, QG_BD_G // BAND_WIDTH),
            col(MEM_WIDTH, MG_MM_G // MEM_WIDTH),
        ] + [col(half, mgb + c) for c in range(6)] + [
            whole((SB_WIDTH, d)), whole((BAND_WIDTH, d)), whole((MEM_WIDTH, d)), whole((d, d)), whole((1, d)),
        ],
        out_specs=whole((m, d)),
        out_shape=jax.ShapeDtypeStruct((m, d), F32),
        scratch_shapes=[pltpu.VMEM((m, d), BF16)],
        compiler_params=_params(1),
        name="merge_out",
    )(x2d, o_sb, o_bd, o_mm, yqg, yqg, ymg, *([ymg] * 6), w_sb, w_bd, w_mm, w_out, g_post.reshape(1, d))


def _out_projection(x2d, merged, w_out, g_post, *, tm):
    m, d = x2d.shape
    return pl.pallas_call(
        _out_proj_kernel,
        grid=(m // tm,),
        in_specs=[
            pl.BlockSpec((tm, d), lambda i: (i, 0)),
            pl.BlockSpec((tm, d), lambda i: (i, 0)),
            pl.BlockSpec((d, d), lambda i: (0, 0), pipeline_mode=pl.Buffered(1)),
            pl.BlockSpec((1, d), lambda i: (0, 0)),
        ],
        out_specs=pl.BlockSpec((tm, d), lambda i: (i, 0)),
        out_shape=jax.ShapeDtypeStruct((m, d), F32),
        compiler_params=_params(1),
        name="out_projection",
    )(x2d, merged, w_out, g_post.reshape(1, d))


def _head_major(a):
    return jnp.transpose(a, (0, 2, 1, 3))


def kernel(x_prompt, x_sample, cache_sb_k, cache_sb_v, cache_band_k, cache_band_v, cache_mem_k, cache_mem_v, mem_prompt, g_pre, w_in, rel_bias, g_mem, w_mem_kv, w_up_sb, w_up_band, w_up_mem, w_out, g_post):
    depth = w_in.shape[0]
    b, t, d = x_prompt.shape
    bd, n_new, _ = x_sample.shape
    n_mem = mem_prompt.shape[1]
    r_band = cache_band_k.shape[2]
    in_width = w_in.shape[2]
    band_keep = min(BAND_ROWS, t)
    assert COL_MG + 3 * d == in_width
    assert t % SB_TQ == 0 and t % (BAND_TQ * BAND_STEP_GROUPS) == 0 and t % band_keep == 0
    assert r_band == BAND_ROWS and n_new <= CHUNK

    negu2 = jnp.where(jnp.arange(2 * SB_TK)[:, None] % SB_TK >= jnp.arange(SB_TK)[None, :], -1.0, 0.0).astype(BF16)
    mg_width = in_width - COL_MM_Q
    assert COL_MM_Q % MG_TN == 0 and mg_width % MG_TN == 0 and COL_SB_G == 3 * KV_TN and COL_BD_Q == 4 * KV_TN
    cols = jnp.arange(QG_WIDTH)
    qg_scale = jnp.where((cols // SB_WIDTH) % 2 == 0, Q_SCALE, 0.5).astype(F32).reshape(1, QG_WIDTH)
    cols = jnp.arange(mg_width)
    mg_scale = jnp.where(cols < MG_MM_G, Q_SCALE, 0.5).astype(F32).reshape(1, mg_width)
    qg_block = lambda j: j + 2 * ((j + 1) // 2)
    mg_block = lambda j: j + COL_MM_Q // MG_TN

    xp = x_prompt.reshape(b * t, d)
    xs = x_sample.reshape(bd * n_new, d)
    ms = bd * n_new
    outs = [[] for _ in range(10)]
    for l in range(depth):
        w_kvp_b = _kv_weight_bf16(w_in[l])
        w_sb_b = (0.5 * w_up_sb[l]).astype(BF16)
        w_bd_b = (0.5 * w_up_band[l]).astype(BF16)
        w_mm_b = (0.5 * w_up_mem[l]).astype(BF16)
        w_out_b = w_out[l].astype(BF16)
        bias_p, bias_d = _band_bias_tables(rel_bias[l], n_new, r_band)

        (ykv, hp, sbk, sbv, bdk, bdv, ykv_s, hs, sbk2, sbv2, bdk2, bdv2) = _kv_projection(
            xp, xs, g_pre[l], w_kvp_b, n_seq=b, n_seq_s=bd, band_keep=band_keep)
        yqg, yqg_s = _col_projection(hp, hs, w_in[l], qg_scale, qg_block, tm=PROJ_TM, tn=KV_TN,
                                     name="qg_projection")
        ymg, ymg_s = _col_projection(hp, hs, w_in[l], mg_scale, mg_block, tm=PROJ_TM, tn=MG_TN,
                                     name="mg_projection")

        mk, mv = _memory_kv(mem_prompt.reshape(b * n_mem, d), g_mem[l], w_mem_kv[l], tm=n_mem)
        yqg3 = yqg.reshape(b, t, QG_WIDTH)
        ykv3 = ykv.reshape(b, t, KV_WIDTH)
        o_sb, o_bd = _prompt_attention(yqg3, ykv3, bias_p, negu2)
        o_mm = _mem_attention(ymg.reshape(b, t, mg_width), mk.reshape(b, n_mem, MEM_WIDTH),
                              mv.reshape(b, n_mem, MEM_WIDTH), tq=MEM_TQ)
        merged = _merge_branches(yqg, ymg, o_sb.reshape(b * t, -1), o_bd.reshape(b * t, -1),
                                 o_mm.reshape(b * t, -1), w_sb_b, w_bd_b, w_mm_b, tm=OUT_TM)
        xp = _out_projection(xp, merged, w_out_b, g_post[l], tm=OUT_TM)
        outs[0].append(_head_major(sbk))
        outs[1].append(_head_major(sbv))
        outs[2].append(_head_major(bdk))
        outs[3].append(_head_major(bdv))
        outs[4].append(mk.reshape(b, n_mem, MEM_HEADS, HEAD_DIM))
        outs[5].append(mv.reshape(b, n_mem, MEM_HEADS, HEAD_DIM))

        o_sb2, o_bd2, o_mm2 = _decode_attention(
            yqg_s.reshape(bd, n_new, QG_WIDTH), ykv_s.reshape(bd, n_new, KV_WIDTH),
            ymg_s.reshape(bd, n_new, mg_width),
            _head_major(cache_sb_k[l]), _head_major(cache_sb_v[l]),
            _head_major(cache_band_k[l]), _head_major(cache_band_v[l]),
            cache_mem_k[l], cache_mem_v[l],
            bias_d, negu2)
        xs = _merge_out(xs, yqg_s, ymg_s, o_sb2.reshape(ms, -1), o_bd2.reshape(ms, -1), o_mm2.reshape(ms, -1),
                        w_sb_b, w_bd_b, w_mm_b, w_out_b, g_post[l])
        outs[6].append(_head_major(sbk2))
        outs[7].append(_head_major(sbv2))
        outs[8].append(_head_major(bdk2))
        outs[9].append(_head_major(bdv2))

    return (xp.reshape(b, t, d), xs.reshape(bd, n_new, d)) + tuple(jnp.stack(o) for o in outs)
```

```python
import functools
import math

import jax
import jax.numpy as jnp
from jax import lax
from jax.experimental import pallas as pl
from jax.experimental.pallas import tpu as pltpu

F32 = jnp.float32
BF16 = jnp.bfloat16

HEAD_DIM = 128
SB_HEADS = 6
BAND_HEADS = 6
MEM_HEADS = 4
SB_WIDTH = SB_HEADS * HEAD_DIM
BAND_WIDTH = BAND_HEADS * HEAD_DIM
MEM_WIDTH = MEM_HEADS * HEAD_DIM
CHUNK = 64
CHUNK_SHIFT = 6
BAND_LEFT_CHUNKS = 8
BAND_ROWS = BAND_LEFT_CHUNKS * CHUNK
MAX_REL = 256
RMS_EPS = 1e-6
NEG_INF = -1e30
LOG2E = math.log2(math.e)
Q_SCALE = HEAD_DIM ** -0.5 * LOG2E

COL_SB_Q = 0
COL_SB_K = COL_SB_Q + SB_WIDTH
COL_SB_V = COL_SB_K + SB_WIDTH
COL_SB_G = COL_SB_V + SB_WIDTH
COL_BD_Q = COL_SB_G + SB_WIDTH
COL_BD_K = COL_BD_Q + BAND_WIDTH
COL_BD_V = COL_BD_K + BAND_WIDTH
COL_BD_G = COL_BD_V + BAND_WIDTH
COL_MM_Q = COL_BD_G + BAND_WIDTH
COL_MM_G = COL_MM_Q + MEM_WIDTH
COL_MG = COL_MM_G + MEM_WIDTH

KV_SB_K = 0
KV_SB_V = KV_SB_K + SB_WIDTH
KV_BD_K = KV_SB_V + SB_WIDTH
KV_BD_V = KV_BD_K + BAND_WIDTH
KV_WIDTH = KV_BD_V + BAND_WIDTH
QG_SB_Q = 0
QG_SB_G = QG_SB_Q + SB_WIDTH
QG_BD_Q = QG_SB_G + SB_WIDTH
QG_BD_G = QG_BD_Q + BAND_WIDTH
QG_WIDTH = QG_BD_G + BAND_WIDTH
MG_MM_Q = 0
MG_MM_G = MG_MM_Q + MEM_WIDTH
MG_MG = MG_MM_G + MEM_WIDTH

VMEM_LIMIT_BYTES = 56 * 1024 * 1024
MAX_VMEM_LIMIT_BYTES = 58 * 1024 * 1024
COMPILER_TEMP_BYTES = 2 * 1024 * 1024
PROJ_TM = 2048
OUT_TM = 512
MEM_TQ = 2048
CAST_ROWS = 512
KV_TN = SB_WIDTH
MG_TN = 1024
SB_TK = 256
SB_TQ = 16 * SB_TK
SB_DEAD = -160.0
BAND_TQ = 4 * CHUNK
BAND_WIN = BAND_TQ + BAND_ROWS
BAND_STEP_GROUPS = 16
BIAS_LANES = 1024
NEW_PAD = 128
HEAD_SLOT = 2


def _params(n_axes, vmem=VMEM_LIMIT_BYTES):
    return pltpu.CompilerParams(dimension_semantics=("arbitrary",) * n_axes,
                                vmem_limit_bytes=vmem)


def _nt_dot(a, b):
    return lax.dot_general(a, b, (((1,), (1,)), ((), ())), preferred_element_type=F32)


def _dot(a, b):
    return jnp.dot(a, b, preferred_element_type=F32)


def _pre_norm_to(h_ref, x_ref, g_ref):
    x = x_ref[...]
    ms = jnp.mean(x * x, axis=-1, keepdims=True)
    h_ref[...] = ((x * lax.rsqrt(ms + RMS_EPS)) * g_ref[...]).astype(BF16)


def _kv_project_rows(x_ref, g_ref, w_refs, y_ref, h_ref, f32_refs, seqs, rows):
    _pre_norm_to(h_ref, x_ref, g_ref)
    for group, (w_ref, dst_ref) in enumerate(zip(w_refs, f32_refs)):
        acc = _dot(h_ref[...], w_ref[...])
        y_ref[:, group * KV_TN:(group + 1) * KV_TN] = acc.astype(BF16)
        for h in range(SB_HEADS):
            for s in range(seqs):
                dst_ref[s, h] = acc[s * rows:(s + 1) * rows, h * HEAD_DIM:(h + 1) * HEAD_DIM]


def _kv_proj_kernel(x_ref, xs_ref, g_ref, wsk_ref, wsv_ref, wbk_ref, wbv_ref,
                    y_ref, h_ref, sbk_ref, sbv_ref, bdk_ref, bdv_ref,
                    ys_ref, hs_ref, sbks_ref, sbvs_ref, bdks_ref, bdvs_ref, *, rows, seqs_s, rows_s):
    w_refs = (wsk_ref, wsv_ref, wbk_ref, wbv_ref)

    @pl.when(pl.program_id(0) == 0)
    def _():
        _kv_project_rows(xs_ref, g_ref, w_refs, ys_ref, hs_ref, (sbks_ref, sbvs_ref, bdks_ref, bdvs_ref),
                         seqs_s, rows_s)

    _kv_project_rows(x_ref, g_ref, w_refs, y_ref, h_ref, (sbk_ref, sbv_ref, bdk_ref, bdv_ref), 1, rows)


def _kv_projection(x2d, xs2d, g_pre, w_kv_b, *, n_seq, n_seq_s, band_keep):
    m, d = x2d.shape
    ms = xs2d.shape[0]
    assert KV_WIDTH == 4 * KV_TN and SB_HEADS == BAND_HEADS
    tm = band_keep
    seq_rows = m // n_seq
    blocks_per_seq = seq_rows // tm
    rows_s = ms // n_seq_s
    kern = functools.partial(_kv_proj_kernel, rows=tm, seqs_s=n_seq_s, rows_s=rows_s)
    sb_spec = pl.BlockSpec((1, SB_HEADS, tm, HEAD_DIM), lambda i: (i // blocks_per_seq, 0, i % blocks_per_seq, 0))
    bd_spec = pl.BlockSpec((1, BAND_HEADS, tm, HEAD_DIM), lambda i: (i // blocks_per_seq, 0, 0, 0))
    sb_shape = jax.ShapeDtypeStruct((n_seq, SB_HEADS, seq_rows, HEAD_DIM), F32)
    bd_shape = jax.ShapeDtypeStruct((n_seq, BAND_HEADS, band_keep, HEAD_DIM), F32)
    whole = lambda shape: pl.BlockSpec(shape, lambda i: (0,) * len(shape))
    s_shape = (n_seq_s, SB_HEADS, rows_s, HEAD_DIM)

    def w_spec(group):
        return pl.BlockSpec((d, KV_TN), functools.partial(lambda i, c: (0, c), c=group),
                            pipeline_mode=pl.Buffered(1))

    return pl.pallas_call(
        kern,
        grid=(m // tm,),
        in_specs=[
            pl.BlockSpec((tm, d), lambda i: (i, 0)),
            whole((ms, d)),
            pl.BlockSpec((1, d), lambda i: (0, 0)),
            w_spec(0), w_spec(1), w_spec(2), w_spec(3),
        ],
        out_specs=[pl.BlockSpec((tm, KV_WIDTH), lambda i: (i, 0)), pl.BlockSpec((tm, d), lambda i: (i, 0)),
                   sb_spec, sb_spec, bd_spec, bd_spec,
                   whole((ms, KV_WIDTH)), whole((ms, d)),
                   whole(s_shape), whole(s_shape), whole(s_shape), whole(s_shape)],
        out_shape=[jax.ShapeDtypeStruct((m, KV_WIDTH), BF16), jax.ShapeDtypeStruct((m, d), BF16),
                   sb_shape, sb_shape, bd_shape, bd_shape,
                   jax.ShapeDtypeStruct((ms, KV_WIDTH), BF16), jax.ShapeDtypeStruct((ms, d), BF16)]
                  + [jax.ShapeDtypeStruct(s_shape, F32)] * 4,
        compiler_params=_params(1),
        name="kv_projection",
    )(x2d, xs2d, g_pre.reshape(1, d), w_kv_b, w_kv_b, w_kv_b, w_kv_b)


def _col_proj_kernel(h_ref, hs_ref, w_ref, cs_ref, y_ref, ys_ref, wb_ref):
    @pl.when(pl.program_id(1) == 0)
    def _():
        wb_ref[...] = w_ref[...].astype(BF16)
        ys_ref[...] = (_dot(hs_ref[...], wb_ref[...]) * cs_ref[...]).astype(BF16)

    y_ref[...] = (_dot(h_ref[...], wb_ref[...]) * cs_ref[...]).astype(BF16)


def _col_projection(h2d, hs2d, w_in, col_scale, src_block, *, tm, tn, name):
    m, d = h2d.shape
    ms = hs2d.shape[0]
    n = col_scale.shape[1]
    assert n % tn == 0 and m % tm == 0
    vmem = (2 * tm * d * 2 + 2 * d * tn * 4 + 2 * tm * tn * 2 + d * tn * 2 + 2 * tm * tn * 4
            + 2 * ms * (d + tn) * 2 + COMPILER_TEMP_BYTES)
    vmem = min(vmem, MAX_VMEM_LIMIT_BYTES)
    return pl.pallas_call(
        _col_proj_kernel,
        grid=(n // tn, m // tm),
        in_specs=[
            pl.BlockSpec((tm, d), lambda j, i: (i, 0)),
            pl.BlockSpec((ms, d), lambda j, i: (0, 0)),
            pl.BlockSpec((d, tn), lambda j, i: (0, src_block(j))),
            pl.BlockSpec((1, tn), lambda j, i: (0, j)),
        ],
        out_specs=[pl.BlockSpec((tm, tn), lambda j, i: (i, j)), pl.BlockSpec((ms, tn), lambda j, i: (0, j))],
        out_shape=[jax.ShapeDtypeStruct((m, n), BF16), jax.ShapeDtypeStruct((ms, n), BF16)],
        scratch_shapes=[pltpu.VMEM((d, tn), BF16)],
        compiler_params=_params(2, vmem=vmem),
        name=name,
    )(h2d, hs2d, w_in, col_scale)


def _kv_weight_kernel(a_ref, b_ref, c_ref, d_ref, o_ref):
    for group, w_ref in enumerate((a_ref, b_ref, c_ref, d_ref)):
        o_ref[:, group * KV_TN:(group + 1) * KV_TN] = w_ref[...].astype(BF16)


def _kv_weight_bf16(w):
    d = w.shape[0]
    tr = CAST_ROWS
    assert d % tr == 0

    def spec(col0):
        assert col0 % KV_TN == 0
        return pl.BlockSpec((tr, KV_TN), functools.partial(lambda i, c: (i, c), c=col0 // KV_TN))

    return pl.pallas_call(
        _kv_weight_kernel,
        grid=(d // tr,),
        in_specs=[spec(COL_SB_K), spec(COL_SB_V), spec(COL_BD_K), spec(COL_BD_V)],
        out_specs=pl.BlockSpec((tr, KV_WIDTH), lambda i: (i, 0)),
        out_shape=jax.ShapeDtypeStruct((d, KV_WIDTH), BF16),
        compiler_params=_params(1),
        name="kv_weight_cast",
    )(w, w, w, w)


def _memkv_kernel(x_ref, g_ref, w_ref, mk_ref, mv_ref):
    x = x_ref[...]
    ms = jnp.mean(x * x, axis=-1, keepdims=True)
    h = ((x * lax.rsqrt(ms + RMS_EPS)) * g_ref[...]).astype(BF16)
    acc = _dot(h, w_ref[...].astype(BF16))
    mk_ref[...] = acc[:, :MEM_WIDTH]
    mv_ref[...] = acc[:, MEM_WIDTH:]


def _memory_kv(mem2d, g_mem, w_bf16, *, tm):
    m, d = mem2d.shape
    return pl.pallas_call(
        _memkv_kernel,
        grid=(m // tm,),
        in_specs=[
            pl.BlockSpec((tm, d), lambda i: (i, 0)),
            pl.BlockSpec((1, d), lambda i: (0, 0)),
            pl.BlockSpec((d, 2 * MEM_WIDTH), lambda i: (0, 0)),
        ],
        out_specs=[pl.BlockSpec((tm, MEM_WIDTH), lambda i: (i, 0)),
                   pl.BlockSpec((tm, MEM_WIDTH), lambda i: (i, 0))],
        out_shape=[jax.ShapeDtypeStruct((m, MEM_WIDTH), F32),
                   jax.ShapeDtypeStruct((m, MEM_WIDTH), F32)],
        compiler_params=_params(1),
        name="memory_kv",
    )(mem2d, g_mem.reshape(1, d), w_bf16)


def _neg_suffix_matrix(n):
    row = lax.broadcasted_iota(jnp.int32, (2 * n, n), 0)
    col = lax.broadcasted_iota(jnp.int32, (2 * n, n), 1)
    row = jnp.where(row >= n, row - n, row)
    return jnp.where(row >= col, -1.0, 0.0).astype(BF16)


def _sb_weights(z2, carry2, negu2, mask):
    p = jnp.maximum(z2, 0.0) + jnp.log(1.0 + jnp.exp2(-jnp.abs(z2))) * LOG2E
    if mask is not None:
        p = jnp.where(mask, p, 0.0)
    p_hi = p.astype(BF16)
    p_lo = (p - p_hi.astype(F32)).astype(BF16)
    suffix = _dot(jnp.concatenate([p_hi, p_lo], axis=1), negu2)
    w = jnp.exp2(z2 + suffix + carry2)
    if mask is not None:
        w = jnp.where(mask, w, 0.0)
    return w, carry2 - jnp.sum(p, axis=-1, keepdims=True)


def _sb_prompt_body(q_ref, k_ref, v_ref, negu2_ref, o_ref, acc_ref, carry_ref, kpad_ref, vpad_ref,
                    *, t, at_first_step, beside_diagonal):
    i = pl.program_id(2)
    n_sub = SB_TQ // SB_TK
    negu2 = negu2_ref[...]

    @pl.when(i == 0)
    def _():
        kpad_ref[0:SB_TK, :] = jnp.zeros((SB_TK, HEAD_DIM), BF16)
        vpad_ref[0:SB_TK, :] = jnp.zeros((SB_TK, HEAD_DIM), BF16)
        kpad_ref[SB_TK:SB_TK + t, :] = k_ref[...]
        vpad_ref[SB_TK:SB_TK + t, :] = v_ref[...]
        at_first_step()

    def kv_block(j):
        start = pl.multiple_of((j + 1) * SB_TK, SB_TK)
        return kpad_ref[pl.ds(start, SB_TK), :], vpad_ref[pl.ds(start, SB_TK), :]

    row = lax.broadcasted_iota(jnp.int32, (SB_TK, SB_TK), 0)
    col = lax.broadcasted_iota(jnp.int32, (SB_TK, SB_TK), 1)
    for r in range(n_sub):
        rows = slice(r * SB_TK, (r + 1) * SB_TK)
        s = i * n_sub + r
        q = q_ref[rows, :]
        kb, vb = kv_block(s)
        w, carry = _sb_weights(_nt_dot(q, kb), jnp.zeros((SB_TK, 1), F32), negu2, col < row)
        acc = _dot(w.astype(BF16), vb)
        kb, vb = kv_block(s - 1)
        prev_exists = None if r > 0 else (jnp.zeros((SB_TK, SB_TK), jnp.int32) + i) > 0
        w, carry = _sb_weights(_nt_dot(q, kb), carry, negu2, prev_exists)
        acc_ref[rows, :] = acc + _dot(w.astype(BF16), vb)
        carry_ref[rows, :] = carry
    beside_diagonal()

    row_q = lax.broadcasted_iota(jnp.int32, (SB_TQ, 1), 0)
    row_t = lax.broadcasted_iota(jnp.int32, (SB_TQ, SB_TK), 0)
    has_more = row_q >= (2 - n_sub * i) * SB_TK

    def any_alive(carry):
        return (jnp.max(jnp.where(has_more, carry, NEG_INF)) > SB_DEAD).astype(jnp.int32)

    def cond(state):
        j, alive = state
        return jnp.logical_and(j >= 0, alive > 0)

    def body(state):
        j, _ = state
        kb, vb = kv_block(j)
        visits = row_t >= (j - n_sub * i + 2) * SB_TK
        w, carry = _sb_weights(_nt_dot(q_ref[...], kb), carry_ref[...], negu2, visits)
        acc_ref[...] += _dot(w.astype(BF16), vb)
        carry_ref[...] = carry
        return j - 1, any_alive(carry)

    lax.while_loop(cond, body, (n_sub * i + n_sub - 3, any_alive(carry_ref[...])))
    o_ref[...] = acc_ref[...].astype(BF16)


def _sb_decode_body(q_ref, kn_ref, vn_ref, kc_hbm, vc_hbm, negu2_ref, o_ref,
                    acc_ref, carry_ref, kpad_ref, vpad_ref, kbuf_ref, vbuf_ref, sem,
                    *, n_new, n_blocks, overlap):
    b = pl.program_id(0)
    heads = SB_HEADS

    def slot_of(j):
        return jnp.where(j == 0, HEAD_SLOT, j % 2)

    def cache_copies(stream, j, slot):
        rows = pl.ds(pl.multiple_of((n_blocks - 1 - j) * SB_TK, SB_TK), SB_TK)
        return (pltpu.make_async_copy(kc_hbm.at[stream, :, rows, :], kbuf_ref.at[slot], sem.at[0, slot]),
                pltpu.make_async_copy(vc_hbm.at[stream, :, rows, :], vbuf_ref.at[slot], sem.at[1, slot]))

    def start_fetch(stream, j, slot):
        for queue, cp in enumerate(cache_copies(stream, j, slot)):
            cp.start(priority=queue)

    def wait_fetch(j, slot):
        for cp in cache_copies(b, j, slot):
            cp.wait()

    @pl.when(b == 0)
    def _():
        start_fetch(b, 0, HEAD_SLOT)

    overlap()

    def head_cols(h):
        return slice(h * HEAD_DIM, (h + 1) * HEAD_DIM)

    def head_rows(h):
        return slice(h * n_new, (h + 1) * n_new)

    def block(k_of, v_of, negu2, mask):
        z2 = jnp.concatenate([_nt_dot(q_ref[:, head_cols(h)], k_of(h)) for h in range(heads)], axis=0)
        w, carry = _sb_weights(z2, carry_ref[...], negu2, mask)
        wb = w.astype(BF16)
        for h in range(heads):
            acc_ref[head_rows(h), :] += _dot(wb[head_rows(h), :], v_of(h))
        carry_ref[...] = carry

    def any_alive():
        return (jnp.max(carry_ref[...]) > SB_DEAD).astype(jnp.int32)

    acc_ref[...] = jnp.zeros_like(acc_ref)
    carry_ref[...] = jnp.zeros_like(carry_ref)
    kpad_ref[...] = jnp.zeros_like(kpad_ref)
    vpad_ref[...] = jnp.zeros_like(vpad_ref)
    for h in range(heads):
        kpad_ref[h, 0:n_new, :] = kn_ref[:, head_cols(h)]
        vpad_ref[h, 0:n_new, :] = vn_ref[:, head_cols(h)]
    row = lax.broadcasted_iota(jnp.int32, (n_new, NEW_PAD), 0)
    col = lax.broadcasted_iota(jnp.int32, (n_new, NEW_PAD), 1)
    mask = jnp.concatenate([(col < row).astype(jnp.int32)] * heads, axis=0) == 1
    block(lambda h: kpad_ref[h], lambda h: vpad_ref[h], _neg_suffix_matrix(NEW_PAD), mask)

    negu2 = negu2_ref[...]

    def cond(state):
        j, alive = state
        return jnp.logical_and(j < n_blocks, alive > 0)

    def body(state):
        j, _ = state
        slot = slot_of(j)
        wait_fetch(j, slot)

        @pl.when(j + 1 < n_blocks)
        def _():
            start_fetch(b, j + 1, (j + 1) % 2)

        block(lambda h: kbuf_ref[slot, h].astype(BF16), lambda h: vbuf_ref[slot, h].astype(BF16),
              negu2, None)
        return j + 1, any_alive()

    j_end, _ = lax.while_loop(cond, body, (0, any_alive()))

    for h in range(heads):
        o_ref[:, head_cols(h)] = acc_ref[head_rows(h), :].astype(BF16)

    @pl.when(j_end < n_blocks)
    def _():
        wait_fetch(j_end, slot_of(j_end))

    @pl.when(b + 1 < pl.num_programs(0))
    def _():
        start_fetch(b + 1, 0, HEAD_SLOT)


def _softmax2_pv(parts):
    mx = functools.reduce(jnp.maximum, [jnp.max(s, axis=-1, keepdims=True) for s, _ in parts])
    num = None
    den = None
    for s, v in parts:
        p = jnp.exp2(s - mx)
        d = jnp.sum(p, axis=-1, keepdims=True)
        o = _dot(p.astype(BF16), v)
        num = o if num is None else num + o
        den = d if den is None else den + d
    return num / den


def _band_bias_kernel(g_ref, tp_ref, td_ref, *, n_new, r_band):
    x = jnp.broadcast_to(g_ref[...], (BAND_TQ, BIAS_LANES))
    x = pltpu.roll(x, BAND_TQ, 1, stride=1, stride_axis=0)
    tbl = x[:, :BAND_WIN] * LOG2E
    r = lax.broadcasted_iota(jnp.int32, (BAND_TQ, BAND_WIN), 0)
    j = lax.broadcasted_iota(jnp.int32, (BAND_TQ, BAND_WIN), 1)
    dc = (j >> CHUNK_SHIFT) - (r >> CHUNK_SHIFT)
    tp_ref[...] = jnp.where(jnp.logical_and(dc >= 0, dc <= BAND_LEFT_CHUNKS), tbl, NEG_INF)
    jd = lax.broadcasted_iota(jnp.int32, (n_new, r_band + NEW_PAD), 1)
    td_ref[...] = jnp.where(jd < r_band + n_new, tbl[:n_new, :r_band + NEW_PAD], NEG_INF)


def _band_bias_tables(rel_bias, n_new, r_band):
    h = rel_bias.shape[0]
    assert BAND_ROWS == 2 * MAX_REL and r_band == BAND_ROWS and BIAS_LANES == 2 * BAND_ROWS
    rb = rel_bias.astype(F32)
    g = jnp.concatenate([rb[:, :0:-1], jnp.broadcast_to(rb[:, -1:], (h, BIAS_LANES - 2 * MAX_REL))], axis=1)
    kern = functools.partial(_band_bias_kernel, n_new=n_new, r_band=r_band)
    return pl.pallas_call(
        kern,
        grid=(h,),
        in_specs=[pl.BlockSpec((None, 1, BIAS_LANES), lambda i: (i, 0, 0))],
        out_specs=[pl.BlockSpec((None, BAND_TQ, BAND_WIN), lambda i: (i, 0, 0)),
                   pl.BlockSpec((None, n_new, r_band + NEW_PAD), lambda i: (i, 0, 0))],
        out_shape=[jax.ShapeDtypeStruct((h, BAND_TQ, BAND_WIN), F32),
                   jax.ShapeDtypeStruct((h, n_new, r_band + NEW_PAD), F32)],
        compiler_params=_params(1),
        name="band_bias",
    )(g.reshape(h, 1, BIAS_LANES))


def _band_pad(k_ref, v_ref, kpad_ref, vpad_ref, t):
    kpad_ref[0:BAND_ROWS, :] = jnp.zeros((BAND_ROWS, HEAD_DIM), BF16)
    vpad_ref[0:BAND_ROWS, :] = jnp.zeros((BAND_ROWS, HEAD_DIM), BF16)
    kpad_ref[BAND_ROWS:BAND_ROWS + t, :] = k_ref[...]
    vpad_ref[BAND_ROWS:BAND_ROWS + t, :] = v_ref[...]


def _band_groups(q_ref, bias_ref, o_ref, kpad_ref, vpad_ref):
    s_idx = pl.program_id(2)
    col = lax.broadcasted_iota(jnp.int32, (BAND_TQ, BAND_WIN), 1)
    for gg in range(BAND_STEP_GROUPS):
        g = s_idx * BAND_STEP_GROUPS + gg
        start = pl.multiple_of(g * BAND_TQ, BAND_TQ)
        rows = slice(gg * BAND_TQ, (gg + 1) * BAND_TQ)
        s = _nt_dot(q_ref[rows, :], kpad_ref[pl.ds(start, BAND_WIN), :]) + bias_ref[...]
        s = jnp.where(col + g * BAND_TQ >= BAND_ROWS, s, NEG_INF)
        o_ref[rows, :] = _softmax2_pv([(s, vpad_ref[pl.ds(start, BAND_WIN), :])]).astype(BF16)


def _prompt_attn_kernel(qs_ref, ks_ref, vs_ref, negu2_ref, qb_ref, kb_ref, vb_ref, bias_ref,
                        osb_ref, obd_ref, acc_ref, carry_ref, kpads_ref, vpads_ref, kpadb_ref, vpadb_ref, *, t):
    _sb_prompt_body(qs_ref, ks_ref, vs_ref, negu2_ref, osb_ref, acc_ref, carry_ref, kpads_ref, vpads_ref, t=t,
                    at_first_step=lambda: _band_pad(kb_ref, vb_ref, kpadb_ref, vpadb_ref, t),
                    beside_diagonal=lambda: _band_groups(qb_ref, bias_ref, obd_ref, kpadb_ref, vpadb_ref))


def _prompt_attention(yqg3, ykv3, bias_tbl, negu2):
    b, t, _ = yqg3.shape
    tq = SB_TQ
    assert SB_TQ == BAND_TQ * BAND_STEP_GROUPS and SB_TQ // SB_TK >= 3 and SB_HEADS == BAND_HEADS
    q_spec = lambda c0: pl.BlockSpec((None, tq, HEAD_DIM),
                                     functools.partial(lambda b, h, i, c: (b, i, c + h), c=c0 // HEAD_DIM))
    kv_spec = lambda c0: pl.BlockSpec((None, t, HEAD_DIM),
                                      functools.partial(lambda b, h, i, c: (b, 0, c + h), c=c0 // HEAD_DIM))
    out_spec = pl.BlockSpec((None, tq, HEAD_DIM), lambda b, h, i: (b, i, h))
    kern = functools.partial(_prompt_attn_kernel, t=t)
    return pl.pallas_call(
        kern,
        grid=(b, SB_HEADS, t // tq),
        in_specs=[
            q_spec(QG_SB_Q), kv_spec(KV_SB_K), kv_spec(KV_SB_V),
            pl.BlockSpec((2 * SB_TK, SB_TK), lambda b, h, i: (0, 0)),
            q_spec(QG_BD_Q), kv_spec(KV_BD_K), kv_spec(KV_BD_V),
            pl.BlockSpec((None, BAND_TQ, BAND_WIN), lambda b, h, i: (h, 0, 0)),
        ],
        out_specs=[out_spec, out_spec],
        out_shape=[jax.ShapeDtypeStruct((b, t, SB_WIDTH), BF16), jax.ShapeDtypeStruct((b, t, BAND_WIDTH), BF16)],
        scratch_shapes=[pltpu.VMEM((SB_TQ, HEAD_DIM), F32), pltpu.VMEM((SB_TQ, 1), F32),
                        pltpu.VMEM((SB_TK + t, HEAD_DIM), BF16), pltpu.VMEM((SB_TK + t, HEAD_DIM), BF16),
                        pltpu.VMEM((BAND_ROWS + t, HEAD_DIM), BF16), pltpu.VMEM((BAND_ROWS + t, HEAD_DIM), BF16)],
        compiler_params=_params(3, vmem=MAX_VMEM_LIMIT_BYTES),
        name="prompt_attention",
    )(yqg3, ykv3, ykv3, negu2, yqg3, ykv3, ykv3, bias_tbl)


def _band_decode_kernel(q_ref, kn_ref, vn_ref, kc_ref, vc_ref, bias_ref, o_ref, kpad_ref, vpad_ref,
                        *, n_new, r_band):
    kpad_ref[...] = jnp.zeros_like(kpad_ref)
    vpad_ref[...] = jnp.zeros_like(vpad_ref)
    for h in range(BAND_HEADS):
        cols = slice(h * HEAD_DIM, (h + 1) * HEAD_DIM)
        kpad_ref[h, 0:n_new, :] = kn_ref[:, cols]
        vpad_ref[h, 0:n_new, :] = vn_ref[:, cols]
    for h in range(BAND_HEADS):
        cols = slice(h * HEAD_DIM, (h + 1) * HEAD_DIM)
        q = q_ref[:, cols]
        s_cache = _nt_dot(q, kc_ref[h].astype(BF16)) + bias_ref[h, :, 0:r_band]
        s_new = _nt_dot(q, kpad_ref[h]) + bias_ref[h, :, r_band:r_band + NEW_PAD]
        o_ref[:, cols] = _softmax2_pv([(s_cache, vc_ref[h].astype(BF16)),
                                       (s_new, vpad_ref[h])]).astype(BF16)


def _mem_attn_kernel(q_ref, mk_ref, mv_ref, o_ref):
    per_head = len(mk_ref.shape) == 3
    for h in range(MEM_HEADS):
        sl = slice(h * HEAD_DIM, (h + 1) * HEAD_DIM)
        mk = mk_ref[:, h, :] if per_head else mk_ref[:, sl]
        mv = mv_ref[:, h, :] if per_head else mv_ref[:, sl]
        s = _nt_dot(q_ref[:, sl], mk.astype(BF16))
        o_ref[:, sl] = _softmax2_pv([(s, mv.astype(BF16))]).astype(BF16)


def _mem_attention(y3, mk, mv, *, tq):
    b, t, _ = y3.shape
    n_mem = mk.shape[1]
    qb = MG_MM_Q // MEM_WIDTH
    return pl.pallas_call(
        _mem_attn_kernel,
        grid=(b, t // tq),
        in_specs=[
            pl.BlockSpec((None, tq, MEM_WIDTH), lambda b, i: (b, i, qb)),
            pl.BlockSpec((None, n_mem, MEM_WIDTH), lambda b, i: (b, 0, 0)),
            pl.BlockSpec((None, n_mem, MEM_WIDTH), lambda b, i: (b, 0, 0)),
        ],
        out_specs=pl.BlockSpec((None, tq, MEM_WIDTH), lambda b, i: (b, i, 0)),
        out_shape=jax.ShapeDtypeStruct((b, t, MEM_WIDTH), BF16),
        compiler_params=_params(2),
        name="mem_attention",
    )(y3, mk, mv)


def _decode_attn_kernel(qs_ref, kns_ref, vns_ref, kcs_hbm, vcs_hbm, negu2_ref,
                        qb_ref, knb_ref, vnb_ref, kcb_ref, vcb_ref, bias_ref, qm_ref, mk_ref, mv_ref,
                        osb_ref, obd_ref, omm_ref,
                        acc_ref, carry_ref, kpad_ref, vpad_ref, kbuf_ref, vbuf_ref, sem, kpadb_ref, vpadb_ref,
                        *, n_new, n_blocks, r_band):
    def band_and_memory():
        _band_decode_kernel(qb_ref, knb_ref, vnb_ref, kcb_ref, vcb_ref, bias_ref, obd_ref,
                            kpadb_ref, vpadb_ref, n_new=n_new, r_band=r_band)
        _mem_attn_kernel(qm_ref, mk_ref, mv_ref, omm_ref)

    _sb_decode_body(qs_ref, kns_ref, vns_ref, kcs_hbm, vcs_hbm, negu2_ref, osb_ref,
                    acc_ref, carry_ref, kpad_ref, vpad_ref, kbuf_ref, vbuf_ref, sem,
                    n_new=n_new, n_blocks=n_blocks, overlap=band_and_memory)


def _decode_attention(yqg3, ykv3, ymg3, cache_sb_k, cache_sb_v, cache_bd_k, cache_bd_v, mk, mv, bias_tbl, negu2):
    bd, n_new, _ = yqg3.shape
    past = cache_sb_k.shape[2]
    r_band = cache_bd_k.shape[2]
    n_mem = mk.shape[1]
    assert past % SB_TK == 0 and n_new <= NEW_PAD and n_new % 16 == 0
    kern = functools.partial(_decode_attn_kernel, n_new=n_new, n_blocks=past // SB_TK, r_band=r_band)
    slab = lambda width, col0: pl.BlockSpec((None, n_new, width), functools.partial(lambda b, c: (b, 0, c), c=col0 // width))
    band_cache = pl.BlockSpec((None, BAND_HEADS, r_band, HEAD_DIM), lambda b: (b, 0, 0, 0))
    mem_cache = pl.BlockSpec((None, n_mem, MEM_HEADS, HEAD_DIM), lambda b: (b, 0, 0, 0))
    return pl.pallas_call(
        kern,
        grid=(bd,),
        in_specs=[
            slab(SB_WIDTH, QG_SB_Q), slab(SB_WIDTH, KV_SB_K), slab(SB_WIDTH, KV_SB_V),
            pl.BlockSpec(memory_space=pl.ANY), pl.BlockSpec(memory_space=pl.ANY),
            pl.BlockSpec((2 * SB_TK, SB_TK), lambda b: (0, 0)),
            slab(BAND_WIDTH, QG_BD_Q), slab(BAND_WIDTH, KV_BD_K), slab(BAND_WIDTH, KV_BD_V),
            band_cache, band_cache,
            pl.BlockSpec((BAND_HEADS, n_new, r_band + NEW_PAD), lambda b: (0, 0, 0)),
            slab(MEM_WIDTH, MG_MM_Q), mem_cache, mem_cache,
        ],
        out_specs=[pl.BlockSpec((None, n_new, SB_WIDTH), lambda b: (b, 0, 0)),
                   pl.BlockSpec((None, n_new, BAND_WIDTH), lambda b: (b, 0, 0)),
                   pl.BlockSpec((None, n_new, MEM_WIDTH), lambda b: (b, 0, 0))],
        out_shape=[jax.ShapeDtypeStruct((bd, n_new, SB_WIDTH), BF16),
                   jax.ShapeDtypeStruct((bd, n_new, BAND_WIDTH), BF16),
                   jax.ShapeDtypeStruct((bd, n_new, MEM_WIDTH), BF16)],
        scratch_shapes=[pltpu.VMEM((SB_HEADS * n_new, HEAD_DIM), F32),
                        pltpu.VMEM((SB_HEADS * n_new, 1), F32),
                        pltpu.VMEM((SB_HEADS, NEW_PAD, HEAD_DIM), BF16),
                        pltpu.VMEM((SB_HEADS, NEW_PAD, HEAD_DIM), BF16),
                        pltpu.VMEM((HEAD_SLOT + 1, SB_HEADS, SB_TK, HEAD_DIM), F32),
                        pltpu.VMEM((HEAD_SLOT + 1, SB_HEADS, SB_TK, HEAD_DIM), F32),
                        pltpu.SemaphoreType.DMA((2, HEAD_SLOT + 1)),
                        pltpu.VMEM((BAND_HEADS, NEW_PAD, HEAD_DIM), BF16),
                        pltpu.VMEM((BAND_HEADS, NEW_PAD, HEAD_DIM), BF16)],
        compiler_params=_params(1),
        name="decode_attention",
    )(yqg3, ykv3, ykv3, cache_sb_k, cache_sb_v, negu2, yqg3, ykv3, ykv3, cache_bd_k, cache_bd_v, bias_tbl,
      ymg3, mk, mv)


def _silu_of_half(h):
    return h + h * jnp.tanh(h)


def _merge_kernel(osb_ref, obd_ref, omm_ref, gsb_ref, gbd_ref, gmm_ref,
                  mg0_ref, mg1_ref, mg2_ref, mg3_ref, mg4_ref, mg5_ref,
                  wsb_ref, wbd_ref, wmm_ref, merged_ref, *, half):
    u_sb = (osb_ref[...].astype(F32) * _silu_of_half(gsb_ref[...].astype(F32))).astype(BF16)
    u_bd = (obd_ref[...].astype(F32) * _silu_of_half(gbd_ref[...].astype(F32))).astype(BF16)
    u_mm = (omm_ref[...].astype(F32) * _silu_of_half(gmm_ref[...].astype(F32))).astype(BF16)
    mg = ((mg0_ref, mg2_ref, mg4_ref), (mg1_ref, mg3_ref, mg5_ref))
    for n in range(2):
        cols = slice(n * half, (n + 1) * half)
        merged = None
        for m_ref, u, w_ref in zip(mg[n], (u_sb, u_bd, u_mm), (wsb_ref, wbd_ref, wmm_ref)):
            a = _dot(u, w_ref[:, cols])
            term = a + a * jnp.tanh(m_ref[...].astype(F32))
            merged = term if merged is None else merged + term
        merged_ref[:, cols] = merged.astype(BF16)


def _merge_branches(yqg, ymg, o_sb, o_bd, o_mm, w_sb, w_bd, w_mm, *, tm):
    m = yqg.shape[0]
    d = w_sb.shape[1]
    half = d // 2
    assert MG_MG % half == 0
    mgb = MG_MG // half
    const = dict(pipeline_mode=pl.Buffered(1))
    kern = functools.partial(_merge_kernel, half=half)
    return pl.pallas_call(
        kern,
        grid=(m // tm,),
        in_specs=[
            pl.BlockSpec((tm, SB_WIDTH), lambda i: (i, 0)),
            pl.BlockSpec((tm, BAND_WIDTH), lambda i: (i, 0)),
            pl.BlockSpec((tm, MEM_WIDTH), lambda i: (i, 0)),
            pl.BlockSpec((tm, SB_WIDTH), lambda i: (i, QG_SB_G // SB_WIDTH)),
            pl.BlockSpec((tm, BAND_WIDTH), lambda i: (i, QG_BD_G // BAND_WIDTH)),
            pl.BlockSpec((tm, MEM_WIDTH), lambda i: (i, MG_MM_G // MEM_WIDTH)),
        ] + [pl.BlockSpec((tm, half), functools.partial(lambda i, c: (i, c), c=mgb + c)) for c in range(6)] + [
            pl.BlockSpec((SB_WIDTH, d), lambda i: (0, 0), **const),
            pl.BlockSpec((BAND_WIDTH, d), lambda i: (0, 0), **const),
            pl.BlockSpec((MEM_WIDTH, d), lambda i: (0, 0), **const),
        ],
        out_specs=pl.BlockSpec((tm, d), lambda i: (i, 0)),
        out_shape=jax.ShapeDtypeStruct((m, d), BF16),
        compiler_params=_params(1),
        name="merge_branches",
    )(o_sb, o_bd, o_mm, yqg, yqg, ymg, *([ymg] * 6), w_sb, w_bd, w_mm)


def _out_proj_kernel(x_ref, merged_ref, wout_ref, gpost_ref, y_ref):
    y = _dot(merged_ref[...], wout_ref[...])
    ms = jnp.mean(y * y, axis=-1, keepdims=True)
    y_ref[...] = x_ref[...] + (y * lax.rsqrt(ms + RMS_EPS)) * gpost_ref[...]


def _merge_out_kernel(x_ref, osb_ref, obd_ref, omm_ref, gsb_ref, gbd_ref, gmm_ref,
                      mg0_ref, mg1_ref, mg2_ref, mg3_ref, mg4_ref, mg5_ref,
                      wsb_ref, wbd_ref, wmm_ref, wout_ref, gpost_ref, y_ref, merged_ref, *, half):
    _merge_kernel(osb_ref, obd_ref, omm_ref, gsb_ref, gbd_ref, gmm_ref,
                  mg0_ref, mg1_ref, mg2_ref, mg3_ref, mg4_ref, mg5_ref,
                  wsb_ref, wbd_ref, wmm_ref, merged_ref, half=half)
    _out_proj_kernel(x_ref, merged_ref, wout_ref, gpost_ref, y_ref)


def _merge_out(x2d, yqg, ymg, o_sb, o_bd, o_mm, w_sb, w_bd, w_mm, w_out, g_post):
    m, d = x2d.shape
    half = d // 2
    mgb = MG_MG // half
    whole = lambda shape: pl.BlockSpec(shape, lambda i: (0,) * len(shape))
    col = lambda width, c: pl.BlockSpec((m, width), functools.partial(lambda i, c: (0, c), c=c))
    kern = functools.partial(_merge_out_kernel, half=half)
    return pl.pallas_call(
        kern,
        grid=(1,),
        in_specs=[
            whole((m, d)), whole((m, SB_WIDTH)), whole((m, BAND_WIDTH)), whole((m, MEM_WIDTH)),
            col(SB_WIDTH, QG_SB_G // SB_WIDTH), col(BAND_WIDTH, QG_BD_G // BAND_WIDTH),
            col(MEM_WIDTH, MG_MM_G // MEM_WIDTH),
        ] + [col(half, mgb + c) for c in range(6)] + [
            whole((SB_WIDTH, d)), whole((BAND_WIDTH, d)), whole((MEM_WIDTH, d)), whole((d, d)), whole((1, d)),
        ],
        out_specs=whole((m, d)),
        out_shape=jax.ShapeDtypeStruct((m, d), F32),
        scratch_shapes=[pltpu.VMEM((m, d), BF16)],
        compiler_params=_params(1),
        name="merge_out",
    )(x2d, o_sb, o_bd, o_mm, yqg, yqg, ymg, *([ymg] * 6), w_sb, w_bd, w_mm, w_out, g_post.reshape(1, d))


def _out_projection(x2d, merged, w_out, g_post, *, tm):
    m, d = x2d.shape
    return pl.pallas_call(
        _out_proj_kernel,
        grid=(m // tm,),
        in_specs=[
            pl.BlockSpec((tm, d), lambda i: (i, 0)),
            pl.BlockSpec((tm, d), lambda i: (i, 0)),
            pl.BlockSpec((d, d), lambda i: (0, 0), pipeline_mode=pl.Buffered(1)),
            pl.BlockSpec((1, d), lambda i: (0, 0)),
        ],
        out_specs=pl.BlockSpec((tm, d), lambda i: (i, 0)),
        out_shape=jax.ShapeDtypeStruct((m, d), F32),
        compiler_params=_params(1),
        name="out_projection",
    )(x2d, merged, w_out, g_post.reshape(1, d))


def _head_major(a):
    return jnp.transpose(a, (0, 2, 1, 3))


def kernel(x_prompt, x_sample, cache_sb_k, cache_sb_v, cache_band_k, cache_band_v, cache_mem_k, cache_mem_v, mem_prompt, g_pre, w_in, rel_bias, g_mem, w_mem_kv, w_up_sb, w_up_band, w_up_mem, w_out, g_post):
    depth = w_in.shape[0]
    b, t, d = x_prompt.shape
    bd, n_new, _ = x_sample.shape
    n_mem = mem_prompt.shape[1]
    r_band = cache_band_k.shape[2]
    in_width = w_in.shape[2]
    band_keep = min(BAND_ROWS, t)
    assert COL_MG + 3 * d == in_width
    assert t % SB_TQ == 0 and t % (BAND_TQ * BAND_STEP_GROUPS) == 0 and t % band_keep == 0
    assert r_band == BAND_ROWS and n_new <= CHUNK

    negu2 = jnp.where(jnp.arange(2 * SB_TK)[:, None] % SB_TK >= jnp.arange(SB_TK)[None, :], -1.0, 0.0).astype(BF16)
    mg_width = in_width - COL_MM_Q
    assert COL_MM_Q % MG_TN == 0 and mg_width % MG_TN == 0 and COL_SB_G == 3 * KV_TN and COL_BD_Q == 4 * KV_TN
    cols = jnp.arange(QG_WIDTH)
    qg_scale = jnp.where((cols // SB_WIDTH) % 2 == 0, Q_SCALE, 0.5).astype(F32).reshape(1, QG_WIDTH)
    cols = jnp.arange(mg_width)
    mg_scale = jnp.where(cols < MG_MM_G, Q_SCALE, 0.5).astype(F32).reshape(1, mg_width)
    qg_block = lambda j: j + 2 * ((j + 1) // 2)
    mg_block = lambda j: j + COL_MM_Q // MG_TN

    xp = x_prompt.reshape(b * t, d)
    xs = x_sample.reshape(bd * n_new, d)
    ms = bd * n_new
    outs = [[] for _ in range(10)]
    for l in range(depth):
        w_kvp_b = _kv_weight_bf16(w_in[l])
        w_sb_b = (0.5 * w_up_sb[l]).astype(BF16)
        w_bd_b = (0.5 * w_up_band[l]).astype(BF16)
        w_mm_b = (0.5 * w_up_mem[l]).astype(BF16)
        w_out_b = w_out[l].astype(BF16)
        bias_p, bias_d = _band_bias_tables(rel_bias[l], n_new, r_band)

        (ykv, hp, sbk, sbv, bdk, bdv, ykv_s, hs, sbk2, sbv2, bdk2, bdv2) = _kv_projection(
            xp, xs, g_pre[l], w_kvp_b, n_seq=b, n_seq_s=bd, band_keep=band_keep)
        yqg, yqg_s = _col_projection(hp, hs, w_in[l], qg_scale, qg_block, tm=PROJ_TM, tn=KV_TN,
                                     name="qg_projection")
        ymg, ymg_s = _col_projection(hp, hs, w_in[l], mg_scale, mg_block, tm=PROJ_TM, tn=MG_TN,
                                     name="mg_projection")

        mk, mv = _memory_kv(mem_prompt.reshape(b * n_mem, d), g_mem[l], w_mem_kv[l], tm=n_mem)
        yqg3 = yqg.reshape(b, t, QG_WIDTH)
        ykv3 = ykv.reshape(b, t, KV_WIDTH)
        o_sb, o_bd = _prompt_attention(yqg3, ykv3, bias_p, negu2)
        o_mm = _mem_attention(ymg.reshape(b, t, mg_width), mk.reshape(b, n_mem, MEM_WIDTH),
                              mv.reshape(b, n_mem, MEM_WIDTH), tq=MEM_TQ)
        merged = _merge_branches(yqg, ymg, o_sb.reshape(b * t, -1), o_bd.reshape(b * t, -1),
                                 o_mm.reshape(b * t, -1), w_sb_b, w_bd_b, w_mm_b, tm=OUT_TM)
        xp = _out_projection(xp, merged, w_out_b, g_post[l], tm=OUT_TM)
        outs[0].append(_head_major(sbk))
        outs[1].append(_head_major(sbv))
        outs[2].append(_head_major(bdk))
        outs[3].append(_head_major(bdv))
        outs[4].append(mk.reshape(b, n_mem, MEM_HEADS, HEAD_DIM))
        outs[5].append(mv.reshape(b, n_mem, MEM_HEADS, HEAD_DIM))

        o_sb2, o_bd2, o_mm2 = _decode_attention(
            yqg_s.reshape(bd, n_new, QG_WIDTH), ykv_s.reshape(bd, n_new, KV_WIDTH),
            ymg_s.reshape(bd, n_new, mg_width),
            _head_major(cache_sb_k[l]), _head_major(cache_sb_v[l]),
            _head_major(cache_band_k[l]), _head_major(cache_band_v[l]),
            cache_mem_k[l], cache_mem_v[l],
            bias_d, negu2)
        xs = _merge_out(xs, yqg_s, ymg_s, o_sb2.reshape(ms, -1), o_bd2.reshape(ms, -1), o_mm2.reshape(ms, -1),
                        w_sb_b, w_bd_b, w_mm_b, w_out_b, g_post[l])
        outs[6].append(_head_major(sbk2))
        outs[7].append(_head_major(sbv2))
        outs[8].append(_head_major(bdk2))
        outs[9].append(_head_major(bdv2))

    return (xp.reshape(b, t, d), xs.reshape(bd, n_new, d)) + tuple(jnp.stack(o) for o in outs)
```
